```python
import math, functools
import jax, jax.numpy as jnp
from jax import lax
import numpy as np

D_MODEL = 2048
BATCH = 1
SEQ = 8192
DEPTH = 1
DEC_BATCH = 128
DEC_SEQ = 1
PAST_LEN = 8192
PAGE_SIZE = 128

D_MIX = D_MODEL
D_RNN = D_MIX // 2
N_RNN_BLOCKS = 16
RNN_BLOCK = D_RNN // N_RNN_BLOCKS
RNN_CONV = 4
LRU_C = 8.0
N_HEADS = 8
HEAD_DIM = 128
N_KV = 2
GROUP = N_HEADS // N_KV
D_ATTN = N_HEADS * HEAD_DIM
WINDOW = 128
BLOCK = 128
NUM_BUCKETS = 32
MAX_DISTANCE = 128
D_FF = 3 * D_MODEL
FFN_CONV = 3
N_META = 16
EPS = 1e-6
NEG = -1e30
D_IN = 2 * D_RNN + D_ATTN + 2 * N_KV * HEAD_DIM

kernel_name = "hymba_rglru_swa_sink_convffn_step"


def rmsnorm(x, g):
    xf = x.astype(jnp.float32)
    y = xf * lax.rsqrt(jnp.mean(xf * xf, axis=-1, keepdims=True) + EPS)
    return (y * g.astype(jnp.float32)).astype(x.dtype)


def causal_dwconv(x_hist, w, b):
    width = w.shape[0]
    t = x_hist.shape[1] - width + 1
    out = b
    for j in range(width):
        out = out + x_hist[:, j:j + t] * w[j]
    return out


def _lin_comb(left, right):
    a1, b1 = left
    a2, b2 = right
    return a1 * a2, a2 * b1 + b2


def rglru(xc, h0, gate_a_w, gate_a_b, gate_x_w, gate_x_b, rnn_lambda):
    bsz, t, _ = xc.shape
    xb = xc.reshape(bsz, t, N_RNN_BLOCKS, RNN_BLOCK)
    r = jax.nn.sigmoid((jnp.einsum('btnc,ncd->btnd', xb, gate_a_w).reshape(bsz, t, D_RNN) + gate_a_b).astype(jnp.float32))
    i = jax.nn.sigmoid((jnp.einsum('btnc,ncd->btnd', xb, gate_x_w).reshape(bsz, t, D_RNN) + gate_x_b).astype(jnp.float32))
    log_a = -LRU_C * r * jax.nn.softplus(-rnn_lambda.astype(jnp.float32))
    a = jnp.exp(log_a)
    b = jnp.sqrt(-jnp.expm1(2.0 * log_a)) * i * xc.astype(jnp.float32)
    a_cum, b_cum = lax.associative_scan(_lin_comb, (a, b), axis=1)
    h = a_cum * h0.astype(jnp.float32)[:, None] + b_cum
    return h.astype(xc.dtype), h[:, -1].astype(xc.dtype)


def rel_bucket(d):
    d = jnp.maximum(d, 0)
    exact = NUM_BUCKETS // 2
    large = exact + (jnp.log(jnp.maximum(d, 1).astype(jnp.float32) / exact)
                     / math.log(MAX_DISTANCE / exact) * (NUM_BUCKETS - exact)).astype(jnp.int32)
    large = jnp.minimum(large, NUM_BUCKETS - 1)
    return jnp.where(d < exact, d, large)


def rel_bias(d, table):
    q, s = d.shape
    bias = table[rel_bucket(d)].astype(jnp.float32)
    return bias.reshape(q, s, N_KV, GROUP).transpose(2, 3, 0, 1)


def sink_softmax(logits, sinks):
    sink = jnp.broadcast_to(sinks.astype(jnp.float32).reshape(N_KV, GROUP, 1, 1), logits.shape[:-1] + (1,))
    p = jax.nn.softmax(jnp.concatenate([logits, sink], axis=-1), axis=-1)
    return p[..., :-1]


def window_attn_prompt(q, k, v, sinks, table):
    bsz, t = q.shape[:2]
    p_front = (-N_META) % BLOCK
    p_end = (-(t + p_front)) % BLOCK
    pad = ((0, 0), (p_front, p_end), (0, 0), (0, 0))
    qp, kp, vp = jnp.pad(q, pad), jnp.pad(k, pad), jnp.pad(v, pad)
    tp = t + p_front + p_end
    nb = tp // BLOCK
    qb = qp.reshape(bsz, nb, BLOCK, N_KV, GROUP, HEAD_DIM)
    kb = kp.reshape(bsz, nb, BLOCK, N_KV, HEAD_DIM)
    vb = vp.reshape(bsz, nb, BLOCK, N_KV, HEAD_DIM)
    shift = ((0, 0), (1, 0), (0, 0), (0, 0), (0, 0))
    kk = jnp.concatenate([jnp.pad(kb, shift)[:, :-1], kb], axis=2)
    vv = jnp.concatenate([jnp.pad(vb, shift)[:, :-1], vb], axis=2)
    qi = jnp.arange(BLOCK)
    sj = jnp.arange(2 * BLOCK)
    d = BLOCK + qi[:, None] - sj[None, :]
    band = (d >= 0) & (d < WINDOW)
    kidx = jnp.arange(nb)[:, None] * BLOCK - BLOCK + sj[None, :]
    mask = band[None] & (kidx >= p_front)[:, None, :]
    logits = jnp.einsum('bnqkgd,bnskd->bnkgqs', qb, kk).astype(jnp.float32) * (HEAD_DIM ** -0.5)
    logits = logits + rel_bias(d, table)
    logits = jnp.where(mask[None, :, None, None], logits, NEG)
    p = sink_softmax(logits, sinks)
    out = jnp.einsum('bnkgqs,bnskd->bnqkgd', p.astype(v.dtype), vv).reshape(bsz, tp, D_ATTN)
    return out[:, p_front:p_front + t], k[:, -WINDOW:], v[:, -WINDOW:]


def window_attn_sample(q, k, v, k_buf, v_buf, sinks, table):
    bsz, s = q.shape[:2]
    kk = jnp.concatenate([k_buf, k], axis=1)
    vv = jnp.concatenate([v_buf, v], axis=1)
    d = (WINDOW + jnp.arange(s))[:, None] - jnp.arange(WINDOW + s)[None, :]
    mask = (d >= 0) & (d < WINDOW)
    qg = q.reshape(bsz, s, N_KV, GROUP, HEAD_DIM)
    logits = jnp.einsum('bqkgd,bskd->bkgqs', qg, kk).astype(jnp.float32) * (HEAD_DIM ** -0.5)
    logits = logits + rel_bias(d, table)
    logits = jnp.where(mask, logits, NEG)
    p = sink_softmax(logits, sinks)
    out = jnp.einsum('bkgqs,bskd->bqkgd', p.astype(v.dtype), vv).reshape(bsz, s, D_ATTN)
    return out, kk[:, -WINDOW:], vv[:, -WINDOW:]


def decoder_layer(x, rnn_conv_hist, rnn_h0, ffn_hist, attn_fn,
                  norm_mix, w_in, rnn_conv_w, rnn_conv_b, gate_a_w, gate_a_b, gate_x_w, gate_x_b,
                  rnn_lambda, norm_rnn_out, norm_attn_out, w_out,
                  norm_ffn, w_up, w_gate, ffn_conv_w, ffn_conv_b, w_down):
    bsz, t, _ = x.shape
    h = rmsnorm(x, norm_mix)
    z = h @ w_in
    o1 = D_RNN
    o2 = o1 + D_RNN
    o3 = o2 + D_ATTN
    o4 = o3 + N_KV * HEAD_DIM
    xr, gr, q, k, v = z[..., :o1], z[..., o1:o2], z[..., o2:o3], z[..., o3:o4], z[..., o4:]
    xr_hist = jnp.concatenate([rnn_conv_hist, xr], axis=1)
    xc = causal_dwconv(xr_hist, rnn_conv_w, rnn_conv_b)
    y_rnn, h_last = rglru(xc, rnn_h0, gate_a_w, gate_a_b, gate_x_w, gate_x_b, rnn_lambda)
    y_rnn = y_rnn * jax.nn.gelu(gr)
    y_attn, k_state, v_state = attn_fn(q.reshape(bsz, t, N_HEADS, HEAD_DIM),
                                       k.reshape(bsz, t, N_KV, HEAD_DIM),
                                       v.reshape(bsz, t, N_KV, HEAD_DIM))
    mix = jnp.concatenate([rmsnorm(y_rnn, norm_rnn_out), rmsnorm(y_attn, norm_attn_out)], axis=-1)
    x = x + mix @ w_out
    h2 = rmsnorm(x, norm_ffn)
    u = h2 @ w_up
    u_hist = jnp.concatenate([ffn_hist, u], axis=1)
    uc = causal_dwconv(u_hist, ffn_conv_w, ffn_conv_b)
    x = x + (jax.nn.gelu(uc) * (h2 @ w_gate)) @ w_down
    return x, xr_hist[:, -(RNN_CONV - 1):], h_last, k_state, v_state, u_hist[:, -(FFN_CONV - 1):]


def setup_inputs(seed: int = 0) -> dict:
    key = jax.random.key(seed)
    ks = jax.random.split(key, 32)
    f32 = jnp.float32
    nrm = lambda k, shape, s: jax.random.normal(k, shape, f32) * s
    u = jax.random.uniform(ks[10], (DEPTH, D_RNN), f32, 0.9, 0.999)
    s_gate = u ** (1.0 / LRU_C)
    rnn_lambda = jnp.log(s_gate) - jnp.log1p(-s_gate)
    return {
        "x_prompt": nrm(ks[0], (BATCH, SEQ, D_MODEL), 1.0),
        "x_sample": nrm(ks[1], (DEC_BATCH, DEC_SEQ, D_MODEL), 1.0),
        "state_rnn_conv": nrm(ks[2], (DEPTH, DEC_BATCH, RNN_CONV - 1, D_RNN), 1.0),
        "state_rnn_h": nrm(ks[3], (DEPTH, DEC_BATCH, D_RNN), 0.5),
        "cache_k_win": nrm(ks[4], (DEPTH, DEC_BATCH, WINDOW, N_KV, HEAD_DIM), 1.0),
        "cache_v_win": nrm(ks[5], (DEPTH, DEC_BATCH, WINDOW, N_KV, HEAD_DIM), 1.0),
        "state_ffn_conv": nrm(ks[6], (DEPTH, DEC_BATCH, FFN_CONV - 1, D_FF), 1.0),
        "meta_tokens": nrm(ks[7], (N_META, D_MODEL), 1.0),
        "rel_bias_table": nrm(ks[8], (NUM_BUCKETS, N_HEADS), 0.5),
        "norm_mix": 1.0 + nrm(ks[9], (DEPTH, D_MODEL), 0.02),
        "w_in": nrm(ks[11], (DEPTH, D_MODEL, D_IN), D_MODEL ** -0.5),
        "rnn_conv_w": nrm(ks[12], (DEPTH, RNN_CONV, D_RNN), RNN_CONV ** -0.5),
        "rnn_conv_b": nrm(ks[13], (DEPTH, D_RNN), 0.01),
        "gate_a_w": nrm(ks[14], (DEPTH, N_RNN_BLOCKS, RNN_BLOCK, RNN_BLOCK), RNN_BLOCK ** -0.5),
        "gate_a_b": nrm(ks[15], (DEPTH, D_RNN), 0.01),
        "gate_x_w": nrm(ks[16], (DEPTH, N_RNN_BLOCKS, RNN_BLOCK, RNN_BLOCK), RNN_BLOCK ** -0.5),
        "gate_x_b": nrm(ks[17], (DEPTH, D_RNN), 0.01),
        "rnn_lambda": rnn_lambda,
        "attn_sinks": nrm(ks[18], (DEPTH, N_HEADS), 0.5),
        "norm_rnn_out": 1.0 + nrm(ks[19], (DEPTH, D_RNN), 0.02),
        "norm_attn_out": 1.0 + nrm(ks[20], (DEPTH, D_ATTN), 0.02),
        "w_out": nrm(ks[21], (DEPTH, D_MIX, D_MODEL), D_MIX ** -0.5),
        "norm_ffn": 1.0 + nrm(ks[22], (DEPTH, D_MODEL), 0.02),
        "w_up": nrm(ks[23], (DEPTH, D_MODEL, D_FF), D_MODEL ** -0.5),
        "w_gate": nrm(ks[24], (DEPTH, D_MODEL, D_FF), D_MODEL ** -0.5),
        "ffn_conv_w": nrm(ks[25], (DEPTH, FFN_CONV, D_FF), FFN_CONV ** -0.5),
        "ffn_conv_b": nrm(ks[26], (DEPTH, D_FF), 0.01),
        "w_down": nrm(ks[27], (DEPTH, D_FF, D_MODEL), D_FF ** -0.5),
        "norm_final": 1.0 + nrm(ks[28], (D_MODEL,), 0.02),
    }


def reference(x_prompt, x_sample, state_rnn_conv, state_rnn_h, cache_k_win, cache_v_win, state_ffn_conv,
              meta_tokens, rel_bias_table, norm_mix, w_in, rnn_conv_w, rnn_conv_b,
              gate_a_w, gate_a_b, gate_x_w, gate_x_b, rnn_lambda, attn_sinks,
              norm_rnn_out, norm_attn_out, w_out, norm_ffn, w_up, w_gate,
              ffn_conv_w, ffn_conv_b, w_down, norm_final):
    dt = x_prompt.dtype
    meta = jnp.broadcast_to(meta_tokens.astype(dt)[None], (x_prompt.shape[0], N_META, D_MODEL))
    xp = jnp.concatenate([meta, x_prompt], axis=1)
    xs = x_sample
    bp = xp.shape[0]
    p_states = ([], [], [], [], [])
    s_states = ([], [], [], [], [])
    for l in range(DEPTH):
        lw = (norm_mix[l], w_in[l], rnn_conv_w[l], rnn_conv_b[l], gate_a_w[l], gate_a_b[l],
              gate_x_w[l], gate_x_b[l], rnn_lambda[l], norm_rnn_out[l], norm_attn_out[l], w_out[l],
              norm_ffn[l], w_up[l], w_gate[l], ffn_conv_w[l], ffn_conv_b[l], w_down[l])
        attn_p = functools.partial(window_attn_prompt, sinks=attn_sinks[l], table=rel_bias_table)
        xp, *ps = decoder_layer(xp,
                                jnp.zeros((bp, RNN_CONV - 1, D_RNN), dt),
                                jnp.zeros((bp, D_RNN), dt),
                                jnp.zeros((bp, FFN_CONV - 1, D_FF), dt),
                                attn_p, *lw)
        attn_s = functools.partial(window_attn_sample, k_buf=cache_k_win[l], v_buf=cache_v_win[l],
                                   sinks=attn_sinks[l], table=rel_bias_table)
        xs, *ss = decoder_layer(xs, state_rnn_conv[l], state_rnn_h[l], state_ffn_conv[l], attn_s, *lw)
        for lst, val in zip(p_states, ps):
            lst.append(val)
        for lst, val in zip(s_states, ss):
            lst.append(val)
    y_prompt = rmsnorm(xp, norm_final)[:, N_META:]
    y_sample = rmsnorm(xs, norm_final)
    return (y_prompt, y_sample,
            jnp.stack(p_states[0]), jnp.stack(p_states[1]), jnp.stack(p_states[2]),
            jnp.stack(p_states[3]), jnp.stack(p_states[4]),
            jnp.stack(s_states[0]), jnp.stack(s_states[1]), jnp.stack(s_states[2]),
            jnp.stack(s_states[3]), jnp.stack(s_states[4]))
```

```python
import functools
import math

import numpy as np
import jax
import jax.numpy as jnp
from jax import lax
from jax.experimental import pallas as pl
from jax.experimental.pallas import tpu as pltpu

F32 = jnp.float32
BF16 = jnp.bfloat16

D_MODEL = 2048
SEQ = 8192
DEC_BATCH = 128
D_RNN = 1024
N_RNN_BLOCKS = 16
RNN_BLOCK = D_RNN // N_RNN_BLOCKS
RNN_CONV = 4
LRU_C = 8.0
N_HEADS = 8
HEAD_DIM = 128
N_KV = 2
GROUP = N_HEADS // N_KV
D_ATTN = N_HEADS * HEAD_DIM
WINDOW = 128
BLOCK = 128
NUM_BUCKETS = 32
MAX_DISTANCE = 128
D_FF = 3 * D_MODEL
FFN_CONV = 3
N_META = 16
EPS = 1e-6
NEG = -1e30
D_KV = N_KV * HEAD_DIM
D_IN = 2 * D_RNN + D_ATTN + 2 * D_KV
SCALE = HEAD_DIM ** -0.5

N_PAD = BLOCK - N_META
R_TAIL = DEC_BATCH + BLOCK
R_ALL = SEQ + R_TAIL
SAMPLE_BLK = SEQ // BLOCK
PREFIX_BLK = SAMPLE_BLK + 1
N_BLK = R_ALL // BLOCK

TM = 768
N_RT = R_ALL // TM
LAST_P = SEQ - (N_RT - 1) * TM
TN_IN = 512
TF = 512
N_FT = D_FF // TF
HALO = 8
GW = 256
N_GW = D_RNN // GW
SB = 16
VMEM_LIMIT = 56 * 1024 * 1024


def _rms(x, g):
    return x * lax.rsqrt(jnp.mean(x * x, axis=-1, keepdims=True) + EPS) * g


def _rel_buckets(d):
    d = np.maximum(d, 0)
    exact = NUM_BUCKETS // 2
    ratio = np.maximum(d, 1).astype(np.float32) / np.float32(exact)
    large = exact + (np.log(ratio) / np.float32(math.log(MAX_DISTANCE / exact))
                     * np.float32(NUM_BUCKETS - exact)).astype(np.int32)
    large = np.minimum(large, NUM_BUCKETS - 1)
    return np.where(d < exact, d, large).astype(np.int32)


def _table_lookup(bucket, tab_ref, h):
    out = jnp.zeros(bucket.shape, F32)
    for b in range(NUM_BUCKETS):
        out = jnp.where(bucket == b, tab_ref[b, h], out)
    return out


def _gates(xc, wg_ref, gab, gxb, lam):
    xcb = xc.astype(BF16)
    ga, gx = [], []
    for j in range(N_GW):
        gj = jnp.dot(xcb[:, GW * j:GW * (j + 1)], wg_ref[j], preferred_element_type=F32)
        ga.append(gj[:, :GW])
        gx.append(gj[:, GW:])
    r = jax.nn.sigmoid(jnp.concatenate(ga, axis=1) + gab)
    i = jax.nn.sigmoid(jnp.concatenate(gx, axis=1) + gxb)
    log_a = -LRU_C * r * jax.nn.softplus(-lam)
    a = jnp.exp(log_a)
    b = jnp.sqrt(1.0 - a * a) * i * xc
    return a, b


def _x_tile(i, xp_ref, xt_ref, fn, dst):
    @pl.when(i < N_RT - 1)
    def _():
        dst[...] = fn(xp_ref[...])

    @pl.when(i == N_RT - 1)
    def _():
        dst[0:LAST_P] = fn(xp_ref[0:LAST_P])
        dst[LAST_P:TM] = fn(xt_ref[...])


def _inproj_kernel(xp_ref, xt_ref, g_ref, w_ref, z_ref, h_scr):
    i = pl.program_id(0)

    @pl.when(pl.program_id(1) == 0)
    def _():
        _x_tile(i, xp_ref, xt_ref, lambda x: _rms(x, g_ref[...]).astype(BF16), h_scr)

    z_ref[...] = jnp.dot(h_scr[...], w_ref[...], preferred_element_type=F32)


def _inproj(xp, xt, g, w):
    return pl.pallas_call(
        _inproj_kernel,
        grid=(N_RT, D_IN // TN_IN),
        in_specs=[
            pl.BlockSpec((TM, D_MODEL), lambda i, j: (i, 0)),
            pl.BlockSpec((R_TAIL, D_MODEL), lambda i, j: (0, 0)),
            pl.BlockSpec((1, D_MODEL), lambda i, j: (0, 0)),
            pl.BlockSpec((D_MODEL, TN_IN), lambda i, j: (0, j)),
        ],
        out_specs=pl.BlockSpec((TM, TN_IN), lambda i, j: (i, j)),
        out_shape=jax.ShapeDtypeStruct((R_ALL, D_IN), F32),
        scratch_shapes=[pltpu.VMEM((TM, D_MODEL), BF16)],
        compiler_params=pltpu.CompilerParams(
            dimension_semantics=("arbitrary", "arbitrary"), vmem_limit_bytes=VMEM_LIMIT),
        name="inproj",
    )(xp, xt, g, w)


def _mixer_kernel(tab_ref, sink_ref, xr_ref, gr_ref, q_ref, kv_ref, kvp_ref, bucket_ref,
                  cw_ref, cb_ref, wg_ref, gab_ref, gxb_ref, lam_ref, nr_ref, na_ref,
                  mix_ref, hlast_ref, xbuf, h_scr, bias_scr):
    s = pl.program_id(0)

    @pl.when(s == 0)
    def _():
        xbuf[0:HALO] = jnp.zeros((HALO, D_RNN), F32)
        h_scr[...] = jnp.zeros((HALO, D_RNN), F32)
        bucket = bucket_ref[...]
        for h in range(N_HEADS):
            bias_scr[h] = _table_lookup(bucket, tab_ref, h)

    xr = xr_ref[...]
    xbuf[HALO:HALO + BLOCK] = xr
    cw = cw_ref[...]
    xc = cb_ref[...]
    for j in range(RNN_CONV - 1):
        lo = HALO - (RNN_CONV - 1) + j
        xc = xc + xbuf[lo:lo + BLOCK] * cw[j:j + 1]
    xc = xc + xr * cw[RNN_CONV - 1:RNN_CONV]
    xbuf[0:HALO] = xr[BLOCK - HALO:BLOCK]

    a, b = _gates(xc, wg_ref, gab_ref[...], gxb_ref[...], lam_ref[...])
    row = lax.broadcasted_iota(jnp.int32, (BLOCK, D_RNN), 0)
    b = jnp.where(jnp.logical_and(s == 0, row < N_PAD), 0.0, b)

    ng = BLOCK // HALO
    a3 = a.reshape(ng, HALO, D_RNN)
    b3 = b.reshape(ng, HALO, D_RNN)
    sub = lax.broadcasted_iota(jnp.int32, (ng, HALO, D_RNN), 1)
    sh = 1
    while sh < HALO:
        a_prev = pltpu.roll(a3, sh, 1)
        b_prev = pltpu.roll(b3, sh, 1)
        m = sub >= sh
        b3 = jnp.where(m, a3 * b_prev + b3, b3)
        a3 = jnp.where(m, a3 * a_prev, a3)
        sh *= 2
    h = h_scr[0:1]
    hs = []
    for g in range(ng):
        hg = a3[g] * h + b3[g]
        hs.append(hg)
        h = hg[HALO - 1:HALO]
    h_all = jnp.concatenate(hs, axis=0)
    h_scr[...] = jnp.broadcast_to(h, (HALO, D_RNN))
    hlast_ref[...] = jnp.broadcast_to(h, (HALO, D_RNN))
    y_rnn = h_all * jax.nn.gelu(gr_ref[...])
    mix_ref[:, 0:D_RNN] = _rms(y_rnn, nr_ref[...]).astype(BF16)

    q = q_ref[...]
    kv = kv_ref[...]
    kvp = kvp_ref[...]
    col = lax.broadcasted_iota(jnp.int32, (BLOCK, 2 * BLOCK), 1)
    rowq = lax.broadcasted_iota(jnp.int32, (BLOCK, 2 * BLOCK), 0)
    d = BLOCK + rowq - col
    first_key = jnp.where(s == 0, BLOCK + N_PAD, jnp.where(s == 1, N_PAD, 0))
    mask = (d >= 0) & (d < WINDOW) & (col >= first_key)
    outs = []
    for kh in range(N_KV):
        ks = slice(kh * HEAD_DIM, (kh + 1) * HEAD_DIM)
        vs = slice(D_KV + kh * HEAD_DIM, D_KV + (kh + 1) * HEAD_DIM)
        qs = jnp.concatenate(
            [q[:, (kh * GROUP + g) * HEAD_DIM:(kh * GROUP + g + 1) * HEAD_DIM] for g in range(GROUP)],
            axis=0).astype(BF16)
        kk = jnp.concatenate([kvp[:, ks], kv[:, ks]], axis=0).astype(BF16)
        vv = jnp.concatenate([kvp[:, vs], kv[:, vs]], axis=0).astype(BF16)
        sc = lax.dot_general(qs, kk, (((1,), (1,)), ((), ())), preferred_element_type=F32)
        es, dens = [], []
        for g in range(GROUP):
            hh = kh * GROUP + g
            lg = sc[g * BLOCK:(g + 1) * BLOCK] * SCALE + bias_scr[hh]
            lg = jnp.where(mask, lg, NEG)
            sink = sink_ref[hh]
            mx = jnp.maximum(jnp.max(lg, axis=-1, keepdims=True), sink)
            e = jnp.exp(lg - mx)
            dens.append(jnp.sum(e, axis=-1, keepdims=True) + jnp.exp(sink - mx))
            es.append(e.astype(BF16))
        pv = jnp.dot(jnp.concatenate(es, axis=0), vv, preferred_element_type=F32)
        for g in range(GROUP):
            outs.append(pv[g * BLOCK:(g + 1) * BLOCK] / dens[g])
    y_attn = jnp.concatenate(outs, axis=1)
    mix_ref[:, D_RNN:] = _rms(y_attn, na_ref[...]).astype(BF16)


def _mixer(tab, sinks, z, bucket, cw, cb, wg, gab, gxb, lam, nr, na):
    def cur(s):
        return jnp.where(s == 0, PREFIX_BLK, s - 1)

    def prev(s):
        return jnp.where(s <= 1, PREFIX_BLK, s - 2)

    vec = lambda n: pl.BlockSpec((1, n), lambda s: (0, 0))
    smem = pl.BlockSpec(memory_space=pltpu.SMEM)
    return pl.pallas_call(
        _mixer_kernel,
        grid=(SAMPLE_BLK + 1,),
        in_specs=[
            smem, smem,
            pl.BlockSpec((BLOCK, D_RNN), lambda s: (cur(s), 0)),
            pl.BlockSpec((BLOCK, D_RNN), lambda s: (cur(s), 1)),
            pl.BlockSpec((BLOCK, D_ATTN), lambda s: (cur(s), 2)),
            pl.BlockSpec((BLOCK, 2 * D_KV), lambda s: (cur(s), (2 * D_RNN + D_ATTN) // (2 * D_KV))),
            pl.BlockSpec((BLOCK, 2 * D_KV), lambda s: (prev(s), (2 * D_RNN + D_ATTN) // (2 * D_KV))),
            pl.BlockSpec((BLOCK, 2 * BLOCK), lambda s: (0, 0)),
            pl.BlockSpec((RNN_CONV, D_RNN), lambda s: (0, 0)),
            vec(D_RNN),
            pl.BlockSpec((N_GW, GW, 2 * GW), lambda s: (0, 0, 0)),
            vec(D_RNN), vec(D_RNN), vec(D_RNN), vec(D_RNN), vec(D_ATTN),
        ],
        out_specs=[
            pl.BlockSpec((BLOCK, D_MODEL), lambda s: (cur(s), 0)),
            pl.BlockSpec((HALO, D_RNN), lambda s: (0, 0)),
        ],
        out_shape=[
            jax.ShapeDtypeStruct((R_ALL, D_MODEL), BF16),
            jax.ShapeDtypeStruct((HALO, D_RNN), F32),
        ],
        scratch_shapes=[
            pltpu.VMEM((HALO + BLOCK, D_RNN), F32),
            pltpu.VMEM((HALO, D_RNN), F32),
            pltpu.VMEM((N_HEADS, BLOCK, 2 * BLOCK), F32),
        ],
        compiler_params=pltpu.CompilerParams(
            dimension_semantics=("arbitrary",), vmem_limit_bytes=VMEM_LIMIT),
        name="mixer",
    )(tab, sinks, z, z, z, z, z, bucket, cw, cb, wg, gab, gxb, lam, nr, na)


def _smixer_kernel(tab_ref, sink_ref, xr_ref, gr_ref, q_ref, kv_ref, st_ref, h0_ref, kc_ref, vc_ref,
                   bucket_ref, cw_ref, cb_ref, wg_ref, gab_ref, gxb_ref, lam_ref, nr_ref, na_ref,
                   mix_ref, conv_ref, h_ref, ko_ref, vo_ref, bias_scr):
    c = pl.program_id(0)
    nrow = GROUP * SB
    ncol = SB * WINDOW

    @pl.when(c == 0)
    def _():
        bucket = bucket_ref[...]
        for h in range(N_HEADS):
            kh, g = divmod(h, GROUP)
            bias_scr[kh, g * SB:(g + 1) * SB] = _table_lookup(bucket, tab_ref, h)

    xr = xr_ref[...]
    st = st_ref[...]
    cw = cw_ref[...]
    xc = cb_ref[...]
    for j in range(RNN_CONV - 1):
        xc = xc + st[:, j * D_RNN:(j + 1) * D_RNN] * cw[j:j + 1]
    xc = xc + xr * cw[RNN_CONV - 1:RNN_CONV]
    conv_ref[:, 0:(RNN_CONV - 2) * D_RNN] = st[:, D_RNN:]
    conv_ref[:, (RNN_CONV - 2) * D_RNN:] = xr
    a, b = _gates(xc, wg_ref, gab_ref[...], gxb_ref[...], lam_ref[...])
    h = a * h0_ref[...] + b
    h_ref[...] = h
    y_rnn = h * jax.nn.gelu(gr_ref[...])
    mix_ref[:, 0:D_RNN] = _rms(y_rnn, nr_ref[...]).astype(BF16)

    q = q_ref[...]
    kv = kv_ref[...]
    col = lax.broadcasted_iota(jnp.int32, (nrow, ncol), 1)
    row = lax.broadcasted_iota(jnp.int32, (nrow, ncol), 0)
    valid = ((col >> 7) == (row & (SB - 1))) & ((col & (WINDOW - 1)) >= 1)
    outs = []
    for kh in range(N_KV):
        ks = slice(kh * HEAD_DIM, (kh + 1) * HEAD_DIM)
        vs = slice(D_KV + kh * HEAD_DIM, D_KV + (kh + 1) * HEAD_DIM)
        qs = jnp.concatenate(
            [q[:, (kh * GROUP + g) * HEAD_DIM:(kh * GROUP + g + 1) * HEAD_DIM] for g in range(GROUP)],
            axis=0).astype(BF16)
        kf = kc_ref[:, :, ks].reshape(ncol, HEAD_DIM).astype(BF16)
        vf = vc_ref[:, :, ks].reshape(ncol, HEAD_DIM).astype(BF16)
        k_new = jnp.concatenate([kv[:, ks]] * GROUP, axis=0).astype(BF16).astype(F32)
        v_new = jnp.concatenate([kv[:, vs]] * GROUP, axis=0).astype(BF16).astype(F32)
        sc = lax.dot_general(qs, kf, (((1,), (1,)), ((), ())), preferred_element_type=F32)
        lg = jnp.where(valid, sc * SCALE + bias_scr[kh], NEG)
        rg = lax.broadcasted_iota(jnp.int32, (nrow, 1), 0) >> 4
        sink = jnp.zeros((nrow, 1), F32)
        bias_new = jnp.zeros((nrow, 1), F32)
        for g in range(GROUP):
            sink = jnp.where(rg == g, sink_ref[kh * GROUP + g], sink)
            bias_new = jnp.where(rg == g, tab_ref[0, kh * GROUP + g], bias_new)
        lg_new = jnp.sum(qs.astype(F32) * k_new, axis=-1, keepdims=True) * SCALE + bias_new
        mx = jnp.maximum(jnp.maximum(jnp.max(lg, axis=-1, keepdims=True), lg_new), sink)
        e = jnp.exp(lg - mx)
        e_new = jnp.exp(lg_new - mx)
        den = jnp.sum(e, axis=-1, keepdims=True) + e_new + jnp.exp(sink - mx)
        pv = jnp.dot(e.astype(BF16), vf, preferred_element_type=F32)
        pv = pv + e_new.astype(BF16).astype(F32) * v_new
        o = pv / den
        for g in range(GROUP):
            outs.append(o[g * SB:(g + 1) * SB])
    y_attn = jnp.concatenate(outs, axis=1)
    mix_ref[:, D_RNN:] = _rms(y_attn, na_ref[...]).astype(BF16)

    ko_ref[...] = pltpu.roll(kc_ref[...], WINDOW - 1, 1)
    vo_ref[...] = pltpu.roll(vc_ref[...], WINDOW - 1, 1)
    for bi in range(SB):
        ko_ref[bi, WINDOW - 1:WINDOW, :] = kv[bi:bi + 1, 0:D_KV]
        vo_ref[bi, WINDOW - 1:WINDOW, :] = kv[bi:bi + 1, D_KV:]


def _smixer(tab, sinks, z, st, h0, kc, vc, bucket, cw, cb, wg, gab, gxb, lam, nr, na):
    base = SEQ // SB
    vec = lambda n: pl.BlockSpec((1, n), lambda c: (0, 0))
    smem = pl.BlockSpec(memory_space=pltpu.SMEM)
    cache = pl.BlockSpec((SB, WINDOW, D_KV), lambda c: (c, 0, 0))
    return pl.pallas_call(
        _smixer_kernel,
        grid=(DEC_BATCH // SB,),
        in_specs=[
            smem, smem,
            pl.BlockSpec((SB, D_RNN), lambda c: (base + c, 0)),
            pl.BlockSpec((SB, D_RNN), lambda c: (base + c, 1)),
            pl.BlockSpec((SB, D_ATTN), lambda c: (base + c, 2)),
            pl.BlockSpec((SB, 2 * D_KV), lambda c: (base + c, (2 * D_RNN + D_ATTN) // (2 * D_KV))),
            pl.BlockSpec((SB, (RNN_CONV - 1) * D_RNN), lambda c: (c, 0)),
            pl.BlockSpec((SB, D_RNN), lambda c: (c, 0)),
            cache, cache,
            pl.BlockSpec((SB, SB * WINDOW), lambda c: (0, 0)),
            pl.BlockSpec((RNN_CONV, D_RNN), lambda c: (0, 0)),
            vec(D_RNN),
            pl.BlockSpec((N_GW, GW, 2 * GW), lambda c: (0, 0, 0)),
            vec(D_RNN), vec(D_RNN), vec(D_RNN), vec(D_RNN), vec(D_ATTN),
        ],
        out_specs=[
            pl.BlockSpec((SB, D_MODEL), lambda c: (c, 0)),
            pl.BlockSpec((SB, (RNN_CONV - 1) * D_RNN), lambda c: (c, 0)),
            pl.BlockSpec((SB, D_RNN), lambda c: (c, 0)),
            cache, cache,
        ],
        out_shape=[
            jax.ShapeDtypeStruct((DEC_BATCH, D_MODEL), BF16),
            jax.ShapeDtypeStruct((DEC_BATCH, (RNN_CONV - 1) * D_RNN), F32),
            jax.ShapeDtypeStruct((DEC_BATCH, D_RNN), F32),
            jax.ShapeDtypeStruct((DEC_BATCH, WINDOW, D_KV), F32),
            jax.ShapeDtypeStruct((DEC_BATCH, WINDOW, D_KV), F32),
        ],
        scratch_shapes=[pltpu.VMEM((N_KV, GROUP * SB, SB * WINDOW), F32)],
        compiler_params=pltpu.CompilerParams(
            dimension_semantics=("arbitrary",), vmem_limit_bytes=VMEM_LIMIT),
        name="smixer",
    )(tab, sinks, z, z, z, z, st, h0, kc, vc, bucket, cw, cb, wg, gab, gxb, lam, nr, na)


def _outproj_kernel(xp_ref, xt_ref, mix_ref, mixs_ref, w_ref, x1_ref, lhs_scr):
    i = pl.program_id(0)

    @pl.when(i < N_RT - 1)
    def _():
        x1_ref[...] = xp_ref[...] + jnp.dot(mix_ref[...], w_ref[...], preferred_element_type=F32)

    @pl.when(i == N_RT - 1)
    def _():
        s0, s1 = LAST_P, LAST_P + DEC_BATCH
        lhs_scr[0:s0] = mix_ref[0:s0]
        lhs_scr[s0:s1] = mixs_ref[...]
        lhs_scr[s1:TM] = mix_ref[s1:TM]
        y = jnp.dot(lhs_scr[...], w_ref[...], preferred_element_type=F32)
        x1_ref[0:s0] = xp_ref[0:s0] + y[0:s0]
        x1_ref[s0:TM] = xt_ref[...] + y[s0:TM]


def _outproj(xp, xt, mix, mixs, w):
    return pl.pallas_call(
        _outproj_kernel,
        grid=(N_RT,),
        in_specs=[
            pl.BlockSpec((TM, D_MODEL), lambda i: (i, 0)),
            pl.BlockSpec((R_TAIL, D_MODEL), lambda i: (0, 0)),
            pl.BlockSpec((TM, D_MODEL), lambda i: (i, 0)),
            pl.BlockSpec((DEC_BATCH, D_MODEL), lambda i: (0, 0)),
            pl.BlockSpec((D_MODEL, D_MODEL), lambda i: (0, 0)),
        ],
        out_specs=pl.BlockSpec((TM, D_MODEL), lambda i: (i, 0)),
        out_shape=jax.ShapeDtypeStruct((R_ALL, D_MODEL), F32),
        scratch_shapes=[pltpu.VMEM((TM, D_MODEL), BF16)],
        compiler_params=pltpu.CompilerParams(
            dimension_semantics=("arbitrary",), vmem_limit_bytes=VMEM_LIMIT),
        name="outproj",
    )(xp, xt, mix, mixs, w)


U_LO = LAST_P - HALO
U_HI = LAST_P + DEC_BATCH
N_UST = U_HI - U_LO


def _ffn_kernel(x1_ref, halo_ref, st0_ref, st1_ref, gn_ref, gf_ref, wu_ref, wg_ref, cw_ref, cb_ref, wd_ref,
                y_ref, ys_ref, ust_ref, h2_scr, hh_scr, ubuf, abuf):
    i = pl.program_id(0)
    j = pl.program_id(1)

    @pl.when(j == 0)
    def _():
        h2_scr[...] = _rms(x1_ref[...], gn_ref[...]).astype(BF16)
        hh_scr[...] = _rms(halo_ref[...], gn_ref[...]).astype(BF16)
        y_ref[...] = jnp.zeros((TM, D_MODEL), F32)

    wu = wu_ref[...]
    u = jnp.dot(h2_scr[...], wu, preferred_element_type=F32)
    ubuf[0:HALO] = jnp.dot(hh_scr[...], wu, preferred_element_type=F32)
    ubuf[HALO:HALO + TM] = u
    gate = jnp.dot(h2_scr[...], wg_ref[...], preferred_element_type=F32)
    cw = cw_ref[...]
    cb = cb_ref[...]
    uc = cb + ubuf[HALO - 2:HALO - 2 + TM] * cw[0:1] + ubuf[HALO - 1:HALO - 1 + TM] * cw[1:2] + u * cw[2:3]
    abuf[...] = (jax.nn.gelu(uc) * gate).astype(BF16)
    ust_ref[...] = u[U_LO:U_HI]

    @pl.when(i == N_RT - 1)
    def _():
        us = u[LAST_P:U_HI]
        ucs = cb + st0_ref[...] * cw[0:1] + st1_ref[...] * cw[1:2] + us * cw[2:3]
        abuf[LAST_P:U_HI] = (jax.nn.gelu(ucs) * gate[LAST_P:U_HI]).astype(BF16)

    y_ref[...] += jnp.dot(abuf[...], wd_ref[...], preferred_element_type=F32)

    @pl.when(j == N_FT - 1)
    def _():
        y = _rms(x1_ref[...] + y_ref[...], gf_ref[...])
        y_ref[...] = y

        @pl.when(i == N_RT - 1)
        def _():
            ys_ref[...] = y[LAST_P:U_HI]


def _ffn(x1, st, gn, gf, wu, wg, cw, cb, wd):
    def halo_idx(i, j):
        return (jnp.where(i == 0, R_ALL // HALO - 1, i * (TM // HALO) - 1), 0)

    return pl.pallas_call(
        _ffn_kernel,
        grid=(N_RT, N_FT),
        in_specs=[
            pl.BlockSpec((TM, D_MODEL), lambda i, j: (i, 0)),
            pl.BlockSpec((HALO, D_MODEL), halo_idx),
            pl.BlockSpec((DEC_BATCH, TF), lambda i, j: (0, j)),
            pl.BlockSpec((DEC_BATCH, TF), lambda i, j: (0, N_FT + j)),
            pl.BlockSpec((1, D_MODEL), lambda i, j: (0, 0)),
            pl.BlockSpec((1, D_MODEL), lambda i, j: (0, 0)),
            pl.BlockSpec((D_MODEL, TF), lambda i, j: (0, j)),
            pl.BlockSpec((D_MODEL, TF), lambda i, j: (0, j)),
            pl.BlockSpec((FFN_CONV, TF), lambda i, j: (0, j)),
            pl.BlockSpec((1, TF), lambda i, j: (0, j)),
            pl.BlockSpec((TF, D_MODEL), lambda i, j: (j, 0)),
        ],
        out_specs=[
            pl.BlockSpec((TM, D_MODEL), lambda i, j: (i, 0)),
            pl.BlockSpec((DEC_BATCH, D_MODEL), lambda i, j: (0, 0)),
            pl.BlockSpec((N_UST, TF), lambda i, j: (i, j)),
        ],
        out_shape=[
            jax.ShapeDtypeStruct((SEQ, D_MODEL), F32),
            jax.ShapeDtypeStruct((DEC_BATCH, D_MODEL), F32),
            jax.ShapeDtypeStruct((N_RT * N_UST, D_FF), F32),
        ],
        scratch_shapes=[
            pltpu.VMEM((TM, D_MODEL), BF16),
            pltpu.VMEM((HALO, D_MODEL), BF16),
            pltpu.VMEM((HALO + TM, TF), F32),
            pltpu.VMEM((TM, TF), BF16),
        ],
        compiler_params=pltpu.CompilerParams(
            dimension_semantics=("arbitrary", "arbitrary"), vmem_limit_bytes=VMEM_LIMIT),
        name="ffn",
    )(x1, x1, st, st, gn, gf, wu, wg, cw, cb, wd)


def _gate_weights(wa, wx):
    per = GW // RNN_BLOCK
    eye = jnp.eye(per, dtype=wa.dtype)

    def bd(w):
        w = w.reshape(N_GW, per, RNN_BLOCK, RNN_BLOCK)
        return jnp.einsum('gpcd,pq->gpcqd', w, eye).reshape(N_GW, GW, GW)

    return jnp.concatenate([bd(wa), bd(wx)], axis=-1).astype(BF16)


def kernel(x_prompt, x_sample, state_rnn_conv, state_rnn_h, cache_k_win, cache_v_win, state_ffn_conv,
           meta_tokens, rel_bias_table, norm_mix, w_in, rnn_conv_w, rnn_conv_b, gate_a_w, gate_a_b,
           gate_x_w, gate_x_b, rnn_lambda, attn_sinks, norm_rnn_out, norm_attn_out, w_out, norm_ffn,
           w_up, w_gate, ffn_conv_w, ffn_conv_b, w_down, norm_final):
    l = 0
    xp = x_prompt[0]
    xt = jnp.concatenate([x_sample[:, 0, :], jnp.zeros((N_PAD, D_MODEL), F32), meta_tokens], axis=0)
    row = lambda v: v.reshape(1, -1)

    qi = np.arange(BLOCK)[:, None]
    sj = np.arange(2 * BLOCK)[None, :]
    bucket_p = jnp.asarray(_rel_buckets(BLOCK + qi - sj))
    bucket_s = jnp.asarray(np.tile(_rel_buckets(WINDOW - np.arange(WINDOW))[None, :], (SB, SB)))

    wg = _gate_weights(gate_a_w[l], gate_x_w[l])
    seq_w = (rnn_conv_w[l], row(rnn_conv_b[l]), wg, row(gate_a_b[l]), row(gate_x_b[l]), row(rnn_lambda[l]),
             row(norm_rnn_out[l]), row(norm_attn_out[l]))

    z = _inproj(xp, xt, row(norm_mix[l]), w_in[l].astype(BF16))
    mix, h_last = _mixer(rel_bias_table, attn_sinks[l], z, bucket_p, *seq_w)
    mix_s, conv_s, h_s, k_s, v_s = _smixer(
        rel_bias_table, attn_sinks[l], z,
        state_rnn_conv[l].reshape(DEC_BATCH, (RNN_CONV - 1) * D_RNN), state_rnn_h[l],
        cache_k_win[l].reshape(DEC_BATCH, WINDOW, D_KV), cache_v_win[l].reshape(DEC_BATCH, WINDOW, D_KV),
        bucket_s, *seq_w)
    x1 = _outproj(xp, xt, mix, mix_s, w_out[l].astype(BF16))
    y_p, y_s, ust = _ffn(x1, state_ffn_conv[l].reshape(DEC_BATCH, (FFN_CONV - 1) * D_FF),
                         row(norm_ffn[l]), row(norm_final), w_up[l].astype(BF16), w_gate[l].astype(BF16),
                         ffn_conv_w[l], row(ffn_conv_b[l]), w_down[l].astype(BF16))

    ust = ust[(N_RT - 1) * N_UST:]
    k_off = 2 * D_RNN + D_ATTN
    kv_p = z[SEQ - WINDOW:SEQ, k_off:]
    p_states = (
        z[SEQ - (RNN_CONV - 1):SEQ, 0:D_RNN][None, None],
        h_last[0:1][None],
        kv_p[:, :D_KV].reshape(1, 1, WINDOW, N_KV, HEAD_DIM),
        kv_p[:, D_KV:].reshape(1, 1, WINDOW, N_KV, HEAD_DIM),
        ust[HALO - (FFN_CONV - 1):HALO][None, None],
    )
    s_states = (
        conv_s.reshape(1, DEC_BATCH, RNN_CONV - 1, D_RNN),
        h_s[None],
        k_s.reshape(1, DEC_BATCH, WINDOW, N_KV, HEAD_DIM),
        v_s.reshape(1, DEC_BATCH, WINDOW, N_KV, HEAD_DIM),
        jnp.stack([state_ffn_conv[l][:, FFN_CONV - 2, :], ust[HALO:]], axis=1)[None],
    )
    return (y_p[None], y_s[:, None, :]) + p_states + s_states
```

```python
import functools
import math

import numpy as np
import jax
import jax.numpy as jnp
from jax import lax
from jax.experimental import pallas as pl
from jax.experimental.pallas import tpu as pltpu

F32 = jnp.float32
BF16 = jnp.bfloat16

D_MODEL = 2048
SEQ = 8192
DEC_BATCH = 128
D_RNN = 1024
N_RNN_BLOCKS = 16
RNN_BLOCK = D_RNN // N_RNN_BLOCKS
RNN_CONV = 4
LRU_C = 8.0
N_HEADS = 8
HEAD_DIM = 128
N_KV = 2
GROUP = N_HEADS // N_KV
D_ATTN = N_HEADS * HEAD_DIM
WINDOW = 128
BLOCK = 128
NUM_BUCKETS = 32
MAX_DISTANCE = 128
D_FF = 3 * D_MODEL
FFN_CONV = 3
N_META = 16
EPS = 1e-6
NEG = -1e30
D_KV = N_KV * HEAD_DIM
D_IN = 2 * D_RNN + D_ATTN + 2 * D_KV
SCALE = HEAD_DIM ** -0.5

N_PAD = BLOCK - N_META
R_TAIL = DEC_BATCH + BLOCK
R_ALL = SEQ + R_TAIL
SAMPLE_BLK = SEQ // BLOCK
PREFIX_BLK = SAMPLE_BLK + 1
N_BLK = R_ALL // BLOCK

TM = 768
N_RT = R_ALL // TM
LAST_P = SEQ - (N_RT - 1) * TM
TN_IN = 512
TM_IN = 1408
N_RT_IN = R_ALL // TM_IN
TF = 512
N_FT = D_FF // TF
HALO = 8
FH = 16
GW = 256
N_GW = D_RNN // GW
SB = 16
VMEM_LIMIT = 56 * 1024 * 1024


def _rms(x, g):
    return x * lax.rsqrt(jnp.mean(x * x, axis=-1, keepdims=True) + EPS) * g


def _rel_buckets(d):
    d = np.maximum(d, 0)
    exact = NUM_BUCKETS // 2
    ratio = np.maximum(d, 1).astype(np.float32) / np.float32(exact)
    large = exact + (np.log(ratio) / np.float32(math.log(MAX_DISTANCE / exact))
                     * np.float32(NUM_BUCKETS - exact)).astype(np.int32)
    large = np.minimum(large, NUM_BUCKETS - 1)
    return np.where(d < exact, d, large).astype(np.int32)


def _table_lookup(bucket, tab_ref, h):
    out = jnp.zeros(bucket.shape, F32)
    for b in range(NUM_BUCKETS):
        out = jnp.where(bucket == b, tab_ref[b, h], out)
    return out


def _gates(xc, wg_ref, gab, gxb, lam):
    xcb = xc.astype(BF16)
    ga, gx = [], []
    for j in range(N_GW):
        gj = jnp.dot(xcb[:, GW * j:GW * (j + 1)], wg_ref[j], preferred_element_type=F32)
        ga.append(gj[:, :GW])
        gx.append(gj[:, GW:])
    r = jax.nn.sigmoid(jnp.concatenate(ga, axis=1) + gab)
    i = jax.nn.sigmoid(jnp.concatenate(gx, axis=1) + gxb)
    log_a = -LRU_C * r * jax.nn.softplus(-lam)
    a = jnp.exp(log_a)
    b = jnp.sqrt(1.0 - a * a) * i * xc
    return a, b


def _x_tile(i, n_tiles, xp_ref, xt_ref, fn, dst):
    tm = xp_ref.shape[0]
    last_p = tm - R_TAIL

    @pl.when(i < n_tiles - 1)
    def _():
        dst[...] = fn(xp_ref[...])

    @pl.when(i == n_tiles - 1)
    def _():
        dst[0:last_p] = fn(xp_ref[0:last_p])
        dst[last_p:tm] = fn(xt_ref[...])


def _inproj_kernel(xp_ref, xt_ref, g_ref, w_ref, z_ref, h_scr):
    i = pl.program_id(0)

    @pl.when(pl.program_id(1) == 0)
    def _():
        _x_tile(i, N_RT_IN, xp_ref, xt_ref, lambda x: _rms(x, g_ref[...]).astype(BF16), h_scr)

    z_ref[...] = jnp.dot(h_scr[...], w_ref[...].astype(BF16), preferred_element_type=F32)


def _inproj(xp, xt, g, w):
    return pl.pallas_call(
        _inproj_kernel,
        grid=(N_RT_IN, D_IN // TN_IN),
        in_specs=[
            pl.BlockSpec((TM_IN, D_MODEL), lambda i, j: (i, 0)),
            pl.BlockSpec((R_TAIL, D_MODEL), lambda i, j: (0, 0)),
            pl.BlockSpec((1, D_MODEL), lambda i, j: (0, 0)),
            pl.BlockSpec((D_MODEL, TN_IN), lambda i, j: (0, j)),
        ],
        out_specs=pl.BlockSpec((TM_IN, TN_IN), lambda i, j: (i, j)),
        out_shape=jax.ShapeDtypeStruct((R_ALL, D_IN), F32),
        scratch_shapes=[pltpu.VMEM((TM_IN, D_MODEL), BF16)],
        compiler_params=pltpu.CompilerParams(
            dimension_semantics=("arbitrary", "arbitrary"), vmem_limit_bytes=VMEM_LIMIT),
        name="inproj",
    )(xp, xt, g, w)


def _mixer_kernel(tab_ref, sink_ref, xr_ref, gr_ref, q_ref, kv_ref, kvp_ref, bucket_ref,
                  cw_ref, cb_ref, wg_ref, gab_ref, gxb_ref, lam_ref, nr_ref, na_ref,
                  mix_ref, hlast_ref, xbuf, h_scr, bias_scr):
    s = pl.program_id(0)

    @pl.when(s == 0)
    def _():
        xbuf[0:HALO] = jnp.zeros((HALO, D_RNN), F32)
        h_scr[...] = jnp.zeros((HALO, D_RNN), F32)
        bucket = bucket_ref[...]
        for h in range(N_HEADS):
            bias_scr[h] = _table_lookup(bucket, tab_ref, h)

    xr = xr_ref[...]
    xbuf[HALO:HALO + BLOCK] = xr
    cw = cw_ref[...]
    xc = cb_ref[...]
    for j in range(RNN_CONV - 1):
        lo = HALO - (RNN_CONV - 1) + j
        xc = xc + xbuf[lo:lo + BLOCK] * cw[j:j + 1]
    xc = xc + xr * cw[RNN_CONV - 1:RNN_CONV]
    xbuf[0:HALO] = xr[BLOCK - HALO:BLOCK]

    a, b = _gates(xc, wg_ref, gab_ref[...], gxb_ref[...], lam_ref[...])
    row = lax.broadcasted_iota(jnp.int32, (BLOCK, D_RNN), 0)
    b = jnp.where(jnp.logical_and(s == 0, row < N_PAD), 0.0, b)

    ng = BLOCK // HALO
    a3 = a.reshape(ng, HALO, D_RNN)
    b3 = b.reshape(ng, HALO, D_RNN)
    sub = lax.broadcasted_iota(jnp.int32, (ng, HALO, D_RNN), 1)
    sh = 1
    while sh < HALO:
        a_prev = pltpu.roll(a3, sh, 1)
        b_prev = pltpu.roll(b3, sh, 1)
        m = sub >= sh
        b3 = jnp.where(m, a3 * b_prev + b3, b3)
        a3 = jnp.where(m, a3 * a_prev, a3)
        sh *= 2
    h = h_scr[0:1]
    hs = []
    for g in range(ng):
        hg = a3[g] * h + b3[g]
        hs.append(hg)
        h = hg[HALO - 1:HALO]
    h_all = jnp.concatenate(hs, axis=0)
    h_scr[...] = jnp.broadcast_to(h, (HALO, D_RNN))
    hlast_ref[...] = jnp.broadcast_to(h, (HALO, D_RNN))
    y_rnn = h_all * jax.nn.gelu(gr_ref[...])
    mix_ref[:, 0:D_RNN] = _rms(y_rnn, nr_ref[...]).astype(BF16)

    q = q_ref[...]
    kv = kv_ref[...]
    kvp = kvp_ref[...]
    col = lax.broadcasted_iota(jnp.int32, (BLOCK, 2 * BLOCK), 1)
    rowq = lax.broadcasted_iota(jnp.int32, (BLOCK, 2 * BLOCK), 0)
    d = BLOCK + rowq - col
    first_key = jnp.where(s == 0, BLOCK + N_PAD, jnp.where(s == 1, N_PAD, 0))
    mask = (d >= 0) & (d < WINDOW) & (col >= first_key)
    outs = []
    for kh in range(N_KV):
        ks = slice(kh * HEAD_DIM, (kh + 1) * HEAD_DIM)
        vs = slice(D_KV + kh * HEAD_DIM, D_KV + (kh + 1) * HEAD_DIM)
        qs = jnp.concatenate(
            [q[:, (kh * GROUP + g) * HEAD_DIM:(kh * GROUP + g + 1) * HEAD_DIM] for g in range(GROUP)],
            axis=0).astype(BF16)
        kk = jnp.concatenate([kvp[:, ks], kv[:, ks]], axis=0).astype(BF16)
        vv = jnp.concatenate([kvp[:, vs], kv[:, vs]], axis=0).astype(BF16)
        sc = lax.dot_general(qs, kk, (((1,), (1,)), ((), ())), preferred_element_type=F32)
        es, dens = [], []
        for g in range(GROUP):
            hh = kh * GROUP + g
            lg = sc[g * BLOCK:(g + 1) * BLOCK] * SCALE + bias_scr[hh]
            lg = jnp.where(mask, lg, NEG)
            sink = sink_ref[hh]
            mx = jnp.maximum(jnp.max(lg, axis=-1, keepdims=True), sink)
            e = jnp.exp(lg - mx)
            dens.append(jnp.sum(e, axis=-1, keepdims=True) + jnp.exp(sink - mx))
            es.append(e.astype(BF16))
        pv = jnp.dot(jnp.concatenate(es, axis=0), vv, preferred_element_type=F32)
        for g in range(GROUP):
            outs.append(pv[g * BLOCK:(g + 1) * BLOCK] / dens[g])
    y_attn = jnp.concatenate(outs, axis=1)
    mix_ref[:, D_RNN:] = _rms(y_attn, na_ref[...]).astype(BF16)


def _mixer(tab, sinks, z, bucket, cw, cb, wg, gab, gxb, lam, nr, na):
    def cur(s):
        return jnp.where(s == 0, PREFIX_BLK, s - 1)

    def prev(s):
        return jnp.where(s <= 1, PREFIX_BLK, s - 2)

    vec = lambda n: pl.BlockSpec((1, n), lambda s: (0, 0))
    smem = pl.BlockSpec(memory_space=pltpu.SMEM)
    return pl.pallas_call(
        _mixer_kernel,
        grid=(SAMPLE_BLK + 1,),
        in_specs=[
            smem, smem,
            pl.BlockSpec((BLOCK, D_RNN), lambda s: (cur(s), 0)),
            pl.BlockSpec((BLOCK, D_RNN), lambda s: (cur(s), 1)),
            pl.BlockSpec((BLOCK, D_ATTN), lambda s: (cur(s), 2)),
            pl.BlockSpec((BLOCK, 2 * D_KV), lambda s: (cur(s), (2 * D_RNN + D_ATTN) // (2 * D_KV))),
            pl.BlockSpec((BLOCK, 2 * D_KV), lambda s: (prev(s), (2 * D_RNN + D_ATTN) // (2 * D_KV))),
            pl.BlockSpec((BLOCK, 2 * BLOCK), lambda s: (0, 0)),
            pl.BlockSpec((RNN_CONV, D_RNN), lambda s: (0, 0)),
            vec(D_RNN),
            pl.BlockSpec((N_GW, GW, 2 * GW), lambda s: (0, 0, 0)),
            vec(D_RNN), vec(D_RNN), vec(D_RNN), vec(D_RNN), vec(D_ATTN),
        ],
        out_specs=[
            pl.BlockSpec((BLOCK, D_MODEL), lambda s: (cur(s), 0)),
            pl.BlockSpec((HALO, D_RNN), lambda s: (0, 0)),
        ],
        out_shape=[
            jax.ShapeDtypeStruct((R_ALL, D_MODEL), BF16),
            jax.ShapeDtypeStruct((HALO, D_RNN), F32),
        ],
        scratch_shapes=[
            pltpu.VMEM((HALO + BLOCK, D_RNN), F32),
            pltpu.VMEM((HALO, D_RNN), F32),
            pltpu.VMEM((N_HEADS, BLOCK, 2 * BLOCK), F32),
        ],
        compiler_params=pltpu.CompilerParams(
            dimension_semantics=("arbitrary",), vmem_limit_bytes=VMEM_LIMIT),
        name="mixer",
    )(tab, sinks, z, z, z, z, z, bucket, cw, cb, wg, gab, gxb, lam, nr, na)


def _smixer_kernel(tab_ref, sink_ref, xr_ref, gr_ref, q_ref, kv_ref, st_ref, h0_ref, kc_ref, vc_ref,
                   bucket_ref, cw_ref, cb_ref, wg_ref, gab_ref, gxb_ref, lam_ref, nr_ref, na_ref,
                   mix_ref, conv_ref, h_ref, ko_ref, vo_ref, bias_scr):
    c = pl.program_id(0)
    nrow = N_HEADS * SB
    ncol = SB * WINDOW * N_KV

    @pl.when(c == 0)
    def _():
        bucket = bucket_ref[...]
        for h in range(N_HEADS):
            bias_scr[h * SB:(h + 1) * SB] = _table_lookup(bucket, tab_ref, h)

    xr = xr_ref[...]
    st = st_ref[...]
    cw = cw_ref[...]
    xc = cb_ref[...]
    for j in range(RNN_CONV - 1):
        xc = xc + st[:, j * D_RNN:(j + 1) * D_RNN] * cw[j:j + 1]
    xc = xc + xr * cw[RNN_CONV - 1:RNN_CONV]
    conv_ref[:, 0:(RNN_CONV - 2) * D_RNN] = st[:, D_RNN:]
    conv_ref[:, (RNN_CONV - 2) * D_RNN:] = xr
    a, b = _gates(xc, wg_ref, gab_ref[...], gxb_ref[...], lam_ref[...])
    h = a * h0_ref[...] + b
    h_ref[...] = h
    y_rnn = h * jax.nn.gelu(gr_ref[...])
    mix_ref[:, 0:D_RNN] = _rms(y_rnn, nr_ref[...]).astype(BF16)

    q = q_ref[...]
    kv = kv_ref[...]
    col = lax.broadcasted_iota(jnp.int32, (nrow, ncol), 1)
    row = lax.broadcasted_iota(jnp.int32, (nrow, ncol), 0)
    pos = (col >> 1) & (WINDOW - 1)
    valid = ((col >> 8) == (row & (SB - 1))) & ((col & 1) == (row >> 6)) & (pos >= 1)
    qs = jnp.concatenate([q[:, h * HEAD_DIM:(h + 1) * HEAD_DIM] for h in range(N_HEADS)], axis=0).astype(BF16)
    new_rows = lambda off: jnp.concatenate(
        [kv[:, off + (h // GROUP) * HEAD_DIM:off + (h // GROUP + 1) * HEAD_DIM] for h in range(N_HEADS)],
        axis=0).astype(BF16).astype(F32)
    k_new = new_rows(0)
    v_new = new_rows(D_KV)
    sc = lax.dot_general(qs, kc_ref[...].astype(BF16), (((1,), (1,)), ((), ())), preferred_element_type=F32)
    lg = jnp.where(valid, sc * SCALE + bias_scr[...], NEG)
    rh = lax.broadcasted_iota(jnp.int32, (nrow, 1), 0) >> 4
    sink = jnp.zeros((nrow, 1), F32)
    bias_new = jnp.zeros((nrow, 1), F32)
    for h in range(N_HEADS):
        sink = jnp.where(rh == h, sink_ref[h], sink)
        bias_new = jnp.where(rh == h, tab_ref[0, h], bias_new)
    lg_new = jnp.sum(qs.astype(F32) * k_new, axis=-1, keepdims=True) * SCALE + bias_new
    mx = jnp.maximum(jnp.maximum(jnp.max(lg, axis=-1, keepdims=True), lg_new), sink)
    e = jnp.exp(lg - mx)
    e_new = jnp.exp(lg_new - mx)
    den = jnp.sum(e, axis=-1, keepdims=True) + e_new + jnp.exp(sink - mx)
    pv = jnp.dot(e.astype(BF16), vc_ref[...].astype(BF16), preferred_element_type=F32)
    o = (pv + e_new.astype(BF16).astype(F32) * v_new) / den
    y_attn = jnp.concatenate([o[h * SB:(h + 1) * SB] for h in range(N_HEADS)], axis=1)
    mix_ref[:, D_RNN:] = _rms(y_attn, na_ref[...]).astype(BF16)

    per = WINDOW * N_KV
    slide = lambda ref: pltpu.roll(ref[...].reshape(SB, per, HEAD_DIM), per - N_KV, 1).reshape(SB * per, HEAD_DIM)
    ko_ref[...] = slide(kc_ref)
    vo_ref[...] = slide(vc_ref)
    for bi in range(SB):
        for kh in range(N_KV):
            r = (bi + 1) * per - N_KV + kh
            ko_ref[r:r + 1, :] = kv[bi:bi + 1, kh * HEAD_DIM:(kh + 1) * HEAD_DIM]
            vo_ref[r:r + 1, :] = kv[bi:bi + 1, D_KV + kh * HEAD_DIM:D_KV + (kh + 1) * HEAD_DIM]


def _smixer(tab, sinks, z, st, h0, kc, vc, bucket, cw, cb, wg, gab, gxb, lam, nr, na):
    base = SEQ // SB
    vec = lambda n: pl.BlockSpec((1, n), lambda c: (0, 0))
    smem = pl.BlockSpec(memory_space=pltpu.SMEM)
    cache = pl.BlockSpec((SB * WINDOW * N_KV, HEAD_DIM), lambda c: (c, 0))
    return pl.pallas_call(
        _smixer_kernel,
        grid=(DEC_BATCH // SB,),
        in_specs=[
            smem, smem,
            pl.BlockSpec((SB, D_RNN), lambda c: (base + c, 0)),
            pl.BlockSpec((SB, D_RNN), lambda c: (base + c, 1)),
            pl.BlockSpec((SB, D_ATTN), lambda c: (base + c, 2)),
            pl.BlockSpec((SB, 2 * D_KV), lambda c: (base + c, (2 * D_RNN + D_ATTN) // (2 * D_KV))),
            pl.BlockSpec((SB, (RNN_CONV - 1) * D_RNN), lambda c: (c, 0)),
            pl.BlockSpec((SB, D_RNN), lambda c: (c, 0)),
            cache, cache,
            pl.BlockSpec((SB, SB * WINDOW * N_KV), lambda c: (0, 0)),
            pl.BlockSpec((RNN_CONV, D_RNN), lambda c: (0, 0)),
            vec(D_RNN),
            pl.BlockSpec((N_GW, GW, 2 * GW), lambda c: (0, 0, 0)),
            vec(D_RNN), vec(D_RNN), vec(D_RNN), vec(D_RNN), vec(D_ATTN),
        ],
        out_specs=[
            pl.BlockSpec((SB, D_MODEL), lambda c: (c, 0)),
            pl.BlockSpec((SB, (RNN_CONV - 1) * D_RNN), lambda c: (c, 0)),
            pl.BlockSpec((SB, D_RNN), lambda c: (c, 0)),
            cache, cache,
        ],
        out_shape=[
            jax.ShapeDtypeStruct((DEC_BATCH, D_MODEL), BF16),
            jax.ShapeDtypeStruct((DEC_BATCH, (RNN_CONV - 1) * D_RNN), F32),
            jax.ShapeDtypeStruct((DEC_BATCH, D_RNN), F32),
            jax.ShapeDtypeStruct((DEC_BATCH * WINDOW * N_KV, HEAD_DIM), F32),
            jax.ShapeDtypeStruct((DEC_BATCH * WINDOW * N_KV, HEAD_DIM), F32),
        ],
        scratch_shapes=[pltpu.VMEM((N_HEADS * SB, SB * WINDOW * N_KV), F32)],
        compiler_params=pltpu.CompilerParams(
            dimension_semantics=("arbitrary",), vmem_limit_bytes=VMEM_LIMIT),
        name="smixer",
    )(tab, sinks, z, z, z, z, st, h0, kc, vc, bucket, cw, cb, wg, gab, gxb, lam, nr, na)


def _outproj_kernel(xp_ref, xt_ref, mix_ref, mixs_ref, w_ref, x1_ref, lhs_scr):
    i = pl.program_id(0)

    @pl.when(i < N_RT - 1)
    def _():
        x1_ref[...] = xp_ref[...] + jnp.dot(mix_ref[...], w_ref[...], preferred_element_type=F32)

    @pl.when(i == N_RT - 1)
    def _():
        s0, s1 = LAST_P, LAST_P + DEC_BATCH
        lhs_scr[0:s0] = mix_ref[0:s0]
        lhs_scr[s0:s1] = mixs_ref[...]
        lhs_scr[s1:TM] = mix_ref[s1:TM]
        y = jnp.dot(lhs_scr[...], w_ref[...], preferred_element_type=F32)
        x1_ref[0:s0] = xp_ref[0:s0] + y[0:s0]
        x1_ref[s0:TM] = xt_ref[...] + y[s0:TM]


def _outproj(xp, xt, mix, mixs, w):
    return pl.pallas_call(
        _outproj_kernel,
        grid=(N_RT,),
        in_specs=[
            pl.BlockSpec((TM, D_MODEL), lambda i: (i, 0)),
            pl.BlockSpec((R_TAIL, D_MODEL), lambda i: (0, 0)),
            pl.BlockSpec((TM, D_MODEL), lambda i: (i, 0)),
            pl.BlockSpec((DEC_BATCH, D_MODEL), lambda i: (0, 0)),
            pl.BlockSpec((D_MODEL, D_MODEL), lambda i: (0, 0)),
        ],
        out_specs=pl.BlockSpec((TM, D_MODEL), lambda i: (i, 0)),
        out_shape=jax.ShapeDtypeStruct((R_ALL, D_MODEL), F32),
        scratch_shapes=[pltpu.VMEM((TM, D_MODEL), BF16)],
        compiler_params=pltpu.CompilerParams(
            dimension_semantics=("arbitrary",), vmem_limit_bytes=VMEM_LIMIT),
        name="outproj",
    )(xp, xt, mix, mixs, w)


U_LO = LAST_P - HALO
U_HI = LAST_P + DEC_BATCH
N_UST = U_HI - U_LO


def _ffn_kernel(x1_ref, halo_ref, st0_ref, st1_ref, gn_ref, gf_ref, wu_ref, wg_ref, cw_ref, cb_ref, wd_ref,
                y_ref, ys_ref, ust_ref, h2_scr, ubuf, abuf):
    i = pl.program_id(0)
    j = pl.program_id(1)

    @pl.when(j == 0)
    def _():
        h2_scr[0:FH] = _rms(halo_ref[...], gn_ref[...]).astype(BF16)
        h2_scr[FH:FH + TM] = _rms(x1_ref[...], gn_ref[...]).astype(BF16)
        y_ref[...] = jnp.zeros((TM, D_MODEL), F32)

    ubuf[...] = jnp.dot(h2_scr[...], wu_ref[...], preferred_element_type=F32)
    gate = jnp.dot(h2_scr[FH:FH + TM], wg_ref[...], preferred_element_type=F32)
    cw = cw_ref[...]
    cb = cb_ref[...]
    u = ubuf[FH:FH + TM]
    uc = cb + ubuf[FH - 2:FH - 2 + TM] * cw[0:1] + ubuf[FH - 1:FH - 1 + TM] * cw[1:2] + u * cw[2:3]
    abuf[...] = (jax.nn.gelu(uc) * gate).astype(BF16)
    ust_ref[...] = u[U_LO:U_HI]

    @pl.when(i == N_RT - 1)
    def _():
        us = u[LAST_P:U_HI]
        ucs = cb + st0_ref[...] * cw[0:1] + st1_ref[...] * cw[1:2] + us * cw[2:3]
        abuf[LAST_P:U_HI] = (jax.nn.gelu(ucs) * gate[LAST_P:U_HI]).astype(BF16)

    y_ref[...] += jnp.dot(abuf[...], wd_ref[...], preferred_element_type=F32)

    @pl.when(j == N_FT - 1)
    def _():
        y = _rms(x1_ref[...] + y_ref[...], gf_ref[...])
        y_ref[...] = y

        @pl.when(i == N_RT - 1)
        def _():
            ys_ref[...] = y[LAST_P:U_HI]


def _ffn(x1, st, gn, gf, wu, wg, cw, cb, wd):
    def halo_idx(i, j):
        return (jnp.where(i == 0, R_ALL // FH - 1, i * (TM // FH) - 1), 0)

    return pl.pallas_call(
        _ffn_kernel,
        grid=(N_RT, N_FT),
        in_specs=[
            pl.BlockSpec((TM, D_MODEL), lambda i, j: (i, 0)),
            pl.BlockSpec((FH, D_MODEL), halo_idx),
            pl.BlockSpec((DEC_BATCH, TF), lambda i, j: (0, j)),
            pl.BlockSpec((DEC_BATCH, TF), lambda i, j: (0, N_FT + j)),
            pl.BlockSpec((1, D_MODEL), lambda i, j: (0, 0)),
            pl.BlockSpec((1, D_MODEL), lambda i, j: (0, 0)),
            pl.BlockSpec((D_MODEL, TF), lambda i, j: (0, j)),
            pl.BlockSpec((D_MODEL, TF), lambda i, j: (0, j)),
            pl.BlockSpec((FFN_CONV, TF), lambda i, j: (0, j)),
            pl.BlockSpec((1, TF), lambda i, j: (0, j)),
            pl.BlockSpec((TF, D_MODEL), lambda i, j: (j, 0)),
        ],
        out_specs=[
            pl.BlockSpec((TM, D_MODEL), lambda i, j: (i, 0)),
            pl.BlockSpec((DEC_BATCH, D_MODEL), lambda i, j: (0, 0)),
            pl.BlockSpec((N_UST, TF), lambda i, j: (i, j)),
        ],
        out_shape=[
            jax.ShapeDtypeStruct((SEQ, D_MODEL), F32),
            jax.ShapeDtypeStruct((DEC_BATCH, D_MODEL), F32),
            jax.ShapeDtypeStruct((N_RT * N_UST, D_FF), F32),
        ],
        scratch_shapes=[
            pltpu.VMEM((FH + TM, D_MODEL), BF16),
            pltpu.VMEM((FH + TM, TF), F32),
            pltpu.VMEM((TM, TF), BF16),
        ],
        compiler_params=pltpu.CompilerParams(
            dimension_semantics=("arbitrary", "arbitrary"), vmem_limit_bytes=VMEM_LIMIT),
        name="ffn",
    )(x1, x1, st, st, gn, gf, wu, wg, cw, cb, wd)


def _gate_weights(wa, wx):
    per = GW // RNN_BLOCK
    eye = jnp.eye(per, dtype=wa.dtype)

    def bd(w):
        w = w.reshape(N_GW, per, RNN_BLOCK, RNN_BLOCK)
        return jnp.einsum('gpcd,pq->gpcqd', w, eye).reshape(N_GW, GW, GW)

    return jnp.concatenate([bd(wa), bd(wx)], axis=-1).astype(BF16)


def kernel(x_prompt, x_sample, state_rnn_conv, state_rnn_h, cache_k_win, cache_v_win, state_ffn_conv,
           meta_tokens, rel_bias_table, norm_mix, w_in, rnn_conv_w, rnn_conv_b, gate_a_w, gate_a_b,
           gate_x_w, gate_x_b, rnn_lambda, attn_sinks, norm_rnn_out, norm_attn_out, w_out, norm_ffn,
           w_up, w_gate, ffn_conv_w, ffn_conv_b, w_down, norm_final):
    l = 0
    xp = x_prompt[0]
    xt = jnp.concatenate([x_sample[:, 0, :], jnp.zeros((N_PAD, D_MODEL), F32), meta_tokens], axis=0)
    row = lambda v: v.reshape(1, -1)

    qi = np.arange(BLOCK)[:, None]
    sj = np.arange(2 * BLOCK)[None, :]
    bucket_p = jnp.asarray(_rel_buckets(BLOCK + qi - sj))
    pos = (np.arange(SB * WINDOW * N_KV) >> 1) & (WINDOW - 1)
    bucket_s = jnp.asarray(np.tile(_rel_buckets(WINDOW - pos)[None, :], (SB, 1)))

    wg = _gate_weights(gate_a_w[l], gate_x_w[l])
    seq_w = (rnn_conv_w[l], row(rnn_conv_b[l]), wg, row(gate_a_b[l]), row(gate_x_b[l]), row(rnn_lambda[l]),
             row(norm_rnn_out[l]), row(norm_attn_out[l]))

    z = _inproj(xp, xt, row(norm_mix[l]), w_in[l])
    mix, h_last = _mixer(rel_bias_table, attn_sinks[l], z, bucket_p, *seq_w)
    mix_s, conv_s, h_s, k_s, v_s = _smixer(
        rel_bias_table, attn_sinks[l], z,
        state_rnn_conv[l].reshape(DEC_BATCH, (RNN_CONV - 1) * D_RNN), state_rnn_h[l],
        cache_k_win[l].reshape(DEC_BATCH * WINDOW * N_KV, HEAD_DIM),
        cache_v_win[l].reshape(DEC_BATCH * WINDOW * N_KV, HEAD_DIM),
        bucket_s, *seq_w)
    x1 = _outproj(xp, xt, mix, mix_s, w_out[l].astype(BF16))
    y_p, y_s, ust = _ffn(x1, state_ffn_conv[l].reshape(DEC_BATCH, (FFN_CONV - 1) * D_FF),
                         row(norm_ffn[l]), row(norm_final), w_up[l].astype(BF16), w_gate[l].astype(BF16),
                         ffn_conv_w[l], row(ffn_conv_b[l]), w_down[l].astype(BF16))

    ust = ust[(N_RT - 1) * N_UST:]
    k_off = 2 * D_RNN + D_ATTN
    kv_p = z[SEQ - WINDOW:SEQ, k_off:]
    p_states = (
        z[SEQ - (RNN_CONV - 1):SEQ, 0:D_RNN][None, None],
        h_last[0:1][None],
        kv_p[:, :D_KV].reshape(1, 1, WINDOW, N_KV, HEAD_DIM),
        kv_p[:, D_KV:].reshape(1, 1, WINDOW, N_KV, HEAD_DIM),
        ust[HALO - (FFN_CONV - 1):HALO][None, None],
    )
    s_states = (
        conv_s.reshape(1, DEC_BATCH, RNN_CONV - 1, D_RNN),
        h_s[None],
        k_s.reshape(1, DEC_BATCH, WINDOW, N_KV, HEAD_DIM),
        v_s.reshape(1, DEC_BATCH, WINDOW, N_KV, HEAD_DIM),
        jnp.stack([state_ffn_conv[l][:, FFN_CONV - 2, :], ust[HALO:]], axis=1)[None],
    )
    return (y_p[None], y_s[:, None, :]) + p_states + s_states
```

```python
import functools
import math

import numpy as np
import jax
import jax.numpy as jnp
from jax import lax
from jax.experimental import pallas as pl
from jax.experimental.pallas import tpu as pltpu

F32 = jnp.float32
BF16 = jnp.bfloat16

D_MODEL = 2048
SEQ = 8192
DEC_BATCH = 128
D_RNN = 1024
N_RNN_BLOCKS = 16
RNN_BLOCK = D_RNN // N_RNN_BLOCKS
RNN_CONV = 4
LRU_C = 8.0
N_HEADS = 8
HEAD_DIM = 128
N_KV = 2
GROUP = N_HEADS // N_KV
D_ATTN = N_HEADS * HEAD_DIM
WINDOW = 128
BLOCK = 128
NUM_BUCKETS = 32
MAX_DISTANCE = 128
D_FF = 3 * D_MODEL
FFN_CONV = 3
N_META = 16
EPS = 1e-6
NEG = -1e30
D_KV = N_KV * HEAD_DIM
D_IN = 2 * D_RNN + D_ATTN + 2 * D_KV
SCALE = HEAD_DIM ** -0.5

N_PAD = BLOCK - N_META
R_TAIL = DEC_BATCH + BLOCK
R_ALL = SEQ + R_TAIL
SAMPLE_BLK = SEQ // BLOCK
PREFIX_BLK = SAMPLE_BLK + 1
N_BLK = R_ALL // BLOCK

TM = 768
N_RT = R_ALL // TM
LAST_P = SEQ - (N_RT - 1) * TM
TN_IN = 512
TM_IN = 1408
N_RT_IN = R_ALL // TM_IN
TF = 512
N_FT = D_FF // TF
HALO = 8
FH = 16
GW = 256
N_GW = D_RNN // GW
SB = 16
VMEM_LIMIT = 56 * 1024 * 1024


def _rms(x, g):
    return x * lax.rsqrt(jnp.mean(x * x, axis=-1, keepdims=True) + EPS) * g


def _rel_buckets(d):
    d = np.maximum(d, 0)
    exact = NUM_BUCKETS // 2
    ratio = np.maximum(d, 1).astype(np.float32) / np.float32(exact)
    large = exact + (np.log(ratio) / np.float32(math.log(MAX_DISTANCE / exact))
                     * np.float32(NUM_BUCKETS - exact)).astype(np.int32)
    large = np.minimum(large, NUM_BUCKETS - 1)
    return np.where(d < exact, d, large).astype(np.int32)


def _table_lookup(bucket, tab_ref, h):
    out = jnp.zeros(bucket.shape, F32)
    for b in range(NUM_BUCKETS):
        out = jnp.where(bucket == b, tab_ref[b, h], out)
    return out


def _gates(xc, wg_ref, gab, gxb, lam):
    xcb = xc.astype(BF16)
    ga, gx = [], []
    for j in range(N_GW):
        gj = jnp.dot(xcb[:, GW * j:GW * (j + 1)], wg_ref[j], preferred_element_type=F32)
        ga.append(gj[:, :GW])
        gx.append(gj[:, GW:])
    r = jax.nn.sigmoid(jnp.concatenate(ga, axis=1) + gab)
    i = jax.nn.sigmoid(jnp.concatenate(gx, axis=1) + gxb)
    log_a = -LRU_C * r * jax.nn.softplus(-lam)
    a = jnp.exp(log_a)
    b = jnp.sqrt(1.0 - a * a) * i * xc
    return a, b


def _x_tile(i, n_tiles, xp_ref, xt_ref, fn, dst):
    tm = xp_ref.shape[0]
    last_p = tm - R_TAIL

    @pl.when(i < n_tiles - 1)
    def _():
        dst[...] = fn(xp_ref[...])

    @pl.when(i == n_tiles - 1)
    def _():
        dst[0:last_p] = fn(xp_ref[0:last_p])
        dst[last_p:tm] = fn(xt_ref[...])


def _inproj_kernel(xp_ref, xt_ref, g_ref, w_ref, z_ref, h_scr):
    i = pl.program_id(0)

    @pl.when(pl.program_id(1) == 0)
    def _():
        _x_tile(i, N_RT_IN, xp_ref, xt_ref, lambda x: _rms(x, g_ref[...]).astype(BF16), h_scr)

    z_ref[...] = jnp.dot(h_scr[...], w_ref[...].astype(BF16), preferred_element_type=F32)


def _inproj(xp, xt, g, w):
    return pl.pallas_call(
        _inproj_kernel,
        grid=(N_RT_IN, D_IN // TN_IN),
        in_specs=[
            pl.BlockSpec((TM_IN, D_MODEL), lambda i, j: (i, 0)),
            pl.BlockSpec((R_TAIL, D_MODEL), lambda i, j: (0, 0)),
            pl.BlockSpec((1, D_MODEL), lambda i, j: (0, 0)),
            pl.BlockSpec((D_MODEL, TN_IN), lambda i, j: (0, j)),
        ],
        out_specs=pl.BlockSpec((TM_IN, TN_IN), lambda i, j: (i, j)),
        out_shape=jax.ShapeDtypeStruct((R_ALL, D_IN), F32),
        scratch_shapes=[pltpu.VMEM((TM_IN, D_MODEL), BF16)],
        compiler_params=pltpu.CompilerParams(
            dimension_semantics=("arbitrary", "arbitrary"), vmem_limit_bytes=VMEM_LIMIT),
        name="inproj",
    )(xp, xt, g, w)


def _mixer_kernel(tab_ref, sink_ref, xr_ref, gr_ref, q_ref, kv_ref, kvp_ref, bucket_ref,
                  cw_ref, cb_ref, wg_ref, gab_ref, gxb_ref, lam_ref, nr_ref, na_ref,
                  wo_f, wu_f, wgt_f, wd_f,
                  mix_ref, hlast_ref, wo_b, wu_b, wgt_b, wd_b, xbuf, h_scr, bias_scr):
    s = pl.program_id(0)

    for w_f, w_b in ((wo_f, wo_b), (wu_f, wu_b), (wgt_f, wgt_b), (wd_f, wd_b)):
        w_b[...] = w_f[...].astype(BF16)

    @pl.when(s == 0)
    def _():
        xbuf[0:HALO] = jnp.zeros((HALO, D_RNN), F32)
        h_scr[...] = jnp.zeros((HALO, D_RNN), F32)
        bucket = bucket_ref[...]
        for h in range(N_HEADS):
            bias_scr[h] = _table_lookup(bucket, tab_ref, h)

    xr = xr_ref[...]
    xbuf[HALO:HALO + BLOCK] = xr
    cw = cw_ref[...]
    xc = cb_ref[...]
    for j in range(RNN_CONV - 1):
        lo = HALO - (RNN_CONV - 1) + j
        xc = xc + xbuf[lo:lo + BLOCK] * cw[j:j + 1]
    xc = xc + xr * cw[RNN_CONV - 1:RNN_CONV]
    xbuf[0:HALO] = xr[BLOCK - HALO:BLOCK]

    a, b = _gates(xc, wg_ref, gab_ref[...], gxb_ref[...], lam_ref[...])
    row = lax.broadcasted_iota(jnp.int32, (BLOCK, D_RNN), 0)
    b = jnp.where(jnp.logical_and(s == 0, row < N_PAD), 0.0, b)

    ng = BLOCK // HALO
    a3 = a.reshape(ng, HALO, D_RNN)
    b3 = b.reshape(ng, HALO, D_RNN)
    sub = lax.broadcasted_iota(jnp.int32, (ng, HALO, D_RNN), 1)
    sh = 1
    while sh < HALO:
        a_prev = pltpu.roll(a3, sh, 1)
        b_prev = pltpu.roll(b3, sh, 1)
        m = sub >= sh
        b3 = jnp.where(m, a3 * b_prev + b3, b3)
        a3 = jnp.where(m, a3 * a_prev, a3)
        sh *= 2
    h = h_scr[0:1]
    hs = []
    for g in range(ng):
        hg = a3[g] * h + b3[g]
        hs.append(hg)
        h = hg[HALO - 1:HALO]
    h_all = jnp.concatenate(hs, axis=0)
    h_scr[...] = jnp.broadcast_to(h, (HALO, D_RNN))
    hlast_ref[...] = jnp.broadcast_to(h, (HALO, D_RNN))
    y_rnn = h_all * jax.nn.gelu(gr_ref[...])
    mix_ref[:, 0:D_RNN] = _rms(y_rnn, nr_ref[...]).astype(BF16)

    q = q_ref[...]
    kv = kv_ref[...]
    kvp = kvp_ref[...]
    col = lax.broadcasted_iota(jnp.int32, (BLOCK, 2 * BLOCK), 1)
    rowq = lax.broadcasted_iota(jnp.int32, (BLOCK, 2 * BLOCK), 0)
    d = BLOCK + rowq - col
    first_key = jnp.where(s == 0, BLOCK + N_PAD, jnp.where(s == 1, N_PAD, 0))
    mask = (d >= 0) & (d < WINDOW) & (col >= first_key)
    outs = []
    for kh in range(N_KV):
        ks = slice(kh * HEAD_DIM, (kh + 1) * HEAD_DIM)
        vs = slice(D_KV + kh * HEAD_DIM, D_KV + (kh + 1) * HEAD_DIM)
        qs = jnp.concatenate(
            [q[:, (kh * GROUP + g) * HEAD_DIM:(kh * GROUP + g + 1) * HEAD_DIM] for g in range(GROUP)],
            axis=0).astype(BF16)
        kk = jnp.concatenate([kvp[:, ks], kv[:, ks]], axis=0).astype(BF16)
        vv = jnp.concatenate([kvp[:, vs], kv[:, vs]], axis=0).astype(BF16)
        sc = lax.dot_general(qs, kk, (((1,), (1,)), ((), ())), preferred_element_type=F32)
        es, dens = [], []
        for g in range(GROUP):
            hh = kh * GROUP + g
            lg = sc[g * BLOCK:(g + 1) * BLOCK] * SCALE + bias_scr[hh]
            lg = jnp.where(mask, lg, NEG)
            sink = sink_ref[hh]
            mx = jnp.maximum(jnp.max(lg, axis=-1, keepdims=True), sink)
            e = jnp.exp(lg - mx)
            dens.append(jnp.sum(e, axis=-1, keepdims=True) + jnp.exp(sink - mx))
            es.append(e.astype(BF16))
        pv = jnp.dot(jnp.concatenate(es, axis=0), vv, preferred_element_type=F32)
        for g in range(GROUP):
            outs.append(pv[g * BLOCK:(g + 1) * BLOCK] / dens[g])
    y_attn = jnp.concatenate(outs, axis=1)
    mix_ref[:, D_RNN:] = _rms(y_attn, na_ref[...]).astype(BF16)


def _mixer(tab, sinks, z, bucket, cw, cb, wg, gab, gxb, lam, nr, na, w_out, w_up, w_gate, w_down):
    def cur(s):
        return jnp.where(s == 0, PREFIX_BLK, s - 1)

    def cur_out(s):
        return jnp.where(s == 0, SAMPLE_BLK, s - 1)

    def slab(w):
        rows = w.shape[0] // SAMPLE_BLK
        return pl.BlockSpec((rows, w.shape[1]), lambda s: (jnp.maximum(s - 1, 0), 0))

    weights = (w_out, w_up, w_gate, w_down)

    def prev(s):
        return jnp.where(s <= 1, PREFIX_BLK, s - 2)

    vec = lambda n: pl.BlockSpec((1, n), lambda s: (0, 0))
    smem = pl.BlockSpec(memory_space=pltpu.SMEM)
    return pl.pallas_call(
        _mixer_kernel,
        grid=(SAMPLE_BLK + 1,),
        in_specs=[
            smem, smem,
            pl.BlockSpec((BLOCK, D_RNN), lambda s: (cur(s), 0)),
            pl.BlockSpec((BLOCK, D_RNN), lambda s: (cur(s), 1)),
            pl.BlockSpec((BLOCK, D_ATTN), lambda s: (cur(s), 2)),
            pl.BlockSpec((BLOCK, 2 * D_KV), lambda s: (cur(s), (2 * D_RNN + D_ATTN) // (2 * D_KV))),
            pl.BlockSpec((BLOCK, 2 * D_KV), lambda s: (prev(s), (2 * D_RNN + D_ATTN) // (2 * D_KV))),
            pl.BlockSpec((BLOCK, 2 * BLOCK), lambda s: (0, 0)),
            pl.BlockSpec((RNN_CONV, D_RNN), lambda s: (0, 0)),
            vec(D_RNN),
            pl.BlockSpec((N_GW, GW, 2 * GW), lambda s: (0, 0, 0)),
            vec(D_RNN), vec(D_RNN), vec(D_RNN), vec(D_RNN), vec(D_ATTN),
        ] + [slab(w) for w in weights],
        out_specs=[
            pl.BlockSpec((BLOCK, D_MODEL), lambda s: (cur_out(s), 0)),
            pl.BlockSpec((HALO, D_RNN), lambda s: (0, 0)),
        ] + [slab(w) for w in weights],
        out_shape=[
            jax.ShapeDtypeStruct((SEQ + BLOCK, D_MODEL), BF16),
            jax.ShapeDtypeStruct((HALO, D_RNN), F32),
        ] + [jax.ShapeDtypeStruct(w.shape, BF16) for w in weights],
        scratch_shapes=[
            pltpu.VMEM((HALO + BLOCK, D_RNN), F32),
            pltpu.VMEM((HALO, D_RNN), F32),
            pltpu.VMEM((N_HEADS, BLOCK, 2 * BLOCK), F32),
        ],
        compiler_params=pltpu.CompilerParams(
            dimension_semantics=("arbitrary",), vmem_limit_bytes=VMEM_LIMIT),
        name="mixer",
    )(tab, sinks, z, z, z, z, z, bucket, cw, cb, wg, gab, gxb, lam, nr, na, *weights)


def _smixer_kernel(tab_ref, sink_ref, xr_ref, gr_ref, q_ref, kv_ref, st_ref, h0_ref, kc_ref, vc_ref,
                   bucket_ref, cw_ref, cb_ref, wg_ref, gab_ref, gxb_ref, lam_ref, nr_ref, na_ref,
                   mix_ref, conv_ref, h_ref, ko_ref, vo_ref, bias_scr):
    c = pl.program_id(0)
    nrow = N_HEADS * SB
    ncol = SB * WINDOW * N_KV

    @pl.when(c == 0)
    def _():
        bucket = bucket_ref[...]
        for h in range(N_HEADS):
            bias_scr[h * SB:(h + 1) * SB] = _table_lookup(bucket, tab_ref, h)

    xr = xr_ref[...]
    st = st_ref[...]
    cw = cw_ref[...]
    xc = cb_ref[...]
    for j in range(RNN_CONV - 1):
        xc = xc + st[:, j * D_RNN:(j + 1) * D_RNN] * cw[j:j + 1]
    xc = xc + xr * cw[RNN_CONV - 1:RNN_CONV]
    conv_ref[:, 0:(RNN_CONV - 2) * D_RNN] = st[:, D_RNN:]
    conv_ref[:, (RNN_CONV - 2) * D_RNN:] = xr
    a, b = _gates(xc, wg_ref, gab_ref[...], gxb_ref[...], lam_ref[...])
    h = a * h0_ref[...] + b
    h_ref[...] = h
    y_rnn = h * jax.nn.gelu(gr_ref[...])
    mix_ref[:, 0:D_RNN] = _rms(y_rnn, nr_ref[...]).astype(BF16)

    q = q_ref[...]
    kv = kv_ref[...]
    col = lax.broadcasted_iota(jnp.int32, (nrow, ncol), 1)
    row = lax.broadcasted_iota(jnp.int32, (nrow, ncol), 0)
    pos = (col >> 1) & (WINDOW - 1)
    valid = ((col >> 8) == (row & (SB - 1))) & ((col & 1) == (row >> 6)) & (pos >= 1)
    qs = jnp.concatenate([q[:, h * HEAD_DIM:(h + 1) * HEAD_DIM] for h in range(N_HEADS)], axis=0).astype(BF16)
    new_rows = lambda off: jnp.concatenate(
        [kv[:, off + (h // GROUP) * HEAD_DIM:off + (h // GROUP + 1) * HEAD_DIM] for h in range(N_HEADS)],
        axis=0).astype(BF16).astype(F32)
    k_new = new_rows(0)
    v_new = new_rows(D_KV)
    sc = lax.dot_general(qs, kc_ref[...].astype(BF16), (((1,), (1,)), ((), ())), preferred_element_type=F32)
    lg = jnp.where(valid, sc * SCALE + bias_scr[...], NEG)
    rh = lax.broadcasted_iota(jnp.int32, (nrow, 1), 0) >> 4
    sink = jnp.zeros((nrow, 1), F32)
    bias_new = jnp.zeros((nrow, 1), F32)
    for h in range(N_HEADS):
        sink = jnp.where(rh == h, sink_ref[h], sink)
        bias_new = jnp.where(rh == h, tab_ref[0, h], bias_new)
    lg_new = jnp.sum(qs.astype(F32) * k_new, axis=-1, keepdims=True) * SCALE + bias_new
    mx = jnp.maximum(jnp.maximum(jnp.max(lg, axis=-1, keepdims=True), lg_new), sink)
    e = jnp.exp(lg - mx)
    e_new = jnp.exp(lg_new - mx)
    den = jnp.sum(e, axis=-1, keepdims=True) + e_new + jnp.exp(sink - mx)
    pv = jnp.dot(e.astype(BF16), vc_ref[...].astype(BF16), preferred_element_type=F32)
    o = (pv + e_new.astype(BF16).astype(F32) * v_new) / den
    y_attn = jnp.concatenate([o[h * SB:(h + 1) * SB] for h in range(N_HEADS)], axis=1)
    mix_ref[:, D_RNN:] = _rms(y_attn, na_ref[...]).astype(BF16)

    per = WINDOW * N_KV
    slide = lambda ref: pltpu.roll(ref[...].reshape(SB, per, HEAD_DIM), per - N_KV, 1).reshape(SB * per, HEAD_DIM)
    ko_ref[...] = slide(kc_ref)
    vo_ref[...] = slide(vc_ref)
    for bi in range(SB):
        for kh in range(N_KV):
            r = (bi + 1) * per - N_KV + kh
            ko_ref[r:r + 1, :] = kv[bi:bi + 1, kh * HEAD_DIM:(kh + 1) * HEAD_DIM]
            vo_ref[r:r + 1, :] = kv[bi:bi + 1, D_KV + kh * HEAD_DIM:D_KV + (kh + 1) * HEAD_DIM]


def _smixer(tab, sinks, z, st, h0, kc, vc, bucket, cw, cb, wg, gab, gxb, lam, nr, na):
    base = SEQ // SB
    vec = lambda n: pl.BlockSpec((1, n), lambda c: (0, 0))
    smem = pl.BlockSpec(memory_space=pltpu.SMEM)
    cache = pl.BlockSpec((SB * WINDOW * N_KV, HEAD_DIM), lambda c: (c, 0))
    return pl.pallas_call(
        _smixer_kernel,
        grid=(DEC_BATCH // SB,),
        in_specs=[
            smem, smem,
            pl.BlockSpec((SB, D_RNN), lambda c: (base + c, 0)),
            pl.BlockSpec((SB, D_RNN), lambda c: (base + c, 1)),
            pl.BlockSpec((SB, D_ATTN), lambda c: (base + c, 2)),
            pl.BlockSpec((SB, 2 * D_KV), lambda c: (base + c, (2 * D_RNN + D_ATTN) // (2 * D_KV))),
            pl.BlockSpec((SB, (RNN_CONV - 1) * D_RNN), lambda c: (c, 0)),
            pl.BlockSpec((SB, D_RNN), lambda c: (c, 0)),
            cache, cache,
            pl.BlockSpec((SB, SB * WINDOW * N_KV), lambda c: (0, 0)),
            pl.BlockSpec((RNN_CONV, D_RNN), lambda c: (0, 0)),
            vec(D_RNN),
            pl.BlockSpec((N_GW, GW, 2 * GW), lambda c: (0, 0, 0)),
            vec(D_RNN), vec(D_RNN), vec(D_RNN), vec(D_RNN), vec(D_ATTN),
        ],
        out_specs=[
            pl.BlockSpec((SB, D_MODEL), lambda c: (c, 0)),
            pl.BlockSpec((SB, (RNN_CONV - 1) * D_RNN), lambda c: (c, 0)),
            pl.BlockSpec((SB, D_RNN), lambda c: (c, 0)),
            cache, cache,
        ],
        out_shape=[
            jax.ShapeDtypeStruct((DEC_BATCH, D_MODEL), BF16),
            jax.ShapeDtypeStruct((DEC_BATCH, (RNN_CONV - 1) * D_RNN), F32),
            jax.ShapeDtypeStruct((DEC_BATCH, D_RNN), F32),
            jax.ShapeDtypeStruct((DEC_BATCH * WINDOW * N_KV, HEAD_DIM), F32),
            jax.ShapeDtypeStruct((DEC_BATCH * WINDOW * N_KV, HEAD_DIM), F32),
        ],
        scratch_shapes=[pltpu.VMEM((N_HEADS * SB, SB * WINDOW * N_KV), F32)],
        compiler_params=pltpu.CompilerParams(
            dimension_semantics=("arbitrary",), vmem_limit_bytes=VMEM_LIMIT),
        name="smixer",
    )(tab, sinks, z, z, z, z, st, h0, kc, vc, bucket, cw, cb, wg, gab, gxb, lam, nr, na)


def _outproj_kernel(xp_ref, xt_ref, mix_ref, mixs_ref, w_ref, x1_ref, lhs_scr):
    i = pl.program_id(0)

    @pl.when(i < N_RT - 1)
    def _():
        x1_ref[...] = xp_ref[...] + jnp.dot(mix_ref[...], w_ref[...], preferred_element_type=F32)

    @pl.when(i == N_RT - 1)
    def _():
        s0, s1 = LAST_P, LAST_P + DEC_BATCH
        lhs_scr[0:s0] = mix_ref[0:s0]
        lhs_scr[s0:s1] = mixs_ref[...]
        lhs_scr[s1:TM] = mix_ref[s0:s0 + BLOCK]
        y = jnp.dot(lhs_scr[...], w_ref[...], preferred_element_type=F32)
        x1_ref[0:s0] = xp_ref[0:s0] + y[0:s0]
        x1_ref[s0:TM] = xt_ref[...] + y[s0:TM]


def _outproj(xp, xt, mix, mixs, w):
    return pl.pallas_call(
        _outproj_kernel,
        grid=(N_RT,),
        in_specs=[
            pl.BlockSpec((TM, D_MODEL), lambda i: (i, 0)),
            pl.BlockSpec((R_TAIL, D_MODEL), lambda i: (0, 0)),
            pl.BlockSpec((TM, D_MODEL), lambda i: (i, 0)),
            pl.BlockSpec((DEC_BATCH, D_MODEL), lambda i: (0, 0)),
            pl.BlockSpec((D_MODEL, D_MODEL), lambda i: (0, 0)),
        ],
        out_specs=pl.BlockSpec((TM, D_MODEL), lambda i: (i, 0)),
        out_shape=jax.ShapeDtypeStruct((R_ALL, D_MODEL), F32),
        scratch_shapes=[pltpu.VMEM((TM, D_MODEL), BF16)],
        compiler_params=pltpu.CompilerParams(
            dimension_semantics=("arbitrary",), vmem_limit_bytes=VMEM_LIMIT),
        name="outproj",
    )(xp, xt, mix, mixs, w)


U_LO = LAST_P - HALO
U_HI = LAST_P + DEC_BATCH
N_UST = U_HI - U_LO


def _ffn_kernel(x1_ref, halo_ref, st0_ref, st1_ref, gn_ref, gf_ref, wu_ref, wg_ref, cw_ref, cb_ref, wd_ref,
                y_ref, ys_ref, ust_ref, h2_scr, ubuf, abuf):
    i = pl.program_id(0)
    j = pl.program_id(1)

    @pl.when(j == 0)
    def _():
        h2_scr[0:FH] = _rms(halo_ref[...], gn_ref[...]).astype(BF16)
        h2_scr[FH:FH + TM] = _rms(x1_ref[...], gn_ref[...]).astype(BF16)
        y_ref[...] = jnp.zeros((TM, D_MODEL), F32)

    ubuf[...] = jnp.dot(h2_scr[...], wu_ref[...], preferred_element_type=F32)
    gate = jnp.dot(h2_scr[FH:FH + TM], wg_ref[...], preferred_element_type=F32)
    cw = cw_ref[...]
    cb = cb_ref[...]
    u = ubuf[FH:FH + TM]
    uc = cb + ubuf[FH - 2:FH - 2 + TM] * cw[0:1] + ubuf[FH - 1:FH - 1 + TM] * cw[1:2] + u * cw[2:3]
    abuf[...] = (jax.nn.gelu(uc) * gate).astype(BF16)
    ust_ref[...] = u[U_LO:U_HI]

    @pl.when(i == N_RT - 1)
    def _():
        us = u[LAST_P:U_HI]
        ucs = cb + st0_ref[...] * cw[0:1] + st1_ref[...] * cw[1:2] + us * cw[2:3]
        abuf[LAST_P:U_HI] = (jax.nn.gelu(ucs) * gate[LAST_P:U_HI]).astype(BF16)

    y_ref[...] += jnp.dot(abuf[...], wd_ref[...], preferred_element_type=F32)

    @pl.when(j == N_FT - 1)
    def _():
        y = _rms(x1_ref[...] + y_ref[...], gf_ref[...])
        y_ref[...] = y

        @pl.when(i == N_RT - 1)
        def _():
            ys_ref[...] = y[LAST_P:U_HI]


def _ffn(x1, st, gn, gf, wu, wg, cw, cb, wd):
    def halo_idx(i, j):
        return (jnp.where(i == 0, R_ALL // FH - 1, i * (TM // FH) - 1), 0)

    return pl.pallas_call(
        _ffn_kernel,
        grid=(N_RT, N_FT),
        in_specs=[
            pl.BlockSpec((TM, D_MODEL), lambda i, j: (i, 0)),
            pl.BlockSpec((FH, D_MODEL), halo_idx),
            pl.BlockSpec((DEC_BATCH, TF), lambda i, j: (0, j)),
            pl.BlockSpec((DEC_BATCH, TF), lambda i, j: (0, N_FT + j)),
            pl.BlockSpec((1, D_MODEL), lambda i, j: (0, 0)),
            pl.BlockSpec((1, D_MODEL), lambda i, j: (0, 0)),
            pl.BlockSpec((D_MODEL, TF), lambda i, j: (0, j)),
            pl.BlockSpec((D_MODEL, TF), lambda i, j: (0, j)),
            pl.BlockSpec((FFN_CONV, TF), lambda i, j: (0, j)),
            pl.BlockSpec((1, TF), lambda i, j: (0, j)),
            pl.BlockSpec((TF, D_MODEL), lambda i, j: (j, 0)),
        ],
        out_specs=[
            pl.BlockSpec((TM, D_MODEL), lambda i, j: (i, 0)),
            pl.BlockSpec((DEC_BATCH, D_MODEL), lambda i, j: (0, 0)),
            pl.BlockSpec((N_UST, TF), lambda i, j: (i, j)),
        ],
        out_shape=[
            jax.ShapeDtypeStruct((SEQ, D_MODEL), F32),
            jax.ShapeDtypeStruct((DEC_BATCH, D_MODEL), F32),
            jax.ShapeDtypeStruct((N_RT * N_UST, D_FF), F32),
        ],
        scratch_shapes=[
            pltpu.VMEM((FH + TM, D_MODEL), BF16),
            pltpu.VMEM((FH + TM, TF), F32),
            pltpu.VMEM((TM, TF), BF16),
        ],
        compiler_params=pltpu.CompilerParams(
            dimension_semantics=("arbitrary", "arbitrary"), vmem_limit_bytes=VMEM_LIMIT),
        name="ffn",
    )(x1, x1, st, st, gn, gf, wu, wg, cw, cb, wd)


def _gate_weights(wa, wx):
    per = GW // RNN_BLOCK
    eye = jnp.eye(per, dtype=wa.dtype)

    def bd(w):
        w = w.reshape(N_GW, per, RNN_BLOCK, RNN_BLOCK)
        return jnp.einsum('gpcd,pq->gpcqd', w, eye).reshape(N_GW, GW, GW)

    return jnp.concatenate([bd(wa), bd(wx)], axis=-1).astype(BF16)


def kernel(x_prompt, x_sample, state_rnn_conv, state_rnn_h, cache_k_win, cache_v_win, state_ffn_conv,
           meta_tokens, rel_bias_table, norm_mix, w_in, rnn_conv_w, rnn_conv_b, gate_a_w, gate_a_b,
           gate_x_w, gate_x_b, rnn_lambda, attn_sinks, norm_rnn_out, norm_attn_out, w_out, norm_ffn,
           w_up, w_gate, ffn_conv_w, ffn_conv_b, w_down, norm_final):
    l = 0
    xp = x_prompt[0]
    xt = jnp.concatenate([x_sample[:, 0, :], jnp.zeros((N_PAD, D_MODEL), F32), meta_tokens], axis=0)
    row = lambda v: v.reshape(1, -1)

    qi = np.arange(BLOCK)[:, None]
    sj = np.arange(2 * BLOCK)[None, :]
    bucket_p = jnp.asarray(_rel_buckets(BLOCK + qi - sj))
    pos = (np.arange(SB * WINDOW * N_KV) >> 1) & (WINDOW - 1)
    bucket_s = jnp.asarray(np.tile(_rel_buckets(WINDOW - pos)[None, :], (SB, 1)))

    wg = _gate_weights(gate_a_w[l], gate_x_w[l])
    seq_w = (rnn_conv_w[l], row(rnn_conv_b[l]), wg, row(gate_a_b[l]), row(gate_x_b[l]), row(rnn_lambda[l]),
             row(norm_rnn_out[l]), row(norm_attn_out[l]))

    z = _inproj(xp, xt, row(norm_mix[l]), w_in[l])
    mix, h_last, w_out_b, w_up_b, w_gate_b, w_down_b = _mixer(
        rel_bias_table, attn_sinks[l], z, bucket_p, *seq_w, w_out[l], w_up[l], w_gate[l], w_down[l])
    mix_s, conv_s, h_s, k_s, v_s = _smixer(
        rel_bias_table, attn_sinks[l], z,
        state_rnn_conv[l].reshape(DEC_BATCH, (RNN_CONV - 1) * D_RNN), state_rnn_h[l],
        cache_k_win[l].reshape(DEC_BATCH * WINDOW * N_KV, HEAD_DIM),
        cache_v_win[l].reshape(DEC_BATCH * WINDOW * N_KV, HEAD_DIM),
        bucket_s, *seq_w)
    x1 = _outproj(xp, xt, mix, mix_s, w_out_b)
    y_p, y_s, ust = _ffn(x1, state_ffn_conv[l].reshape(DEC_BATCH, (FFN_CONV - 1) * D_FF),
                         row(norm_ffn[l]), row(norm_final), w_up_b, w_gate_b,
                         ffn_conv_w[l], row(ffn_conv_b[l]), w_down_b)

    ust = ust[(N_RT - 1) * N_UST:]
    k_off = 2 * D_RNN + D_ATTN
    kv_p = z[SEQ - WINDOW:SEQ, k_off:]
    p_states = (
        z[SEQ - (RNN_CONV - 1):SEQ, 0:D_RNN][None, None],
        h_last[0:1][None],
        kv_p[:, :D_KV].reshape(1, 1, WINDOW, N_KV, HEAD_DIM),
        kv_p[:, D_KV:].reshape(1, 1, WINDOW, N_KV, HEAD_DIM),
        ust[HALO - (FFN_CONV - 1):HALO][None, None],
    )
    s_states = (
        conv_s.reshape(1, DEC_BATCH, RNN_CONV - 1, D_RNN),
        h_s[None],
        k_s.reshape(1, DEC_BATCH, WINDOW, N_KV, HEAD_DIM),
        v_s.reshape(1, DEC_BATCH, WINDOW, N_KV, HEAD_DIM),
        jnp.stack([state_ffn_conv[l][:, FFN_CONV - 2, :], ust[HALO:]], axis=1)[None],
    )
    return (y_p[None], y_s[:, None, :]) + p_states + s_states
```

```python
import math

import numpy as np
import jax
import jax.numpy as jnp
from jax import lax
from jax.experimental import pallas as pl
from jax.experimental.pallas import tpu as pltpu

F32 = jnp.float32
BF16 = jnp.bfloat16

D_MODEL = 2048
SEQ = 8192
DEC_BATCH = 128
D_RNN = 1024
N_RNN_BLOCKS = 16
RNN_BLOCK = D_RNN // N_RNN_BLOCKS
RNN_CONV = 4
LRU_C = 8.0
N_HEADS = 8
HEAD_DIM = 128
N_KV = 2
GROUP = N_HEADS // N_KV
D_ATTN = N_HEADS * HEAD_DIM
WINDOW = 128
BLOCK = 128
NUM_BUCKETS = 32
MAX_DISTANCE = 128
D_FF = 3 * D_MODEL
FFN_CONV = 3
N_META = 16
EPS = 1e-6
NEG = -1e30
D_KV = N_KV * HEAD_DIM
D_IN = 2 * D_RNN + D_ATTN + 2 * D_KV
SCALE = HEAD_DIM ** -0.5
O_GR = D_RNN
O_Q = 2 * D_RNN
O_KV = 2 * D_RNN + D_ATTN

N_PAD = BLOCK - N_META
R_TAIL = DEC_BATCH + BLOCK
R_ALL = SEQ + R_TAIL
N_PBLK = SEQ // BLOCK

FB = 2 * BLOCK
N_FSTEP = (N_PBLK + 2) // 2

TM = 768
N_RT = R_ALL // TM
LAST_P = SEQ - (N_RT - 1) * TM
TF = 512
N_FT = D_FF // TF
HALO = 8
FH = 16
GW = 256
N_GW = D_RNN // GW
SB = 16
VMEM_LIMIT = 56 * 1024 * 1024


def _rms(x, g):
    return x * lax.rsqrt(jnp.mean(x * x, axis=-1, keepdims=True) + EPS) * g


def _rel_buckets(d):
    d = np.maximum(d, 0)
    exact = NUM_BUCKETS // 2
    ratio = np.maximum(d, 1).astype(np.float32) / np.float32(exact)
    large = exact + (np.log(ratio) / np.float32(math.log(MAX_DISTANCE / exact))
                     * np.float32(NUM_BUCKETS - exact)).astype(np.int32)
    large = np.minimum(large, NUM_BUCKETS - 1)
    return np.where(d < exact, d, large).astype(np.int32)


def _table_lookup(bucket, tab_ref, h):
    out = jnp.zeros(bucket.shape, F32)
    for b in range(NUM_BUCKETS):
        out = jnp.where(bucket == b, tab_ref[b, h], out)
    return out


def _gates(xc, wg_ref, gab, gxb, lam):
    xcb = xc.astype(BF16)
    ga, gx = [], []
    for j in range(N_GW):
        gj = jnp.dot(xcb[:, GW * j:GW * (j + 1)], wg_ref[j], preferred_element_type=F32)
        ga.append(gj[:, :GW])
        gx.append(gj[:, GW:])
    r = jax.nn.sigmoid(jnp.concatenate(ga, axis=1) + gab)
    i = jax.nn.sigmoid(jnp.concatenate(gx, axis=1) + gxb)
    log_a = -LRU_C * r * jax.nn.softplus(-lam)
    a = jnp.exp(log_a)
    t = 1.0 - a * a
    b = jnp.where(t > 0.0, t * lax.rsqrt(t), 0.0) * i * xc
    return a, b


def _front_kernel(tab_ref, sink_ref, xa_ref, xb_ref, xt_ref, gm_ref, win_ref, wout_ref, bucket_ref,
                  cw_ref, cb_ref, wg_ref, gab_ref, gxb_ref, lam_ref, nr_ref, na_ref,
                  x1_ref, zs_ref, hlast_ref, kvlast_ref, xrt_ref,
                  zb0, zb1, xsave, xbuf, h_scr, kvbuf, bias_scr):
    k = pl.program_id(0)
    last = N_FSTEP - 1

    @pl.when(k == 0)
    def _():
        zb0[...] = jnp.zeros((BLOCK, D_IN), F32)
        xsave[...] = jnp.zeros((BLOCK, D_MODEL), F32)
        xbuf[0:HALO] = jnp.zeros((HALO, D_RNN), F32)
        h_scr[...] = jnp.zeros((HALO, D_RNN), F32)
        kvbuf[...] = jnp.zeros((BLOCK, 2 * D_KV), F32)
        bucket = bucket_ref[...]
        for h in range(N_HEADS):
            bias_scr[h] = _table_lookup(bucket, tab_ref, h)

    def inproj(x, zb):
        zb[...] = jnp.dot(_rms(x, gm_ref[...]).astype(BF16), win_ref[...], preferred_element_type=F32)

    def mix_out(zb, x, is_prefix, first_key, rows):
        xr = zb[:, 0:D_RNN]
        xbuf[HALO:HALO + BLOCK] = xr
        cw = cw_ref[...]
        xc = cb_ref[...]
        for j in range(RNN_CONV - 1):
            lo = HALO - (RNN_CONV - 1) + j
            xc = xc + xbuf[lo:lo + BLOCK] * cw[j:j + 1]
        xc = xc + xr * cw[RNN_CONV - 1:RNN_CONV]
        xbuf[0:HALO] = xr[BLOCK - HALO:BLOCK]

        a, b = _gates(xc, wg_ref, gab_ref[...], gxb_ref[...], lam_ref[...])
        h = h_scr[0:1]
        if is_prefix is not None:
            row = lax.broadcasted_iota(jnp.int32, (BLOCK, D_RNN), 0)
            b = jnp.where(jnp.logical_and(is_prefix, row < N_PAD), 0.0, b)
            h = jnp.where(is_prefix, 0.0, h)

        ng = BLOCK // HALO
        a3 = a.reshape(ng, HALO, D_RNN)
        b3 = b.reshape(ng, HALO, D_RNN)
        sub = lax.broadcasted_iota(jnp.int32, (ng, HALO, D_RNN), 1)
        sh = 1
        while sh < HALO:
            a_prev = pltpu.roll(a3, sh, 1)
            b_prev = pltpu.roll(b3, sh, 1)
            m = sub >= sh
            b3 = jnp.where(m, a3 * b_prev + b3, b3)
            a3 = jnp.where(m, a3 * a_prev, a3)
            sh *= 2
        hs = []
        for g in range(ng):
            hg = a3[g] * h + b3[g]
            hs.append(hg)
            h = hg[HALO - 1:HALO]
        h_all = jnp.concatenate(hs, axis=0)
        h_scr[...] = jnp.broadcast_to(h, (HALO, D_RNN))
        y_rnn = h_all * jax.nn.gelu(zb[:, O_GR:O_GR + D_RNN])
        mix_a = _rms(y_rnn, nr_ref[...]).astype(BF16)

        q = zb[:, O_Q:O_Q + D_ATTN]
        kv = zb[:, O_KV:O_KV + 2 * D_KV]
        kvp = kvbuf[...]
        kvbuf[...] = kv
        col = lax.broadcasted_iota(jnp.int32, (BLOCK, 2 * BLOCK), 1)
        rowq = lax.broadcasted_iota(jnp.int32, (BLOCK, 2 * BLOCK), 0)
        d = BLOCK + rowq - col
        mask = (d >= 0) & (d < WINDOW) & (col >= first_key)
        outs = []
        for kh in range(N_KV):
            ks = slice(kh * HEAD_DIM, (kh + 1) * HEAD_DIM)
            vs = slice(D_KV + kh * HEAD_DIM, D_KV + (kh + 1) * HEAD_DIM)
            qs = jnp.concatenate(
                [q[:, (kh * GROUP + g) * HEAD_DIM:(kh * GROUP + g + 1) * HEAD_DIM] for g in range(GROUP)],
                axis=0).astype(BF16)
            kk = jnp.concatenate([kvp[:, ks], kv[:, ks]], axis=0).astype(BF16)
            vv = jnp.concatenate([kvp[:, vs], kv[:, vs]], axis=0).astype(BF16)
            sc = lax.dot_general(qs, kk, (((1,), (1,)), ((), ())), preferred_element_type=F32)
            es, dens = [], []
            for g in range(GROUP):
                hh = kh * GROUP + g
                lg = sc[g * BLOCK:(g + 1) * BLOCK] * SCALE + bias_scr[hh]
                lg = jnp.where(mask, lg, NEG)
                sink = sink_ref[hh]
                mx = jnp.maximum(jnp.max(lg, axis=-1, keepdims=True), sink)
                e = jnp.exp(lg - mx)
                dens.append(jnp.sum(e, axis=-1, keepdims=True) + jnp.exp(sink - mx))
                es.append(e.astype(BF16))
            pv = jnp.dot(jnp.concatenate(es, axis=0), vv, preferred_element_type=F32)
            for g in range(GROUP):
                outs.append(pv[g * BLOCK:(g + 1) * BLOCK] / dens[g])
        mix_b = _rms(jnp.concatenate(outs, axis=1), na_ref[...]).astype(BF16)

        mix = jnp.concatenate([mix_a, mix_b], axis=1)
        x1_ref[rows] = x + jnp.dot(mix, wout_ref[...], preferred_element_type=F32)
        return h, kv, xr

    xa = jnp.where(k == 0, xt_ref[DEC_BATCH:R_TAIL], xa_ref[...])
    xb = jnp.where(k == last, xt_ref[0:DEC_BATCH], xb_ref[...])
    inproj(xa, zb1)
    mix_out(zb0, xsave[...], None, jnp.where(k == 1, N_PAD, 0), slice(0, BLOCK))
    h, kv, xr = mix_out(zb1, xa, k == 0, jnp.where(k == 0, BLOCK + N_PAD, 0), slice(BLOCK, FB))
    inproj(xb, zb0)
    xsave[...] = xb
    hlast_ref[...] = jnp.broadcast_to(h, (HALO, D_RNN))
    kvlast_ref[...] = kv
    xrt_ref[...] = xr[BLOCK - HALO:BLOCK]

    @pl.when(k == last)
    def _():
        zs_ref[...] = zb0[...]


def _front(tab, sinks, xp, xt, gm, w_in, w_out, bucket, cw, cb, wg, gab, gxb, lam, nr, na):
    vec = lambda n: pl.BlockSpec((1, n), lambda k: (0, 0))
    smem = pl.BlockSpec(memory_space=pltpu.SMEM)
    once = lambda shape: pl.BlockSpec(shape, lambda k: (0,) * len(shape), pipeline_mode=pl.Buffered(1))
    return pl.pallas_call(
        _front_kernel,
        grid=(N_FSTEP,),
        in_specs=[
            smem, smem,
            pl.BlockSpec((BLOCK, D_MODEL), lambda k: (jnp.maximum(2 * k - 1, 0), 0)),
            pl.BlockSpec((BLOCK, D_MODEL), lambda k: (jnp.minimum(2 * k, N_PBLK - 1), 0)),
            once((R_TAIL, D_MODEL)),
            vec(D_MODEL),
            once((D_MODEL, D_IN)),
            once((D_MODEL, D_MODEL)),
            pl.BlockSpec((BLOCK, 2 * BLOCK), lambda k: (0, 0)),
            pl.BlockSpec((RNN_CONV, D_RNN), lambda k: (0, 0)),
            vec(D_RNN),
            pl.BlockSpec((N_GW, GW, 2 * GW), lambda k: (0, 0, 0)),
            vec(D_RNN), vec(D_RNN), vec(D_RNN), vec(D_RNN), vec(D_ATTN),
        ],
        out_specs=[
            pl.BlockSpec((FB, D_MODEL), lambda k: (jnp.where(k == 0, SEQ // FB, k - 1), 0)),
            pl.BlockSpec((BLOCK, D_IN), lambda k: (0, 0)),
            pl.BlockSpec((HALO, D_RNN), lambda k: (0, 0)),
            pl.BlockSpec((BLOCK, 2 * D_KV), lambda k: (0, 0)),
            pl.BlockSpec((HALO, D_RNN), lambda k: (0, 0)),
        ],
        out_shape=[
            jax.ShapeDtypeStruct((R_ALL, D_MODEL), F32),
            jax.ShapeDtypeStruct((DEC_BATCH, D_IN), F32),
            jax.ShapeDtypeStruct((HALO, D_RNN), F32),
            jax.ShapeDtypeStruct((BLOCK, 2 * D_KV), F32),
            jax.ShapeDtypeStruct((HALO, D_RNN), F32),
        ],
        scratch_shapes=[
            pltpu.VMEM((BLOCK, D_IN), F32),
            pltpu.VMEM((BLOCK, D_IN), F32),
            pltpu.VMEM((BLOCK, D_MODEL), F32),
            pltpu.VMEM((HALO + BLOCK, D_RNN), F32),
            pltpu.VMEM((HALO, D_RNN), F32),
            pltpu.VMEM((BLOCK, 2 * D_KV), F32),
            pltpu.VMEM((N_HEADS, BLOCK, 2 * BLOCK), F32),
        ],
        compiler_params=pltpu.CompilerParams(
            dimension_semantics=("arbitrary",), vmem_limit_bytes=VMEM_LIMIT),
        name="front",
    )(tab, sinks, xp, xp, xt, gm, w_in, w_out, bucket, cw, cb, wg, gab, gxb, lam, nr, na)


def _smixer_kernel(tab_ref, sink_ref, z_ref, st_ref, h0_ref, kc_ref, vc_ref, xs_ref, wout_ref,
                   bucket_ref, cw_ref, cb_ref, wg_ref, gab_ref, gxb_ref, lam_ref, nr_ref, na_ref,
                   x1s_ref, conv_ref, h_ref, ko_ref, vo_ref, bias_scr, mix_scr):
    c = pl.program_id(0)
    nrow = N_HEADS * SB
    ncol = SB * WINDOW * N_KV

    @pl.when(c == 0)
    def _():
        bucket = bucket_ref[...]
        for h in range(N_HEADS):
            bias_scr[h * SB:(h + 1) * SB] = _table_lookup(bucket, tab_ref, h)

    r0 = pl.multiple_of(c * SB, SB)

    xr = z_ref[:, 0:D_RNN]
    st = st_ref[...]
    cw = cw_ref[...]
    xc = cb_ref[...]
    for j in range(RNN_CONV - 1):
        xc = xc + st[:, j * D_RNN:(j + 1) * D_RNN] * cw[j:j + 1]
    xc = xc + xr * cw[RNN_CONV - 1:RNN_CONV]
    conv_ref[:, 0:(RNN_CONV - 2) * D_RNN] = st[:, D_RNN:]
    conv_ref[:, (RNN_CONV - 2) * D_RNN:] = xr
    a, b = _gates(xc, wg_ref, gab_ref[...], gxb_ref[...], lam_ref[...])
    h = a * h0_ref[...] + b
    h_ref[...] = h
    y_rnn = h * jax.nn.gelu(z_ref[:, O_GR:O_GR + D_RNN])
    mix_scr[pl.ds(r0, SB), 0:D_RNN] = _rms(y_rnn, nr_ref[...]).astype(BF16)

    q = z_ref[:, O_Q:O_Q + D_ATTN]
    kv = z_ref[:, O_KV:O_KV + 2 * D_KV]
    col = lax.broadcasted_iota(jnp.int32, (nrow, ncol), 1)
    row = lax.broadcasted_iota(jnp.int32, (nrow, ncol), 0)
    pos = (col >> 1) & (WINDOW - 1)
    valid = ((col >> 8) == (row & (SB - 1))) & ((col & 1) == (row >> 6)) & (pos >= 1)
    qs = jnp.concatenate([q[:, h * HEAD_DIM:(h + 1) * HEAD_DIM] for h in range(N_HEADS)], axis=0).astype(BF16)
    new_rows = lambda off: jnp.concatenate(
        [kv[:, off + (h // GROUP) * HEAD_DIM:off + (h // GROUP + 1) * HEAD_DIM] for h in range(N_HEADS)],
        axis=0).astype(BF16).astype(F32)
    k_new = new_rows(0)
    v_new = new_rows(D_KV)
    sc = lax.dot_general(qs, kc_ref[...].astype(BF16), (((1,), (1,)), ((), ())), preferred_element_type=F32)
    lg = jnp.where(valid, sc * SCALE + bias_scr[...], NEG)
    rh = lax.broadcasted_iota(jnp.int32, (nrow, 1), 0) >> 4
    sink = jnp.zeros((nrow, 1), F32)
    bias_new = jnp.zeros((nrow, 1), F32)
    for h in range(N_HEADS):
        sink = jnp.where(rh == h, sink_ref[h], sink)
        bias_new = jnp.where(rh == h, tab_ref[0, h], bias_new)
    lg_new = jnp.sum(qs.astype(F32) * k_new, axis=-1, keepdims=True) * SCALE + bias_new
    mx = jnp.maximum(jnp.maximum(jnp.max(lg, axis=-1, keepdims=True), lg_new), sink)
    e = jnp.exp(lg - mx)
    e_new = jnp.exp(lg_new - mx)
    den = jnp.sum(e, axis=-1, keepdims=True) + e_new + jnp.exp(sink - mx)
    pv = jnp.dot(e.astype(BF16), vc_ref[...].astype(BF16), preferred_element_type=F32)
    o = (pv + e_new.astype(BF16).astype(F32) * v_new) / den
    y_attn = jnp.concatenate([o[h * SB:(h + 1) * SB] for h in range(N_HEADS)], axis=1)
    mix_scr[pl.ds(r0, SB), D_RNN:] = _rms(y_attn, na_ref[...]).astype(BF16)

    per = WINDOW * N_KV
    slide = lambda ref: pltpu.roll(ref[...].reshape(SB, per, HEAD_DIM), per - N_KV, 1).reshape(SB * per, HEAD_DIM)
    ko_ref[...] = slide(kc_ref)
    vo_ref[...] = slide(vc_ref)
    for bi in range(SB):
        for kh in range(N_KV):
            r = (bi + 1) * per - N_KV + kh
            ko_ref[r:r + 1, :] = kv[bi:bi + 1, kh * HEAD_DIM:(kh + 1) * HEAD_DIM]
            vo_ref[r:r + 1, :] = kv[bi:bi + 1, D_KV + kh * HEAD_DIM:D_KV + (kh + 1) * HEAD_DIM]

    @pl.when(c == DEC_BATCH // SB - 1)
    def _():
        x1s_ref[...] = xs_ref[...] + jnp.dot(mix_scr[...], wout_ref[...], preferred_element_type=F32)


def _smixer(tab, sinks, zs, st, h0, kc, vc, xs, w_out, bucket, cw, cb, wg, gab, gxb, lam, nr, na):
    vec = lambda n: pl.BlockSpec((1, n), lambda c: (0, 0))
    smem = pl.BlockSpec(memory_space=pltpu.SMEM)
    cache = pl.BlockSpec((SB * WINDOW * N_KV, HEAD_DIM), lambda c: (c, 0))
    return pl.pallas_call(
        _smixer_kernel,
        grid=(DEC_BATCH // SB,),
        in_specs=[
            smem, smem,
            pl.BlockSpec((SB, D_IN), lambda c: (c, 0)),
            pl.BlockSpec((SB, (RNN_CONV - 1) * D_RNN), lambda c: (c, 0)),
            pl.BlockSpec((SB, D_RNN), lambda c: (c, 0)),
            cache, cache,
            pl.BlockSpec((DEC_BATCH, D_MODEL), lambda c: (0, 0)),
            pl.BlockSpec((D_MODEL, D_MODEL), lambda c: (0, 0), pipeline_mode=pl.Buffered(1)),
            pl.BlockSpec((SB, SB * WINDOW * N_KV), lambda c: (0, 0)),
            pl.BlockSpec((RNN_CONV, D_RNN), lambda c: (0, 0)),
            vec(D_RNN),
            pl.BlockSpec((N_GW, GW, 2 * GW), lambda c: (0, 0, 0)),
            vec(D_RNN), vec(D_RNN), vec(D_RNN), vec(D_RNN), vec(D_ATTN),
        ],
        out_specs=[
            pl.BlockSpec((DEC_BATCH, D_MODEL), lambda c: (0, 0)),
            pl.BlockSpec((SB, (RNN_CONV - 1) * D_RNN), lambda c: (c, 0)),
            pl.BlockSpec((SB, D_RNN), lambda c: (c, 0)),
            cache, cache,
        ],
        out_shape=[
            jax.ShapeDtypeStruct((DEC_BATCH, D_MODEL), F32),
            jax.ShapeDtypeStruct((DEC_BATCH, (RNN_CONV - 1) * D_RNN), F32),
            jax.ShapeDtypeStruct((DEC_BATCH, D_RNN), F32),
            jax.ShapeDtypeStruct((DEC_BATCH * WINDOW * N_KV, HEAD_DIM), F32),
            jax.ShapeDtypeStruct((DEC_BATCH * WINDOW * N_KV, HEAD_DIM), F32),
        ],
        scratch_shapes=[
            pltpu.VMEM((N_HEADS * SB, SB * WINDOW * N_KV), F32),
            pltpu.VMEM((DEC_BATCH, D_MODEL), BF16),
        ],
        compiler_params=pltpu.CompilerParams(
            dimension_semantics=("arbitrary",), vmem_limit_bytes=VMEM_LIMIT),
        name="smixer",
    )(tab, sinks, zs, st, h0, kc, vc, xs, w_out, bucket, cw, cb, wg, gab, gxb, lam, nr, na)


U_LO = LAST_P - HALO
U_HI = LAST_P + DEC_BATCH
N_UST = U_HI - U_LO


def _ffn_kernel(x1_ref, halo_ref, x1s_ref, st0_ref, st1_ref, gn_ref, gf_ref, wu_ref, wg_ref, cw_ref, cb_ref,
                wd_ref, y_ref, ys_ref, ust_ref, h2_scr, ubuf, abuf):
    i = pl.program_id(0)
    j = pl.program_id(1)

    @pl.when(j == 0)
    def _():
        h2_scr[0:FH] = _rms(halo_ref[...], gn_ref[...]).astype(BF16)
        h2_scr[FH:FH + TM] = _rms(x1_ref[...], gn_ref[...]).astype(BF16)
        y_ref[...] = jnp.zeros((TM, D_MODEL), F32)

        @pl.when(i == N_RT - 1)
        def _():
            h2_scr[FH + LAST_P:FH + U_HI] = _rms(x1s_ref[...], gn_ref[...]).astype(BF16)

    ubuf[...] = jnp.dot(h2_scr[...], wu_ref[...], preferred_element_type=F32)
    gate = jnp.dot(h2_scr[FH:FH + TM], wg_ref[...], preferred_element_type=F32)
    cw = cw_ref[...]
    cb = cb_ref[...]
    u = ubuf[FH:FH + TM]
    uc = cb + ubuf[FH - 2:FH - 2 + TM] * cw[0:1] + ubuf[FH - 1:FH - 1 + TM] * cw[1:2] + u * cw[2:3]
    abuf[...] = (jax.nn.gelu(uc) * gate).astype(BF16)
    ust_ref[...] = u[U_LO:U_HI]

    @pl.when(i == N_RT - 1)
    def _():
        us = u[LAST_P:U_HI]
        ucs = cb + st0_ref[...] * cw[0:1] + st1_ref[...] * cw[1:2] + us * cw[2:3]
        abuf[LAST_P:U_HI] = (jax.nn.gelu(ucs) * gate[LAST_P:U_HI]).astype(BF16)

    y_ref[...] += jnp.dot(abuf[...], wd_ref[...], preferred_element_type=F32)

    @pl.when(j == N_FT - 1)
    def _():
        @pl.when(i == N_RT - 1)
        def _():
            ys_ref[...] = _rms(x1s_ref[...] + y_ref[LAST_P:U_HI], gf_ref[...])

        y_ref[...] = _rms(x1_ref[...] + y_ref[...], gf_ref[...])


def _ffn(x1, x1s, st, gn, gf, wu, wg, cw, cb, wd):
    def halo_idx(i, j):
        return (jnp.where(i == 0, R_ALL // FH - 1, i * (TM // FH) - 1), 0)

    return pl.pallas_call(
        _ffn_kernel,
        grid=(N_RT, N_FT),
        in_specs=[
            pl.BlockSpec((TM, D_MODEL), lambda i, j: (i, 0)),
            pl.BlockSpec((FH, D_MODEL), halo_idx),
            pl.BlockSpec((DEC_BATCH, D_MODEL), lambda i, j: (0, 0)),
            pl.BlockSpec((DEC_BATCH, TF), lambda i, j: (0, j)),
            pl.BlockSpec((DEC_BATCH, TF), lambda i, j: (0, N_FT + j)),
            pl.BlockSpec((1, D_MODEL), lambda i, j: (0, 0)),
            pl.BlockSpec((1, D_MODEL), lambda i, j: (0, 0)),
            pl.BlockSpec((D_MODEL, TF), lambda i, j: (0, j)),
            pl.BlockSpec((D_MODEL, TF), lambda i, j: (0, j)),
            pl.BlockSpec((FFN_CONV, TF), lambda i, j: (0, j)),
            pl.BlockSpec((1, TF), lambda i, j: (0, j)),
            pl.BlockSpec((TF, D_MODEL), lambda i, j: (j, 0)),
        ],
        out_specs=[
            pl.BlockSpec((TM, D_MODEL), lambda i, j: (i, 0)),
            pl.BlockSpec((DEC_BATCH, D_MODEL), lambda i, j: (0, 0)),
            pl.BlockSpec((N_UST, TF), lambda i, j: (i, j)),
        ],
        out_shape=[
            jax.ShapeDtypeStruct((SEQ, D_MODEL), F32),
            jax.ShapeDtypeStruct((DEC_BATCH, D_MODEL), F32),
            jax.ShapeDtypeStruct((N_RT * N_UST, D_FF), F32),
        ],
        scratch_shapes=[
            pltpu.VMEM((FH + TM, D_MODEL), BF16),
            pltpu.VMEM((FH + TM, TF), F32),
            pltpu.VMEM((TM, TF), BF16),
        ],
        compiler_params=pltpu.CompilerParams(
            dimension_semantics=("arbitrary", "arbitrary"), vmem_limit_bytes=VMEM_LIMIT),
        name="ffn",
    )(x1, x1, x1s, st, st, gn, gf, wu, wg, cw, cb, wd)


def _gate_weights(wa, wx):
    per = GW // RNN_BLOCK
    eye = jnp.eye(per, dtype=wa.dtype)

    def bd(w):
        w = w.reshape(N_GW, per, RNN_BLOCK, RNN_BLOCK)
        return jnp.einsum('gpcd,pq->gpcqd', w, eye).reshape(N_GW, GW, GW)

    return jnp.concatenate([bd(wa), bd(wx)], axis=-1).astype(BF16)


def kernel(x_prompt, x_sample, state_rnn_conv, state_rnn_h, cache_k_win, cache_v_win, state_ffn_conv,
           meta_tokens, rel_bias_table, norm_mix, w_in, rnn_conv_w, rnn_conv_b, gate_a_w, gate_a_b,
           gate_x_w, gate_x_b, rnn_lambda, attn_sinks, norm_rnn_out, norm_attn_out, w_out, norm_ffn,
           w_up, w_gate, ffn_conv_w, ffn_conv_b, w_down, norm_final):
    l = 0
    xp = x_prompt[0]
    xs = x_sample[:, 0, :]
    xt = jnp.concatenate([xs, jnp.zeros((N_PAD, D_MODEL), F32), meta_tokens], axis=0)
    row = lambda v: v.reshape(1, -1)

    qi = np.arange(BLOCK)[:, None]
    sj = np.arange(2 * BLOCK)[None, :]
    bucket_p = jnp.asarray(_rel_buckets(BLOCK + qi - sj))
    pos = (np.arange(SB * WINDOW * N_KV) >> 1) & (WINDOW - 1)
    bucket_s = jnp.asarray(np.tile(_rel_buckets(WINDOW - pos)[None, :], (SB, 1)))

    wg = _gate_weights(gate_a_w[l], gate_x_w[l])
    seq_w = (rnn_conv_w[l], row(rnn_conv_b[l]), wg, row(gate_a_b[l]), row(gate_x_b[l]), row(rnn_lambda[l]),
             row(norm_rnn_out[l]), row(norm_attn_out[l]))
    w_out_b = w_out[l].astype(BF16)

    x1, zs, h_last, kv_last, xr_tail = _front(
        rel_bias_table, attn_sinks[l], xp, xt, row(norm_mix[l]), w_in[l].astype(BF16), w_out_b, bucket_p, *seq_w)
    x1s, conv_s, h_s, k_s, v_s = _smixer(
        rel_bias_table, attn_sinks[l], zs,
        state_rnn_conv[l].reshape(DEC_BATCH, (RNN_CONV - 1) * D_RNN), state_rnn_h[l],
        cache_k_win[l].reshape(DEC_BATCH * WINDOW * N_KV, HEAD_DIM),
        cache_v_win[l].reshape(DEC_BATCH * WINDOW * N_KV, HEAD_DIM),
        xs, w_out_b, bucket_s, *seq_w)
    y_p, y_s, ust = _ffn(x1, x1s, state_ffn_conv[l].reshape(DEC_BATCH, (FFN_CONV - 1) * D_FF),
                         row(norm_ffn[l]), row(norm_final), w_up[l].astype(BF16), w_gate[l].astype(BF16),
                         ffn_conv_w[l], row(ffn_conv_b[l]), w_down[l].astype(BF16))

    ust = ust[(N_RT - 1) * N_UST:]
    p_states = (
        xr_tail[HALO - (RNN_CONV - 1):HALO][None, None],
        h_last[0:1][None],
        kv_last[:, :D_KV].reshape(1, 1, WINDOW, N_KV, HEAD_DIM),
        kv_last[:, D_KV:].reshape(1, 1, WINDOW, N_KV, HEAD_DIM),
        ust[HALO - (FFN_CONV - 1):HALO][None, None],
    )
    s_states = (
        conv_s.reshape(1, DEC_BATCH, RNN_CONV - 1, D_RNN),
        h_s[None],
        k_s.reshape(1, DEC_BATCH, WINDOW, N_KV, HEAD_DIM),
        v_s.reshape(1, DEC_BATCH, WINDOW, N_KV, HEAD_DIM),
        jnp.stack([state_ffn_conv[l][:, FFN_CONV - 2, :], ust[HALO:]], axis=1)[None],
    )
    return (y_p[None], y_s[:, None, :]) + p_states + s_states
```

```python
import math

import numpy as np
import jax
import jax.numpy as jnp
from jax import lax
from jax.experimental import pallas as pl
from jax.experimental.pallas import tpu as pltpu

F32 = jnp.float32
BF16 = jnp.bfloat16

D_MODEL = 2048
SEQ = 8192
DEC_BATCH = 128
D_RNN = 1024
N_RNN_BLOCKS = 16
RNN_BLOCK = D_RNN // N_RNN_BLOCKS
RNN_CONV = 4
LRU_C = 8.0
N_HEADS = 8
HEAD_DIM = 128
N_KV = 2
GROUP = N_HEADS // N_KV
D_ATTN = N_HEADS * HEAD_DIM
WINDOW = 128
BLOCK = 128
NUM_BUCKETS = 32
MAX_DISTANCE = 128
D_FF = 3 * D_MODEL
FFN_CONV = 3
N_META = 16
EPS = 1e-6
NEG = -1e30
D_KV = N_KV * HEAD_DIM
D_IN = 2 * D_RNN + D_ATTN + 2 * D_KV
SCALE = HEAD_DIM ** -0.5
O_GR = D_RNN
O_Q = 2 * D_RNN
O_KV = 2 * D_RNN + D_ATTN

N_PAD = BLOCK - N_META
R_TAIL = DEC_BATCH + BLOCK
R_ALL = SEQ + R_TAIL
N_PBLK = SEQ // BLOCK

FB = 2 * BLOCK
N_FSTEP = (N_PBLK + 2) // 2
N_SLAB = N_FSTEP - 1

TM = 768
N_RT = R_ALL // TM
LAST_P = SEQ - (N_RT - 1) * TM
TF = 512
N_FT = D_FF // TF
HALO = 8
FH = 16
GW = 256
N_GW = D_RNN // GW
SB = 16
VMEM_LIMIT = 56 * 1024 * 1024


def _rms(x, g):
    return x * lax.rsqrt(jnp.mean(x * x, axis=-1, keepdims=True) + EPS) * g


def _rel_buckets(d):
    d = np.maximum(d, 0)
    exact = NUM_BUCKETS // 2
    ratio = np.maximum(d, 1).astype(np.float32) / np.float32(exact)
    large = exact + (np.log(ratio) / np.float32(math.log(MAX_DISTANCE / exact))
                     * np.float32(NUM_BUCKETS - exact)).astype(np.int32)
    large = np.minimum(large, NUM_BUCKETS - 1)
    return np.where(d < exact, d, large).astype(np.int32)


def _table_lookup(bucket, tab_ref, h):
    out = jnp.zeros(bucket.shape, F32)
    for b in range(NUM_BUCKETS):
        out = jnp.where(bucket == b, tab_ref[b, h], out)
    return out


def _gates(xc, wg_ref, gab, gxb, lam):
    xcb = xc.astype(BF16)
    ga, gx = [], []
    for j in range(N_GW):
        gj = jnp.dot(xcb[:, GW * j:GW * (j + 1)], wg_ref[j], preferred_element_type=F32)
        ga.append(gj[:, :GW])
        gx.append(gj[:, GW:])
    r = jax.nn.sigmoid(jnp.concatenate(ga, axis=1) + gab)
    i = jax.nn.sigmoid(jnp.concatenate(gx, axis=1) + gxb)
    log_a = -LRU_C * r * jax.nn.softplus(-lam)
    a = jnp.exp(log_a)
    t = 1.0 - a * a
    b = jnp.where(t > 0.0, t * lax.rsqrt(t), 0.0) * i * xc
    return a, b


def _front_kernel(tab_ref, sink_ref, xa_ref, xb_ref, xt_ref, gm_ref, win_ref, wout_ref, bucket_ref,
                  cw_ref, cb_ref, wg_ref, gab_ref, gxb_ref, lam_ref, nr_ref, na_ref, wu_f, wgt_f, wd_f,
                  x1_ref, zs_ref, hlast_ref, kvlast_ref, xrt_ref, wu_b, wgt_b, wd_b,
                  zb_e, zb_o, xs_e, xs_o, mix_e, xbuf, h_scr, kvbuf, bias_scr,
                  su, sg, sd, tu, tg, td, sem_in, sem_out):
    k = pl.program_id(0)
    last = N_FSTEP - 1

    stages = ((wu_f, su, tu, wu_b), (wgt_f, sg, tg, wgt_b), (wd_f, sd, td, wd_b))

    def slab_in(s):
        return [pltpu.make_async_copy(w.at[pl.ds(s * st.shape[0], st.shape[0])], st, sem_in.at[n])
                for n, (w, st, _, _) in enumerate(stages)]

    def slab_out(s):
        return [pltpu.make_async_copy(t, w.at[pl.ds(s * t.shape[0], t.shape[0])], sem_out.at[n])
                for n, (_, _, t, w) in enumerate(stages)]

    @pl.when(k == 0)
    def _():
        for c in slab_in(0):
            c.start()

    @pl.when(k >= 1)
    def _():
        for c in slab_out(k - 1):
            c.wait()

    @pl.when(k < N_SLAB)
    def _():
        for c in slab_in(k):
            c.wait()
        for _, st, t, _ in stages:
            t[...] = st[...].astype(BF16)
        for c in slab_out(k):
            c.start()

    @pl.when(k < N_SLAB - 1)
    def _():
        for c in slab_in(k + 1):
            c.start()

    def inproj(x, zb):
        zb[...] = jnp.dot(_rms(x, gm_ref[...]).astype(BF16), win_ref[...], preferred_element_type=F32)

    def outproj(mix, x, rows):
        x1_ref[rows] = x + jnp.dot(mix, wout_ref[...], preferred_element_type=F32)

    @pl.when(k == 0)
    def _():
        prefix = xt_ref[DEC_BATCH:R_TAIL]
        inproj(prefix, zb_o)
        xs_o[...] = prefix
        xs_e[...] = jnp.zeros((BLOCK, D_MODEL), F32)
        mix_e[...] = jnp.zeros((BLOCK, D_MODEL), BF16)
        xbuf[0:HALO] = jnp.zeros((HALO, D_RNN), F32)
        h_scr[...] = jnp.zeros((HALO, D_RNN), F32)
        kvbuf[...] = jnp.zeros((BLOCK, 2 * D_KV), F32)
        bucket = bucket_ref[...]
        for h in range(N_HEADS):
            bias_scr[h] = _table_lookup(bucket, tab_ref, h)

    def mixer(zb, is_prefix, first_key):
        xr = zb[:, 0:D_RNN]
        xbuf[HALO:HALO + BLOCK] = xr
        cw = cw_ref[...]
        xc = cb_ref[...]
        for j in range(RNN_CONV - 1):
            lo = HALO - (RNN_CONV - 1) + j
            xc = xc + xbuf[lo:lo + BLOCK] * cw[j:j + 1]
        xc = xc + xr * cw[RNN_CONV - 1:RNN_CONV]
        xbuf[0:HALO] = xr[BLOCK - HALO:BLOCK]

        a, b = _gates(xc, wg_ref, gab_ref[...], gxb_ref[...], lam_ref[...])
        h = h_scr[0:1]
        if is_prefix is not None:
            row = lax.broadcasted_iota(jnp.int32, (BLOCK, D_RNN), 0)
            b = jnp.where(jnp.logical_and(is_prefix, row < N_PAD), 0.0, b)
            h = jnp.where(is_prefix, 0.0, h)

        ng = BLOCK // HALO
        a3 = a.reshape(ng, HALO, D_RNN)
        b3 = b.reshape(ng, HALO, D_RNN)
        sub = lax.broadcasted_iota(jnp.int32, (ng, HALO, D_RNN), 1)
        sh = 1
        while sh < HALO:
            a_prev = pltpu.roll(a3, sh, 1)
            b_prev = pltpu.roll(b3, sh, 1)
            m = sub >= sh
            b3 = jnp.where(m, a3 * b_prev + b3, b3)
            a3 = jnp.where(m, a3 * a_prev, a3)
            sh *= 2
        hs = []
        for g in range(ng):
            hg = a3[g] * h + b3[g]
            hs.append(hg)
            h = hg[HALO - 1:HALO]
        h_all = jnp.concatenate(hs, axis=0)
        h_scr[...] = jnp.broadcast_to(h, (HALO, D_RNN))
        y_rnn = h_all * jax.nn.gelu(zb[:, O_GR:O_GR + D_RNN])
        mix_a = _rms(y_rnn, nr_ref[...]).astype(BF16)

        q = zb[:, O_Q:O_Q + D_ATTN]
        kv = zb[:, O_KV:O_KV + 2 * D_KV]
        kvp = kvbuf[...]
        kvbuf[...] = kv
        col = lax.broadcasted_iota(jnp.int32, (BLOCK, 2 * BLOCK), 1)
        rowq = lax.broadcasted_iota(jnp.int32, (BLOCK, 2 * BLOCK), 0)
        d = BLOCK + rowq - col
        mask = (d >= 0) & (d < WINDOW) & (col >= first_key)
        outs = []
        for kh in range(N_KV):
            ks = slice(kh * HEAD_DIM, (kh + 1) * HEAD_DIM)
            vs = slice(D_KV + kh * HEAD_DIM, D_KV + (kh + 1) * HEAD_DIM)
            qs = jnp.concatenate(
                [q[:, (kh * GROUP + g) * HEAD_DIM:(kh * GROUP + g + 1) * HEAD_DIM] for g in range(GROUP)],
                axis=0).astype(BF16)
            kk = jnp.concatenate([kvp[:, ks], kv[:, ks]], axis=0).astype(BF16)
            vv = jnp.concatenate([kvp[:, vs], kv[:, vs]], axis=0).astype(BF16)
            sc = lax.dot_general(qs, kk, (((1,), (1,)), ((), ())), preferred_element_type=F32)
            es, dens = [], []
            for g in range(GROUP):
                hh = kh * GROUP + g
                lg = sc[g * BLOCK:(g + 1) * BLOCK] * SCALE + bias_scr[hh]
                lg = jnp.where(mask, lg, NEG)
                sink = sink_ref[hh]
                mx = jnp.maximum(jnp.max(lg, axis=-1, keepdims=True), sink)
                e = jnp.exp(lg - mx)
                dens.append(jnp.sum(e, axis=-1, keepdims=True) + jnp.exp(sink - mx))
                es.append(e.astype(BF16))
            pv = jnp.dot(jnp.concatenate(es, axis=0), vv, preferred_element_type=F32)
            for g in range(GROUP):
                outs.append(pv[g * BLOCK:(g + 1) * BLOCK] / dens[g])
        mix_b = _rms(jnp.concatenate(outs, axis=1), na_ref[...]).astype(BF16)

        return jnp.concatenate([mix_a, mix_b], axis=1), h, kv, xr

    xa = jnp.where(k == last, xt_ref[0:DEC_BATCH], xa_ref[...])
    inproj(xa, zb_e)
    outproj(mix_e[...], xs_e[...], slice(0, BLOCK))
    mix_o, h, kv, xr = mixer(zb_o, k == 0, jnp.where(k == 0, BLOCK + N_PAD, 0))
    hlast_ref[...] = jnp.broadcast_to(h, (HALO, D_RNN))
    kvlast_ref[...] = kv
    xrt_ref[...] = xr[BLOCK - HALO:BLOCK]

    xb = xb_ref[...]
    inproj(xb, zb_o)
    outproj(mix_o, xs_o[...], slice(BLOCK, FB))
    mix_e[...] = mixer(zb_e, None, jnp.where(k == 0, N_PAD, 0))[0]
    xs_e[...] = xa
    xs_o[...] = xb

    @pl.when(k == last)
    def _():
        zs_ref[...] = zb_e[...]


def _front(tab, sinks, xp, xt, gm, w_in, w_out, bucket, cw, cb, wg, gab, gxb, lam, nr, na, w_up, w_gate, w_down):
    ffn_w = (w_up, w_gate, w_down)
    slab = lambda w: (w.shape[0] // N_SLAB, w.shape[1])
    hbm = pl.BlockSpec(memory_space=pl.ANY)
    vec = lambda n: pl.BlockSpec((1, n), lambda k: (0, 0))
    smem = pl.BlockSpec(memory_space=pltpu.SMEM)
    once = lambda shape: pl.BlockSpec(shape, lambda k: (0,) * len(shape), pipeline_mode=pl.Buffered(1))
    return pl.pallas_call(
        _front_kernel,
        grid=(N_FSTEP,),
        in_specs=[
            smem, smem,
            pl.BlockSpec((BLOCK, D_MODEL), lambda k: (jnp.minimum(2 * k, N_PBLK - 1), 0)),
            pl.BlockSpec((BLOCK, D_MODEL), lambda k: (jnp.minimum(2 * k + 1, N_PBLK - 1), 0)),
            once((R_TAIL, D_MODEL)),
            vec(D_MODEL),
            once((D_MODEL, D_IN)),
            once((D_MODEL, D_MODEL)),
            pl.BlockSpec((BLOCK, 2 * BLOCK), lambda k: (0, 0)),
            pl.BlockSpec((RNN_CONV, D_RNN), lambda k: (0, 0)),
            vec(D_RNN),
            pl.BlockSpec((N_GW, GW, 2 * GW), lambda k: (0, 0, 0)),
            vec(D_RNN), vec(D_RNN), vec(D_RNN), vec(D_RNN), vec(D_ATTN),
            hbm, hbm, hbm,
        ],
        out_specs=[
            pl.BlockSpec((FB, D_MODEL), lambda k: (jnp.where(k == 0, SEQ // FB, k - 1), 0)),
            pl.BlockSpec((BLOCK, D_IN), lambda k: (0, 0)),
            pl.BlockSpec((HALO, D_RNN), lambda k: (0, 0)),
            pl.BlockSpec((BLOCK, 2 * D_KV), lambda k: (0, 0)),
            pl.BlockSpec((HALO, D_RNN), lambda k: (0, 0)),
            hbm, hbm, hbm,
        ],
        out_shape=[
            jax.ShapeDtypeStruct((R_ALL, D_MODEL), F32),
            jax.ShapeDtypeStruct((DEC_BATCH, D_IN), F32),
            jax.ShapeDtypeStruct((HALO, D_RNN), F32),
            jax.ShapeDtypeStruct((BLOCK, 2 * D_KV), F32),
            jax.ShapeDtypeStruct((HALO, D_RNN), F32),
        ] + [jax.ShapeDtypeStruct(w.shape, BF16) for w in ffn_w],
        scratch_shapes=[
            pltpu.VMEM((BLOCK, D_IN), F32),
            pltpu.VMEM((BLOCK, D_IN), F32),
            pltpu.VMEM((BLOCK, D_MODEL), F32),
            pltpu.VMEM((BLOCK, D_MODEL), F32),
            pltpu.VMEM((BLOCK, D_MODEL), BF16),
            pltpu.VMEM((HALO + BLOCK, D_RNN), F32),
            pltpu.VMEM((HALO, D_RNN), F32),
            pltpu.VMEM((BLOCK, 2 * D_KV), F32),
            pltpu.VMEM((N_HEADS, BLOCK, 2 * BLOCK), F32),
        ] + [pltpu.VMEM(slab(w), F32) for w in ffn_w] + [pltpu.VMEM(slab(w), BF16) for w in ffn_w] + [
            pltpu.SemaphoreType.DMA((len(ffn_w),)),
            pltpu.SemaphoreType.DMA((len(ffn_w),)),
        ],
        compiler_params=pltpu.CompilerParams(
            dimension_semantics=("arbitrary",), vmem_limit_bytes=VMEM_LIMIT),
        name="front",
    )(tab, sinks, xp, xp, xt, gm, w_in, w_out, bucket, cw, cb, wg, gab, gxb, lam, nr, na, *ffn_w)


def _smixer_kernel(tab_ref, sink_ref, z_ref, st_ref, h0_ref, kc_ref, vc_ref, xs_ref, wout_ref,
                   bucket_ref, cw_ref, cb_ref, wg_ref, gab_ref, gxb_ref, lam_ref, nr_ref, na_ref,
                   x1s_ref, conv_ref, h_ref, ko_ref, vo_ref, bias_scr, mix_scr):
    c = pl.program_id(0)
    nrow = N_HEADS * SB
    ncol = SB * WINDOW * N_KV

    @pl.when(c == 0)
    def _():
        bucket = bucket_ref[...]
        for h in range(N_HEADS):
            bias_scr[h * SB:(h + 1) * SB] = _table_lookup(bucket, tab_ref, h)

    r0 = pl.multiple_of(c * SB, SB)

    xr = z_ref[:, 0:D_RNN]
    st = st_ref[...]
    cw = cw_ref[...]
    xc = cb_ref[...]
    for j in range(RNN_CONV - 1):
        xc = xc + st[:, j * D_RNN:(j + 1) * D_RNN] * cw[j:j + 1]
    xc = xc + xr * cw[RNN_CONV - 1:RNN_CONV]
    conv_ref[:, 0:(RNN_CONV - 2) * D_RNN] = st[:, D_RNN:]
    conv_ref[:, (RNN_CONV - 2) * D_RNN:] = xr
    a, b = _gates(xc, wg_ref, gab_ref[...], gxb_ref[...], lam_ref[...])
    h = a * h0_ref[...] + b
    h_ref[...] = h
    y_rnn = h * jax.nn.gelu(z_ref[:, O_GR:O_GR + D_RNN])
    mix_scr[pl.ds(r0, SB), 0:D_RNN] = _rms(y_rnn, nr_ref[...]).astype(BF16)

    q = z_ref[:, O_Q:O_Q + D_ATTN]
    kv = z_ref[:, O_KV:O_KV + 2 * D_KV]
    col = lax.broadcasted_iota(jnp.int32, (nrow, ncol), 1)
    row = lax.broadcasted_iota(jnp.int32, (nrow, ncol), 0)
    pos = (col >> 1) & (WINDOW - 1)
    valid = ((col >> 8) == (row & (SB - 1))) & ((col & 1) == (row >> 6)) & (pos >= 1)
    qs = jnp.concatenate([q[:, h * HEAD_DIM:(h + 1) * HEAD_DIM] for h in range(N_HEADS)], axis=0).astype(BF16)
    new_rows = lambda off: jnp.concatenate(
        [kv[:, off + (h // GROUP) * HEAD_DIM:off + (h // GROUP + 1) * HEAD_DIM] for h in range(N_HEADS)],
        axis=0).astype(BF16).astype(F32)
    k_new = new_rows(0)
    v_new = new_rows(D_KV)
    sc = lax.dot_general(qs, kc_ref[...].astype(BF16), (((1,), (1,)), ((), ())), preferred_element_type=F32)
    lg = jnp.where(valid, sc * SCALE + bias_scr[...], NEG)
    rh = lax.broadcasted_iota(jnp.int32, (nrow, 1), 0) >> 4
    sink = jnp.zeros((nrow, 1), F32)
    bias_new = jnp.zeros((nrow, 1), F32)
    for h in range(N_HEADS):
        sink = jnp.where(rh == h, sink_ref[h], sink)
        bias_new = jnp.where(rh == h, tab_ref[0, h], bias_new)
    lg_new = jnp.sum(qs.astype(F32) * k_new, axis=-1, keepdims=True) * SCALE + bias_new
    mx = jnp.maximum(jnp.maximum(jnp.max(lg, axis=-1, keepdims=True), lg_new), sink)
    e = jnp.exp(lg - mx)
    e_new = jnp.exp(lg_new - mx)
    den = jnp.sum(e, axis=-1, keepdims=True) + e_new + jnp.exp(sink - mx)
    pv = jnp.dot(e.astype(BF16), vc_ref[...].astype(BF16), preferred_element_type=F32)
    o = (pv + e_new.astype(BF16).astype(F32) * v_new) / den
    y_attn = jnp.concatenate([o[h * SB:(h + 1) * SB] for h in range(N_HEADS)], axis=1)
    mix_scr[pl.ds(r0, SB), D_RNN:] = _rms(y_attn, na_ref[...]).astype(BF16)

    per = WINDOW * N_KV
    slide = lambda ref: pltpu.roll(ref[...].reshape(SB, per, HEAD_DIM), per - N_KV, 1).reshape(SB * per, HEAD_DIM)
    ko_ref[...] = slide(kc_ref)
    vo_ref[...] = slide(vc_ref)
    for bi in range(SB):
        for kh in range(N_KV):
            r = (bi + 1) * per - N_KV + kh
            ko_ref[r:r + 1, :] = kv[bi:bi + 1, kh * HEAD_DIM:(kh + 1) * HEAD_DIM]
            vo_ref[r:r + 1, :] = kv[bi:bi + 1, D_KV + kh * HEAD_DIM:D_KV + (kh + 1) * HEAD_DIM]

    @pl.when(c == DEC_BATCH // SB - 1)
    def _():
        x1s_ref[...] = xs_ref[...] + jnp.dot(mix_scr[...], wout_ref[...], preferred_element_type=F32)


def _smixer(tab, sinks, zs, st, h0, kc, vc, xs, w_out, bucket, cw, cb, wg, gab, gxb, lam, nr, na):
    vec = lambda n: pl.BlockSpec((1, n), lambda c: (0, 0))
    smem = pl.BlockSpec(memory_space=pltpu.SMEM)
    cache = pl.BlockSpec((SB * WINDOW * N_KV, HEAD_DIM), lambda c: (c, 0))
    return pl.pallas_call(
        _smixer_kernel,
        grid=(DEC_BATCH // SB,),
        in_specs=[
            smem, smem,
            pl.BlockSpec((SB, D_IN), lambda c: (c, 0)),
            pl.BlockSpec((SB, (RNN_CONV - 1) * D_RNN), lambda c: (c, 0)),
            pl.BlockSpec((SB, D_RNN), lambda c: (c, 0)),
            cache, cache,
            pl.BlockSpec((DEC_BATCH, D_MODEL), lambda c: (0, 0)),
            pl.BlockSpec((D_MODEL, D_MODEL), lambda c: (0, 0), pipeline_mode=pl.Buffered(1)),
            pl.BlockSpec((SB, SB * WINDOW * N_KV), lambda c: (0, 0)),
            pl.BlockSpec((RNN_CONV, D_RNN), lambda c: (0, 0)),
            vec(D_RNN),
            pl.BlockSpec((N_GW, GW, 2 * GW), lambda c: (0, 0, 0)),
            vec(D_RNN), vec(D_RNN), vec(D_RNN), vec(D_RNN), vec(D_ATTN),
        ],
        out_specs=[
            pl.BlockSpec((DEC_BATCH, D_MODEL), lambda c: (0, 0)),
            pl.BlockSpec((SB, (RNN_CONV - 1) * D_RNN), lambda c: (c, 0)),
            pl.BlockSpec((SB, D_RNN), lambda c: (c, 0)),
            cache, cache,
        ],
        out_shape=[
            jax.ShapeDtypeStruct((DEC_BATCH, D_MODEL), F32),
            jax.ShapeDtypeStruct((DEC_BATCH, (RNN_CONV - 1) * D_RNN), F32),
            jax.ShapeDtypeStruct((DEC_BATCH, D_RNN), F32),
            jax.ShapeDtypeStruct((DEC_BATCH * WINDOW * N_KV, HEAD_DIM), F32),
            jax.ShapeDtypeStruct((DEC_BATCH * WINDOW * N_KV, HEAD_DIM), F32),
        ],
        scratch_shapes=[
            pltpu.VMEM((N_HEADS * SB, SB * WINDOW * N_KV), F32),
            pltpu.VMEM((DEC_BATCH, D_MODEL), BF16),
        ],
        compiler_params=pltpu.CompilerParams(
            dimension_semantics=("arbitrary",), vmem_limit_bytes=VMEM_LIMIT),
        name="smixer",
    )(tab, sinks, zs, st, h0, kc, vc, xs, w_out, bucket, cw, cb, wg, gab, gxb, lam, nr, na)


U_LO = LAST_P - HALO
U_HI = LAST_P + DEC_BATCH
N_UST = U_HI - U_LO


def _ffn_kernel(x1_ref, halo_ref, x1s_ref, st0_ref, st1_ref, gn_ref, gf_ref, wu_ref, wg_ref, cw_ref, cb_ref,
                wd_ref, y_ref, ys_ref, ust_ref, h2_scr, ubuf, abuf):
    i = pl.program_id(0)
    j = pl.program_id(1)

    @pl.when(j == 0)
    def _():
        h2_scr[0:FH] = _rms(halo_ref[...], gn_ref[...]).astype(BF16)
        h2_scr[FH:FH + TM] = _rms(x1_ref[...], gn_ref[...]).astype(BF16)
        y_ref[...] = jnp.zeros((TM, D_MODEL), F32)

        @pl.when(i == N_RT - 1)
        def _():
            h2_scr[FH + LAST_P:FH + U_HI] = _rms(x1s_ref[...], gn_ref[...]).astype(BF16)

    ubuf[...] = jnp.dot(h2_scr[...], wu_ref[...], preferred_element_type=F32)
    gate = jnp.dot(h2_scr[FH:FH + TM], wg_ref[...], preferred_element_type=F32)
    cw = cw_ref[...]
    cb = cb_ref[...]
    u = ubuf[FH:FH + TM]
    uc = cb + ubuf[FH - 2:FH - 2 + TM] * cw[0:1] + ubuf[FH - 1:FH - 1 + TM] * cw[1:2] + u * cw[2:3]
    abuf[...] = (jax.nn.gelu(uc) * gate).astype(BF16)
    ust_ref[...] = u[U_LO:U_HI]

    @pl.when(i == N_RT - 1)
    def _():
        us = u[LAST_P:U_HI]
        ucs = cb + st0_ref[...] * cw[0:1] + st1_ref[...] * cw[1:2] + us * cw[2:3]
        abuf[LAST_P:U_HI] = (jax.nn.gelu(ucs) * gate[LAST_P:U_HI]).astype(BF16)

    y_ref[...] += jnp.dot(abuf[...], wd_ref[...], preferred_element_type=F32)

    @pl.when(j == N_FT - 1)
    def _():
        @pl.when(i == N_RT - 1)
        def _():
            ys_ref[...] = _rms(x1s_ref[...] + y_ref[LAST_P:U_HI], gf_ref[...])

        y_ref[...] = _rms(x1_ref[...] + y_ref[...], gf_ref[...])


def _ffn(x1, x1s, st, gn, gf, wu, wg, cw, cb, wd):
    def halo_idx(i, j):
        return (jnp.where(i == 0, R_ALL // FH - 1, i * (TM // FH) - 1), 0)

    return pl.pallas_call(
        _ffn_kernel,
        grid=(N_RT, N_FT),
        in_specs=[
            pl.BlockSpec((TM, D_MODEL), lambda i, j: (i, 0)),
            pl.BlockSpec((FH, D_MODEL), halo_idx),
            pl.BlockSpec((DEC_BATCH, D_MODEL), lambda i, j: (0, 0)),
            pl.BlockSpec((DEC_BATCH, TF), lambda i, j: (0, j)),
            pl.BlockSpec((DEC_BATCH, TF), lambda i, j: (0, N_FT + j)),
            pl.BlockSpec((1, D_MODEL), lambda i, j: (0, 0)),
            pl.BlockSpec((1, D_MODEL), lambda i, j: (0, 0)),
            pl.BlockSpec((D_MODEL, TF), lambda i, j: (0, j)),
            pl.BlockSpec((D_MODEL, TF), lambda i, j: (0, j)),
            pl.BlockSpec((FFN_CONV, TF), lambda i, j: (0, j)),
            pl.BlockSpec((1, TF), lambda i, j: (0, j)),
            pl.BlockSpec((TF, D_MODEL), lambda i, j: (j, 0)),
        ],
        out_specs=[
            pl.BlockSpec((TM, D_MODEL), lambda i, j: (i, 0)),
            pl.BlockSpec((DEC_BATCH, D_MODEL), lambda i, j: (0, 0)),
            pl.BlockSpec((N_UST, TF), lambda i, j: (i, j)),
        ],
        out_shape=[
            jax.ShapeDtypeStruct((SEQ, D_MODEL), F32),
            jax.ShapeDtypeStruct((DEC_BATCH, D_MODEL), F32),
            jax.ShapeDtypeStruct((N_RT * N_UST, D_FF), F32),
        ],
        scratch_shapes=[
            pltpu.VMEM((FH + TM, D_MODEL), BF16),
            pltpu.VMEM((FH + TM, TF), F32),
            pltpu.VMEM((TM, TF), BF16),
        ],
        compiler_params=pltpu.CompilerParams(
            dimension_semantics=("arbitrary", "arbitrary"), vmem_limit_bytes=VMEM_LIMIT),
        name="ffn",
    )(x1, x1, x1s, st, st, gn, gf, wu, wg, cw, cb, wd)


def _gate_weights(wa, wx):
    per = GW // RNN_BLOCK
    eye = jnp.eye(per, dtype=wa.dtype)

    def bd(w):
        w = w.reshape(N_GW, per, RNN_BLOCK, RNN_BLOCK)
        return jnp.einsum('gpcd,pq->gpcqd', w, eye).reshape(N_GW, GW, GW)

    return jnp.concatenate([bd(wa), bd(wx)], axis=-1).astype(BF16)


def kernel(x_prompt, x_sample, state_rnn_conv, state_rnn_h, cache_k_win, cache_v_win, state_ffn_conv,
           meta_tokens, rel_bias_table, norm_mix, w_in, rnn_conv_w, rnn_conv_b, gate_a_w, gate_a_b,
           gate_x_w, gate_x_b, rnn_lambda, attn_sinks, norm_rnn_out, norm_attn_out, w_out, norm_ffn,
           w_up, w_gate, ffn_conv_w, ffn_conv_b, w_down, norm_final):
    l = 0
    xp = x_prompt[0]
    xs = x_sample[:, 0, :]
    xt = jnp.concatenate([xs, jnp.zeros((N_PAD, D_MODEL), F32), meta_tokens], axis=0)
    row = lambda v: v.reshape(1, -1)

    qi = np.arange(BLOCK)[:, None]
    sj = np.arange(2 * BLOCK)[None, :]
    bucket_p = jnp.asarray(_rel_buckets(BLOCK + qi - sj))
    pos = (np.arange(SB * WINDOW * N_KV) >> 1) & (WINDOW - 1)
    bucket_s = jnp.asarray(np.tile(_rel_buckets(WINDOW - pos)[None, :], (SB, 1)))

    wg = _gate_weights(gate_a_w[l], gate_x_w[l])
    seq_w = (rnn_conv_w[l], row(rnn_conv_b[l]), wg, row(gate_a_b[l]), row(gate_x_b[l]), row(rnn_lambda[l]),
             row(norm_rnn_out[l]), row(norm_attn_out[l]))
    w_out_b = w_out[l].astype(BF16)

    x1, zs, h_last, kv_last, xr_tail, w_up_b, w_gate_b, w_down_b = _front(
        rel_bias_table, attn_sinks[l], xp, xt, row(norm_mix[l]), w_in[l].astype(BF16), w_out_b, bucket_p, *seq_w,
        w_up[l], w_gate[l], w_down[l])
    x1s, conv_s, h_s, k_s, v_s = _smixer(
        rel_bias_table, attn_sinks[l], zs,
        state_rnn_conv[l].reshape(DEC_BATCH, (RNN_CONV - 1) * D_RNN), state_rnn_h[l],
        cache_k_win[l].reshape(DEC_BATCH * WINDOW * N_KV, HEAD_DIM),
        cache_v_win[l].reshape(DEC_BATCH * WINDOW * N_KV, HEAD_DIM),
        xs, w_out_b, bucket_s, *seq_w)
    y_p, y_s, ust = _ffn(x1, x1s, state_ffn_conv[l].reshape(DEC_BATCH, (FFN_CONV - 1) * D_FF),
                         row(norm_ffn[l]), row(norm_final), w_up_b, w_gate_b,
                         ffn_conv_w[l], row(ffn_conv_b[l]), w_down_b)

    ust = ust[(N_RT - 1) * N_UST:]
    p_states = (
        xr_tail[HALO - (RNN_CONV - 1):HALO][None, None],
        h_last[0:1][None],
        kv_last[:, :D_KV].reshape(1, 1, WINDOW, N_KV, HEAD_DIM),
        kv_last[:, D_KV:].reshape(1, 1, WINDOW, N_KV, HEAD_DIM),
        ust[HALO - (FFN_CONV - 1):HALO][None, None],
    )
    s_states = (
        conv_s.reshape(1, DEC_BATCH, RNN_CONV - 1, D_RNN),
        h_s[None],
        k_s.reshape(1, DEC_BATCH, WINDOW, N_KV, HEAD_DIM),
        v_s.reshape(1, DEC_BATCH, WINDOW, N_KV, HEAD_DIM),
        jnp.stack([state_ffn_conv[l][:, FFN_CONV - 2, :], ust[HALO:]], axis=1)[None],
    )
    return (y_p[None], y_s[:, None, :]) + p_states + s_states
```

```python
import math

import numpy as np
import jax
import jax.numpy as jnp
from jax import lax
from jax.experimental import pallas as pl
from jax.experimental.pallas import tpu as pltpu

F32 = jnp.float32
BF16 = jnp.bfloat16

D_MODEL = 2048
SEQ = 8192
DEC_BATCH = 128
D_RNN = 1024
N_RNN_BLOCKS = 16
RNN_BLOCK = D_RNN // N_RNN_BLOCKS
RNN_CONV = 4
LRU_C = 8.0
N_HEADS = 8
HEAD_DIM = 128
N_KV = 2
GROUP = N_HEADS // N_KV
D_ATTN = N_HEADS * HEAD_DIM
WINDOW = 128
BLOCK = 128
NUM_BUCKETS = 32
MAX_DISTANCE = 128
D_FF = 3 * D_MODEL
FFN_CONV = 3
N_META = 16
EPS = 1e-6
NEG = -1e30
D_KV = N_KV * HEAD_DIM
D_IN = 2 * D_RNN + D_ATTN + 2 * D_KV
SCALE = HEAD_DIM ** -0.5
O_GR = D_RNN
O_Q = 2 * D_RNN
O_KV = 2 * D_RNN + D_ATTN

N_PAD = BLOCK - N_META
R_TAIL = DEC_BATCH + BLOCK
R_ALL = SEQ + R_TAIL
N_PBLK = SEQ // BLOCK

FB = 2 * BLOCK
N_FSTEP = (N_PBLK + 2) // 2
N_SLAB = N_FSTEP - 1

TM = 768
N_RT = R_ALL // TM
LAST_P = SEQ - (N_RT - 1) * TM
TF = 512
N_FT = D_FF // TF
N_CH = 2
HALO = 8
FH = 16
GW = 256
N_GW = D_RNN // GW
SB = 16
VMEM_LIMIT = 56 * 1024 * 1024


def _rms(x, g):
    return x * lax.rsqrt(jnp.mean(x * x, axis=-1, keepdims=True) + EPS) * g


def _rel_buckets(d):
    d = np.maximum(d, 0)
    exact = NUM_BUCKETS // 2
    ratio = np.maximum(d, 1).astype(np.float32) / np.float32(exact)
    large = exact + (np.log(ratio) / np.float32(math.log(MAX_DISTANCE / exact))
                     * np.float32(NUM_BUCKETS - exact)).astype(np.int32)
    large = np.minimum(large, NUM_BUCKETS - 1)
    return np.where(d < exact, d, large).astype(np.int32)


def _table_lookup(bucket, tab_ref, h):
    out = jnp.zeros(bucket.shape, F32)
    for b in range(NUM_BUCKETS):
        out = jnp.where(bucket == b, tab_ref[b, h], out)
    return out


def _gates(xc, wg_ref, gab, gxb, lam):
    xcb = xc.astype(BF16)
    ga, gx = [], []
    for j in range(N_GW):
        gj = jnp.dot(xcb[:, GW * j:GW * (j + 1)], wg_ref[j], preferred_element_type=F32)
        ga.append(gj[:, :GW])
        gx.append(gj[:, GW:])
    r = jax.nn.sigmoid(jnp.concatenate(ga, axis=1) + gab)
    i = jax.nn.sigmoid(jnp.concatenate(gx, axis=1) + gxb)
    log_a = -LRU_C * r * jax.nn.softplus(-lam)
    a = jnp.exp(log_a)
    t = 1.0 - a * a
    b = jnp.where(t > 0.0, t * lax.rsqrt(t), 0.0) * i * xc
    return a, b


def _front_kernel(tab_ref, sink_ref, xa_ref, xb_ref, xt_ref, gm_ref, win_ref, wout_ref, bucket_ref,
                  cw_ref, cb_ref, wg_ref, gab_ref, gxb_ref, lam_ref, nr_ref, na_ref, wu_f, wgt_f, wd_f,
                  x1_ref, zs_ref, hlast_ref, kvlast_ref, xrt_ref, wu_b, wgt_b, wd_b,
                  zb_e, zb_o, xs_e, xs_o, mix_e, xbuf, h_scr, kvbuf, bias_scr,
                  su, sg, sd, tu, tg, td, sem_in, sem_out):
    k = pl.program_id(0)
    last = N_FSTEP - 1

    stages = ((wu_f, su, tu, wu_b), (wgt_f, sg, tg, wgt_b), (wd_f, sd, td, wd_b))

    def slab_in(s):
        return [pltpu.make_async_copy(w.at[pl.ds(s * st.shape[0], st.shape[0])], st, sem_in.at[n])
                for n, (w, st, _, _) in enumerate(stages)]

    def slab_out(s):
        return [pltpu.make_async_copy(t, w.at[pl.ds(s * t.shape[0], t.shape[0])], sem_out.at[n])
                for n, (_, _, t, w) in enumerate(stages)]

    @pl.when(k == 0)
    def _():
        for c in slab_in(0):
            c.start()

    @pl.when(k >= 1)
    def _():
        for c in slab_out(k - 1):
            c.wait()

    @pl.when(k < N_SLAB)
    def _():
        for c in slab_in(k):
            c.wait()
        for _, st, t, _ in stages:
            t[...] = st[...].astype(BF16)
        for c in slab_out(k):
            c.start()

    @pl.when(k < N_SLAB - 1)
    def _():
        for c in slab_in(k + 1):
            c.start()

    def inproj(x, zb):
        zb[...] = jnp.dot(_rms(x, gm_ref[...]).astype(BF16), win_ref[...], preferred_element_type=F32)

    def outproj(mix, x, rows):
        x1_ref[rows] = x + jnp.dot(mix, wout_ref[...], preferred_element_type=F32)

    @pl.when(k == 0)
    def _():
        prefix = xt_ref[DEC_BATCH:R_TAIL]
        inproj(prefix, zb_o)
        xs_o[...] = prefix
        xs_e[...] = jnp.zeros((BLOCK, D_MODEL), F32)
        mix_e[...] = jnp.zeros((BLOCK, D_MODEL), BF16)
        xbuf[0:HALO] = jnp.zeros((HALO, D_RNN), F32)
        h_scr[...] = jnp.zeros((HALO, D_RNN), F32)
        kvbuf[...] = jnp.zeros((BLOCK, 2 * D_KV), F32)
        bucket = bucket_ref[...]
        for h in range(N_HEADS):
            bias_scr[h] = _table_lookup(bucket, tab_ref, h)

    def mixer(zb, is_prefix, first_key):
        xr = zb[:, 0:D_RNN]
        xbuf[HALO:HALO + BLOCK] = xr
        cw = cw_ref[...]
        xc = cb_ref[...]
        for j in range(RNN_CONV - 1):
            lo = HALO - (RNN_CONV - 1) + j
            xc = xc + xbuf[lo:lo + BLOCK] * cw[j:j + 1]
        xc = xc + xr * cw[RNN_CONV - 1:RNN_CONV]
        xbuf[0:HALO] = xr[BLOCK - HALO:BLOCK]

        a, b = _gates(xc, wg_ref, gab_ref[...], gxb_ref[...], lam_ref[...])
        h = h_scr[0:1]
        if is_prefix is not None:
            row = lax.broadcasted_iota(jnp.int32, (BLOCK, D_RNN), 0)
            b = jnp.where(jnp.logical_and(is_prefix, row < N_PAD), 0.0, b)
            h = jnp.where(is_prefix, 0.0, h)

        ng = BLOCK // HALO
        a3 = a.reshape(ng, HALO, D_RNN)
        b3 = b.reshape(ng, HALO, D_RNN)
        sub = lax.broadcasted_iota(jnp.int32, (ng, HALO, D_RNN), 1)
        sh = 1
        while sh < HALO:
            a_prev = pltpu.roll(a3, sh, 1)
            b_prev = pltpu.roll(b3, sh, 1)
            m = sub >= sh
            b3 = jnp.where(m, a3 * b_prev + b3, b3)
            a3 = jnp.where(m, a3 * a_prev, a3)
            sh *= 2
        hs = []
        for g in range(ng):
            hg = a3[g] * h + b3[g]
            hs.append(hg)
            h = hg[HALO - 1:HALO]
        h_all = jnp.concatenate(hs, axis=0)
        h_scr[...] = jnp.broadcast_to(h, (HALO, D_RNN))
        y_rnn = h_all * jax.nn.gelu(zb[:, O_GR:O_GR + D_RNN])
        mix_a = _rms(y_rnn, nr_ref[...]).astype(BF16)

        q = zb[:, O_Q:O_Q + D_ATTN]
        kv = zb[:, O_KV:O_KV + 2 * D_KV]
        kvp = kvbuf[...]
        kvbuf[...] = kv
        col = lax.broadcasted_iota(jnp.int32, (BLOCK, 2 * BLOCK), 1)
        rowq = lax.broadcasted_iota(jnp.int32, (BLOCK, 2 * BLOCK), 0)
        d = BLOCK + rowq - col
        mask = (d >= 0) & (d < WINDOW) & (col >= first_key)
        outs = []
        for kh in range(N_KV):
            ks = slice(kh * HEAD_DIM, (kh + 1) * HEAD_DIM)
            vs = slice(D_KV + kh * HEAD_DIM, D_KV + (kh + 1) * HEAD_DIM)
            qs = jnp.concatenate(
                [q[:, (kh * GROUP + g) * HEAD_DIM:(kh * GROUP + g + 1) * HEAD_DIM] for g in range(GROUP)],
                axis=0).astype(BF16)
            kk = jnp.concatenate([kvp[:, ks], kv[:, ks]], axis=0).astype(BF16)
            vv = jnp.concatenate([kvp[:, vs], kv[:, vs]], axis=0).astype(BF16)
            sc = lax.dot_general(qs, kk, (((1,), (1,)), ((), ())), preferred_element_type=F32)
            es, dens = [], []
            for g in range(GROUP):
                hh = kh * GROUP + g
                lg = sc[g * BLOCK:(g + 1) * BLOCK] * SCALE + bias_scr[hh]
                lg = jnp.where(mask, lg, NEG)
                sink = sink_ref[hh]
                mx = jnp.maximum(jnp.max(lg, axis=-1, keepdims=True), sink)
                e = jnp.exp(lg - mx)
                dens.append(jnp.sum(e, axis=-1, keepdims=True) + jnp.exp(sink - mx))
                es.append(e.astype(BF16))
            pv = jnp.dot(jnp.concatenate(es, axis=0), vv, preferred_element_type=F32)
            for g in range(GROUP):
                outs.append(pv[g * BLOCK:(g + 1) * BLOCK] / dens[g])
        mix_b = _rms(jnp.concatenate(outs, axis=1), na_ref[...]).astype(BF16)

        return jnp.concatenate([mix_a, mix_b], axis=1), h, kv, xr

    xa = jnp.where(k == last, xt_ref[0:DEC_BATCH], xa_ref[...])
    inproj(xa, zb_e)
    outproj(mix_e[...], xs_e[...], slice(0, BLOCK))
    mix_o, h, kv, xr = mixer(zb_o, k == 0, jnp.where(k == 0, BLOCK + N_PAD, 0))
    hlast_ref[...] = jnp.broadcast_to(h, (HALO, D_RNN))
    kvlast_ref[...] = kv
    xrt_ref[...] = xr[BLOCK - HALO:BLOCK]

    xb = xb_ref[...]
    inproj(xb, zb_o)
    outproj(mix_o, xs_o[...], slice(BLOCK, FB))
    mix_e[...] = mixer(zb_e, None, jnp.where(k == 0, N_PAD, 0))[0]
    xs_e[...] = xa
    xs_o[...] = xb

    @pl.when(k == last)
    def _():
        zs_ref[...] = zb_e[...]


def _front(tab, sinks, xp, xt, gm, w_in, w_out, bucket, cw, cb, wg, gab, gxb, lam, nr, na, w_up, w_gate, w_down):
    ffn_w = (w_up, w_gate, w_down)
    slab = lambda w: (w.shape[0] // N_SLAB, w.shape[1])
    hbm = pl.BlockSpec(memory_space=pl.ANY)
    vec = lambda n: pl.BlockSpec((1, n), lambda k: (0, 0))
    smem = pl.BlockSpec(memory_space=pltpu.SMEM)
    once = lambda shape: pl.BlockSpec(shape, lambda k: (0,) * len(shape), pipeline_mode=pl.Buffered(1))
    return pl.pallas_call(
        _front_kernel,
        grid=(N_FSTEP,),
        in_specs=[
            smem, smem,
            pl.BlockSpec((BLOCK, D_MODEL), lambda k: (jnp.minimum(2 * k, N_PBLK - 1), 0)),
            pl.BlockSpec((BLOCK, D_MODEL), lambda k: (jnp.minimum(2 * k + 1, N_PBLK - 1), 0)),
            once((R_TAIL, D_MODEL)),
            vec(D_MODEL),
            once((D_MODEL, D_IN)),
            once((D_MODEL, D_MODEL)),
            pl.BlockSpec((BLOCK, 2 * BLOCK), lambda k: (0, 0)),
            pl.BlockSpec((RNN_CONV, D_RNN), lambda k: (0, 0)),
            vec(D_RNN),
            pl.BlockSpec((N_GW, GW, 2 * GW), lambda k: (0, 0, 0)),
            vec(D_RNN), vec(D_RNN), vec(D_RNN), vec(D_RNN), vec(D_ATTN),
            hbm, hbm, hbm,
        ],
        out_specs=[
            pl.BlockSpec((FB, D_MODEL), lambda k: (jnp.where(k == 0, SEQ // FB, k - 1), 0)),
            pl.BlockSpec((BLOCK, D_IN), lambda k: (0, 0)),
            pl.BlockSpec((HALO, D_RNN), lambda k: (0, 0)),
            pl.BlockSpec((BLOCK, 2 * D_KV), lambda k: (0, 0)),
            pl.BlockSpec((HALO, D_RNN), lambda k: (0, 0)),
            hbm, hbm, hbm,
        ],
        out_shape=[
            jax.ShapeDtypeStruct((R_ALL, D_MODEL), F32),
            jax.ShapeDtypeStruct((DEC_BATCH, D_IN), F32),
            jax.ShapeDtypeStruct((HALO, D_RNN), F32),
            jax.ShapeDtypeStruct((BLOCK, 2 * D_KV), F32),
            jax.ShapeDtypeStruct((HALO, D_RNN), F32),
        ] + [jax.ShapeDtypeStruct(w.shape, BF16) for w in ffn_w],
        scratch_shapes=[
            pltpu.VMEM((BLOCK, D_IN), F32),
            pltpu.VMEM((BLOCK, D_IN), F32),
            pltpu.VMEM((BLOCK, D_MODEL), F32),
            pltpu.VMEM((BLOCK, D_MODEL), F32),
            pltpu.VMEM((BLOCK, D_MODEL), BF16),
            pltpu.VMEM((HALO + BLOCK, D_RNN), F32),
            pltpu.VMEM((HALO, D_RNN), F32),
            pltpu.VMEM((BLOCK, 2 * D_KV), F32),
            pltpu.VMEM((N_HEADS, BLOCK, 2 * BLOCK), F32),
        ] + [pltpu.VMEM(slab(w), F32) for w in ffn_w] + [pltpu.VMEM(slab(w), BF16) for w in ffn_w] + [
            pltpu.SemaphoreType.DMA((len(ffn_w),)),
            pltpu.SemaphoreType.DMA((len(ffn_w),)),
        ],
        compiler_params=pltpu.CompilerParams(
            dimension_semantics=("arbitrary",), vmem_limit_bytes=VMEM_LIMIT),
        name="front",
    )(tab, sinks, xp, xp, xt, gm, w_in, w_out, bucket, cw, cb, wg, gab, gxb, lam, nr, na, *ffn_w)


def _smixer_kernel(tab_ref, sink_ref, z_ref, st_ref, h0_ref, kc_ref, vc_ref, xs_ref, wout_ref,
                   bucket_ref, cw_ref, cb_ref, wg_ref, gab_ref, gxb_ref, lam_ref, nr_ref, na_ref,
                   x1s_ref, conv_ref, h_ref, ko_ref, vo_ref, bias_scr, mix_scr):
    c = pl.program_id(0)
    nrow = N_HEADS * SB
    ncol = SB * WINDOW * N_KV

    @pl.when(c == 0)
    def _():
        bucket = bucket_ref[...]
        for h in range(N_HEADS):
            bias_scr[h * SB:(h + 1) * SB] = _table_lookup(bucket, tab_ref, h)

    r0 = pl.multiple_of(c * SB, SB)

    xr = z_ref[:, 0:D_RNN]
    st = st_ref[...]
    cw = cw_ref[...]
    xc = cb_ref[...]
    for j in range(RNN_CONV - 1):
        xc = xc + st[:, j * D_RNN:(j + 1) * D_RNN] * cw[j:j + 1]
    xc = xc + xr * cw[RNN_CONV - 1:RNN_CONV]
    conv_ref[:, 0:(RNN_CONV - 2) * D_RNN] = st[:, D_RNN:]
    conv_ref[:, (RNN_CONV - 2) * D_RNN:] = xr
    a, b = _gates(xc, wg_ref, gab_ref[...], gxb_ref[...], lam_ref[...])
    h = a * h0_ref[...] + b
    h_ref[...] = h
    y_rnn = h * jax.nn.gelu(z_ref[:, O_GR:O_GR + D_RNN])
    mix_scr[pl.ds(r0, SB), 0:D_RNN] = _rms(y_rnn, nr_ref[...]).astype(BF16)

    q = z_ref[:, O_Q:O_Q + D_ATTN]
    kv = z_ref[:, O_KV:O_KV + 2 * D_KV]
    col = lax.broadcasted_iota(jnp.int32, (nrow, ncol), 1)
    row = lax.broadcasted_iota(jnp.int32, (nrow, ncol), 0)
    pos = (col >> 1) & (WINDOW - 1)
    valid = ((col >> 8) == (row & (SB - 1))) & ((col & 1) == (row >> 6)) & (pos >= 1)
    qs = jnp.concatenate([q[:, h * HEAD_DIM:(h + 1) * HEAD_DIM] for h in range(N_HEADS)], axis=0).astype(BF16)
    new_rows = lambda off: jnp.concatenate(
        [kv[:, off + (h // GROUP) * HEAD_DIM:off + (h // GROUP + 1) * HEAD_DIM] for h in range(N_HEADS)],
        axis=0).astype(BF16).astype(F32)
    k_new = new_rows(0)
    v_new = new_rows(D_KV)
    sc = lax.dot_general(qs, kc_ref[...].astype(BF16), (((1,), (1,)), ((), ())), preferred_element_type=F32)
    lg = jnp.where(valid, sc * SCALE + bias_scr[...], NEG)
    rh = lax.broadcasted_iota(jnp.int32, (nrow, 1), 0) >> 4
    sink = jnp.zeros((nrow, 1), F32)
    bias_new = jnp.zeros((nrow, 1), F32)
    for h in range(N_HEADS):
        sink = jnp.where(rh == h, sink_ref[h], sink)
        bias_new = jnp.where(rh == h, tab_ref[0, h], bias_new)
    lg_new = jnp.sum(qs.astype(F32) * k_new, axis=-1, keepdims=True) * SCALE + bias_new
    mx = jnp.maximum(jnp.maximum(jnp.max(lg, axis=-1, keepdims=True), lg_new), sink)
    e = jnp.exp(lg - mx)
    e_new = jnp.exp(lg_new - mx)
    den = jnp.sum(e, axis=-1, keepdims=True) + e_new + jnp.exp(sink - mx)
    pv = jnp.dot(e.astype(BF16), vc_ref[...].astype(BF16), preferred_element_type=F32)
    o = (pv + e_new.astype(BF16).astype(F32) * v_new) / den
    y_attn = jnp.concatenate([o[h * SB:(h + 1) * SB] for h in range(N_HEADS)], axis=1)
    mix_scr[pl.ds(r0, SB), D_RNN:] = _rms(y_attn, na_ref[...]).astype(BF16)

    per = WINDOW * N_KV
    slide = lambda ref: pltpu.roll(ref[...].reshape(SB, per, HEAD_DIM), per - N_KV, 1).reshape(SB * per, HEAD_DIM)
    ko_ref[...] = slide(kc_ref)
    vo_ref[...] = slide(vc_ref)
    for bi in range(SB):
        for kh in range(N_KV):
            r = (bi + 1) * per - N_KV + kh
            ko_ref[r:r + 1, :] = kv[bi:bi + 1, kh * HEAD_DIM:(kh + 1) * HEAD_DIM]
            vo_ref[r:r + 1, :] = kv[bi:bi + 1, D_KV + kh * HEAD_DIM:D_KV + (kh + 1) * HEAD_DIM]

    @pl.when(c == DEC_BATCH // SB - 1)
    def _():
        x1s_ref[...] = xs_ref[...] + jnp.dot(mix_scr[...], wout_ref[...], preferred_element_type=F32)


def _smixer(tab, sinks, zs, st, h0, kc, vc, xs, w_out, bucket, cw, cb, wg, gab, gxb, lam, nr, na):
    vec = lambda n: pl.BlockSpec((1, n), lambda c: (0, 0))
    smem = pl.BlockSpec(memory_space=pltpu.SMEM)
    cache = pl.BlockSpec((SB * WINDOW * N_KV, HEAD_DIM), lambda c: (c, 0))
    return pl.pallas_call(
        _smixer_kernel,
        grid=(DEC_BATCH // SB,),
        in_specs=[
            smem, smem,
            pl.BlockSpec((SB, D_IN), lambda c: (c, 0)),
            pl.BlockSpec((SB, (RNN_CONV - 1) * D_RNN), lambda c: (c, 0)),
            pl.BlockSpec((SB, D_RNN), lambda c: (c, 0)),
            cache, cache,
            pl.BlockSpec((DEC_BATCH, D_MODEL), lambda c: (0, 0)),
            pl.BlockSpec((D_MODEL, D_MODEL), lambda c: (0, 0), pipeline_mode=pl.Buffered(1)),
            pl.BlockSpec((SB, SB * WINDOW * N_KV), lambda c: (0, 0)),
            pl.BlockSpec((RNN_CONV, D_RNN), lambda c: (0, 0)),
            vec(D_RNN),
            pl.BlockSpec((N_GW, GW, 2 * GW), lambda c: (0, 0, 0)),
            vec(D_RNN), vec(D_RNN), vec(D_RNN), vec(D_RNN), vec(D_ATTN),
        ],
        out_specs=[
            pl.BlockSpec((DEC_BATCH, D_MODEL), lambda c: (0, 0)),
            pl.BlockSpec((SB, (RNN_CONV - 1) * D_RNN), lambda c: (c, 0)),
            pl.BlockSpec((SB, D_RNN), lambda c: (c, 0)),
            cache, cache,
        ],
        out_shape=[
            jax.ShapeDtypeStruct((DEC_BATCH, D_MODEL), F32),
            jax.ShapeDtypeStruct((DEC_BATCH, (RNN_CONV - 1) * D_RNN), F32),
            jax.ShapeDtypeStruct((DEC_BATCH, D_RNN), F32),
            jax.ShapeDtypeStruct((DEC_BATCH * WINDOW * N_KV, HEAD_DIM), F32),
            jax.ShapeDtypeStruct((DEC_BATCH * WINDOW * N_KV, HEAD_DIM), F32),
        ],
        scratch_shapes=[
            pltpu.VMEM((N_HEADS * SB, SB * WINDOW * N_KV), F32),
            pltpu.VMEM((DEC_BATCH, D_MODEL), BF16),
        ],
        compiler_params=pltpu.CompilerParams(
            dimension_semantics=("arbitrary",), vmem_limit_bytes=VMEM_LIMIT),
        name="smixer",
    )(tab, sinks, zs, st, h0, kc, vc, xs, w_out, bucket, cw, cb, wg, gab, gxb, lam, nr, na)


U_LO = LAST_P - HALO
U_HI = LAST_P + DEC_BATCH
N_UST = U_HI - U_LO


def _ffn_kernel(x1_ref, halo_ref, x1s_ref, st0_ref, st1_ref, gn_ref, gf_ref, wu_ref, wg_ref, cw_ref, cb_ref,
                wd_ref, y_ref, ys_ref, ust_ref, h2_scr, ubuf, abuf):
    i = pl.program_id(0)
    j = pl.program_id(1)

    @pl.when(j == 0)
    def _():
        h2_scr[0:FH] = _rms(halo_ref[...], gn_ref[...]).astype(BF16)
        h2_scr[FH:FH + TM] = _rms(x1_ref[...], gn_ref[...]).astype(BF16)
        y_ref[...] = jnp.zeros((TM, D_MODEL), F32)

        @pl.when(i == N_RT - 1)
        def _():
            h2_scr[FH + LAST_P:FH + U_HI] = _rms(x1s_ref[...], gn_ref[...]).astype(BF16)

    cw = cw_ref[...]
    cb = cb_ref[...]
    wu = wu_ref[...]
    wg = wg_ref[...]
    bounds = [c * TM // N_CH for c in range(N_CH + 1)]
    gates = []
    for c in range(N_CH):
        lo = FH + bounds[c] if c else 0
        hi = FH + bounds[c + 1]
        ubuf[lo:hi] = jnp.dot(h2_scr[lo:hi], wu, preferred_element_type=F32)
        gates.append(jnp.dot(h2_scr[FH + bounds[c]:hi], wg, preferred_element_type=F32))
    for c in range(N_CH):
        lo, hi = bounds[c], bounds[c + 1]
        u = ubuf[FH + lo:FH + hi]
        uc = cb + ubuf[FH - 2 + lo:FH - 2 + hi] * cw[0:1] + ubuf[FH - 1 + lo:FH - 1 + hi] * cw[1:2] + u * cw[2:3]
        act = jax.nn.gelu(uc) * gates[c]
        abuf[lo:hi] = act.astype(BF16)
        if lo <= LAST_P and U_HI <= hi:
            rs = slice(LAST_P - lo, U_HI - lo)
            ucs = cb + st0_ref[...] * cw[0:1] + st1_ref[...] * cw[1:2] + u[rs] * cw[2:3]
            act_s = jax.nn.gelu(ucs) * gates[c][rs]
            abuf[LAST_P:U_HI] = jnp.where(i == N_RT - 1, act_s, act[rs]).astype(BF16)
        y_ref[lo:hi] += jnp.dot(abuf[lo:hi], wd_ref[...], preferred_element_type=F32)
    ust_ref[...] = ubuf[FH + U_LO:FH + U_HI]

    @pl.when(j == N_FT - 1)
    def _():
        @pl.when(i == N_RT - 1)
        def _():
            ys_ref[...] = _rms(x1s_ref[...] + y_ref[LAST_P:U_HI], gf_ref[...])

        y_ref[...] = _rms(x1_ref[...] + y_ref[...], gf_ref[...])


def _ffn(x1, x1s, st, gn, gf, wu, wg, cw, cb, wd):
    def halo_idx(i, j):
        return (jnp.where(i == 0, R_ALL // FH - 1, i * (TM // FH) - 1), 0)

    return pl.pallas_call(
        _ffn_kernel,
        grid=(N_RT, N_FT),
        in_specs=[
            pl.BlockSpec((TM, D_MODEL), lambda i, j: (i, 0)),
            pl.BlockSpec((FH, D_MODEL), halo_idx),
            pl.BlockSpec((DEC_BATCH, D_MODEL), lambda i, j: (0, 0)),
            pl.BlockSpec((DEC_BATCH, TF), lambda i, j: (0, j)),
            pl.BlockSpec((DEC_BATCH, TF), lambda i, j: (0, N_FT + j)),
            pl.BlockSpec((1, D_MODEL), lambda i, j: (0, 0)),
            pl.BlockSpec((1, D_MODEL), lambda i, j: (0, 0)),
            pl.BlockSpec((D_MODEL, TF), lambda i, j: (0, j)),
            pl.BlockSpec((D_MODEL, TF), lambda i, j: (0, j)),
            pl.BlockSpec((FFN_CONV, TF), lambda i, j: (0, j)),
            pl.BlockSpec((1, TF), lambda i, j: (0, j)),
            pl.BlockSpec((TF, D_MODEL), lambda i, j: (j, 0)),
        ],
        out_specs=[
            pl.BlockSpec((TM, D_MODEL), lambda i, j: (i, 0)),
            pl.BlockSpec((DEC_BATCH, D_MODEL), lambda i, j: (0, 0)),
            pl.BlockSpec((N_UST, TF), lambda i, j: (i, j)),
        ],
        out_shape=[
            jax.ShapeDtypeStruct((SEQ, D_MODEL), F32),
            jax.ShapeDtypeStruct((DEC_BATCH, D_MODEL), F32),
            jax.ShapeDtypeStruct((N_RT * N_UST, D_FF), F32),
        ],
        scratch_shapes=[
            pltpu.VMEM((FH + TM, D_MODEL), BF16),
            pltpu.VMEM((FH + TM, TF), F32),
            pltpu.VMEM((TM, TF), BF16),
        ],
        compiler_params=pltpu.CompilerParams(
            dimension_semantics=("arbitrary", "arbitrary"), vmem_limit_bytes=VMEM_LIMIT),
        name="ffn",
    )(x1, x1, x1s, st, st, gn, gf, wu, wg, cw, cb, wd)


def _gate_weights(wa, wx):
    per = GW // RNN_BLOCK
    eye = jnp.eye(per, dtype=wa.dtype)

    def bd(w):
        w = w.reshape(N_GW, per, RNN_BLOCK, RNN_BLOCK)
        return jnp.einsum('gpcd,pq->gpcqd', w, eye).reshape(N_GW, GW, GW)

    return jnp.concatenate([bd(wa), bd(wx)], axis=-1).astype(BF16)


def kernel(x_prompt, x_sample, state_rnn_conv, state_rnn_h, cache_k_win, cache_v_win, state_ffn_conv,
           meta_tokens, rel_bias_table, norm_mix, w_in, rnn_conv_w, rnn_conv_b, gate_a_w, gate_a_b,
           gate_x_w, gate_x_b, rnn_lambda, attn_sinks, norm_rnn_out, norm_attn_out, w_out, norm_ffn,
           w_up, w_gate, ffn_conv_w, ffn_conv_b, w_down, norm_final):
    l = 0
    xp = x_prompt[0]
    xs = x_sample[:, 0, :]
    xt = jnp.concatenate([xs, jnp.zeros((N_PAD, D_MODEL), F32), meta_tokens], axis=0)
    row = lambda v: v.reshape(1, -1)

    qi = np.arange(BLOCK)[:, None]
    sj = np.arange(2 * BLOCK)[None, :]
    bucket_p = jnp.asarray(_rel_buckets(BLOCK + qi - sj))
    pos = (np.arange(SB * WINDOW * N_KV) >> 1) & (WINDOW - 1)
    bucket_s = jnp.asarray(np.tile(_rel_buckets(WINDOW - pos)[None, :], (SB, 1)))

    wg = _gate_weights(gate_a_w[l], gate_x_w[l])
    seq_w = (rnn_conv_w[l], row(rnn_conv_b[l]), wg, row(gate_a_b[l]), row(gate_x_b[l]), row(rnn_lambda[l]),
             row(norm_rnn_out[l]), row(norm_attn_out[l]))
    w_out_b = w_out[l].astype(BF16)

    x1, zs, h_last, kv_last, xr_tail, w_up_b, w_gate_b, w_down_b = _front(
        rel_bias_table, attn_sinks[l], xp, xt, row(norm_mix[l]), w_in[l].astype(BF16), w_out_b, bucket_p, *seq_w,
        w_up[l], w_gate[l], w_down[l])
    x1s, conv_s, h_s, k_s, v_s = _smixer(
        rel_bias_table, attn_sinks[l], zs,
        state_rnn_conv[l].reshape(DEC_BATCH, (RNN_CONV - 1) * D_RNN), state_rnn_h[l],
        cache_k_win[l].reshape(DEC_BATCH * WINDOW * N_KV, HEAD_DIM),
        cache_v_win[l].reshape(DEC_BATCH * WINDOW * N_KV, HEAD_DIM),
        xs, w_out_b, bucket_s, *seq_w)
    y_p, y_s, ust = _ffn(x1, x1s, state_ffn_conv[l].reshape(DEC_BATCH, (FFN_CONV - 1) * D_FF),
                         row(norm_ffn[l]), row(norm_final), w_up_b, w_gate_b,
                         ffn_conv_w[l], row(ffn_conv_b[l]), w_down_b)

    ust = ust[(N_RT - 1) * N_UST:]
    p_states = (
        xr_tail[HALO - (RNN_CONV - 1):HALO][None, None],
        h_last[0:1][None],
        kv_last[:, :D_KV].reshape(1, 1, WINDOW, N_KV, HEAD_DIM),
        kv_last[:, D_KV:].reshape(1, 1, WINDOW, N_KV, HEAD_DIM),
        ust[HALO - (FFN_CONV - 1):HALO][None, None],
    )
    s_states = (
        conv_s.reshape(1, DEC_BATCH, RNN_CONV - 1, D_RNN),
        h_s[None],
        k_s.reshape(1, DEC_BATCH, WINDOW, N_KV, HEAD_DIM),
        v_s.reshape(1, DEC_BATCH, WINDOW, N_KV, HEAD_DIM),
        jnp.stack([state_ffn_conv[l][:, FFN_CONV - 2, :], ust[HALO:]], axis=1)[None],
    )
    return (y_p[None], y_s[:, None, :]) + p_states + s_states
```

```python
import math

import numpy as np
import jax
import jax.numpy as jnp
from jax import lax
from jax.experimental import pallas as pl
from jax.experimental.pallas import tpu as pltpu

F32 = jnp.float32
BF16 = jnp.bfloat16

D_MODEL = 2048
SEQ = 8192
DEC_BATCH = 128
D_RNN = 1024
N_RNN_BLOCKS = 16
RNN_BLOCK = D_RNN // N_RNN_BLOCKS
RNN_CONV = 4
LRU_C = 8.0
N_HEADS = 8
HEAD_DIM = 128
N_KV = 2
GROUP = N_HEADS // N_KV
D_ATTN = N_HEADS * HEAD_DIM
WINDOW = 128
BLOCK = 128
NUM_BUCKETS = 32
MAX_DISTANCE = 128
D_FF = 3 * D_MODEL
FFN_CONV = 3
N_META = 16
EPS = 1e-6
NEG = -1e30
D_KV = N_KV * HEAD_DIM
D_IN = 2 * D_RNN + D_ATTN + 2 * D_KV
SCALE = HEAD_DIM ** -0.5
O_GR = D_RNN
O_Q = 2 * D_RNN
O_KV = 2 * D_RNN + D_ATTN

N_PAD = BLOCK - N_META
R_TAIL = DEC_BATCH + BLOCK
R_ALL = SEQ + R_TAIL
N_PBLK = SEQ // BLOCK

FB = 2 * BLOCK
N_FSTEP = (N_PBLK + 2) // 2
PC = 512
N_IN_EARLY = 5
N_SLAB = N_FSTEP - 1

TM = 768
N_RT = R_ALL // TM
LAST_P = SEQ - (N_RT - 1) * TM
TF = 512
N_FT = D_FF // TF
HALO = 8
FH = 16
GW = 256
N_GW = D_RNN // GW
SB = 16
VMEM_LIMIT = 56 * 1024 * 1024
VMEM_LIMIT_FRONT = 60 * 1024 * 1024


def _rms(x, g):
    return x * lax.rsqrt(jnp.mean(x * x, axis=-1, keepdims=True) + EPS) * g


def _rel_buckets(d):
    d = np.maximum(d, 0)
    exact = NUM_BUCKETS // 2
    ratio = np.maximum(d, 1).astype(np.float32) / np.float32(exact)
    large = exact + (np.log(ratio) / np.float32(math.log(MAX_DISTANCE / exact))
                     * np.float32(NUM_BUCKETS - exact)).astype(np.int32)
    large = np.minimum(large, NUM_BUCKETS - 1)
    return np.where(d < exact, d, large).astype(np.int32)


def _table_lookup(bucket, tab_ref, h):
    out = jnp.zeros(bucket.shape, F32)
    for b in range(NUM_BUCKETS):
        out = jnp.where(bucket == b, tab_ref[b, h], out)
    return out


def _gates(xc, wg_ref, gab, gxb, lam):
    xcb = xc.astype(BF16)
    ga, gx = [], []
    for j in range(N_GW):
        gj = jnp.dot(xcb[:, GW * j:GW * (j + 1)], wg_ref[j], preferred_element_type=F32)
        ga.append(gj[:, :GW])
        gx.append(gj[:, GW:])
    r = jax.nn.sigmoid(jnp.concatenate(ga, axis=1) + gab)
    i = jax.nn.sigmoid(jnp.concatenate(gx, axis=1) + gxb)
    log_a = -LRU_C * r * jax.nn.softplus(-lam)
    a = jnp.exp(log_a)
    t = 1.0 - a * a
    b = jnp.where(t > 0.0, t * lax.rsqrt(t), 0.0) * i * xc
    return a, b


def _front_kernel(tab_ref, sink_ref, xa_ref, xb_ref, xt_ref, gm_ref, win_ref, wout_ref, bucket_ref,
                  cw_ref, cb_ref, wg_ref, gab_ref, gxb_ref, lam_ref, nr_ref, na_ref, wu_f, wgt_f, wd_f,
                  x1_ref, zs_ref, hlast_ref, kvlast_ref, xrt_ref, wu_b, wgt_b, wd_b,
                  zp, zb_o, lhs_p, xsp, mixp, xbuf, h_scr, kvbuf, bias_scr,
                  su, sg, sd, tu, tg, td, sem_in, sem_out):
    k = pl.program_id(0)
    last = N_FSTEP - 1

    stages = ((wu_f, su, tu, wu_b), (wgt_f, sg, tg, wgt_b), (wd_f, sd, td, wd_b))

    def slab_in(s):
        return [pltpu.make_async_copy(w.at[pl.ds(s * st.shape[0], st.shape[0])], st, sem_in.at[n])
                for n, (w, st, _, _) in enumerate(stages)]

    def slab_out(s):
        return [pltpu.make_async_copy(t, w.at[pl.ds(s * t.shape[0], t.shape[0])], sem_out.at[n])
                for n, (_, _, t, w) in enumerate(stages)]

    @pl.when(k == 0)
    def _():
        for c in slab_in(0):
            c.start()

    @pl.when(k >= 1)
    def _():
        for c in slab_out(k - 1):
            c.wait()

    @pl.when(k < N_SLAB)
    def _():
        for c in slab_in(k):
            c.wait()
        for _, st, t, _ in stages:
            t[...] = st[...].astype(BF16)
        for c in slab_out(k):
            c.start()

    @pl.when(k < N_SLAB - 1)
    def _():
        for c in slab_in(k + 1):
            c.start()

    def inproj_chunk(c):
        def run():
            cs = slice(c * PC, (c + 1) * PC)
            zp[:, cs] = jnp.dot(lhs_p[...], win_ref[:, cs], preferred_element_type=F32)
        return run

    def outproj_chunk(c):
        def run():
            cs = slice(c * PC, (c + 1) * PC)
            x1_ref[:, cs] = xsp[:, cs] + jnp.dot(mixp[...], wout_ref[:, cs], preferred_element_type=F32)
        return run

    @pl.when(k == 0)
    def _():
        prefix = xt_ref[DEC_BATCH:R_TAIL]
        zb_o[...] = jnp.dot(_rms(prefix, gm_ref[...]).astype(BF16), win_ref[...], preferred_element_type=F32)
        xsp[0:BLOCK] = jnp.zeros((BLOCK, D_MODEL), F32)
        xsp[BLOCK:FB] = prefix
        mixp[...] = jnp.zeros((FB, D_MODEL), BF16)
        xbuf[0:HALO] = jnp.zeros((HALO, D_RNN), F32)
        h_scr[...] = jnp.zeros((HALO, D_RNN), F32)
        kvbuf[...] = jnp.zeros((BLOCK, 2 * D_KV), F32)
        bucket = bucket_ref[...]
        for h in range(N_HEADS):
            bias_scr[h] = _table_lookup(bucket, tab_ref, h)

    def mixer(zb, is_prefix, first_key, fillers):
        fill = iter(fillers)

        def between():
            f = next(fill, None)
            if f is not None:
                f()

        between()
        xr = zb[:, 0:D_RNN]
        xbuf[HALO:HALO + BLOCK] = xr
        cw = cw_ref[...]
        xc = cb_ref[...]
        for j in range(RNN_CONV - 1):
            lo = HALO - (RNN_CONV - 1) + j
            xc = xc + xbuf[lo:lo + BLOCK] * cw[j:j + 1]
        xc = xc + xr * cw[RNN_CONV - 1:RNN_CONV]
        xbuf[0:HALO] = xr[BLOCK - HALO:BLOCK]

        a, b = _gates(xc, wg_ref, gab_ref[...], gxb_ref[...], lam_ref[...])
        between()
        h = h_scr[0:1]
        if is_prefix is not None:
            row = lax.broadcasted_iota(jnp.int32, (BLOCK, D_RNN), 0)
            b = jnp.where(jnp.logical_and(is_prefix, row < N_PAD), 0.0, b)
            h = jnp.where(is_prefix, 0.0, h)

        ng = BLOCK // HALO
        a3 = a.reshape(ng, HALO, D_RNN)
        b3 = b.reshape(ng, HALO, D_RNN)
        sub = lax.broadcasted_iota(jnp.int32, (ng, HALO, D_RNN), 1)
        sh = 1
        while sh < HALO:
            a_prev = pltpu.roll(a3, sh, 1)
            b_prev = pltpu.roll(b3, sh, 1)
            m = sub >= sh
            b3 = jnp.where(m, a3 * b_prev + b3, b3)
            a3 = jnp.where(m, a3 * a_prev, a3)
            sh *= 2
        hs = []
        for g in range(ng):
            hg = a3[g] * h + b3[g]
            hs.append(hg)
            h = hg[HALO - 1:HALO]
        h_all = jnp.concatenate(hs, axis=0)
        h_scr[...] = jnp.broadcast_to(h, (HALO, D_RNN))
        between()
        y_rnn = h_all * jax.nn.gelu(zb[:, O_GR:O_GR + D_RNN])
        mix_a = _rms(y_rnn, nr_ref[...]).astype(BF16)

        q = zb[:, O_Q:O_Q + D_ATTN]
        kv = zb[:, O_KV:O_KV + 2 * D_KV]
        kvp = kvbuf[...]
        kvbuf[...] = kv
        col = lax.broadcasted_iota(jnp.int32, (BLOCK, 2 * BLOCK), 1)
        rowq = lax.broadcasted_iota(jnp.int32, (BLOCK, 2 * BLOCK), 0)
        d = BLOCK + rowq - col
        mask = (d >= 0) & (d < WINDOW) & (col >= first_key)
        outs = []
        for kh in range(N_KV):
            between()
            ks = slice(kh * HEAD_DIM, (kh + 1) * HEAD_DIM)
            vs = slice(D_KV + kh * HEAD_DIM, D_KV + (kh + 1) * HEAD_DIM)
            qs = jnp.concatenate(
                [q[:, (kh * GROUP + g) * HEAD_DIM:(kh * GROUP + g + 1) * HEAD_DIM] for g in range(GROUP)],
                axis=0).astype(BF16)
            kk = jnp.concatenate([kvp[:, ks], kv[:, ks]], axis=0).astype(BF16)
            vv = jnp.concatenate([kvp[:, vs], kv[:, vs]], axis=0).astype(BF16)
            sc = lax.dot_general(qs, kk, (((1,), (1,)), ((), ())), preferred_element_type=F32)
            es, dens = [], []
            for g in range(GROUP):
                hh = kh * GROUP + g
                lg = sc[g * BLOCK:(g + 1) * BLOCK] * SCALE + bias_scr[hh]
                lg = jnp.where(mask, lg, NEG)
                sink = sink_ref[hh]
                mx = jnp.maximum(jnp.max(lg, axis=-1, keepdims=True), sink)
                e = jnp.exp(lg - mx)
                dens.append(jnp.sum(e, axis=-1, keepdims=True) + jnp.exp(sink - mx))
                es.append(e.astype(BF16))
            pv = jnp.dot(jnp.concatenate(es, axis=0), vv, preferred_element_type=F32)
            for g in range(GROUP):
                outs.append(pv[g * BLOCK:(g + 1) * BLOCK] / dens[g])
        between()
        mix_b = _rms(jnp.concatenate(outs, axis=1), na_ref[...]).astype(BF16)

        return jnp.concatenate([mix_a, mix_b], axis=1), h, kv, xr

    xa = jnp.where(k == last, xt_ref[0:DEC_BATCH], xa_ref[...])
    xb = xb_ref[...]
    lhs_p[0:BLOCK] = _rms(xa, gm_ref[...]).astype(BF16)
    lhs_p[BLOCK:FB] = _rms(xb, gm_ref[...]).astype(BF16)
    n_in, n_out = D_IN // PC, D_MODEL // PC
    mix_o, h, kv, xr = mixer(zb_o, k == 0, jnp.where(k == 0, BLOCK + N_PAD, 0),
                             [inproj_chunk(c) for c in range(N_IN_EARLY)])
    mixp[BLOCK:FB] = mix_o
    hlast_ref[...] = jnp.broadcast_to(h, (HALO, D_RNN))
    kvlast_ref[...] = kv
    xrt_ref[...] = xr[BLOCK - HALO:BLOCK]

    mix_e = mixer(zp.at[0:BLOCK], None, jnp.where(k == 0, N_PAD, 0),
                  [inproj_chunk(c) for c in range(N_IN_EARLY, n_in)] + [outproj_chunk(c) for c in range(n_out)])[0]
    mixp[0:BLOCK] = mix_e
    xsp[0:BLOCK] = xa
    xsp[BLOCK:FB] = xb
    zb_o[...] = zp[BLOCK:FB]

    @pl.when(k == last)
    def _():
        zs_ref[...] = zp[0:BLOCK]


def _front(tab, sinks, xp, xt, gm, w_in, w_out, bucket, cw, cb, wg, gab, gxb, lam, nr, na, w_up, w_gate, w_down):
    ffn_w = (w_up, w_gate, w_down)
    slab = lambda w: (w.shape[0] // N_SLAB, w.shape[1])
    hbm = pl.BlockSpec(memory_space=pl.ANY)
    vec = lambda n: pl.BlockSpec((1, n), lambda k: (0, 0))
    smem = pl.BlockSpec(memory_space=pltpu.SMEM)
    once = lambda shape: pl.BlockSpec(shape, lambda k: (0,) * len(shape), pipeline_mode=pl.Buffered(1))
    return pl.pallas_call(
        _front_kernel,
        grid=(N_FSTEP,),
        in_specs=[
            smem, smem,
            pl.BlockSpec((BLOCK, D_MODEL), lambda k: (jnp.minimum(2 * k, N_PBLK - 1), 0)),
            pl.BlockSpec((BLOCK, D_MODEL), lambda k: (jnp.minimum(2 * k + 1, N_PBLK - 1), 0)),
            once((R_TAIL, D_MODEL)),
            vec(D_MODEL),
            once((D_MODEL, D_IN)),
            once((D_MODEL, D_MODEL)),
            pl.BlockSpec((BLOCK, 2 * BLOCK), lambda k: (0, 0)),
            pl.BlockSpec((RNN_CONV, D_RNN), lambda k: (0, 0)),
            vec(D_RNN),
            pl.BlockSpec((N_GW, GW, 2 * GW), lambda k: (0, 0, 0)),
            vec(D_RNN), vec(D_RNN), vec(D_RNN), vec(D_RNN), vec(D_ATTN),
            hbm, hbm, hbm,
        ],
        out_specs=[
            pl.BlockSpec((FB, D_MODEL), lambda k: (jnp.where(k == 0, SEQ // FB, k - 1), 0)),
            pl.BlockSpec((BLOCK, D_IN), lambda k: (0, 0)),
            pl.BlockSpec((HALO, D_RNN), lambda k: (0, 0)),
            pl.BlockSpec((BLOCK, 2 * D_KV), lambda k: (0, 0)),
            pl.BlockSpec((HALO, D_RNN), lambda k: (0, 0)),
            hbm, hbm, hbm,
        ],
        out_shape=[
            jax.ShapeDtypeStruct((R_ALL, D_MODEL), F32),
            jax.ShapeDtypeStruct((DEC_BATCH, D_IN), F32),
            jax.ShapeDtypeStruct((HALO, D_RNN), F32),
            jax.ShapeDtypeStruct((BLOCK, 2 * D_KV), F32),
            jax.ShapeDtypeStruct((HALO, D_RNN), F32),
        ] + [jax.ShapeDtypeStruct(w.shape, BF16) for w in ffn_w],
        scratch_shapes=[
            pltpu.VMEM((FB, D_IN), F32),
            pltpu.VMEM((BLOCK, D_IN), F32),
            pltpu.VMEM((FB, D_MODEL), BF16),
            pltpu.VMEM((FB, D_MODEL), F32),
            pltpu.VMEM((FB, D_MODEL), BF16),
            pltpu.VMEM((HALO + BLOCK, D_RNN), F32),
            pltpu.VMEM((HALO, D_RNN), F32),
            pltpu.VMEM((BLOCK, 2 * D_KV), F32),
            pltpu.VMEM((N_HEADS, BLOCK, 2 * BLOCK), F32),
        ] + [pltpu.VMEM(slab(w), F32) for w in ffn_w] + [pltpu.VMEM(slab(w), BF16) for w in ffn_w] + [
            pltpu.SemaphoreType.DMA((len(ffn_w),)),
            pltpu.SemaphoreType.DMA((len(ffn_w),)),
        ],
        compiler_params=pltpu.CompilerParams(
            dimension_semantics=("arbitrary",), vmem_limit_bytes=VMEM_LIMIT_FRONT),
        name="front",
    )(tab, sinks, xp, xp, xt, gm, w_in, w_out, bucket, cw, cb, wg, gab, gxb, lam, nr, na, *ffn_w)


def _smixer_kernel(tab_ref, sink_ref, z_ref, st_ref, h0_ref, kc_ref, vc_ref, xs_ref, wout_ref,
                   bucket_ref, cw_ref, cb_ref, wg_ref, gab_ref, gxb_ref, lam_ref, nr_ref, na_ref,
                   x1s_ref, conv_ref, h_ref, ko_ref, vo_ref, bias_scr, mix_scr):
    c = pl.program_id(0)
    nrow = N_HEADS * SB
    ncol = SB * WINDOW * N_KV

    @pl.when(c == 0)
    def _():
        bucket = bucket_ref[...]
        for h in range(N_HEADS):
            bias_scr[h * SB:(h + 1) * SB] = _table_lookup(bucket, tab_ref, h)

    r0 = pl.multiple_of(c * SB, SB)

    xr = z_ref[:, 0:D_RNN]
    st = st_ref[...]
    cw = cw_ref[...]
    xc = cb_ref[...]
    for j in range(RNN_CONV - 1):
        xc = xc + st[:, j * D_RNN:(j + 1) * D_RNN] * cw[j:j + 1]
    xc = xc + xr * cw[RNN_CONV - 1:RNN_CONV]
    conv_ref[:, 0:(RNN_CONV - 2) * D_RNN] = st[:, D_RNN:]
    conv_ref[:, (RNN_CONV - 2) * D_RNN:] = xr
    a, b = _gates(xc, wg_ref, gab_ref[...], gxb_ref[...], lam_ref[...])
    h = a * h0_ref[...] + b
    h_ref[...] = h
    y_rnn = h * jax.nn.gelu(z_ref[:, O_GR:O_GR + D_RNN])
    mix_scr[pl.ds(r0, SB), 0:D_RNN] = _rms(y_rnn, nr_ref[...]).astype(BF16)

    q = z_ref[:, O_Q:O_Q + D_ATTN]
    kv = z_ref[:, O_KV:O_KV + 2 * D_KV]
    col = lax.broadcasted_iota(jnp.int32, (nrow, ncol), 1)
    row = lax.broadcasted_iota(jnp.int32, (nrow, ncol), 0)
    pos = (col >> 1) & (WINDOW - 1)
    valid = ((col >> 8) == (row & (SB - 1))) & ((col & 1) == (row >> 6)) & (pos >= 1)
    qs = jnp.concatenate([q[:, h * HEAD_DIM:(h + 1) * HEAD_DIM] for h in range(N_HEADS)], axis=0).astype(BF16)
    new_rows = lambda off: jnp.concatenate(
        [kv[:, off + (h // GROUP) * HEAD_DIM:off + (h // GROUP + 1) * HEAD_DIM] for h in range(N_HEADS)],
        axis=0).astype(BF16).astype(F32)
    k_new = new_rows(0)
    v_new = new_rows(D_KV)
    sc = lax.dot_general(qs, kc_ref[...].astype(BF16), (((1,), (1,)), ((), ())), preferred_element_type=F32)
    lg = jnp.where(valid, sc * SCALE + bias_scr[...], NEG)
    rh = lax.broadcasted_iota(jnp.int32, (nrow, 1), 0) >> 4
    sink = jnp.zeros((nrow, 1), F32)
    bias_new = jnp.zeros((nrow, 1), F32)
    for h in range(N_HEADS):
        sink = jnp.where(rh == h, sink_ref[h], sink)
        bias_new = jnp.where(rh == h, tab_ref[0, h], bias_new)
    lg_new = jnp.sum(qs.astype(F32) * k_new, axis=-1, keepdims=True) * SCALE + bias_new
    mx = jnp.maximum(jnp.maximum(jnp.max(lg, axis=-1, keepdims=True), lg_new), sink)
    e = jnp.exp(lg - mx)
    e_new = jnp.exp(lg_new - mx)
    den = jnp.sum(e, axis=-1, keepdims=True) + e_new + jnp.exp(sink - mx)
    pv = jnp.dot(e.astype(BF16), vc_ref[...].astype(BF16), preferred_element_type=F32)
    o = (pv + e_new.astype(BF16).astype(F32) * v_new) / den
    y_attn = jnp.concatenate([o[h * SB:(h + 1) * SB] for h in range(N_HEADS)], axis=1)
    mix_scr[pl.ds(r0, SB), D_RNN:] = _rms(y_attn, na_ref[...]).astype(BF16)

    per = WINDOW * N_KV
    slide = lambda ref: pltpu.roll(ref[...].reshape(SB, per, HEAD_DIM), per - N_KV, 1).reshape(SB * per, HEAD_DIM)
    ko_ref[...] = slide(kc_ref)
    vo_ref[...] = slide(vc_ref)
    for bi in range(SB):
        for kh in range(N_KV):
            r = (bi + 1) * per - N_KV + kh
            ko_ref[r:r + 1, :] = kv[bi:bi + 1, kh * HEAD_DIM:(kh + 1) * HEAD_DIM]
            vo_ref[r:r + 1, :] = kv[bi:bi + 1, D_KV + kh * HEAD_DIM:D_KV + (kh + 1) * HEAD_DIM]

    @pl.when(c == DEC_BATCH // SB - 1)
    def _():
        x1s_ref[...] = xs_ref[...] + jnp.dot(mix_scr[...], wout_ref[...], preferred_element_type=F32)


def _smixer(tab, sinks, zs, st, h0, kc, vc, xs, w_out, bucket, cw, cb, wg, gab, gxb, lam, nr, na):
    vec = lambda n: pl.BlockSpec((1, n), lambda c: (0, 0))
    smem = pl.BlockSpec(memory_space=pltpu.SMEM)
    cache = pl.BlockSpec((SB * WINDOW * N_KV, HEAD_DIM), lambda c: (c, 0))
    return pl.pallas_call(
        _smixer_kernel,
        grid=(DEC_BATCH // SB,),
        in_specs=[
            smem, smem,
            pl.BlockSpec((SB, D_IN), lambda c: (c, 0)),
            pl.BlockSpec((SB, (RNN_CONV - 1) * D_RNN), lambda c: (c, 0)),
            pl.BlockSpec((SB, D_RNN), lambda c: (c, 0)),
            cache, cache,
            pl.BlockSpec((DEC_BATCH, D_MODEL), lambda c: (0, 0)),
            pl.BlockSpec((D_MODEL, D_MODEL), lambda c: (0, 0), pipeline_mode=pl.Buffered(1)),
            pl.BlockSpec((SB, SB * WINDOW * N_KV), lambda c: (0, 0)),
            pl.BlockSpec((RNN_CONV, D_RNN), lambda c: (0, 0)),
            vec(D_RNN),
            pl.BlockSpec((N_GW, GW, 2 * GW), lambda c: (0, 0, 0)),
            vec(D_RNN), vec(D_RNN), vec(D_RNN), vec(D_RNN), vec(D_ATTN),
        ],
        out_specs=[
            pl.BlockSpec((DEC_BATCH, D_MODEL), lambda c: (0, 0)),
            pl.BlockSpec((SB, (RNN_CONV - 1) * D_RNN), lambda c: (c, 0)),
            pl.BlockSpec((SB, D_RNN), lambda c: (c, 0)),
            cache, cache,
        ],
        out_shape=[
            jax.ShapeDtypeStruct((DEC_BATCH, D_MODEL), F32),
            jax.ShapeDtypeStruct((DEC_BATCH, (RNN_CONV - 1) * D_RNN), F32),
            jax.ShapeDtypeStruct((DEC_BATCH, D_RNN), F32),
            jax.ShapeDtypeStruct((DEC_BATCH * WINDOW * N_KV, HEAD_DIM), F32),
            jax.ShapeDtypeStruct((DEC_BATCH * WINDOW * N_KV, HEAD_DIM), F32),
        ],
        scratch_shapes=[
            pltpu.VMEM((N_HEADS * SB, SB * WINDOW * N_KV), F32),
            pltpu.VMEM((DEC_BATCH, D_MODEL), BF16),
        ],
        compiler_params=pltpu.CompilerParams(
            dimension_semantics=("arbitrary",), vmem_limit_bytes=VMEM_LIMIT),
        name="smixer",
    )(tab, sinks, zs, st, h0, kc, vc, xs, w_out, bucket, cw, cb, wg, gab, gxb, lam, nr, na)


U_LO = LAST_P - HALO
U_HI = LAST_P + DEC_BATCH
N_UST = U_HI - U_LO


def _ffn_kernel(x1_ref, halo_ref, x1s_ref, st0_ref, st1_ref, gn_ref, gf_ref, wu_ref, wg_ref, cw_ref, cb_ref,
                wd_ref, y_ref, ys_ref, ust_ref, h2_scr, ubuf, abuf):
    i = pl.program_id(0)
    j = pl.program_id(1)

    @pl.when(j == 0)
    def _():
        h2_scr[0:FH] = _rms(halo_ref[...], gn_ref[...]).astype(BF16)
        h2_scr[FH:FH + TM] = _rms(x1_ref[...], gn_ref[...]).astype(BF16)
        y_ref[...] = jnp.zeros((TM, D_MODEL), F32)

        @pl.when(i == N_RT - 1)
        def _():
            h2_scr[FH + LAST_P:FH + U_HI] = _rms(x1s_ref[...], gn_ref[...]).astype(BF16)

    ubuf[...] = jnp.dot(h2_scr[...], wu_ref[...], preferred_element_type=F32)
    gate = jnp.dot(h2_scr[FH:FH + TM], wg_ref[...], preferred_element_type=F32)
    cw = cw_ref[...]
    cb = cb_ref[...]
    u = ubuf[FH:FH + TM]
    uc = cb + ubuf[FH - 2:FH - 2 + TM] * cw[0:1] + ubuf[FH - 1:FH - 1 + TM] * cw[1:2] + u * cw[2:3]
    abuf[...] = (jax.nn.gelu(uc) * gate).astype(BF16)
    ust_ref[...] = u[U_LO:U_HI]

    @pl.when(i == N_RT - 1)
    def _():
        us = u[LAST_P:U_HI]
        ucs = cb + st0_ref[...] * cw[0:1] + st1_ref[...] * cw[1:2] + us * cw[2:3]
        abuf[LAST_P:U_HI] = (jax.nn.gelu(ucs) * gate[LAST_P:U_HI]).astype(BF16)

    y_ref[...] += jnp.dot(abuf[...], wd_ref[...], preferred_element_type=F32)

    @pl.when(j == N_FT - 1)
    def _():
        @pl.when(i == N_RT - 1)
        def _():
            ys_ref[...] = _rms(x1s_ref[...] + y_ref[LAST_P:U_HI], gf_ref[...])

        y_ref[...] = _rms(x1_ref[...] + y_ref[...], gf_ref[...])


def _ffn(x1, x1s, st, gn, gf, wu, wg, cw, cb, wd):
    def halo_idx(i, j):
        return (jnp.where(i == 0, R_ALL // FH - 1, i * (TM // FH) - 1), 0)

    return pl.pallas_call(
        _ffn_kernel,
        grid=(N_RT, N_FT),
        in_specs=[
            pl.BlockSpec((TM, D_MODEL), lambda i, j: (i, 0)),
            pl.BlockSpec((FH, D_MODEL), halo_idx),
            pl.BlockSpec((DEC_BATCH, D_MODEL), lambda i, j: (0, 0)),
            pl.BlockSpec((DEC_BATCH, TF), lambda i, j: (0, j)),
            pl.BlockSpec((DEC_BATCH, TF), lambda i, j: (0, N_FT + j)),
            pl.BlockSpec((1, D_MODEL), lambda i, j: (0, 0)),
            pl.BlockSpec((1, D_MODEL), lambda i, j: (0, 0)),
            pl.BlockSpec((D_MODEL, TF), lambda i, j: (0, j)),
            pl.BlockSpec((D_MODEL, TF), lambda i, j: (0, j)),
            pl.BlockSpec((FFN_CONV, TF), lambda i, j: (0, j)),
            pl.BlockSpec((1, TF), lambda i, j: (0, j)),
            pl.BlockSpec((TF, D_MODEL), lambda i, j: (j, 0)),
        ],
        out_specs=[
            pl.BlockSpec((TM, D_MODEL), lambda i, j: (i, 0)),
            pl.BlockSpec((DEC_BATCH, D_MODEL), lambda i, j: (0, 0)),
            pl.BlockSpec((N_UST, TF), lambda i, j: (i, j)),
        ],
        out_shape=[
            jax.ShapeDtypeStruct((SEQ, D_MODEL), F32),
            jax.ShapeDtypeStruct((DEC_BATCH, D_MODEL), F32),
            jax.ShapeDtypeStruct((N_RT * N_UST, D_FF), F32),
        ],
        scratch_shapes=[
            pltpu.VMEM((FH + TM, D_MODEL), BF16),
            pltpu.VMEM((FH + TM, TF), F32),
            pltpu.VMEM((TM, TF), BF16),
        ],
        compiler_params=pltpu.CompilerParams(
            dimension_semantics=("arbitrary", "arbitrary"), vmem_limit_bytes=VMEM_LIMIT),
        name="ffn",
    )(x1, x1, x1s, st, st, gn, gf, wu, wg, cw, cb, wd)


def _gate_weights(wa, wx):
    per = GW // RNN_BLOCK
    eye = jnp.eye(per, dtype=wa.dtype)

    def bd(w):
        w = w.reshape(N_GW, per, RNN_BLOCK, RNN_BLOCK)
        return jnp.einsum('gpcd,pq->gpcqd', w, eye).reshape(N_GW, GW, GW)

    return jnp.concatenate([bd(wa), bd(wx)], axis=-1).astype(BF16)


def kernel(x_prompt, x_sample, state_rnn_conv, state_rnn_h, cache_k_win, cache_v_win, state_ffn_conv,
           meta_tokens, rel_bias_table, norm_mix, w_in, rnn_conv_w, rnn_conv_b, gate_a_w, gate_a_b,
           gate_x_w, gate_x_b, rnn_lambda, attn_sinks, norm_rnn_out, norm_attn_out, w_out, norm_ffn,
           w_up, w_gate, ffn_conv_w, ffn_conv_b, w_down, norm_final):
    l = 0
    xp = x_prompt[0]
    xs = x_sample[:, 0, :]
    xt = jnp.concatenate([xs, jnp.zeros((N_PAD, D_MODEL), F32), meta_tokens], axis=0)
    row = lambda v: v.reshape(1, -1)

    qi = np.arange(BLOCK)[:, None]
    sj = np.arange(2 * BLOCK)[None, :]
    bucket_p = jnp.asarray(_rel_buckets(BLOCK + qi - sj))
    pos = (np.arange(SB * WINDOW * N_KV) >> 1) & (WINDOW - 1)
    bucket_s = jnp.asarray(np.tile(_rel_buckets(WINDOW - pos)[None, :], (SB, 1)))

    wg = _gate_weights(gate_a_w[l], gate_x_w[l])
    seq_w = (rnn_conv_w[l], row(rnn_conv_b[l]), wg, row(gate_a_b[l]), row(gate_x_b[l]), row(rnn_lambda[l]),
             row(norm_rnn_out[l]), row(norm_attn_out[l]))
    w_out_b = w_out[l].astype(BF16)

    x1, zs, h_last, kv_last, xr_tail, w_up_b, w_gate_b, w_down_b = _front(
        rel_bias_table, attn_sinks[l], xp, xt, row(norm_mix[l]), w_in[l].astype(BF16), w_out_b, bucket_p, *seq_w,
        w_up[l], w_gate[l], w_down[l])
    x1s, conv_s, h_s, k_s, v_s = _smixer(
        rel_bias_table, attn_sinks[l], zs,
        state_rnn_conv[l].reshape(DEC_BATCH, (RNN_CONV - 1) * D_RNN), state_rnn_h[l],
        cache_k_win[l].reshape(DEC_BATCH * WINDOW * N_KV, HEAD_DIM),
        cache_v_win[l].reshape(DEC_BATCH * WINDOW * N_KV, HEAD_DIM),
        xs, w_out_b, bucket_s, *seq_w)
    y_p, y_s, ust = _ffn(x1, x1s, state_ffn_conv[l].reshape(DEC_BATCH, (FFN_CONV - 1) * D_FF),
                         row(norm_ffn[l]), row(norm_final), w_up_b, w_gate_b,
                         ffn_conv_w[l], row(ffn_conv_b[l]), w_down_b)

    ust = ust[(N_RT - 1) * N_UST:]
    p_states = (
        xr_tail[HALO - (RNN_CONV - 1):HALO][None, None],
        h_last[0:1][None],
        kv_last[:, :D_KV].reshape(1, 1, WINDOW, N_KV, HEAD_DIM),
        kv_last[:, D_KV:].reshape(1, 1, WINDOW, N_KV, HEAD_DIM),
        ust[HALO - (FFN_CONV - 1):HALO][None, None],
    )
    s_states = (
        conv_s.reshape(1, DEC_BATCH, RNN_CONV - 1, D_RNN),
        h_s[None],
        k_s.reshape(1, DEC_BATCH, WINDOW, N_KV, HEAD_DIM),
        v_s.reshape(1, DEC_BATCH, WINDOW, N_KV, HEAD_DIM),
        jnp.stack([state_ffn_conv[l][:, FFN_CONV - 2, :], ust[HALO:]], axis=1)[None],
    )
    return (y_p[None], y_s[:, None, :]) + p_states + s_states
```

```python
import math

import numpy as np
import jax
import jax.numpy as jnp
from jax import lax
from jax.experimental import pallas as pl
from jax.experimental.pallas import tpu as pltpu

F32 = jnp.float32
BF16 = jnp.bfloat16

D_MODEL = 2048
SEQ = 8192
DEC_BATCH = 128
D_RNN = 1024
N_RNN_BLOCKS = 16
RNN_BLOCK = D_RNN // N_RNN_BLOCKS
RNN_CONV = 4
LRU_C = 8.0
N_HEADS = 8
HEAD_DIM = 128
N_KV = 2
GROUP = N_HEADS // N_KV
D_ATTN = N_HEADS * HEAD_DIM
WINDOW = 128
BLOCK = 128
NUM_BUCKETS = 32
MAX_DISTANCE = 128
D_FF = 3 * D_MODEL
FFN_CONV = 3
N_META = 16
EPS = 1e-6
NEG = -1e30
D_KV = N_KV * HEAD_DIM
D_IN = 2 * D_RNN + D_ATTN + 2 * D_KV
SCALE = HEAD_DIM ** -0.5
O_GR = D_RNN
O_Q = 2 * D_RNN
O_KV = 2 * D_RNN + D_ATTN

N_PAD = BLOCK - N_META
R_TAIL = DEC_BATCH + BLOCK
R_ALL = SEQ + R_TAIL
N_PBLK = SEQ // BLOCK

FB = 2 * BLOCK
N_FSTEP = (N_PBLK + 2) // 2
PC = 512
N_IN_EARLY = 5
N_SLAB = N_FSTEP - 1

TM = 768
N_RT = R_ALL // TM
LAST_P = SEQ - (N_RT - 1) * TM
TF = 512
N_FT = D_FF // TF
KC = 256
HALO = 8
FH = 16
GW = 256
N_GW = D_RNN // GW
SB = 16
VMEM_LIMIT = 56 * 1024 * 1024
VMEM_LIMIT_FRONT = 60 * 1024 * 1024


def _rms(x, g):
    return x * lax.rsqrt(jnp.mean(x * x, axis=-1, keepdims=True) + EPS) * g


def _rel_buckets(d):
    d = np.maximum(d, 0)
    exact = NUM_BUCKETS // 2
    ratio = np.maximum(d, 1).astype(np.float32) / np.float32(exact)
    large = exact + (np.log(ratio) / np.float32(math.log(MAX_DISTANCE / exact))
                     * np.float32(NUM_BUCKETS - exact)).astype(np.int32)
    large = np.minimum(large, NUM_BUCKETS - 1)
    return np.where(d < exact, d, large).astype(np.int32)


def _table_lookup(bucket, tab_ref, h):
    out = jnp.zeros(bucket.shape, F32)
    for b in range(NUM_BUCKETS):
        out = jnp.where(bucket == b, tab_ref[b, h], out)
    return out


def _gates(xc, wg_ref, gab, gxb, lam):
    xcb = xc.astype(BF16)
    ga, gx = [], []
    for j in range(N_GW):
        gj = jnp.dot(xcb[:, GW * j:GW * (j + 1)], wg_ref[j], preferred_element_type=F32)
        ga.append(gj[:, :GW])
        gx.append(gj[:, GW:])
    r = jax.nn.sigmoid(jnp.concatenate(ga, axis=1) + gab)
    i = jax.nn.sigmoid(jnp.concatenate(gx, axis=1) + gxb)
    log_a = -LRU_C * r * jax.nn.softplus(-lam)
    a = jnp.exp(log_a)
    t = 1.0 - a * a
    b = jnp.where(t > 0.0, t * lax.rsqrt(t), 0.0) * i * xc
    return a, b


def _front_kernel(tab_ref, sink_ref, xa_ref, xb_ref, xt_ref, gm_ref, win_ref, wout_ref, bucket_ref,
                  cw_ref, cb_ref, wg_ref, gab_ref, gxb_ref, lam_ref, nr_ref, na_ref, wu_f, wgt_f, wd_f,
                  x1_ref, zs_ref, hlast_ref, kvlast_ref, xrt_ref, wu_b, wgt_b, wd_b,
                  zp, zb_o, lhs_p, xsp, mixp, xbuf, h_scr, kvbuf, bias_scr,
                  su, sg, sd, tu, tg, td, sem_in, sem_out):
    k = pl.program_id(0)
    last = N_FSTEP - 1

    stages = ((wu_f, su, tu, wu_b), (wgt_f, sg, tg, wgt_b), (wd_f, sd, td, wd_b))

    def slab_in(s):
        return [pltpu.make_async_copy(w.at[pl.ds(s * st.shape[0], st.shape[0])], st, sem_in.at[n])
                for n, (w, st, _, _) in enumerate(stages)]

    def slab_out(s):
        return [pltpu.make_async_copy(t, w.at[pl.ds(s * t.shape[0], t.shape[0])], sem_out.at[n])
                for n, (_, _, t, w) in enumerate(stages)]

    @pl.when(k == 0)
    def _():
        for c in slab_in(0):
            c.start()

    @pl.when(k >= 1)
    def _():
        for c in slab_out(k - 1):
            c.wait()

    @pl.when(k < N_SLAB)
    def _():
        for c in slab_in(k):
            c.wait()
        for _, st, t, _ in stages:
            t[...] = st[...].astype(BF16)
        for c in slab_out(k):
            c.start()

    @pl.when(k < N_SLAB - 1)
    def _():
        for c in slab_in(k + 1):
            c.start()

    def inproj_chunk(c):
        def run():
            cs = slice(c * PC, (c + 1) * PC)
            zp[:, cs] = jnp.dot(lhs_p[...], win_ref[:, cs], preferred_element_type=F32)
        return run

    def outproj_chunk(c):
        def run():
            cs = slice(c * PC, (c + 1) * PC)
            x1_ref[:, cs] = xsp[:, cs] + jnp.dot(mixp[...], wout_ref[:, cs], preferred_element_type=F32)
        return run

    @pl.when(k == 0)
    def _():
        prefix = xt_ref[DEC_BATCH:R_TAIL]
        zb_o[...] = jnp.dot(_rms(prefix, gm_ref[...]).astype(BF16), win_ref[...], preferred_element_type=F32)
        xsp[0:BLOCK] = jnp.zeros((BLOCK, D_MODEL), F32)
        xsp[BLOCK:FB] = prefix
        mixp[...] = jnp.zeros((FB, D_MODEL), BF16)
        xbuf[0:HALO] = jnp.zeros((HALO, D_RNN), F32)
        h_scr[...] = jnp.zeros((HALO, D_RNN), F32)
        kvbuf[...] = jnp.zeros((BLOCK, 2 * D_KV), F32)
        bucket = bucket_ref[...]
        for h in range(N_HEADS):
            bias_scr[h] = _table_lookup(bucket, tab_ref, h)

    def mixer(zb, is_prefix, first_key, fillers):
        fill = iter(fillers)

        def between():
            f = next(fill, None)
            if f is not None:
                f()

        between()
        xr = zb[:, 0:D_RNN]
        xbuf[HALO:HALO + BLOCK] = xr
        cw = cw_ref[...]
        xc = cb_ref[...]
        for j in range(RNN_CONV - 1):
            lo = HALO - (RNN_CONV - 1) + j
            xc = xc + xbuf[lo:lo + BLOCK] * cw[j:j + 1]
        xc = xc + xr * cw[RNN_CONV - 1:RNN_CONV]
        xbuf[0:HALO] = xr[BLOCK - HALO:BLOCK]

        a, b = _gates(xc, wg_ref, gab_ref[...], gxb_ref[...], lam_ref[...])
        between()
        h = h_scr[0:1]
        if is_prefix is not None:
            row = lax.broadcasted_iota(jnp.int32, (BLOCK, D_RNN), 0)
            b = jnp.where(jnp.logical_and(is_prefix, row < N_PAD), 0.0, b)
            h = jnp.where(is_prefix, 0.0, h)

        ng = BLOCK // HALO
        a3 = a.reshape(ng, HALO, D_RNN)
        b3 = b.reshape(ng, HALO, D_RNN)
        sub = lax.broadcasted_iota(jnp.int32, (ng, HALO, D_RNN), 1)
        sh = 1
        while sh < HALO:
            a_prev = pltpu.roll(a3, sh, 1)
            b_prev = pltpu.roll(b3, sh, 1)
            m = sub >= sh
            b3 = jnp.where(m, a3 * b_prev + b3, b3)
            a3 = jnp.where(m, a3 * a_prev, a3)
            sh *= 2
        hs = []
        for g in range(ng):
            hg = a3[g] * h + b3[g]
            hs.append(hg)
            h = hg[HALO - 1:HALO]
        h_all = jnp.concatenate(hs, axis=0)
        h_scr[...] = jnp.broadcast_to(h, (HALO, D_RNN))
        between()
        y_rnn = h_all * jax.nn.gelu(zb[:, O_GR:O_GR + D_RNN])
        mix_a = _rms(y_rnn, nr_ref[...]).astype(BF16)

        q = zb[:, O_Q:O_Q + D_ATTN]
        kv = zb[:, O_KV:O_KV + 2 * D_KV]
        kvp = kvbuf[...]
        kvbuf[...] = kv
        col = lax.broadcasted_iota(jnp.int32, (BLOCK, 2 * BLOCK), 1)
        rowq = lax.broadcasted_iota(jnp.int32, (BLOCK, 2 * BLOCK), 0)
        d = BLOCK + rowq - col
        mask = (d >= 0) & (d < WINDOW) & (col >= first_key)
        outs = []
        for kh in range(N_KV):
            between()
            ks = slice(kh * HEAD_DIM, (kh + 1) * HEAD_DIM)
            vs = slice(D_KV + kh * HEAD_DIM, D_KV + (kh + 1) * HEAD_DIM)
            qs = jnp.concatenate(
                [q[:, (kh * GROUP + g) * HEAD_DIM:(kh * GROUP + g + 1) * HEAD_DIM] for g in range(GROUP)],
                axis=0).astype(BF16)
            kk = jnp.concatenate([kvp[:, ks], kv[:, ks]], axis=0).astype(BF16)
            vv = jnp.concatenate([kvp[:, vs], kv[:, vs]], axis=0).astype(BF16)
            sc = lax.dot_general(qs, kk, (((1,), (1,)), ((), ())), preferred_element_type=F32)
            es, dens = [], []
            for g in range(GROUP):
                hh = kh * GROUP + g
                lg = sc[g * BLOCK:(g + 1) * BLOCK] * SCALE + bias_scr[hh]
                lg = jnp.where(mask, lg, NEG)
                sink = sink_ref[hh]
                mx = jnp.maximum(jnp.max(lg, axis=-1, keepdims=True), sink)
                e = jnp.exp(lg - mx)
                dens.append(jnp.sum(e, axis=-1, keepdims=True) + jnp.exp(sink - mx))
                es.append(e.astype(BF16))
            pv = jnp.dot(jnp.concatenate(es, axis=0), vv, preferred_element_type=F32)
            for g in range(GROUP):
                outs.append(pv[g * BLOCK:(g + 1) * BLOCK] / dens[g])
        between()
        mix_b = _rms(jnp.concatenate(outs, axis=1), na_ref[...]).astype(BF16)

        return jnp.concatenate([mix_a, mix_b], axis=1), h, kv, xr

    xa = jnp.where(k == last, xt_ref[0:DEC_BATCH], xa_ref[...])
    xb = xb_ref[...]
    lhs_p[0:BLOCK] = _rms(xa, gm_ref[...]).astype(BF16)
    lhs_p[BLOCK:FB] = _rms(xb, gm_ref[...]).astype(BF16)
    n_in, n_out = D_IN // PC, D_MODEL // PC
    mix_o, h, kv, xr = mixer(zb_o, k == 0, jnp.where(k == 0, BLOCK + N_PAD, 0),
                             [inproj_chunk(c) for c in range(N_IN_EARLY)])
    mixp[BLOCK:FB] = mix_o
    hlast_ref[...] = jnp.broadcast_to(h, (HALO, D_RNN))
    kvlast_ref[...] = kv
    xrt_ref[...] = xr[BLOCK - HALO:BLOCK]

    mix_e = mixer(zp.at[0:BLOCK], None, jnp.where(k == 0, N_PAD, 0),
                  [inproj_chunk(c) for c in range(N_IN_EARLY, n_in)] + [outproj_chunk(c) for c in range(n_out)])[0]
    mixp[0:BLOCK] = mix_e
    xsp[0:BLOCK] = xa
    xsp[BLOCK:FB] = xb
    zb_o[...] = zp[BLOCK:FB]

    @pl.when(k == last)
    def _():
        zs_ref[...] = zp[0:BLOCK]


def _front(tab, sinks, xp, xt, gm, w_in, w_out, bucket, cw, cb, wg, gab, gxb, lam, nr, na, w_up, w_gate, w_down):
    ffn_w = (w_up, w_gate, w_down)
    slab = lambda w: (w.shape[0] // N_SLAB, w.shape[1])
    hbm = pl.BlockSpec(memory_space=pl.ANY)
    vec = lambda n: pl.BlockSpec((1, n), lambda k: (0, 0))
    smem = pl.BlockSpec(memory_space=pltpu.SMEM)
    once = lambda shape: pl.BlockSpec(shape, lambda k: (0,) * len(shape), pipeline_mode=pl.Buffered(1))
    return pl.pallas_call(
        _front_kernel,
        grid=(N_FSTEP,),
        in_specs=[
            smem, smem,
            pl.BlockSpec((BLOCK, D_MODEL), lambda k: (jnp.minimum(2 * k, N_PBLK - 1), 0)),
            pl.BlockSpec((BLOCK, D_MODEL), lambda k: (jnp.minimum(2 * k + 1, N_PBLK - 1), 0)),
            once((R_TAIL, D_MODEL)),
            vec(D_MODEL),
            once((D_MODEL, D_IN)),
            once((D_MODEL, D_MODEL)),
            pl.BlockSpec((BLOCK, 2 * BLOCK), lambda k: (0, 0)),
            pl.BlockSpec((RNN_CONV, D_RNN), lambda k: (0, 0)),
            vec(D_RNN),
            pl.BlockSpec((N_GW, GW, 2 * GW), lambda k: (0, 0, 0)),
            vec(D_RNN), vec(D_RNN), vec(D_RNN), vec(D_RNN), vec(D_ATTN),
            hbm, hbm, hbm,
        ],
        out_specs=[
            pl.BlockSpec((FB, D_MODEL), lambda k: (jnp.where(k == 0, SEQ // FB, k - 1), 0)),
            pl.BlockSpec((BLOCK, D_IN), lambda k: (0, 0)),
            pl.BlockSpec((HALO, D_RNN), lambda k: (0, 0)),
            pl.BlockSpec((BLOCK, 2 * D_KV), lambda k: (0, 0)),
            pl.BlockSpec((HALO, D_RNN), lambda k: (0, 0)),
            hbm, hbm, hbm,
        ],
        out_shape=[
            jax.ShapeDtypeStruct((R_ALL, D_MODEL), F32),
            jax.ShapeDtypeStruct((DEC_BATCH, D_IN), F32),
            jax.ShapeDtypeStruct((HALO, D_RNN), F32),
            jax.ShapeDtypeStruct((BLOCK, 2 * D_KV), F32),
            jax.ShapeDtypeStruct((HALO, D_RNN), F32),
        ] + [jax.ShapeDtypeStruct(w.shape, BF16) for w in ffn_w],
        scratch_shapes=[
            pltpu.VMEM((FB, D_IN), F32),
            pltpu.VMEM((BLOCK, D_IN), F32),
            pltpu.VMEM((FB, D_MODEL), BF16),
            pltpu.VMEM((FB, D_MODEL), F32),
            pltpu.VMEM((FB, D_MODEL), BF16),
            pltpu.VMEM((HALO + BLOCK, D_RNN), F32),
            pltpu.VMEM((HALO, D_RNN), F32),
            pltpu.VMEM((BLOCK, 2 * D_KV), F32),
            pltpu.VMEM((N_HEADS, BLOCK, 2 * BLOCK), F32),
        ] + [pltpu.VMEM(slab(w), F32) for w in ffn_w] + [pltpu.VMEM(slab(w), BF16) for w in ffn_w] + [
            pltpu.SemaphoreType.DMA((len(ffn_w),)),
            pltpu.SemaphoreType.DMA((len(ffn_w),)),
        ],
        compiler_params=pltpu.CompilerParams(
            dimension_semantics=("arbitrary",), vmem_limit_bytes=VMEM_LIMIT_FRONT),
        name="front",
    )(tab, sinks, xp, xp, xt, gm, w_in, w_out, bucket, cw, cb, wg, gab, gxb, lam, nr, na, *ffn_w)


def _smixer_kernel(tab_ref, sink_ref, z_ref, st_ref, h0_ref, kc_ref, vc_ref, xs_ref, wout_ref,
                   bucket_ref, cw_ref, cb_ref, wg_ref, gab_ref, gxb_ref, lam_ref, nr_ref, na_ref,
                   x1s_ref, conv_ref, h_ref, ko_ref, vo_ref, bias_scr, mix_scr):
    c = pl.program_id(0)
    nrow = N_HEADS * SB
    ncol = SB * WINDOW * N_KV

    @pl.when(c == 0)
    def _():
        bucket = bucket_ref[...]
        for h in range(N_HEADS):
            bias_scr[h * SB:(h + 1) * SB] = _table_lookup(bucket, tab_ref, h)

    r0 = pl.multiple_of(c * SB, SB)

    xr = z_ref[:, 0:D_RNN]
    st = st_ref[...]
    cw = cw_ref[...]
    xc = cb_ref[...]
    for j in range(RNN_CONV - 1):
        xc = xc + st[:, j * D_RNN:(j + 1) * D_RNN] * cw[j:j + 1]
    xc = xc + xr * cw[RNN_CONV - 1:RNN_CONV]
    conv_ref[:, 0:(RNN_CONV - 2) * D_RNN] = st[:, D_RNN:]
    conv_ref[:, (RNN_CONV - 2) * D_RNN:] = xr
    a, b = _gates(xc, wg_ref, gab_ref[...], gxb_ref[...], lam_ref[...])
    h = a * h0_ref[...] + b
    h_ref[...] = h
    y_rnn = h * jax.nn.gelu(z_ref[:, O_GR:O_GR + D_RNN])
    mix_scr[pl.ds(r0, SB), 0:D_RNN] = _rms(y_rnn, nr_ref[...]).astype(BF16)

    q = z_ref[:, O_Q:O_Q + D_ATTN]
    kv = z_ref[:, O_KV:O_KV + 2 * D_KV]
    col = lax.broadcasted_iota(jnp.int32, (nrow, ncol), 1)
    row = lax.broadcasted_iota(jnp.int32, (nrow, ncol), 0)
    pos = (col >> 1) & (WINDOW - 1)
    valid = ((col >> 8) == (row & (SB - 1))) & ((col & 1) == (row >> 6)) & (pos >= 1)
    qs = jnp.concatenate([q[:, h * HEAD_DIM:(h + 1) * HEAD_DIM] for h in range(N_HEADS)], axis=0).astype(BF16)
    new_rows = lambda off: jnp.concatenate(
        [kv[:, off + (h // GROUP) * HEAD_DIM:off + (h // GROUP + 1) * HEAD_DIM] for h in range(N_HEADS)],
        axis=0).astype(BF16).astype(F32)
    k_new = new_rows(0)
    v_new = new_rows(D_KV)
    sc = lax.dot_general(qs, kc_ref[...].astype(BF16), (((1,), (1,)), ((), ())), preferred_element_type=F32)
    lg = jnp.where(valid, sc * SCALE + bias_scr[...], NEG)
    rh = lax.broadcasted_iota(jnp.int32, (nrow, 1), 0) >> 4
    sink = jnp.zeros((nrow, 1), F32)
    bias_new = jnp.zeros((nrow, 1), F32)
    for h in range(N_HEADS):
        sink = jnp.where(rh == h, sink_ref[h], sink)
        bias_new = jnp.where(rh == h, tab_ref[0, h], bias_new)
    lg_new = jnp.sum(qs.astype(F32) * k_new, axis=-1, keepdims=True) * SCALE + bias_new
    mx = jnp.maximum(jnp.maximum(jnp.max(lg, axis=-1, keepdims=True), lg_new), sink)
    e = jnp.exp(lg - mx)
    e_new = jnp.exp(lg_new - mx)
    den = jnp.sum(e, axis=-1, keepdims=True) + e_new + jnp.exp(sink - mx)
    pv = jnp.dot(e.astype(BF16), vc_ref[...].astype(BF16), preferred_element_type=F32)
    o = (pv + e_new.astype(BF16).astype(F32) * v_new) / den
    y_attn = jnp.concatenate([o[h * SB:(h + 1) * SB] for h in range(N_HEADS)], axis=1)
    mix_scr[pl.ds(r0, SB), D_RNN:] = _rms(y_attn, na_ref[...]).astype(BF16)

    per = WINDOW * N_KV
    slide = lambda ref: pltpu.roll(ref[...].reshape(SB, per, HEAD_DIM), per - N_KV, 1).reshape(SB * per, HEAD_DIM)
    ko_ref[...] = slide(kc_ref)
    vo_ref[...] = slide(vc_ref)
    for bi in range(SB):
        for kh in range(N_KV):
            r = (bi + 1) * per - N_KV + kh
            ko_ref[r:r + 1, :] = kv[bi:bi + 1, kh * HEAD_DIM:(kh + 1) * HEAD_DIM]
            vo_ref[r:r + 1, :] = kv[bi:bi + 1, D_KV + kh * HEAD_DIM:D_KV + (kh + 1) * HEAD_DIM]

    @pl.when(c == DEC_BATCH // SB - 1)
    def _():
        x1s_ref[...] = xs_ref[...] + jnp.dot(mix_scr[...], wout_ref[...], preferred_element_type=F32)


def _smixer(tab, sinks, zs, st, h0, kc, vc, xs, w_out, bucket, cw, cb, wg, gab, gxb, lam, nr, na):
    vec = lambda n: pl.BlockSpec((1, n), lambda c: (0, 0))
    smem = pl.BlockSpec(memory_space=pltpu.SMEM)
    cache = pl.BlockSpec((SB * WINDOW * N_KV, HEAD_DIM), lambda c: (c, 0))
    return pl.pallas_call(
        _smixer_kernel,
        grid=(DEC_BATCH // SB,),
        in_specs=[
            smem, smem,
            pl.BlockSpec((SB, D_IN), lambda c: (c, 0)),
            pl.BlockSpec((SB, (RNN_CONV - 1) * D_RNN), lambda c: (c, 0)),
            pl.BlockSpec((SB, D_RNN), lambda c: (c, 0)),
            cache, cache,
            pl.BlockSpec((DEC_BATCH, D_MODEL), lambda c: (0, 0)),
            pl.BlockSpec((D_MODEL, D_MODEL), lambda c: (0, 0), pipeline_mode=pl.Buffered(1)),
            pl.BlockSpec((SB, SB * WINDOW * N_KV), lambda c: (0, 0)),
            pl.BlockSpec((RNN_CONV, D_RNN), lambda c: (0, 0)),
            vec(D_RNN),
            pl.BlockSpec((N_GW, GW, 2 * GW), lambda c: (0, 0, 0)),
            vec(D_RNN), vec(D_RNN), vec(D_RNN), vec(D_RNN), vec(D_ATTN),
        ],
        out_specs=[
            pl.BlockSpec((DEC_BATCH, D_MODEL), lambda c: (0, 0)),
            pl.BlockSpec((SB, (RNN_CONV - 1) * D_RNN), lambda c: (c, 0)),
            pl.BlockSpec((SB, D_RNN), lambda c: (c, 0)),
            cache, cache,
        ],
        out_shape=[
            jax.ShapeDtypeStruct((DEC_BATCH, D_MODEL), F32),
            jax.ShapeDtypeStruct((DEC_BATCH, (RNN_CONV - 1) * D_RNN), F32),
            jax.ShapeDtypeStruct((DEC_BATCH, D_RNN), F32),
            jax.ShapeDtypeStruct((DEC_BATCH * WINDOW * N_KV, HEAD_DIM), F32),
            jax.ShapeDtypeStruct((DEC_BATCH * WINDOW * N_KV, HEAD_DIM), F32),
        ],
        scratch_shapes=[
            pltpu.VMEM((N_HEADS * SB, SB * WINDOW * N_KV), F32),
            pltpu.VMEM((DEC_BATCH, D_MODEL), BF16),
        ],
        compiler_params=pltpu.CompilerParams(
            dimension_semantics=("arbitrary",), vmem_limit_bytes=VMEM_LIMIT),
        name="smixer",
    )(tab, sinks, zs, st, h0, kc, vc, xs, w_out, bucket, cw, cb, wg, gab, gxb, lam, nr, na)


U_LO = LAST_P - HALO
U_HI = LAST_P + DEC_BATCH
N_UST = U_HI - U_LO


def _ffn_kernel(x1_ref, halo_ref, x1s_ref, st0_ref, st1_ref, gn_ref, gf_ref, wu_ref, wg_ref, cw_ref, cb_ref,
                wd_ref, y_ref, ys_ref, ust_ref, h2_scr, ubuf, abuf):
    i = pl.program_id(0)
    j = pl.program_id(1)

    @pl.when(j == 0)
    def _():
        h2_scr[0:FH] = _rms(halo_ref[...], gn_ref[...]).astype(BF16)
        h2_scr[FH:FH + TM] = _rms(x1_ref[...], gn_ref[...]).astype(BF16)
        y_ref[...] = jnp.zeros((TM, D_MODEL), F32)

        @pl.when(i == N_RT - 1)
        def _():
            h2_scr[FH + LAST_P:FH + U_HI] = _rms(x1s_ref[...], gn_ref[...]).astype(BF16)

    ubuf[...] = jnp.dot(h2_scr[...], wu_ref[...], preferred_element_type=F32)
    gate = jnp.dot(h2_scr[FH:FH + TM], wg_ref[...], preferred_element_type=F32)
    cw = cw_ref[...]
    cb = cb_ref[...]
    for c in range(TF // KC):
        cs = slice(c * KC, (c + 1) * KC)
        tap = lambda r0, n: ubuf[r0:r0 + n, cs]
        uc = (cb[:, cs] + tap(FH - 2, TM) * cw[0:1, cs] + tap(FH - 1, TM) * cw[1:2, cs] + tap(FH, TM) * cw[2:3, cs])
        act = jax.nn.gelu(uc) * gate[:, cs]
        abuf[:, cs] = act.astype(BF16)
        ucs = (cb[:, cs] + st0_ref[:, cs] * cw[0:1, cs] + st1_ref[:, cs] * cw[1:2, cs]
               + tap(FH + LAST_P, DEC_BATCH) * cw[2:3, cs])
        act_s = jax.nn.gelu(ucs) * gate[LAST_P:U_HI, cs]
        abuf[LAST_P:U_HI, cs] = jnp.where(i == N_RT - 1, act_s, act[LAST_P:U_HI]).astype(BF16)
        y_ref[...] += jnp.dot(abuf[:, cs], wd_ref[cs, :], preferred_element_type=F32)
    ust_ref[...] = ubuf[FH + U_LO:FH + U_HI]

    @pl.when(j == N_FT - 1)
    def _():
        @pl.when(i == N_RT - 1)
        def _():
            ys_ref[...] = _rms(x1s_ref[...] + y_ref[LAST_P:U_HI], gf_ref[...])

        y_ref[...] = _rms(x1_ref[...] + y_ref[...], gf_ref[...])


def _ffn(x1, x1s, st, gn, gf, wu, wg, cw, cb, wd):
    def halo_idx(i, j):
        return (jnp.where(i == 0, R_ALL // FH - 1, i * (TM // FH) - 1), 0)

    return pl.pallas_call(
        _ffn_kernel,
        grid=(N_RT, N_FT),
        in_specs=[
            pl.BlockSpec((TM, D_MODEL), lambda i, j: (i, 0)),
            pl.BlockSpec((FH, D_MODEL), halo_idx),
            pl.BlockSpec((DEC_BATCH, D_MODEL), lambda i, j: (0, 0)),
            pl.BlockSpec((DEC_BATCH, TF), lambda i, j: (0, j)),
            pl.BlockSpec((DEC_BATCH, TF), lambda i, j: (0, N_FT + j)),
            pl.BlockSpec((1, D_MODEL), lambda i, j: (0, 0)),
            pl.BlockSpec((1, D_MODEL), lambda i, j: (0, 0)),
            pl.BlockSpec((D_MODEL, TF), lambda i, j: (0, j)),
            pl.BlockSpec((D_MODEL, TF), lambda i, j: (0, j)),
            pl.BlockSpec((FFN_CONV, TF), lambda i, j: (0, j)),
            pl.BlockSpec((1, TF), lambda i, j: (0, j)),
            pl.BlockSpec((TF, D_MODEL), lambda i, j: (j, 0)),
        ],
        out_specs=[
            pl.BlockSpec((TM, D_MODEL), lambda i, j: (i, 0)),
            pl.BlockSpec((DEC_BATCH, D_MODEL), lambda i, j: (0, 0)),
            pl.BlockSpec((N_UST, TF), lambda i, j: (i, j)),
        ],
        out_shape=[
            jax.ShapeDtypeStruct((SEQ, D_MODEL), F32),
            jax.ShapeDtypeStruct((DEC_BATCH, D_MODEL), F32),
            jax.ShapeDtypeStruct((N_RT * N_UST, D_FF), F32),
        ],
        scratch_shapes=[
            pltpu.VMEM((FH + TM, D_MODEL), BF16),
            pltpu.VMEM((FH + TM, TF), F32),
            pltpu.VMEM((TM, TF), BF16),
        ],
        compiler_params=pltpu.CompilerParams(
            dimension_semantics=("arbitrary", "arbitrary"), vmem_limit_bytes=VMEM_LIMIT),
        name="ffn",
    )(x1, x1, x1s, st, st, gn, gf, wu, wg, cw, cb, wd)


def _gate_weights(wa, wx):
    per = GW // RNN_BLOCK
    eye = jnp.eye(per, dtype=wa.dtype)

    def bd(w):
        w = w.reshape(N_GW, per, RNN_BLOCK, RNN_BLOCK)
        return jnp.einsum('gpcd,pq->gpcqd', w, eye).reshape(N_GW, GW, GW)

    return jnp.concatenate([bd(wa), bd(wx)], axis=-1).astype(BF16)


def kernel(x_prompt, x_sample, state_rnn_conv, state_rnn_h, cache_k_win, cache_v_win, state_ffn_conv,
           meta_tokens, rel_bias_table, norm_mix, w_in, rnn_conv_w, rnn_conv_b, gate_a_w, gate_a_b,
           gate_x_w, gate_x_b, rnn_lambda, attn_sinks, norm_rnn_out, norm_attn_out, w_out, norm_ffn,
           w_up, w_gate, ffn_conv_w, ffn_conv_b, w_down, norm_final):
    l = 0
    xp = x_prompt[0]
    xs = x_sample[:, 0, :]
    xt = jnp.concatenate([xs, jnp.zeros((N_PAD, D_MODEL), F32), meta_tokens], axis=0)
    row = lambda v: v.reshape(1, -1)

    qi = np.arange(BLOCK)[:, None]
    sj = np.arange(2 * BLOCK)[None, :]
    bucket_p = jnp.asarray(_rel_buckets(BLOCK + qi - sj))
    pos = (np.arange(SB * WINDOW * N_KV) >> 1) & (WINDOW - 1)
    bucket_s = jnp.asarray(np.tile(_rel_buckets(WINDOW - pos)[None, :], (SB, 1)))

    wg = _gate_weights(gate_a_w[l], gate_x_w[l])
    seq_w = (rnn_conv_w[l], row(rnn_conv_b[l]), wg, row(gate_a_b[l]), row(gate_x_b[l]), row(rnn_lambda[l]),
             row(norm_rnn_out[l]), row(norm_attn_out[l]))
    w_out_b = w_out[l].astype(BF16)

    x1, zs, h_last, kv_last, xr_tail, w_up_b, w_gate_b, w_down_b = _front(
        rel_bias_table, attn_sinks[l], xp, xt, row(norm_mix[l]), w_in[l].astype(BF16), w_out_b, bucket_p, *seq_w,
        w_up[l], w_gate[l], w_down[l])
    x1s, conv_s, h_s, k_s, v_s = _smixer(
        rel_bias_table, attn_sinks[l], zs,
        state_rnn_conv[l].reshape(DEC_BATCH, (RNN_CONV - 1) * D_RNN), state_rnn_h[l],
        cache_k_win[l].reshape(DEC_BATCH * WINDOW * N_KV, HEAD_DIM),
        cache_v_win[l].reshape(DEC_BATCH * WINDOW * N_KV, HEAD_DIM),
        xs, w_out_b, bucket_s, *seq_w)
    y_p, y_s, ust = _ffn(x1, x1s, state_ffn_conv[l].reshape(DEC_BATCH, (FFN_CONV - 1) * D_FF),
                         row(norm_ffn[l]), row(norm_final), w_up_b, w_gate_b,
                         ffn_conv_w[l], row(ffn_conv_b[l]), w_down_b)

    ust = ust[(N_RT - 1) * N_UST:]
    p_states = (
        xr_tail[HALO - (RNN_CONV - 1):HALO][None, None],
        h_last[0:1][None],
        kv_last[:, :D_KV].reshape(1, 1, WINDOW, N_KV, HEAD_DIM),
        kv_last[:, D_KV:].reshape(1, 1, WINDOW, N_KV, HEAD_DIM),
        ust[HALO - (FFN_CONV - 1):HALO][None, None],
    )
    s_states = (
        conv_s.reshape(1, DEC_BATCH, RNN_CONV - 1, D_RNN),
        h_s[None],
        k_s.reshape(1, DEC_BATCH, WINDOW, N_KV, HEAD_DIM),
        v_s.reshape(1, DEC_BATCH, WINDOW, N_KV, HEAD_DIM),
        jnp.stack([state_ffn_conv[l][:, FFN_CONV - 2, :], ust[HALO:]], axis=1)[None],
    )
    return (y_p[None], y_s[:, None, :]) + p_states + s_states
```

```python
import math

import numpy as np
import jax
import jax.numpy as jnp
from jax import lax
from jax.experimental import pallas as pl
from jax.experimental.pallas import tpu as pltpu

F32 = jnp.float32
BF16 = jnp.bfloat16

D_MODEL = 2048
SEQ = 8192
DEC_BATCH = 128
D_RNN = 1024
N_RNN_BLOCKS = 16
RNN_BLOCK = D_RNN // N_RNN_BLOCKS
RNN_CONV = 4
LRU_C = 8.0
N_HEADS = 8
HEAD_DIM = 128
N_KV = 2
GROUP = N_HEADS // N_KV
D_ATTN = N_HEADS * HEAD_DIM
WINDOW = 128
BLOCK = 128
NUM_BUCKETS = 32
MAX_DISTANCE = 128
D_FF = 3 * D_MODEL
FFN_CONV = 3
N_META = 16
EPS = 1e-6
NEG = -1e30
D_KV = N_KV * HEAD_DIM
D_IN = 2 * D_RNN + D_ATTN + 2 * D_KV
SCALE = HEAD_DIM ** -0.5
O_GR = D_RNN
O_Q = 2 * D_RNN
O_KV = 2 * D_RNN + D_ATTN

N_PAD = BLOCK - N_META
R_TAIL = DEC_BATCH + BLOCK
R_ALL = SEQ + R_TAIL
N_PBLK = SEQ // BLOCK

FB = 2 * BLOCK
N_FSTEP = (N_PBLK + 2) // 2
PC = 512
N_IN_EARLY = 5
N_SLAB = N_FSTEP - 1

TM = 768
N_RT = R_ALL // TM
LAST_P = SEQ - (N_RT - 1) * TM
TF = 512
N_FT = D_FF // TF
KC = 256
HALO = 8
FH = 16
GW = 256
N_GW = D_RNN // GW
SB = 16
VMEM_LIMIT = 56 * 1024 * 1024
VMEM_LIMIT_FRONT = 60 * 1024 * 1024


def _rms(x, g):
    return x * lax.rsqrt(jnp.mean(x * x, axis=-1, keepdims=True) + EPS) * g


def _rel_buckets(d):
    d = np.maximum(d, 0)
    exact = NUM_BUCKETS // 2
    ratio = np.maximum(d, 1).astype(np.float32) / np.float32(exact)
    large = exact + (np.log(ratio) / np.float32(math.log(MAX_DISTANCE / exact))
                     * np.float32(NUM_BUCKETS - exact)).astype(np.int32)
    large = np.minimum(large, NUM_BUCKETS - 1)
    return np.where(d < exact, d, large).astype(np.int32)


def _table_lookup(bucket, tab_ref):
    outs = [jnp.zeros(bucket.shape, F32) for _ in range(N_HEADS)]
    for b in range(NUM_BUCKETS):
        hit = bucket == b
        outs = [jnp.where(hit, tab_ref[b, h], o) for h, o in enumerate(outs)]
    return outs


def _gates(xc, wg_ref, gab, gxb, lam):
    xcb = xc.astype(BF16)
    ga, gx = [], []
    for j in range(N_GW):
        gj = jnp.dot(xcb[:, GW * j:GW * (j + 1)], wg_ref[j], preferred_element_type=F32)
        ga.append(gj[:, :GW])
        gx.append(gj[:, GW:])
    r = jax.nn.sigmoid(jnp.concatenate(ga, axis=1) + gab)
    i = jax.nn.sigmoid(jnp.concatenate(gx, axis=1) + gxb)
    log_a = -LRU_C * r * jax.nn.softplus(-lam)
    a = jnp.exp(log_a)
    t = 1.0 - a * a
    b = jnp.where(t > 0.0, t * lax.rsqrt(t), 0.0) * i * xc
    return a, b


def _front_kernel(tab_ref, sink_ref, xa_ref, xb_ref, xt_ref, gm_ref, win_ref, wout_ref, bucket_ref,
                  cw_ref, cb_ref, wg_ref, gab_ref, gxb_ref, lam_ref, nr_ref, na_ref, wu_f, wgt_f, wd_f,
                  x1_ref, zs_ref, hlast_ref, kvlast_ref, xrt_ref, wu_b, wgt_b, wd_b,
                  zp, zb_o, lhs_p, xsp, mixp, xbuf, h_scr, kvbuf, bias_scr,
                  su, sg, sd, tu, tg, td, sem_in, sem_out):
    k = pl.program_id(0)
    last = N_FSTEP - 1

    stages = ((wu_f, su, tu, wu_b), (wgt_f, sg, tg, wgt_b), (wd_f, sd, td, wd_b))

    def slab_in(s):
        return [pltpu.make_async_copy(w.at[pl.ds(s * st.shape[0], st.shape[0])], st, sem_in.at[n])
                for n, (w, st, _, _) in enumerate(stages)]

    def slab_out(s):
        return [pltpu.make_async_copy(t, w.at[pl.ds(s * t.shape[0], t.shape[0])], sem_out.at[n])
                for n, (_, _, t, w) in enumerate(stages)]

    @pl.when(k == 0)
    def _():
        for c in slab_in(0):
            c.start()

    @pl.when(k >= 1)
    def _():
        for c in slab_out(k - 1):
            c.wait()

    @pl.when(k < N_SLAB)
    def _():
        for c in slab_in(k):
            c.wait()
        for _, st, t, _ in stages:
            t[...] = st[...].astype(BF16)
        for c in slab_out(k):
            c.start()

    @pl.when(k < N_SLAB - 1)
    def _():
        for c in slab_in(k + 1):
            c.start()

    def inproj_chunk(c):
        def run():
            cs = slice(c * PC, (c + 1) * PC)
            zp[:, cs] = jnp.dot(lhs_p[...], win_ref[:, cs], preferred_element_type=F32)
        return run

    def outproj_chunk(c):
        def run():
            cs = slice(c * PC, (c + 1) * PC)
            x1_ref[:, cs] = xsp[:, cs] + jnp.dot(mixp[...], wout_ref[:, cs], preferred_element_type=F32)
        return run

    @pl.when(k == 0)
    def _():
        prefix = xt_ref[DEC_BATCH:R_TAIL]
        zb_o[...] = jnp.dot(_rms(prefix, gm_ref[...]).astype(BF16), win_ref[...], preferred_element_type=F32)
        xsp[0:BLOCK] = jnp.zeros((BLOCK, D_MODEL), F32)
        xsp[BLOCK:FB] = prefix
        mixp[...] = jnp.zeros((FB, D_MODEL), BF16)
        xbuf[0:HALO] = jnp.zeros((HALO, D_RNN), F32)
        h_scr[...] = jnp.zeros((HALO, D_RNN), F32)
        kvbuf[...] = jnp.zeros((BLOCK, 2 * D_KV), F32)
        for h, bias in enumerate(_table_lookup(bucket_ref[...], tab_ref)):
            bias_scr[h] = bias

    def mixer(zb, is_prefix, first_key, fillers):
        fill = iter(fillers)

        def between():
            f = next(fill, None)
            if f is not None:
                f()

        between()
        xr = zb[:, 0:D_RNN]
        xbuf[HALO:HALO + BLOCK] = xr
        cw = cw_ref[...]
        xc = cb_ref[...]
        for j in range(RNN_CONV - 1):
            lo = HALO - (RNN_CONV - 1) + j
            xc = xc + xbuf[lo:lo + BLOCK] * cw[j:j + 1]
        xc = xc + xr * cw[RNN_CONV - 1:RNN_CONV]
        xbuf[0:HALO] = xr[BLOCK - HALO:BLOCK]

        a, b = _gates(xc, wg_ref, gab_ref[...], gxb_ref[...], lam_ref[...])
        between()
        h = h_scr[0:1]
        if is_prefix is not None:
            row = lax.broadcasted_iota(jnp.int32, (BLOCK, D_RNN), 0)
            b = jnp.where(jnp.logical_and(is_prefix, row < N_PAD), 0.0, b)
            h = jnp.where(is_prefix, 0.0, h)

        ng = BLOCK // HALO
        a3 = a.reshape(ng, HALO, D_RNN)
        b3 = b.reshape(ng, HALO, D_RNN)
        sub = lax.broadcasted_iota(jnp.int32, (ng, HALO, D_RNN), 1)
        sh = 1
        while sh < HALO:
            a_prev = pltpu.roll(a3, sh, 1)
            b_prev = pltpu.roll(b3, sh, 1)
            m = sub >= sh
            b3 = jnp.where(m, a3 * b_prev + b3, b3)
            a3 = jnp.where(m, a3 * a_prev, a3)
            sh *= 2
        hs = []
        for g in range(ng):
            hg = a3[g] * h + b3[g]
            hs.append(hg)
            h = hg[HALO - 1:HALO]
        h_all = jnp.concatenate(hs, axis=0)
        h_scr[...] = jnp.broadcast_to(h, (HALO, D_RNN))
        between()
        y_rnn = h_all * jax.nn.gelu(zb[:, O_GR:O_GR + D_RNN])
        mix_a = _rms(y_rnn, nr_ref[...]).astype(BF16)

        q = zb[:, O_Q:O_Q + D_ATTN]
        kv = zb[:, O_KV:O_KV + 2 * D_KV]
        kvp = kvbuf[...]
        kvbuf[...] = kv
        col = lax.broadcasted_iota(jnp.int32, (BLOCK, 2 * BLOCK), 1)
        rowq = lax.broadcasted_iota(jnp.int32, (BLOCK, 2 * BLOCK), 0)
        d = BLOCK + rowq - col
        mask = (d >= 0) & (d < WINDOW) & (col >= first_key)
        outs = []
        for kh in range(N_KV):
            between()
            ks = slice(kh * HEAD_DIM, (kh + 1) * HEAD_DIM)
            vs = slice(D_KV + kh * HEAD_DIM, D_KV + (kh + 1) * HEAD_DIM)
            qs = jnp.concatenate(
                [q[:, (kh * GROUP + g) * HEAD_DIM:(kh * GROUP + g + 1) * HEAD_DIM] for g in range(GROUP)],
                axis=0).astype(BF16)
            kk = jnp.concatenate([kvp[:, ks], kv[:, ks]], axis=0).astype(BF16)
            vv = jnp.concatenate([kvp[:, vs], kv[:, vs]], axis=0).astype(BF16)
            sc = lax.dot_general(qs, kk, (((1,), (1,)), ((), ())), preferred_element_type=F32)
            es, dens = [], []
            for g in range(GROUP):
                hh = kh * GROUP + g
                lg = sc[g * BLOCK:(g + 1) * BLOCK] * SCALE + bias_scr[hh]
                lg = jnp.where(mask, lg, NEG)
                sink = sink_ref[hh]
                mx = jnp.maximum(jnp.max(lg, axis=-1, keepdims=True), sink)
                e = jnp.exp(lg - mx)
                dens.append(jnp.sum(e, axis=-1, keepdims=True) + jnp.exp(sink - mx))
                es.append(e.astype(BF16))
            pv = jnp.dot(jnp.concatenate(es, axis=0), vv, preferred_element_type=F32)
            for g in range(GROUP):
                outs.append(pv[g * BLOCK:(g + 1) * BLOCK] * (1.0 / dens[g]))
        between()
        mix_b = _rms(jnp.concatenate(outs, axis=1), na_ref[...]).astype(BF16)

        return jnp.concatenate([mix_a, mix_b], axis=1), h, kv, xr

    xa = jnp.where(k == last, xt_ref[0:DEC_BATCH], xa_ref[...])
    xb = xb_ref[...]
    lhs_p[0:BLOCK] = _rms(xa, gm_ref[...]).astype(BF16)
    lhs_p[BLOCK:FB] = _rms(xb, gm_ref[...]).astype(BF16)
    n_in, n_out = D_IN // PC, D_MODEL // PC
    mix_o, h, kv, xr = mixer(zb_o, k == 0, jnp.where(k == 0, BLOCK + N_PAD, 0),
                             [inproj_chunk(c) for c in range(N_IN_EARLY)])
    mixp[BLOCK:FB] = mix_o
    hlast_ref[...] = jnp.broadcast_to(h, (HALO, D_RNN))
    kvlast_ref[...] = kv
    xrt_ref[...] = xr[BLOCK - HALO:BLOCK]

    mix_e = mixer(zp.at[0:BLOCK], None, jnp.where(k == 0, N_PAD, 0),
                  [inproj_chunk(c) for c in range(N_IN_EARLY, n_in)] + [outproj_chunk(c) for c in range(n_out)])[0]
    mixp[0:BLOCK] = mix_e
    xsp[0:BLOCK] = xa
    xsp[BLOCK:FB] = xb
    zb_o[...] = zp[BLOCK:FB]

    @pl.when(k == last)
    def _():
        zs_ref[...] = zp[0:BLOCK]


def _front(tab, sinks, xp, xt, gm, w_in, w_out, bucket, cw, cb, wg, gab, gxb, lam, nr, na, w_up, w_gate, w_down):
    ffn_w = (w_up, w_gate, w_down)
    slab = lambda w: (w.shape[0] // N_SLAB, w.shape[1])
    hbm = pl.BlockSpec(memory_space=pl.ANY)
    vec = lambda n: pl.BlockSpec((1, n), lambda k: (0, 0))
    smem = pl.BlockSpec(memory_space=pltpu.SMEM)
    once = lambda shape: pl.BlockSpec(shape, lambda k: (0,) * len(shape), pipeline_mode=pl.Buffered(1))
    return pl.pallas_call(
        _front_kernel,
        grid=(N_FSTEP,),
        in_specs=[
            smem, smem,
            pl.BlockSpec((BLOCK, D_MODEL), lambda k: (jnp.minimum(2 * k, N_PBLK - 1), 0)),
            pl.BlockSpec((BLOCK, D_MODEL), lambda k: (jnp.minimum(2 * k + 1, N_PBLK - 1), 0)),
            once((R_TAIL, D_MODEL)),
            vec(D_MODEL),
            once((D_MODEL, D_IN)),
            once((D_MODEL, D_MODEL)),
            pl.BlockSpec((BLOCK, 2 * BLOCK), lambda k: (0, 0)),
            pl.BlockSpec((RNN_CONV, D_RNN), lambda k: (0, 0)),
            vec(D_RNN),
            pl.BlockSpec((N_GW, GW, 2 * GW), lambda k: (0, 0, 0)),
            vec(D_RNN), vec(D_RNN), vec(D_RNN), vec(D_RNN), vec(D_ATTN),
            hbm, hbm, hbm,
        ],
        out_specs=[
            pl.BlockSpec((FB, D_MODEL), lambda k: (jnp.where(k == 0, SEQ // FB, k - 1), 0)),
            pl.BlockSpec((BLOCK, D_IN), lambda k: (0, 0)),
            pl.BlockSpec((HALO, D_RNN), lambda k: (0, 0)),
            pl.BlockSpec((BLOCK, 2 * D_KV), lambda k: (0, 0)),
            pl.BlockSpec((HALO, D_RNN), lambda k: (0, 0)),
            hbm, hbm, hbm,
        ],
        out_shape=[
            jax.ShapeDtypeStruct((R_ALL, D_MODEL), F32),
            jax.ShapeDtypeStruct((DEC_BATCH, D_IN), F32),
            jax.ShapeDtypeStruct((HALO, D_RNN), F32),
            jax.ShapeDtypeStruct((BLOCK, 2 * D_KV), F32),
            jax.ShapeDtypeStruct((HALO, D_RNN), F32),
        ] + [jax.ShapeDtypeStruct(w.shape, BF16) for w in ffn_w],
        scratch_shapes=[
            pltpu.VMEM((FB, D_IN), F32),
            pltpu.VMEM((BLOCK, D_IN), F32),
            pltpu.VMEM((FB, D_MODEL), BF16),
            pltpu.VMEM((FB, D_MODEL), F32),
            pltpu.VMEM((FB, D_MODEL), BF16),
            pltpu.VMEM((HALO + BLOCK, D_RNN), F32),
            pltpu.VMEM((HALO, D_RNN), F32),
            pltpu.VMEM((BLOCK, 2 * D_KV), F32),
            pltpu.VMEM((N_HEADS, BLOCK, 2 * BLOCK), F32),
        ] + [pltpu.VMEM(slab(w), F32) for w in ffn_w] + [pltpu.VMEM(slab(w), BF16) for w in ffn_w] + [
            pltpu.SemaphoreType.DMA((len(ffn_w),)),
            pltpu.SemaphoreType.DMA((len(ffn_w),)),
        ],
        compiler_params=pltpu.CompilerParams(
            dimension_semantics=("arbitrary",), vmem_limit_bytes=VMEM_LIMIT_FRONT),
        name="front",
    )(tab, sinks, xp, xp, xt, gm, w_in, w_out, bucket, cw, cb, wg, gab, gxb, lam, nr, na, *ffn_w)


def _smixer_kernel(tab_ref, sink_ref, z_ref, st_ref, h0_ref, kc_ref, vc_ref, xs_ref, wout_ref,
                   bucket_ref, cw_ref, cb_ref, wg_ref, gab_ref, gxb_ref, lam_ref, nr_ref, na_ref,
                   x1s_ref, conv_ref, h_ref, ko_ref, vo_ref, bias_scr, mix_scr):
    c = pl.program_id(0)
    nrow = N_HEADS * SB
    ncol = SB * WINDOW * N_KV

    @pl.when(c == 0)
    def _():
        col = lax.broadcasted_iota(jnp.int32, (SB, ncol), 1)
        row = lax.broadcasted_iota(jnp.int32, (SB, ncol), 0)
        own = ((col >> 8) == row) & (((col >> 1) & (WINDOW - 1)) >= 1)
        for h, bias in enumerate(_table_lookup(bucket_ref[...], tab_ref)):
            ok = own & ((col & 1) == h // GROUP)
            bias_scr[h * SB:(h + 1) * SB] = jnp.where(ok, jnp.concatenate([bias] * (SB // HALO), axis=0), NEG)

    r0 = pl.multiple_of(c * SB, SB)

    xr = z_ref[:, 0:D_RNN]
    cw = cw_ref[...]
    xc = cb_ref[...]
    for j in range(RNN_CONV - 1):
        xc = xc + st_ref[j] * cw[j:j + 1]
    xc = xc + xr * cw[RNN_CONV - 1:RNN_CONV]
    for j in range(1, RNN_CONV - 1):
        conv_ref[j - 1] = st_ref[j]
    conv_ref[RNN_CONV - 2] = xr
    a, b = _gates(xc, wg_ref, gab_ref[...], gxb_ref[...], lam_ref[...])
    h = a * h0_ref[...] + b
    h_ref[...] = h
    y_rnn = h * jax.nn.gelu(z_ref[:, O_GR:O_GR + D_RNN])
    mix_scr[pl.ds(r0, SB), 0:D_RNN] = _rms(y_rnn, nr_ref[...]).astype(BF16)

    q = z_ref[:, O_Q:O_Q + D_ATTN]
    kv = z_ref[:, O_KV:O_KV + 2 * D_KV]
    qs = jnp.concatenate([q[:, h * HEAD_DIM:(h + 1) * HEAD_DIM] for h in range(N_HEADS)], axis=0).astype(BF16)
    new_rows = lambda off: jnp.concatenate(
        [kv[:, off + (h // GROUP) * HEAD_DIM:off + (h // GROUP + 1) * HEAD_DIM] for h in range(N_HEADS)],
        axis=0).astype(BF16).astype(F32)
    k_new = new_rows(0)
    v_new = new_rows(D_KV)
    sc = lax.dot_general(qs, kc_ref[...].astype(BF16), (((1,), (1,)), ((), ())), preferred_element_type=F32)
    lg = sc * SCALE + bias_scr[...]
    rh = lax.broadcasted_iota(jnp.int32, (nrow, 1), 0) >> 4
    sink = jnp.zeros((nrow, 1), F32)
    bias_new = jnp.zeros((nrow, 1), F32)
    for h in range(N_HEADS):
        sink = jnp.where(rh == h, sink_ref[h], sink)
        bias_new = jnp.where(rh == h, tab_ref[0, h], bias_new)
    lg_new = jnp.sum(qs.astype(F32) * k_new, axis=-1, keepdims=True) * SCALE + bias_new
    mx = jnp.maximum(jnp.maximum(jnp.max(lg, axis=-1, keepdims=True), lg_new), sink)
    e = jnp.exp(lg - mx)
    e_new = jnp.exp(lg_new - mx)
    den = jnp.sum(e, axis=-1, keepdims=True) + e_new + jnp.exp(sink - mx)
    pv = jnp.dot(e.astype(BF16), vc_ref[...].astype(BF16), preferred_element_type=F32)
    o = (pv + e_new.astype(BF16).astype(F32) * v_new) / den
    y_attn = jnp.concatenate([o[h * SB:(h + 1) * SB] for h in range(N_HEADS)], axis=1)
    mix_scr[pl.ds(r0, SB), D_RNN:] = _rms(y_attn, na_ref[...]).astype(BF16)

    per = WINDOW * N_KV
    slide = lambda ref: pltpu.roll(ref[...].reshape(SB, per, HEAD_DIM), per - N_KV, 1).reshape(SB * per, HEAD_DIM)
    ko_ref[...] = slide(kc_ref)
    vo_ref[...] = slide(vc_ref)
    for bi in range(SB):
        for kh in range(N_KV):
            r = (bi + 1) * per - N_KV + kh
            ko_ref[r:r + 1, :] = kv[bi:bi + 1, kh * HEAD_DIM:(kh + 1) * HEAD_DIM]
            vo_ref[r:r + 1, :] = kv[bi:bi + 1, D_KV + kh * HEAD_DIM:D_KV + (kh + 1) * HEAD_DIM]

    @pl.when(c == DEC_BATCH // SB - 1)
    def _():
        x1s_ref[...] = xs_ref[...] + jnp.dot(mix_scr[...], wout_ref[...], preferred_element_type=F32)


def _smixer(tab, sinks, zs, st, h0, kc, vc, xs, w_out, bucket, cw, cb, wg, gab, gxb, lam, nr, na):
    vec = lambda n: pl.BlockSpec((1, n), lambda c: (0, 0))
    smem = pl.BlockSpec(memory_space=pltpu.SMEM)
    cache = pl.BlockSpec((SB * WINDOW * N_KV, HEAD_DIM), lambda c: (c, 0))
    return pl.pallas_call(
        _smixer_kernel,
        grid=(DEC_BATCH // SB,),
        in_specs=[
            smem, smem,
            pl.BlockSpec((SB, D_IN), lambda c: (c, 0)),
            pl.BlockSpec((RNN_CONV - 1, SB, D_RNN), lambda c: (0, c, 0)),
            pl.BlockSpec((SB, D_RNN), lambda c: (c, 0)),
            cache, cache,
            pl.BlockSpec((DEC_BATCH, D_MODEL), lambda c: (0, 0)),
            pl.BlockSpec((D_MODEL, D_MODEL), lambda c: (0, 0), pipeline_mode=pl.Buffered(1)),
            pl.BlockSpec((HALO, SB * WINDOW * N_KV), lambda c: (0, 0)),
            pl.BlockSpec((RNN_CONV, D_RNN), lambda c: (0, 0)),
            vec(D_RNN),
            pl.BlockSpec((N_GW, GW, 2 * GW), lambda c: (0, 0, 0)),
            vec(D_RNN), vec(D_RNN), vec(D_RNN), vec(D_RNN), vec(D_ATTN),
        ],
        out_specs=[
            pl.BlockSpec((DEC_BATCH, D_MODEL), lambda c: (0, 0)),
            pl.BlockSpec((RNN_CONV - 1, SB, D_RNN), lambda c: (0, c, 0)),
            pl.BlockSpec((SB, D_RNN), lambda c: (c, 0)),
            cache, cache,
        ],
        out_shape=[
            jax.ShapeDtypeStruct((DEC_BATCH, D_MODEL), F32),
            jax.ShapeDtypeStruct((RNN_CONV - 1, DEC_BATCH, D_RNN), F32),
            jax.ShapeDtypeStruct((DEC_BATCH, D_RNN), F32),
            jax.ShapeDtypeStruct((DEC_BATCH * WINDOW * N_KV, HEAD_DIM), F32),
            jax.ShapeDtypeStruct((DEC_BATCH * WINDOW * N_KV, HEAD_DIM), F32),
        ],
        scratch_shapes=[
            pltpu.VMEM((N_HEADS * SB, SB * WINDOW * N_KV), F32),
            pltpu.VMEM((DEC_BATCH, D_MODEL), BF16),
        ],
        compiler_params=pltpu.CompilerParams(
            dimension_semantics=("arbitrary",), vmem_limit_bytes=VMEM_LIMIT),
        name="smixer",
    )(tab, sinks, zs, st, h0, kc, vc, xs, w_out, bucket, cw, cb, wg, gab, gxb, lam, nr, na)


U_LO = LAST_P - HALO
U_HI = LAST_P + DEC_BATCH
N_UST = U_HI - U_LO


def _ffn_kernel(x1_ref, halo_ref, x1s_ref, st0_ref, st1_ref, gn_ref, gf_ref, wu_ref, wg_ref, cw_ref, cb_ref,
                wd_ref, y_ref, ys_ref, ust_ref, h2_scr, ubuf, abuf):
    i = pl.program_id(0)
    j = pl.program_id(1)

    @pl.when(j == 0)
    def _():
        h2_scr[0:FH] = _rms(halo_ref[...], gn_ref[...]).astype(BF16)
        h2_scr[FH:FH + TM] = _rms(x1_ref[...], gn_ref[...]).astype(BF16)
        y_ref[...] = jnp.zeros((TM, D_MODEL), F32)

        @pl.when(i == N_RT - 1)
        def _():
            h2_scr[FH + LAST_P:FH + U_HI] = _rms(x1s_ref[...], gn_ref[...]).astype(BF16)

    ubuf[...] = jnp.dot(h2_scr[...], wu_ref[...], preferred_element_type=F32)
    gate = jnp.dot(h2_scr[FH:FH + TM], wg_ref[...], preferred_element_type=F32)
    cw = cw_ref[...]
    cb = cb_ref[...]
    for c in range(TF // KC):
        cs = slice(c * KC, (c + 1) * KC)
        tap = lambda r0, n: ubuf[r0:r0 + n, cs]
        uc = (cb[:, cs] + tap(FH - 2, TM) * cw[0:1, cs] + tap(FH - 1, TM) * cw[1:2, cs] + tap(FH, TM) * cw[2:3, cs])
        act = jax.nn.gelu(uc) * gate[:, cs]
        abuf[:, cs] = act.astype(BF16)
        ucs = (cb[:, cs] + st0_ref[:, cs] * cw[0:1, cs] + st1_ref[:, cs] * cw[1:2, cs]
               + tap(FH + LAST_P, DEC_BATCH) * cw[2:3, cs])
        act_s = jax.nn.gelu(ucs) * gate[LAST_P:U_HI, cs]
        abuf[LAST_P:U_HI, cs] = jnp.where(i == N_RT - 1, act_s, act[LAST_P:U_HI]).astype(BF16)
        y_ref[...] += jnp.dot(abuf[:, cs], wd_ref[cs, :], preferred_element_type=F32)
    ust_ref[...] = ubuf[FH + U_LO:FH + U_HI]

    @pl.when(j == N_FT - 1)
    def _():
        @pl.when(i == N_RT - 1)
        def _():
            ys_ref[...] = _rms(x1s_ref[...] + y_ref[LAST_P:U_HI], gf_ref[...])

        y_ref[...] = _rms(x1_ref[...] + y_ref[...], gf_ref[...])


def _ffn(x1, x1s, st, gn, gf, wu, wg, cw, cb, wd):
    def halo_idx(i, j):
        return (jnp.where(i == 0, R_ALL // FH - 1, i * (TM // FH) - 1), 0)

    return pl.pallas_call(
        _ffn_kernel,
        grid=(N_RT, N_FT),
        in_specs=[
            pl.BlockSpec((TM, D_MODEL), lambda i, j: (i, 0)),
            pl.BlockSpec((FH, D_MODEL), halo_idx),
            pl.BlockSpec((DEC_BATCH, D_MODEL), lambda i, j: (0, 0)),
            pl.BlockSpec((DEC_BATCH, TF), lambda i, j: (0, j)),
            pl.BlockSpec((DEC_BATCH, TF), lambda i, j: (0, N_FT + j)),
            pl.BlockSpec((1, D_MODEL), lambda i, j: (0, 0)),
            pl.BlockSpec((1, D_MODEL), lambda i, j: (0, 0)),
            pl.BlockSpec((D_MODEL, TF), lambda i, j: (0, j)),
            pl.BlockSpec((D_MODEL, TF), lambda i, j: (0, j)),
            pl.BlockSpec((FFN_CONV, TF), lambda i, j: (0, j)),
            pl.BlockSpec((1, TF), lambda i, j: (0, j)),
            pl.BlockSpec((TF, D_MODEL), lambda i, j: (j, 0)),
        ],
        out_specs=[
            pl.BlockSpec((TM, D_MODEL), lambda i, j: (i, 0)),
            pl.BlockSpec((DEC_BATCH, D_MODEL), lambda i, j: (0, 0)),
            pl.BlockSpec((N_UST, TF), lambda i, j: (i, j)),
        ],
        out_shape=[
            jax.ShapeDtypeStruct((SEQ, D_MODEL), F32),
            jax.ShapeDtypeStruct((DEC_BATCH, D_MODEL), F32),
            jax.ShapeDtypeStruct((N_RT * N_UST, D_FF), F32),
        ],
        scratch_shapes=[
            pltpu.VMEM((FH + TM, D_MODEL), BF16),
            pltpu.VMEM((FH + TM, TF), F32),
            pltpu.VMEM((TM, TF), BF16),
        ],
        compiler_params=pltpu.CompilerParams(
            dimension_semantics=("arbitrary", "arbitrary"), vmem_limit_bytes=VMEM_LIMIT),
        name="ffn",
    )(x1, x1, x1s, st, st, gn, gf, wu, wg, cw, cb, wd)


def _gate_weights(wa, wx):
    per = GW // RNN_BLOCK
    eye = jnp.eye(per, dtype=wa.dtype)

    def bd(w):
        w = w.reshape(N_GW, per, RNN_BLOCK, RNN_BLOCK)
        return jnp.einsum('gpcd,pq->gpcqd', w, eye).reshape(N_GW, GW, GW)

    return jnp.concatenate([bd(wa), bd(wx)], axis=-1).astype(BF16)


def kernel(x_prompt, x_sample, state_rnn_conv, state_rnn_h, cache_k_win, cache_v_win, state_ffn_conv,
           meta_tokens, rel_bias_table, norm_mix, w_in, rnn_conv_w, rnn_conv_b, gate_a_w, gate_a_b,
           gate_x_w, gate_x_b, rnn_lambda, attn_sinks, norm_rnn_out, norm_attn_out, w_out, norm_ffn,
           w_up, w_gate, ffn_conv_w, ffn_conv_b, w_down, norm_final):
    l = 0
    xp = x_prompt[0]
    xs = x_sample[:, 0, :]
    xt = jnp.concatenate([xs, jnp.zeros((N_PAD, D_MODEL), F32), meta_tokens], axis=0)
    row = lambda v: v.reshape(1, -1)

    qi = np.arange(BLOCK)[:, None]
    sj = np.arange(2 * BLOCK)[None, :]
    bucket_p = jnp.asarray(_rel_buckets(BLOCK + qi - sj))
    pos = (np.arange(SB * WINDOW * N_KV) >> 1) & (WINDOW - 1)
    bucket_s = jnp.asarray(np.tile(_rel_buckets(WINDOW - pos)[None, :], (HALO, 1)))

    wg = _gate_weights(gate_a_w[l], gate_x_w[l])
    seq_w = (rnn_conv_w[l], row(rnn_conv_b[l]), wg, row(gate_a_b[l]), row(gate_x_b[l]), row(rnn_lambda[l]),
             row(norm_rnn_out[l]), row(norm_attn_out[l]))
    w_out_b = w_out[l].astype(BF16)

    x1, zs, h_last, kv_last, xr_tail, w_up_b, w_gate_b, w_down_b = _front(
        rel_bias_table, attn_sinks[l], xp, xt, row(norm_mix[l]), w_in[l].astype(BF16), w_out_b, bucket_p, *seq_w,
        w_up[l], w_gate[l], w_down[l])
    x1s, conv_s, h_s, k_s, v_s = _smixer(
        rel_bias_table, attn_sinks[l], zs,
        jnp.swapaxes(state_rnn_conv[l], 0, 1), state_rnn_h[l],
        cache_k_win[l].reshape(DEC_BATCH * WINDOW * N_KV, HEAD_DIM),
        cache_v_win[l].reshape(DEC_BATCH * WINDOW * N_KV, HEAD_DIM),
        xs, w_out_b, bucket_s, *seq_w)
    y_p, y_s, ust = _ffn(x1, x1s, state_ffn_conv[l].reshape(DEC_BATCH, (FFN_CONV - 1) * D_FF),
                         row(norm_ffn[l]), row(norm_final), w_up_b, w_gate_b,
                         ffn_conv_w[l], row(ffn_conv_b[l]), w_down_b)

    ust = ust[(N_RT - 1) * N_UST:]
    p_states = (
        xr_tail[HALO - (RNN_CONV - 1):HALO][None, None],
        h_last[0:1][None],
        kv_last[:, :D_KV].reshape(1, 1, WINDOW, N_KV, HEAD_DIM),
        kv_last[:, D_KV:].reshape(1, 1, WINDOW, N_KV, HEAD_DIM),
        ust[HALO - (FFN_CONV - 1):HALO][None, None],
    )
    s_states = (
        jnp.swapaxes(conv_s, 0, 1)[None],
        h_s[None],
        k_s.reshape(1, DEC_BATCH, WINDOW, N_KV, HEAD_DIM),
        v_s.reshape(1, DEC_BATCH, WINDOW, N_KV, HEAD_DIM),
        jnp.stack([state_ffn_conv[l][:, FFN_CONV - 2, :], ust[HALO:]], axis=1)[None],
    )
    return (y_p[None], y_s[:, None, :]) + p_states + s_states
```

```python
import math

import numpy as np
import jax
import jax.numpy as jnp
from jax import lax
from jax.experimental import pallas as pl
from jax.experimental.pallas import tpu as pltpu

F32 = jnp.float32
BF16 = jnp.bfloat16

D_MODEL = 2048
SEQ = 8192
DEC_BATCH = 128
D_RNN = 1024
N_RNN_BLOCKS = 16
RNN_BLOCK = D_RNN // N_RNN_BLOCKS
RNN_CONV = 4
LRU_C = 8.0
N_HEADS = 8
HEAD_DIM = 128
N_KV = 2
GROUP = N_HEADS // N_KV
D_ATTN = N_HEADS * HEAD_DIM
WINDOW = 128
BLOCK = 128
NUM_BUCKETS = 32
MAX_DISTANCE = 128
D_FF = 3 * D_MODEL
FFN_CONV = 3
N_META = 16
EPS = 1e-6
NEG = -1e30
D_KV = N_KV * HEAD_DIM
D_IN = 2 * D_RNN + D_ATTN + 2 * D_KV
SCALE = HEAD_DIM ** -0.5
O_GR = D_RNN
O_Q = 2 * D_RNN
O_KV = 2 * D_RNN + D_ATTN

N_PAD = BLOCK - N_META
R_TAIL = DEC_BATCH + BLOCK
R_ALL = SEQ + R_TAIL
N_PBLK = SEQ // BLOCK

FB = 2 * BLOCK
N_FSTEP = (N_PBLK + 2) // 2
PC = 512
N_IN_EARLY = 5
N_SLAB = N_FSTEP - 1

TM = 704
N_RT = R_ALL // TM
LAST_P = SEQ - (N_RT - 1) * TM
TF = 768
N_FT = D_FF // TF
KC = 256
HALO = 8
FH = 16
GW = 256
N_GW = D_RNN // GW
SB = 16
VMEM_LIMIT = 56 * 1024 * 1024
VMEM_LIMIT_BIG = 60 * 1024 * 1024


def _rms(x, g):
    return x * lax.rsqrt(jnp.mean(x * x, axis=-1, keepdims=True) + EPS) * g


def _rel_buckets(d):
    d = np.maximum(d, 0)
    exact = NUM_BUCKETS // 2
    ratio = np.maximum(d, 1).astype(np.float32) / np.float32(exact)
    large = exact + (np.log(ratio) / np.float32(math.log(MAX_DISTANCE / exact))
                     * np.float32(NUM_BUCKETS - exact)).astype(np.int32)
    large = np.minimum(large, NUM_BUCKETS - 1)
    return np.where(d < exact, d, large).astype(np.int32)


def _table_lookup(bucket, tab_ref):
    outs = [jnp.zeros(bucket.shape, F32) for _ in range(N_HEADS)]
    for b in range(NUM_BUCKETS):
        hit = bucket == b
        outs = [jnp.where(hit, tab_ref[b, h], o) for h, o in enumerate(outs)]
    return outs


def _gates(xc, wg_ref, gab, gxb, lam):
    xcb = xc.astype(BF16)
    ga, gx = [], []
    for j in range(N_GW):
        gj = jnp.dot(xcb[:, GW * j:GW * (j + 1)], wg_ref[j], preferred_element_type=F32)
        ga.append(gj[:, :GW])
        gx.append(gj[:, GW:])
    r = jax.nn.sigmoid(jnp.concatenate(ga, axis=1) + gab)
    i = jax.nn.sigmoid(jnp.concatenate(gx, axis=1) + gxb)
    log_a = -LRU_C * r * jax.nn.softplus(-lam)
    a = jnp.exp(log_a)
    t = 1.0 - a * a
    b = jnp.where(t > 0.0, t * lax.rsqrt(t), 0.0) * i * xc
    return a, b


def _front_kernel(tab_ref, sink_ref, xa_ref, xb_ref, xt_ref, gm_ref, win_ref, wout_ref, bucket_ref,
                  cw_ref, cb_ref, wg_ref, gab_ref, gxb_ref, lam_ref, nr_ref, na_ref, wu_f, wgt_f, wd_f,
                  x1_ref, zs_ref, hlast_ref, kvlast_ref, xrt_ref, wu_b, wgt_b, wd_b,
                  zp, zb_o, lhs_p, xsp, mixp, xbuf, h_scr, kvbuf, bias_scr,
                  su, sg, sd, tu, tg, td, sem_in, sem_out):
    k = pl.program_id(0)
    last = N_FSTEP - 1

    stages = ((wu_f, su, tu, wu_b), (wgt_f, sg, tg, wgt_b), (wd_f, sd, td, wd_b))

    def slab_in(s):
        return [pltpu.make_async_copy(w.at[pl.ds(s * st.shape[0], st.shape[0])], st, sem_in.at[n])
                for n, (w, st, _, _) in enumerate(stages)]

    def slab_out(s):
        return [pltpu.make_async_copy(t, w.at[pl.ds(s * t.shape[0], t.shape[0])], sem_out.at[n])
                for n, (_, _, t, w) in enumerate(stages)]

    @pl.when(k == 0)
    def _():
        for c in slab_in(0):
            c.start()

    @pl.when(k >= 1)
    def _():
        for c in slab_out(k - 1):
            c.wait()

    @pl.when(k < N_SLAB)
    def _():
        for c in slab_in(k):
            c.wait()
        for _, st, t, _ in stages:
            t[...] = st[...].astype(BF16)
        for c in slab_out(k):
            c.start()

    @pl.when(k < N_SLAB - 1)
    def _():
        for c in slab_in(k + 1):
            c.start()

    def inproj_chunk(c):
        def run():
            cs = slice(c * PC, (c + 1) * PC)
            zp[:, cs] = jnp.dot(lhs_p[...], win_ref[:, cs], preferred_element_type=F32)
        return run

    def outproj_chunk(c):
        def run():
            cs = slice(c * PC, (c + 1) * PC)
            x1_ref[:, cs] = xsp[:, cs] + jnp.dot(mixp[...], wout_ref[:, cs], preferred_element_type=F32)
        return run

    @pl.when(k == 0)
    def _():
        prefix = xt_ref[DEC_BATCH:R_TAIL]
        zb_o[...] = jnp.dot(_rms(prefix, gm_ref[...]).astype(BF16), win_ref[...], preferred_element_type=F32)
        xsp[0:BLOCK] = jnp.zeros((BLOCK, D_MODEL), F32)
        xsp[BLOCK:FB] = prefix
        mixp[...] = jnp.zeros((FB, D_MODEL), BF16)
        xbuf[0:HALO] = jnp.zeros((HALO, D_RNN), F32)
        h_scr[...] = jnp.zeros((HALO, D_RNN), F32)
        kvbuf[...] = jnp.zeros((BLOCK, 2 * D_KV), F32)
        for h, bias in enumerate(_table_lookup(bucket_ref[...], tab_ref)):
            bias_scr[h] = bias

    def mixer(zb, is_prefix, first_key, fillers):
        fill = iter(fillers)

        def between():
            f = next(fill, None)
            if f is not None:
                f()

        between()
        xr = zb[:, 0:D_RNN]
        xbuf[HALO:HALO + BLOCK] = xr
        cw = cw_ref[...]
        xc = cb_ref[...]
        for j in range(RNN_CONV - 1):
            lo = HALO - (RNN_CONV - 1) + j
            xc = xc + xbuf[lo:lo + BLOCK] * cw[j:j + 1]
        xc = xc + xr * cw[RNN_CONV - 1:RNN_CONV]
        xbuf[0:HALO] = xr[BLOCK - HALO:BLOCK]

        a, b = _gates(xc, wg_ref, gab_ref[...], gxb_ref[...], lam_ref[...])
        between()
        h = h_scr[0:1]
        if is_prefix is not None:
            row = lax.broadcasted_iota(jnp.int32, (BLOCK, D_RNN), 0)
            b = jnp.where(jnp.logical_and(is_prefix, row < N_PAD), 0.0, b)
            h = jnp.where(is_prefix, 0.0, h)

        ng = BLOCK // HALO
        a3 = a.reshape(ng, HALO, D_RNN)
        b3 = b.reshape(ng, HALO, D_RNN)
        sub = lax.broadcasted_iota(jnp.int32, (ng, HALO, D_RNN), 1)
        sh = 1
        while sh < HALO:
            a_prev = pltpu.roll(a3, sh, 1)
            b_prev = pltpu.roll(b3, sh, 1)
            m = sub >= sh
            b3 = jnp.where(m, a3 * b_prev + b3, b3)
            a3 = jnp.where(m, a3 * a_prev, a3)
            sh *= 2
        hs = []
        for g in range(ng):
            hg = a3[g] * h + b3[g]
            hs.append(hg)
            h = hg[HALO - 1:HALO]
        h_all = jnp.concatenate(hs, axis=0)
        h_scr[...] = jnp.broadcast_to(h, (HALO, D_RNN))
        between()
        y_rnn = h_all * jax.nn.gelu(zb[:, O_GR:O_GR + D_RNN])
        mix_a = _rms(y_rnn, nr_ref[...]).astype(BF16)

        q = zb[:, O_Q:O_Q + D_ATTN]
        kv = zb[:, O_KV:O_KV + 2 * D_KV]
        kvp = kvbuf[...]
        kvbuf[...] = kv
        col = lax.broadcasted_iota(jnp.int32, (BLOCK, 2 * BLOCK), 1)
        rowq = lax.broadcasted_iota(jnp.int32, (BLOCK, 2 * BLOCK), 0)
        d = BLOCK + rowq - col
        mask = (d >= 0) & (d < WINDOW) & (col >= first_key)
        outs = []
        for kh in range(N_KV):
            between()
            ks = slice(kh * HEAD_DIM, (kh + 1) * HEAD_DIM)
            vs = slice(D_KV + kh * HEAD_DIM, D_KV + (kh + 1) * HEAD_DIM)
            qs = jnp.concatenate(
                [q[:, (kh * GROUP + g) * HEAD_DIM:(kh * GROUP + g + 1) * HEAD_DIM] for g in range(GROUP)],
                axis=0).astype(BF16)
            kk = jnp.concatenate([kvp[:, ks], kv[:, ks]], axis=0).astype(BF16)
            vv = jnp.concatenate([kvp[:, vs], kv[:, vs]], axis=0).astype(BF16)
            sc = lax.dot_general(qs, kk, (((1,), (1,)), ((), ())), preferred_element_type=F32)
            es, dens = [], []
            for g in range(GROUP):
                hh = kh * GROUP + g
                lg = sc[g * BLOCK:(g + 1) * BLOCK] * SCALE + bias_scr[hh]
                lg = jnp.where(mask, lg, NEG)
                sink = sink_ref[hh]
                mx = jnp.maximum(jnp.max(lg, axis=-1, keepdims=True), sink)
                e = jnp.exp(lg - mx)
                dens.append(jnp.sum(e, axis=-1, keepdims=True) + jnp.exp(sink - mx))
                es.append(e.astype(BF16))
            pv = jnp.dot(jnp.concatenate(es, axis=0), vv, preferred_element_type=F32)
            for g in range(GROUP):
                outs.append(pv[g * BLOCK:(g + 1) * BLOCK] / dens[g])
        between()
        mix_b = _rms(jnp.concatenate(outs, axis=1), na_ref[...]).astype(BF16)

        return jnp.concatenate([mix_a, mix_b], axis=1), h, kv, xr

    xa = jnp.where(k == last, xt_ref[0:DEC_BATCH], xa_ref[...])
    xb = xb_ref[...]
    lhs_p[0:BLOCK] = _rms(xa, gm_ref[...]).astype(BF16)
    lhs_p[BLOCK:FB] = _rms(xb, gm_ref[...]).astype(BF16)
    n_in, n_out = D_IN // PC, D_MODEL // PC
    mix_o, h, kv, xr = mixer(zb_o, k == 0, jnp.where(k == 0, BLOCK + N_PAD, 0),
                             [inproj_chunk(c) for c in range(N_IN_EARLY)])
    mixp[BLOCK:FB] = mix_o
    hlast_ref[...] = jnp.broadcast_to(h, (HALO, D_RNN))
    kvlast_ref[...] = kv
    xrt_ref[...] = xr[BLOCK - HALO:BLOCK]

    mix_e = mixer(zp.at[0:BLOCK], None, jnp.where(k == 0, N_PAD, 0),
                  [inproj_chunk(c) for c in range(N_IN_EARLY, n_in)] + [outproj_chunk(c) for c in range(n_out)])[0]
    mixp[0:BLOCK] = mix_e
    xsp[0:BLOCK] = xa
    xsp[BLOCK:FB] = xb
    zb_o[...] = zp[BLOCK:FB]

    @pl.when(k == last)
    def _():
        zs_ref[...] = zp[0:BLOCK]


def _front(tab, sinks, xp, xt, gm, w_in, w_out, bucket, cw, cb, wg, gab, gxb, lam, nr, na, w_up, w_gate, w_down):
    ffn_w = (w_up, w_gate, w_down)
    slab = lambda w: (w.shape[0] // N_SLAB, w.shape[1])
    hbm = pl.BlockSpec(memory_space=pl.ANY)
    vec = lambda n: pl.BlockSpec((1, n), lambda k: (0, 0))
    smem = pl.BlockSpec(memory_space=pltpu.SMEM)
    once = lambda shape: pl.BlockSpec(shape, lambda k: (0,) * len(shape), pipeline_mode=pl.Buffered(1))
    return pl.pallas_call(
        _front_kernel,
        grid=(N_FSTEP,),
        in_specs=[
            smem, smem,
            pl.BlockSpec((BLOCK, D_MODEL), lambda k: (jnp.minimum(2 * k, N_PBLK - 1), 0)),
            pl.BlockSpec((BLOCK, D_MODEL), lambda k: (jnp.minimum(2 * k + 1, N_PBLK - 1), 0)),
            once((R_TAIL, D_MODEL)),
            vec(D_MODEL),
            once((D_MODEL, D_IN)),
            once((D_MODEL, D_MODEL)),
            pl.BlockSpec((BLOCK, 2 * BLOCK), lambda k: (0, 0)),
            pl.BlockSpec((RNN_CONV, D_RNN), lambda k: (0, 0)),
            vec(D_RNN),
            pl.BlockSpec((N_GW, GW, 2 * GW), lambda k: (0, 0, 0)),
            vec(D_RNN), vec(D_RNN), vec(D_RNN), vec(D_RNN), vec(D_ATTN),
            hbm, hbm, hbm,
        ],
        out_specs=[
            pl.BlockSpec((FB, D_MODEL), lambda k: (jnp.where(k == 0, SEQ // FB, k - 1), 0)),
            pl.BlockSpec((BLOCK, D_IN), lambda k: (0, 0)),
            pl.BlockSpec((HALO, D_RNN), lambda k: (0, 0)),
            pl.BlockSpec((BLOCK, 2 * D_KV), lambda k: (0, 0)),
            pl.BlockSpec((HALO, D_RNN), lambda k: (0, 0)),
            hbm, hbm, hbm,
        ],
        out_shape=[
            jax.ShapeDtypeStruct((R_ALL, D_MODEL), F32),
            jax.ShapeDtypeStruct((DEC_BATCH, D_IN), F32),
            jax.ShapeDtypeStruct((HALO, D_RNN), F32),
            jax.ShapeDtypeStruct((BLOCK, 2 * D_KV), F32),
            jax.ShapeDtypeStruct((HALO, D_RNN), F32),
        ] + [jax.ShapeDtypeStruct(w.shape, BF16) for w in ffn_w],
        scratch_shapes=[
            pltpu.VMEM((FB, D_IN), F32),
            pltpu.VMEM((BLOCK, D_IN), F32),
            pltpu.VMEM((FB, D_MODEL), BF16),
            pltpu.VMEM((FB, D_MODEL), F32),
            pltpu.VMEM((FB, D_MODEL), BF16),
            pltpu.VMEM((HALO + BLOCK, D_RNN), F32),
            pltpu.VMEM((HALO, D_RNN), F32),
            pltpu.VMEM((BLOCK, 2 * D_KV), F32),
            pltpu.VMEM((N_HEADS, BLOCK, 2 * BLOCK), F32),
        ] + [pltpu.VMEM(slab(w), F32) for w in ffn_w] + [pltpu.VMEM(slab(w), BF16) for w in ffn_w] + [
            pltpu.SemaphoreType.DMA((len(ffn_w),)),
            pltpu.SemaphoreType.DMA((len(ffn_w),)),
        ],
        compiler_params=pltpu.CompilerParams(
            dimension_semantics=("arbitrary",), vmem_limit_bytes=VMEM_LIMIT_BIG),
        name="front",
    )(tab, sinks, xp, xp, xt, gm, w_in, w_out, bucket, cw, cb, wg, gab, gxb, lam, nr, na, *ffn_w)


def _smixer_kernel(tab_ref, sink_ref, z_ref, st_ref, h0_ref, kc_ref, vc_ref, xs_ref, wout_ref,
                   bucket_ref, cw_ref, cb_ref, wg_ref, gab_ref, gxb_ref, lam_ref, nr_ref, na_ref,
                   x1s_ref, conv_ref, h_ref, ko_ref, vo_ref, bias_scr, mix_scr):
    c = pl.program_id(0)
    nrow = N_HEADS * SB
    ncol = SB * WINDOW * N_KV

    @pl.when(c == 0)
    def _():
        col = lax.broadcasted_iota(jnp.int32, (SB, ncol), 1)
        row = lax.broadcasted_iota(jnp.int32, (SB, ncol), 0)
        own = ((col >> 8) == row) & (((col >> 1) & (WINDOW - 1)) >= 1)
        for h, bias in enumerate(_table_lookup(bucket_ref[...], tab_ref)):
            ok = own & ((col & 1) == h // GROUP)
            bias_scr[h * SB:(h + 1) * SB] = jnp.where(ok, jnp.concatenate([bias] * (SB // HALO), axis=0), NEG)

    r0 = pl.multiple_of(c * SB, SB)

    xr = z_ref[:, 0:D_RNN]
    cw = cw_ref[...]
    xc = cb_ref[...]
    for j in range(RNN_CONV - 1):
        xc = xc + st_ref[j] * cw[j:j + 1]
    xc = xc + xr * cw[RNN_CONV - 1:RNN_CONV]
    for j in range(1, RNN_CONV - 1):
        conv_ref[j - 1] = st_ref[j]
    conv_ref[RNN_CONV - 2] = xr
    a, b = _gates(xc, wg_ref, gab_ref[...], gxb_ref[...], lam_ref[...])
    h = a * h0_ref[...] + b
    h_ref[...] = h
    y_rnn = h * jax.nn.gelu(z_ref[:, O_GR:O_GR + D_RNN])
    mix_scr[pl.ds(r0, SB), 0:D_RNN] = _rms(y_rnn, nr_ref[...]).astype(BF16)

    q = z_ref[:, O_Q:O_Q + D_ATTN]
    kv = z_ref[:, O_KV:O_KV + 2 * D_KV]
    qs = jnp.concatenate([q[:, h * HEAD_DIM:(h + 1) * HEAD_DIM] for h in range(N_HEADS)], axis=0).astype(BF16)
    new_rows = lambda off: jnp.concatenate(
        [kv[:, off + (h // GROUP) * HEAD_DIM:off + (h // GROUP + 1) * HEAD_DIM] for h in range(N_HEADS)],
        axis=0).astype(BF16).astype(F32)
    k_new = new_rows(0)
    v_new = new_rows(D_KV)
    sc = lax.dot_general(qs, kc_ref[...].astype(BF16), (((1,), (1,)), ((), ())), preferred_element_type=F32)
    lg = sc * SCALE + bias_scr[...]
    rh = lax.broadcasted_iota(jnp.int32, (nrow, 1), 0) >> 4
    sink = jnp.zeros((nrow, 1), F32)
    bias_new = jnp.zeros((nrow, 1), F32)
    for h in range(N_HEADS):
        sink = jnp.where(rh == h, sink_ref[h], sink)
        bias_new = jnp.where(rh == h, tab_ref[0, h], bias_new)
    lg_new = jnp.sum(qs.astype(F32) * k_new, axis=-1, keepdims=True) * SCALE + bias_new
    mx = jnp.maximum(jnp.maximum(jnp.max(lg, axis=-1, keepdims=True), lg_new), sink)
    e = jnp.exp(lg - mx)
    e_new = jnp.exp(lg_new - mx)
    den = jnp.sum(e, axis=-1, keepdims=True) + e_new + jnp.exp(sink - mx)
    pv = jnp.dot(e.astype(BF16), vc_ref[...].astype(BF16), preferred_element_type=F32)
    o = (pv + e_new.astype(BF16).astype(F32) * v_new) / den
    y_attn = jnp.concatenate([o[h * SB:(h + 1) * SB] for h in range(N_HEADS)], axis=1)
    mix_scr[pl.ds(r0, SB), D_RNN:] = _rms(y_attn, na_ref[...]).astype(BF16)

    per = WINDOW * N_KV
    slide = lambda ref: pltpu.roll(ref[...].reshape(SB, per, HEAD_DIM), per - N_KV, 1).reshape(SB * per, HEAD_DIM)
    ko_ref[...] = slide(kc_ref)
    vo_ref[...] = slide(vc_ref)
    for bi in range(SB):
        for kh in range(N_KV):
            r = (bi + 1) * per - N_KV + kh
            ko_ref[r:r + 1, :] = kv[bi:bi + 1, kh * HEAD_DIM:(kh + 1) * HEAD_DIM]
            vo_ref[r:r + 1, :] = kv[bi:bi + 1, D_KV + kh * HEAD_DIM:D_KV + (kh + 1) * HEAD_DIM]

    @pl.when(c == DEC_BATCH // SB - 1)
    def _():
        x1s_ref[...] = xs_ref[...] + jnp.dot(mix_scr[...], wout_ref[...], preferred_element_type=F32)


def _smixer(tab, sinks, zs, st, h0, kc, vc, xs, w_out, bucket, cw, cb, wg, gab, gxb, lam, nr, na):
    vec = lambda n: pl.BlockSpec((1, n), lambda c: (0, 0))
    smem = pl.BlockSpec(memory_space=pltpu.SMEM)
    cache = pl.BlockSpec((SB * WINDOW * N_KV, HEAD_DIM), lambda c: (c, 0))
    return pl.pallas_call(
        _smixer_kernel,
        grid=(DEC_BATCH // SB,),
        in_specs=[
            smem, smem,
            pl.BlockSpec((SB, D_IN), lambda c: (c, 0)),
            pl.BlockSpec((RNN_CONV - 1, SB, D_RNN), lambda c: (0, c, 0)),
            pl.BlockSpec((SB, D_RNN), lambda c: (c, 0)),
            cache, cache,
            pl.BlockSpec((DEC_BATCH, D_MODEL), lambda c: (0, 0)),
            pl.BlockSpec((D_MODEL, D_MODEL), lambda c: (0, 0), pipeline_mode=pl.Buffered(1)),
            pl.BlockSpec((HALO, SB * WINDOW * N_KV), lambda c: (0, 0)),
            pl.BlockSpec((RNN_CONV, D_RNN), lambda c: (0, 0)),
            vec(D_RNN),
            pl.BlockSpec((N_GW, GW, 2 * GW), lambda c: (0, 0, 0)),
            vec(D_RNN), vec(D_RNN), vec(D_RNN), vec(D_RNN), vec(D_ATTN),
        ],
        out_specs=[
            pl.BlockSpec((DEC_BATCH, D_MODEL), lambda c: (0, 0)),
            pl.BlockSpec((RNN_CONV - 1, SB, D_RNN), lambda c: (0, c, 0)),
            pl.BlockSpec((SB, D_RNN), lambda c: (c, 0)),
            cache, cache,
        ],
        out_shape=[
            jax.ShapeDtypeStruct((DEC_BATCH, D_MODEL), F32),
            jax.ShapeDtypeStruct((RNN_CONV - 1, DEC_BATCH, D_RNN), F32),
            jax.ShapeDtypeStruct((DEC_BATCH, D_RNN), F32),
            jax.ShapeDtypeStruct((DEC_BATCH * WINDOW * N_KV, HEAD_DIM), F32),
            jax.ShapeDtypeStruct((DEC_BATCH * WINDOW * N_KV, HEAD_DIM), F32),
        ],
        scratch_shapes=[
            pltpu.VMEM((N_HEADS * SB, SB * WINDOW * N_KV), F32),
            pltpu.VMEM((DEC_BATCH, D_MODEL), BF16),
        ],
        compiler_params=pltpu.CompilerParams(
            dimension_semantics=("arbitrary",), vmem_limit_bytes=VMEM_LIMIT),
        name="smixer",
    )(tab, sinks, zs, st, h0, kc, vc, xs, w_out, bucket, cw, cb, wg, gab, gxb, lam, nr, na)


U_LO = LAST_P - HALO
U_HI = LAST_P + DEC_BATCH
N_UST = U_HI - U_LO


def _ffn_kernel(x1_ref, halo_ref, x1s_ref, st0_ref, st1_ref, gn_ref, gf_ref, wu_ref, wg_ref, cw_ref, cb_ref,
                wd_ref, y_ref, ys_ref, ust_ref, h2_scr, ubuf, abuf):
    i = pl.program_id(0)
    j = pl.program_id(1)

    @pl.when(j == 0)
    def _():
        h2_scr[0:FH] = _rms(halo_ref[...], gn_ref[...]).astype(BF16)
        h2_scr[FH:FH + TM] = _rms(x1_ref[...], gn_ref[...]).astype(BF16)
        y_ref[...] = jnp.zeros((TM, D_MODEL), F32)

        @pl.when(i == N_RT - 1)
        def _():
            h2_scr[FH + LAST_P:FH + U_HI] = _rms(x1s_ref[...], gn_ref[...]).astype(BF16)

    ubuf[...] = jnp.dot(h2_scr[...], wu_ref[...], preferred_element_type=F32)
    gate = jnp.dot(h2_scr[FH:FH + TM], wg_ref[...], preferred_element_type=F32)
    cw = cw_ref[...]
    cb = cb_ref[...]
    for c in range(TF // KC):
        cs = slice(c * KC, (c + 1) * KC)
        tap = lambda r0, n: ubuf[r0:r0 + n, cs]
        uc = (cb[:, cs] + tap(FH - 2, TM) * cw[0:1, cs] + tap(FH - 1, TM) * cw[1:2, cs] + tap(FH, TM) * cw[2:3, cs])
        act = jax.nn.gelu(uc) * gate[:, cs]
        abuf[:, cs] = act.astype(BF16)
        ucs = (cb[:, cs] + st0_ref[:, cs] * cw[0:1, cs] + st1_ref[:, cs] * cw[1:2, cs]
               + tap(FH + LAST_P, DEC_BATCH) * cw[2:3, cs])
        act_s = jax.nn.gelu(ucs) * gate[LAST_P:U_HI, cs]
        abuf[LAST_P:U_HI, cs] = jnp.where(i == N_RT - 1, act_s, act[LAST_P:U_HI]).astype(BF16)
        y_ref[...] += jnp.dot(abuf[:, cs], wd_ref[cs, :], preferred_element_type=F32)
    ust_ref[...] = ubuf[FH + U_LO:FH + U_HI]

    @pl.when(j == N_FT - 1)
    def _():
        @pl.when(i == N_RT - 1)
        def _():
            ys_ref[...] = _rms(x1s_ref[...] + y_ref[LAST_P:U_HI], gf_ref[...])

        y_ref[...] = _rms(x1_ref[...] + y_ref[...], gf_ref[...])


def _ffn(x1, x1s, st, gn, gf, wu, wg, cw, cb, wd):
    def halo_idx(i, j):
        return (jnp.where(i == 0, R_ALL // FH - 1, i * (TM // FH) - 1), 0)

    return pl.pallas_call(
        _ffn_kernel,
        grid=(N_RT, N_FT),
        in_specs=[
            pl.BlockSpec((TM, D_MODEL), lambda i, j: (i, 0)),
            pl.BlockSpec((FH, D_MODEL), halo_idx),
            pl.BlockSpec((DEC_BATCH, D_MODEL), lambda i, j: (0, 0)),
            pl.BlockSpec((DEC_BATCH, TF), lambda i, j: (0, j)),
            pl.BlockSpec((DEC_BATCH, TF), lambda i, j: (0, N_FT + j)),
            pl.BlockSpec((1, D_MODEL), lambda i, j: (0, 0)),
            pl.BlockSpec((1, D_MODEL), lambda i, j: (0, 0)),
            pl.BlockSpec((D_MODEL, TF), lambda i, j: (0, j)),
            pl.BlockSpec((D_MODEL, TF), lambda i, j: (0, j)),
            pl.BlockSpec((FFN_CONV, TF), lambda i, j: (0, j)),
            pl.BlockSpec((1, TF), lambda i, j: (0, j)),
            pl.BlockSpec((TF, D_MODEL), lambda i, j: (j, 0)),
        ],
        out_specs=[
            pl.BlockSpec((TM, D_MODEL), lambda i, j: (i, 0)),
            pl.BlockSpec((DEC_BATCH, D_MODEL), lambda i, j: (0, 0)),
            pl.BlockSpec((N_UST, TF), lambda i, j: (i, j)),
        ],
        out_shape=[
            jax.ShapeDtypeStruct((SEQ, D_MODEL), F32),
            jax.ShapeDtypeStruct((DEC_BATCH, D_MODEL), F32),
            jax.ShapeDtypeStruct((N_RT * N_UST, D_FF), F32),
        ],
        scratch_shapes=[
            pltpu.VMEM((FH + TM, D_MODEL), BF16),
            pltpu.VMEM((FH + TM, TF), F32),
            pltpu.VMEM((TM, TF), BF16),
        ],
        compiler_params=pltpu.CompilerParams(
            dimension_semantics=("arbitrary", "arbitrary"), vmem_limit_bytes=VMEM_LIMIT_BIG),
        name="ffn",
    )(x1, x1, x1s, st, st, gn, gf, wu, wg, cw, cb, wd)


def _gate_weights(wa, wx):
    per = GW // RNN_BLOCK
    eye = jnp.eye(per, dtype=wa.dtype)

    def bd(w):
        w = w.reshape(N_GW, per, RNN_BLOCK, RNN_BLOCK)
        return jnp.einsum('gpcd,pq->gpcqd', w, eye).reshape(N_GW, GW, GW)

    return jnp.concatenate([bd(wa), bd(wx)], axis=-1).astype(BF16)


def kernel(x_prompt, x_sample, state_rnn_conv, state_rnn_h, cache_k_win, cache_v_win, state_ffn_conv,
           meta_tokens, rel_bias_table, norm_mix, w_in, rnn_conv_w, rnn_conv_b, gate_a_w, gate_a_b,
           gate_x_w, gate_x_b, rnn_lambda, attn_sinks, norm_rnn_out, norm_attn_out, w_out, norm_ffn,
           w_up, w_gate, ffn_conv_w, ffn_conv_b, w_down, norm_final):
    l = 0
    xp = x_prompt[0]
    xs = x_sample[:, 0, :]
    xt = jnp.concatenate([xs, jnp.zeros((N_PAD, D_MODEL), F32), meta_tokens], axis=0)
    row = lambda v: v.reshape(1, -1)

    qi = np.arange(BLOCK)[:, None]
    sj = np.arange(2 * BLOCK)[None, :]
    bucket_p = jnp.asarray(_rel_buckets(BLOCK + qi - sj))
    pos = (np.arange(SB * WINDOW * N_KV) >> 1) & (WINDOW - 1)
    bucket_s = jnp.asarray(np.tile(_rel_buckets(WINDOW - pos)[None, :], (HALO, 1)))

    wg = _gate_weights(gate_a_w[l], gate_x_w[l])
    seq_w = (rnn_conv_w[l], row(rnn_conv_b[l]), wg, row(gate_a_b[l]), row(gate_x_b[l]), row(rnn_lambda[l]),
             row(norm_rnn_out[l]), row(norm_attn_out[l]))
    w_out_b = w_out[l].astype(BF16)

    x1, zs, h_last, kv_last, xr_tail, w_up_b, w_gate_b, w_down_b = _front(
        rel_bias_table, attn_sinks[l], xp, xt, row(norm_mix[l]), w_in[l].astype(BF16), w_out_b, bucket_p, *seq_w,
        w_up[l], w_gate[l], w_down[l])
    x1s, conv_s, h_s, k_s, v_s = _smixer(
        rel_bias_table, attn_sinks[l], zs,
        jnp.swapaxes(state_rnn_conv[l], 0, 1), state_rnn_h[l],
        cache_k_win[l].reshape(DEC_BATCH * WINDOW * N_KV, HEAD_DIM),
        cache_v_win[l].reshape(DEC_BATCH * WINDOW * N_KV, HEAD_DIM),
        xs, w_out_b, bucket_s, *seq_w)
    y_p, y_s, ust = _ffn(x1, x1s, state_ffn_conv[l].reshape(DEC_BATCH, (FFN_CONV - 1) * D_FF),
                         row(norm_ffn[l]), row(norm_final), w_up_b, w_gate_b,
                         ffn_conv_w[l], row(ffn_conv_b[l]), w_down_b)

    ust = ust[(N_RT - 1) * N_UST:]
    p_states = (
        xr_tail[HALO - (RNN_CONV - 1):HALO][None, None],
        h_last[0:1][None],
        kv_last[:, :D_KV].reshape(1, 1, WINDOW, N_KV, HEAD_DIM),
        kv_last[:, D_KV:].reshape(1, 1, WINDOW, N_KV, HEAD_DIM),
        ust[HALO - (FFN_CONV - 1):HALO][None, None],
    )
    s_states = (
        jnp.swapaxes(conv_s, 0, 1)[None],
        h_s[None],
        k_s.reshape(1, DEC_BATCH, WINDOW, N_KV, HEAD_DIM),
        v_s.reshape(1, DEC_BATCH, WINDOW, N_KV, HEAD_DIM),
        jnp.stack([state_ffn_conv[l][:, FFN_CONV - 2, :], ust[HALO:]], axis=1)[None],
    )
    return (y_p[None], y_s[:, None, :]) + p_states + s_states
```

```python
import math

import numpy as np
import jax
import jax.numpy as jnp
from jax import lax
from jax.experimental import pallas as pl
from jax.experimental.pallas import tpu as pltpu

F32 = jnp.float32
BF16 = jnp.bfloat16

D_MODEL = 2048
SEQ = 8192
DEC_BATCH = 128
D_RNN = 1024
N_RNN_BLOCKS = 16
RNN_BLOCK = D_RNN // N_RNN_BLOCKS
RNN_CONV = 4
LRU_C = 8.0
N_HEADS = 8
HEAD_DIM = 128
N_KV = 2
GROUP = N_HEADS // N_KV
D_ATTN = N_HEADS * HEAD_DIM
WINDOW = 128
BLOCK = 128
NUM_BUCKETS = 32
MAX_DISTANCE = 128
D_FF = 3 * D_MODEL
FFN_CONV = 3
N_META = 16
EPS = 1e-6
NEG = -1e30
D_KV = N_KV * HEAD_DIM
D_IN = 2 * D_RNN + D_ATTN + 2 * D_KV
SCALE = HEAD_DIM ** -0.5
O_GR = D_RNN
O_Q = 2 * D_RNN
O_KV = 2 * D_RNN + D_ATTN

N_PAD = BLOCK - N_META
R_TAIL = DEC_BATCH + BLOCK
R_ALL = SEQ + R_TAIL
N_PBLK = SEQ // BLOCK

FB = 2 * BLOCK
N_FSTEP = (N_PBLK + 2) // 2
PC = 512
N_IN_EARLY = 5
N_SLAB = N_FSTEP - 1

TM = 704
N_RT = R_ALL // TM
LAST_P = SEQ - (N_RT - 1) * TM
TF = 768
N_FT = D_FF // TF
KC = 256
HALO = 8
FH = 16
GW = 256
N_GW = D_RNN // GW
SB = 16
VMEM_LIMIT = 56 * 1024 * 1024
VMEM_LIMIT_BIG = 60 * 1024 * 1024


def _rms(x, g):
    return x * lax.rsqrt(jnp.mean(x * x, axis=-1, keepdims=True) + EPS) * g


def _rel_buckets(d):
    d = np.maximum(d, 0)
    exact = NUM_BUCKETS // 2
    ratio = np.maximum(d, 1).astype(np.float32) / np.float32(exact)
    large = exact + (np.log(ratio) / np.float32(math.log(MAX_DISTANCE / exact))
                     * np.float32(NUM_BUCKETS - exact)).astype(np.int32)
    large = np.minimum(large, NUM_BUCKETS - 1)
    return np.where(d < exact, d, large).astype(np.int32)


def _table_lookup(bucket, tab_ref):
    outs = [jnp.zeros(bucket.shape, F32) for _ in range(N_HEADS)]
    for b in range(NUM_BUCKETS):
        hit = bucket == b
        outs = [jnp.where(hit, tab_ref[b, h], o) for h, o in enumerate(outs)]
    return outs


def _gates(xc, wg_ref, gab, gxb, lam):
    xcb = xc.astype(BF16)
    ga, gx = [], []
    for j in range(N_GW):
        gj = jnp.dot(xcb[:, GW * j:GW * (j + 1)], wg_ref[j], preferred_element_type=F32)
        ga.append(gj[:, :GW])
        gx.append(gj[:, GW:])
    r = jax.nn.sigmoid(jnp.concatenate(ga, axis=1) + gab)
    i = jax.nn.sigmoid(jnp.concatenate(gx, axis=1) + gxb)
    log_a = -LRU_C * r * jax.nn.softplus(-lam)
    a = jnp.exp(log_a)
    t = 1.0 - a * a
    b = jnp.where(t > 0.0, t * lax.rsqrt(t), 0.0) * i * xc
    return a, b


def _front_kernel(tab_ref, sink_ref, xa_ref, xb_ref, xt_ref, gm_ref, win_f, wout_f, bucket_ref,
                  cw_ref, cb_ref, wg_ref, gab_ref, gxb_ref, lam_ref, nr_ref, na_ref, wu_f, wgt_f, wd_f,
                  x1_ref, zs_ref, hlast_ref, kvlast_ref, xrt_ref, wout_hbm, wu_b, wgt_b, wd_b,
                  win_ref, wout_ref, zp, zb_o, lhs_p, xsp, mixp, xbuf, h_scr, kvbuf, bias_scr,
                  su, sg, sd, tu, tg, td, sem_in, sem_out, sem_w):
    k = pl.program_id(0)
    last = N_FSTEP - 1

    stages = ((wu_f, su, tu, wu_b), (wgt_f, sg, tg, wgt_b), (wd_f, sd, td, wd_b))

    def slab_in(s):
        return [pltpu.make_async_copy(w.at[pl.ds(s * st.shape[0], st.shape[0])], st, sem_in.at[n])
                for n, (w, st, _, _) in enumerate(stages)]

    def slab_out(s):
        return [pltpu.make_async_copy(t, w.at[pl.ds(s * t.shape[0], t.shape[0])], sem_out.at[n])
                for n, (_, _, t, w) in enumerate(stages)]

    @pl.when(k == 0)
    def _():
        for c in slab_in(0):
            c.start()

    @pl.when(k >= 1)
    def _():
        for c in slab_out(k - 1):
            c.wait()

    @pl.when(k < N_SLAB)
    def _():
        for c in slab_in(k):
            c.wait()
        for _, st, t, _ in stages:
            t[...] = st[...].astype(BF16)
        for c in slab_out(k):
            c.start()

    @pl.when(k < N_SLAB - 1)
    def _():
        for c in slab_in(k + 1):
            c.start()

    wout_copy = pltpu.make_async_copy(wout_ref, wout_hbm, sem_w.at[2])

    @pl.when(k == 0)
    def _():
        def stage(w_f, w_b, buf):
            rows = buf.shape[0] // 2
            n = w_f.shape[0] // rows
            half = lambda c: pl.ds((c % 2) * rows, rows)
            chunk = lambda c: pltpu.make_async_copy(w_f.at[pl.ds(c * rows, rows)], buf.at[half(c)], sem_w.at[c % 2])
            chunk(0).start()
            for c in range(n):
                if c + 1 < n:
                    chunk(c + 1).start()
                chunk(c).wait()
                w_b[c * rows:(c + 1) * rows] = buf[half(c)].astype(BF16)

        stage(win_f, win_ref, zp)
        stage(wout_f, wout_ref, xsp)
        wout_copy.start()

    def inproj_chunk(c):
        def run():
            cs = slice(c * PC, (c + 1) * PC)
            zp[:, cs] = jnp.dot(lhs_p[...], win_ref[:, cs], preferred_element_type=F32)
        return run

    def outproj_chunk(c):
        def run():
            cs = slice(c * PC, (c + 1) * PC)
            x1_ref[:, cs] = xsp[:, cs] + jnp.dot(mixp[...], wout_ref[:, cs], preferred_element_type=F32)
        return run

    @pl.when(k == 0)
    def _():
        prefix = xt_ref[DEC_BATCH:R_TAIL]
        zb_o[...] = jnp.dot(_rms(prefix, gm_ref[...]).astype(BF16), win_ref[...], preferred_element_type=F32)
        xsp[0:BLOCK] = jnp.zeros((BLOCK, D_MODEL), F32)
        xsp[BLOCK:FB] = prefix
        mixp[...] = jnp.zeros((FB, D_MODEL), BF16)
        xbuf[0:HALO] = jnp.zeros((HALO, D_RNN), F32)
        h_scr[...] = jnp.zeros((HALO, D_RNN), F32)
        kvbuf[...] = jnp.zeros((BLOCK, 2 * D_KV), F32)
        for h, bias in enumerate(_table_lookup(bucket_ref[...], tab_ref)):
            bias_scr[h] = bias

    def mixer(zb, is_prefix, first_key, fillers):
        fill = iter(fillers)

        def between():
            f = next(fill, None)
            if f is not None:
                f()

        between()
        xr = zb[:, 0:D_RNN]
        xbuf[HALO:HALO + BLOCK] = xr
        cw = cw_ref[...]
        xc = cb_ref[...]
        for j in range(RNN_CONV - 1):
            lo = HALO - (RNN_CONV - 1) + j
            xc = xc + xbuf[lo:lo + BLOCK] * cw[j:j + 1]
        xc = xc + xr * cw[RNN_CONV - 1:RNN_CONV]
        xbuf[0:HALO] = xr[BLOCK - HALO:BLOCK]

        a, b = _gates(xc, wg_ref, gab_ref[...], gxb_ref[...], lam_ref[...])
        between()
        h = h_scr[0:1]
        if is_prefix is not None:
            row = lax.broadcasted_iota(jnp.int32, (BLOCK, D_RNN), 0)
            b = jnp.where(jnp.logical_and(is_prefix, row < N_PAD), 0.0, b)
            h = jnp.where(is_prefix, 0.0, h)

        ng = BLOCK // HALO
        a3 = a.reshape(ng, HALO, D_RNN)
        b3 = b.reshape(ng, HALO, D_RNN)
        sub = lax.broadcasted_iota(jnp.int32, (ng, HALO, D_RNN), 1)
        sh = 1
        while sh < HALO:
            a_prev = pltpu.roll(a3, sh, 1)
            b_prev = pltpu.roll(b3, sh, 1)
            m = sub >= sh
            b3 = jnp.where(m, a3 * b_prev + b3, b3)
            a3 = jnp.where(m, a3 * a_prev, a3)
            sh *= 2
        hs = []
        for g in range(ng):
            hg = a3[g] * h + b3[g]
            hs.append(hg)
            h = hg[HALO - 1:HALO]
        h_all = jnp.concatenate(hs, axis=0)
        h_scr[...] = jnp.broadcast_to(h, (HALO, D_RNN))
        between()
        y_rnn = h_all * jax.nn.gelu(zb[:, O_GR:O_GR + D_RNN])
        mix_a = _rms(y_rnn, nr_ref[...]).astype(BF16)

        q = zb[:, O_Q:O_Q + D_ATTN]
        kv = zb[:, O_KV:O_KV + 2 * D_KV]
        kvp = kvbuf[...]
        kvbuf[...] = kv
        col = lax.broadcasted_iota(jnp.int32, (BLOCK, 2 * BLOCK), 1)
        rowq = lax.broadcasted_iota(jnp.int32, (BLOCK, 2 * BLOCK), 0)
        d = BLOCK + rowq - col
        mask = (d >= 0) & (d < WINDOW) & (col >= first_key)
        outs = []
        for kh in range(N_KV):
            between()
            ks = slice(kh * HEAD_DIM, (kh + 1) * HEAD_DIM)
            vs = slice(D_KV + kh * HEAD_DIM, D_KV + (kh + 1) * HEAD_DIM)
            qs = jnp.concatenate(
                [q[:, (kh * GROUP + g) * HEAD_DIM:(kh * GROUP + g + 1) * HEAD_DIM] for g in range(GROUP)],
                axis=0).astype(BF16)
            kk = jnp.concatenate([kvp[:, ks], kv[:, ks]], axis=0).astype(BF16)
            vv = jnp.concatenate([kvp[:, vs], kv[:, vs]], axis=0).astype(BF16)
            sc = lax.dot_general(qs, kk, (((1,), (1,)), ((), ())), preferred_element_type=F32)
            es, dens = [], []
            for g in range(GROUP):
                hh = kh * GROUP + g
                lg = sc[g * BLOCK:(g + 1) * BLOCK] * SCALE + bias_scr[hh]
                lg = jnp.where(mask, lg, NEG)
                sink = sink_ref[hh]
                mx = jnp.maximum(jnp.max(lg, axis=-1, keepdims=True), sink)
                e = jnp.exp(lg - mx)
                dens.append(jnp.sum(e, axis=-1, keepdims=True) + jnp.exp(sink - mx))
                es.append(e.astype(BF16))
            pv = jnp.dot(jnp.concatenate(es, axis=0), vv, preferred_element_type=F32)
            for g in range(GROUP):
                outs.append(pv[g * BLOCK:(g + 1) * BLOCK] / dens[g])
        between()
        mix_b = _rms(jnp.concatenate(outs, axis=1), na_ref[...]).astype(BF16)

        return jnp.concatenate([mix_a, mix_b], axis=1), h, kv, xr

    xa = jnp.where(k == last, xt_ref[0:DEC_BATCH], xa_ref[...])
    xb = xb_ref[...]
    lhs_p[0:BLOCK] = _rms(xa, gm_ref[...]).astype(BF16)
    lhs_p[BLOCK:FB] = _rms(xb, gm_ref[...]).astype(BF16)
    n_in, n_out = D_IN // PC, D_MODEL // PC
    mix_o, h, kv, xr = mixer(zb_o, k == 0, jnp.where(k == 0, BLOCK + N_PAD, 0),
                             [inproj_chunk(c) for c in range(N_IN_EARLY)])
    mixp[BLOCK:FB] = mix_o
    hlast_ref[...] = jnp.broadcast_to(h, (HALO, D_RNN))
    kvlast_ref[...] = kv
    xrt_ref[...] = xr[BLOCK - HALO:BLOCK]

    mix_e = mixer(zp.at[0:BLOCK], None, jnp.where(k == 0, N_PAD, 0),
                  [inproj_chunk(c) for c in range(N_IN_EARLY, n_in)] + [outproj_chunk(c) for c in range(n_out)])[0]
    mixp[0:BLOCK] = mix_e
    xsp[0:BLOCK] = xa
    xsp[BLOCK:FB] = xb
    zb_o[...] = zp[BLOCK:FB]

    @pl.when(k == last)
    def _():
        zs_ref[...] = zp[0:BLOCK]
        wout_copy.wait()


def _front(tab, sinks, xp, xt, gm, w_in, w_out, bucket, cw, cb, wg, gab, gxb, lam, nr, na, w_up, w_gate, w_down):
    ffn_w = (w_up, w_gate, w_down)
    slab = lambda w: (w.shape[0] // N_SLAB, w.shape[1])
    hbm = pl.BlockSpec(memory_space=pl.ANY)
    vec = lambda n: pl.BlockSpec((1, n), lambda k: (0, 0))
    smem = pl.BlockSpec(memory_space=pltpu.SMEM)
    once = lambda shape: pl.BlockSpec(shape, lambda k: (0,) * len(shape), pipeline_mode=pl.Buffered(1))
    return pl.pallas_call(
        _front_kernel,
        grid=(N_FSTEP,),
        in_specs=[
            smem, smem,
            pl.BlockSpec((BLOCK, D_MODEL), lambda k: (jnp.minimum(2 * k, N_PBLK - 1), 0)),
            pl.BlockSpec((BLOCK, D_MODEL), lambda k: (jnp.minimum(2 * k + 1, N_PBLK - 1), 0)),
            once((R_TAIL, D_MODEL)),
            vec(D_MODEL),
            hbm, hbm,
            pl.BlockSpec((BLOCK, 2 * BLOCK), lambda k: (0, 0)),
            pl.BlockSpec((RNN_CONV, D_RNN), lambda k: (0, 0)),
            vec(D_RNN),
            pl.BlockSpec((N_GW, GW, 2 * GW), lambda k: (0, 0, 0)),
            vec(D_RNN), vec(D_RNN), vec(D_RNN), vec(D_RNN), vec(D_ATTN),
            hbm, hbm, hbm,
        ],
        out_specs=[
            pl.BlockSpec((FB, D_MODEL), lambda k: (jnp.where(k == 0, SEQ // FB, k - 1), 0)),
            pl.BlockSpec((BLOCK, D_IN), lambda k: (0, 0)),
            pl.BlockSpec((HALO, D_RNN), lambda k: (0, 0)),
            pl.BlockSpec((BLOCK, 2 * D_KV), lambda k: (0, 0)),
            pl.BlockSpec((HALO, D_RNN), lambda k: (0, 0)),
            hbm, hbm, hbm, hbm,
        ],
        out_shape=[
            jax.ShapeDtypeStruct((R_ALL, D_MODEL), F32),
            jax.ShapeDtypeStruct((DEC_BATCH, D_IN), F32),
            jax.ShapeDtypeStruct((HALO, D_RNN), F32),
            jax.ShapeDtypeStruct((BLOCK, 2 * D_KV), F32),
            jax.ShapeDtypeStruct((HALO, D_RNN), F32),
        ] + [jax.ShapeDtypeStruct(w.shape, BF16) for w in (w_out,) + ffn_w],
        scratch_shapes=[
            pltpu.VMEM(w_in.shape, BF16),
            pltpu.VMEM(w_out.shape, BF16),
            pltpu.VMEM((FB, D_IN), F32),
            pltpu.VMEM((BLOCK, D_IN), F32),
            pltpu.VMEM((FB, D_MODEL), BF16),
            pltpu.VMEM((FB, D_MODEL), F32),
            pltpu.VMEM((FB, D_MODEL), BF16),
            pltpu.VMEM((HALO + BLOCK, D_RNN), F32),
            pltpu.VMEM((HALO, D_RNN), F32),
            pltpu.VMEM((BLOCK, 2 * D_KV), F32),
            pltpu.VMEM((N_HEADS, BLOCK, 2 * BLOCK), F32),
        ] + [pltpu.VMEM(slab(w), F32) for w in ffn_w] + [pltpu.VMEM(slab(w), BF16) for w in ffn_w] + [
            pltpu.SemaphoreType.DMA((len(ffn_w),)),
            pltpu.SemaphoreType.DMA((len(ffn_w),)),
            pltpu.SemaphoreType.DMA((3,)),
        ],
        compiler_params=pltpu.CompilerParams(
            dimension_semantics=("arbitrary",), vmem_limit_bytes=VMEM_LIMIT_BIG),
        name="front",
    )(tab, sinks, xp, xp, xt, gm, w_in, w_out, bucket, cw, cb, wg, gab, gxb, lam, nr, na, *ffn_w)


def _smixer_kernel(tab_ref, sink_ref, z_ref, st_ref, h0_ref, kc_ref, vc_ref, xs_ref, wout_ref,
                   bucket_ref, cw_ref, cb_ref, wg_ref, gab_ref, gxb_ref, lam_ref, nr_ref, na_ref,
                   x1s_ref, conv_ref, h_ref, ko_ref, vo_ref, bias_scr, mix_scr):
    c = pl.program_id(0)
    nrow = N_HEADS * SB
    ncol = SB * WINDOW * N_KV

    @pl.when(c == 0)
    def _():
        col = lax.broadcasted_iota(jnp.int32, (SB, ncol), 1)
        row = lax.broadcasted_iota(jnp.int32, (SB, ncol), 0)
        own = ((col >> 8) == row) & (((col >> 1) & (WINDOW - 1)) >= 1)
        for h, bias in enumerate(_table_lookup(bucket_ref[...], tab_ref)):
            ok = own & ((col & 1) == h // GROUP)
            bias_scr[h * SB:(h + 1) * SB] = jnp.where(ok, jnp.concatenate([bias] * (SB // HALO), axis=0), NEG)

    r0 = pl.multiple_of(c * SB, SB)

    xr = z_ref[:, 0:D_RNN]
    cw = cw_ref[...]
    xc = cb_ref[...]
    for j in range(RNN_CONV - 1):
        xc = xc + st_ref[j] * cw[j:j + 1]
    xc = xc + xr * cw[RNN_CONV - 1:RNN_CONV]
    for j in range(1, RNN_CONV - 1):
        conv_ref[j - 1] = st_ref[j]
    conv_ref[RNN_CONV - 2] = xr
    a, b = _gates(xc, wg_ref, gab_ref[...], gxb_ref[...], lam_ref[...])
    h = a * h0_ref[...] + b
    h_ref[...] = h
    y_rnn = h * jax.nn.gelu(z_ref[:, O_GR:O_GR + D_RNN])
    mix_scr[pl.ds(r0, SB), 0:D_RNN] = _rms(y_rnn, nr_ref[...]).astype(BF16)

    q = z_ref[:, O_Q:O_Q + D_ATTN]
    kv = z_ref[:, O_KV:O_KV + 2 * D_KV]
    qs = jnp.concatenate([q[:, h * HEAD_DIM:(h + 1) * HEAD_DIM] for h in range(N_HEADS)], axis=0).astype(BF16)
    new_rows = lambda off: jnp.concatenate(
        [kv[:, off + (h // GROUP) * HEAD_DIM:off + (h // GROUP + 1) * HEAD_DIM] for h in range(N_HEADS)],
        axis=0).astype(BF16).astype(F32)
    k_new = new_rows(0)
    v_new = new_rows(D_KV)
    sc = lax.dot_general(qs, kc_ref[...].astype(BF16), (((1,), (1,)), ((), ())), preferred_element_type=F32)
    lg = sc * SCALE + bias_scr[...]
    rh = lax.broadcasted_iota(jnp.int32, (nrow, 1), 0) >> 4
    sink = jnp.zeros((nrow, 1), F32)
    bias_new = jnp.zeros((nrow, 1), F32)
    for h in range(N_HEADS):
        sink = jnp.where(rh == h, sink_ref[h], sink)
        bias_new = jnp.where(rh == h, tab_ref[0, h], bias_new)
    lg_new = jnp.sum(qs.astype(F32) * k_new, axis=-1, keepdims=True) * SCALE + bias_new
    mx = jnp.maximum(jnp.maximum(jnp.max(lg, axis=-1, keepdims=True), lg_new), sink)
    e = jnp.exp(lg - mx)
    e_new = jnp.exp(lg_new - mx)
    den = jnp.sum(e, axis=-1, keepdims=True) + e_new + jnp.exp(sink - mx)
    pv = jnp.dot(e.astype(BF16), vc_ref[...].astype(BF16), preferred_element_type=F32)
    o = (pv + e_new.astype(BF16).astype(F32) * v_new) / den
    y_attn = jnp.concatenate([o[h * SB:(h + 1) * SB] for h in range(N_HEADS)], axis=1)
    mix_scr[pl.ds(r0, SB), D_RNN:] = _rms(y_attn, na_ref[...]).astype(BF16)

    per = WINDOW * N_KV
    slide = lambda ref: pltpu.roll(ref[...].reshape(SB, per, HEAD_DIM), per - N_KV, 1).reshape(SB * per, HEAD_DIM)
    ko_ref[...] = slide(kc_ref)
    vo_ref[...] = slide(vc_ref)
    for bi in range(SB):
        for kh in range(N_KV):
            r = (bi + 1) * per - N_KV + kh
            ko_ref[r:r + 1, :] = kv[bi:bi + 1, kh * HEAD_DIM:(kh + 1) * HEAD_DIM]
            vo_ref[r:r + 1, :] = kv[bi:bi + 1, D_KV + kh * HEAD_DIM:D_KV + (kh + 1) * HEAD_DIM]

    @pl.when(c == DEC_BATCH // SB - 1)
    def _():
        x1s_ref[...] = xs_ref[...] + jnp.dot(mix_scr[...], wout_ref[...], preferred_element_type=F32)


def _smixer(tab, sinks, zs, st, h0, kc, vc, xs, w_out, bucket, cw, cb, wg, gab, gxb, lam, nr, na):
    vec = lambda n: pl.BlockSpec((1, n), lambda c: (0, 0))
    smem = pl.BlockSpec(memory_space=pltpu.SMEM)
    cache = pl.BlockSpec((SB * WINDOW * N_KV, HEAD_DIM), lambda c: (c, 0))
    return pl.pallas_call(
        _smixer_kernel,
        grid=(DEC_BATCH // SB,),
        in_specs=[
            smem, smem,
            pl.BlockSpec((SB, D_IN), lambda c: (c, 0)),
            pl.BlockSpec((RNN_CONV - 1, SB, D_RNN), lambda c: (0, c, 0)),
            pl.BlockSpec((SB, D_RNN), lambda c: (c, 0)),
            cache, cache,
            pl.BlockSpec((DEC_BATCH, D_MODEL), lambda c: (0, 0)),
            pl.BlockSpec((D_MODEL, D_MODEL), lambda c: (0, 0), pipeline_mode=pl.Buffered(1)),
            pl.BlockSpec((HALO, SB * WINDOW * N_KV), lambda c: (0, 0)),
            pl.BlockSpec((RNN_CONV, D_RNN), lambda c: (0, 0)),
            vec(D_RNN),
            pl.BlockSpec((N_GW, GW, 2 * GW), lambda c: (0, 0, 0)),
            vec(D_RNN), vec(D_RNN), vec(D_RNN), vec(D_RNN), vec(D_ATTN),
        ],
        out_specs=[
            pl.BlockSpec((DEC_BATCH, D_MODEL), lambda c: (0, 0)),
            pl.BlockSpec((RNN_CONV - 1, SB, D_RNN), lambda c: (0, c, 0)),
            pl.BlockSpec((SB, D_RNN), lambda c: (c, 0)),
            cache, cache,
        ],
        out_shape=[
            jax.ShapeDtypeStruct((DEC_BATCH, D_MODEL), F32),
            jax.ShapeDtypeStruct((RNN_CONV - 1, DEC_BATCH, D_RNN), F32),
            jax.ShapeDtypeStruct((DEC_BATCH, D_RNN), F32),
            jax.ShapeDtypeStruct((DEC_BATCH * WINDOW * N_KV, HEAD_DIM), F32),
            jax.ShapeDtypeStruct((DEC_BATCH * WINDOW * N_KV, HEAD_DIM), F32),
        ],
        scratch_shapes=[
            pltpu.VMEM((N_HEADS * SB, SB * WINDOW * N_KV), F32),
            pltpu.VMEM((DEC_BATCH, D_MODEL), BF16),
        ],
        compiler_params=pltpu.CompilerParams(
            dimension_semantics=("arbitrary",), vmem_limit_bytes=VMEM_LIMIT),
        name="smixer",
    )(tab, sinks, zs, st, h0, kc, vc, xs, w_out, bucket, cw, cb, wg, gab, gxb, lam, nr, na)


U_LO = LAST_P - HALO
U_HI = LAST_P + DEC_BATCH
N_UST = U_HI - U_LO


def _ffn_kernel(x1_ref, halo_ref, x1s_ref, st0_ref, st1_ref, gn_ref, gf_ref, wu_ref, wg_ref, cw_ref, cb_ref,
                wd_ref, y_ref, ys_ref, ust_ref, h2_scr, ubuf, abuf):
    i = pl.program_id(0)
    j = pl.program_id(1)

    @pl.when(j == 0)
    def _():
        h2_scr[0:FH] = _rms(halo_ref[...], gn_ref[...]).astype(BF16)
        h2_scr[FH:FH + TM] = _rms(x1_ref[...], gn_ref[...]).astype(BF16)
        y_ref[...] = jnp.zeros((TM, D_MODEL), F32)

        @pl.when(i == N_RT - 1)
        def _():
            h2_scr[FH + LAST_P:FH + U_HI] = _rms(x1s_ref[...], gn_ref[...]).astype(BF16)

    ubuf[...] = jnp.dot(h2_scr[...], wu_ref[...], preferred_element_type=F32)
    gate = jnp.dot(h2_scr[FH:FH + TM], wg_ref[...], preferred_element_type=F32)
    cw = cw_ref[...]
    cb = cb_ref[...]
    for c in range(TF // KC):
        cs = slice(c * KC, (c + 1) * KC)
        tap = lambda r0, n: ubuf[r0:r0 + n, cs]
        uc = (cb[:, cs] + tap(FH - 2, TM) * cw[0:1, cs] + tap(FH - 1, TM) * cw[1:2, cs] + tap(FH, TM) * cw[2:3, cs])
        act = jax.nn.gelu(uc) * gate[:, cs]
        abuf[:, cs] = act.astype(BF16)
        ucs = (cb[:, cs] + st0_ref[:, cs] * cw[0:1, cs] + st1_ref[:, cs] * cw[1:2, cs]
               + tap(FH + LAST_P, DEC_BATCH) * cw[2:3, cs])
        act_s = jax.nn.gelu(ucs) * gate[LAST_P:U_HI, cs]
        abuf[LAST_P:U_HI, cs] = jnp.where(i == N_RT - 1, act_s, act[LAST_P:U_HI]).astype(BF16)
        y_ref[...] += jnp.dot(abuf[:, cs], wd_ref[cs, :], preferred_element_type=F32)
    ust_ref[...] = ubuf[FH + U_LO:FH + U_HI]

    @pl.when(j == N_FT - 1)
    def _():
        @pl.when(i == N_RT - 1)
        def _():
            ys_ref[...] = _rms(x1s_ref[...] + y_ref[LAST_P:U_HI], gf_ref[...])

        y_ref[...] = _rms(x1_ref[...] + y_ref[...], gf_ref[...])


def _ffn(x1, x1s, st, gn, gf, wu, wg, cw, cb, wd):
    def halo_idx(i, j):
        return (jnp.where(i == 0, R_ALL // FH - 1, i * (TM // FH) - 1), 0)

    return pl.pallas_call(
        _ffn_kernel,
        grid=(N_RT, N_FT),
        in_specs=[
            pl.BlockSpec((TM, D_MODEL), lambda i, j: (i, 0)),
            pl.BlockSpec((FH, D_MODEL), halo_idx),
            pl.BlockSpec((DEC_BATCH, D_MODEL), lambda i, j: (0, 0)),
            pl.BlockSpec((DEC_BATCH, TF), lambda i, j: (0, j)),
            pl.BlockSpec((DEC_BATCH, TF), lambda i, j: (0, N_FT + j)),
            pl.BlockSpec((1, D_MODEL), lambda i, j: (0, 0)),
            pl.BlockSpec((1, D_MODEL), lambda i, j: (0, 0)),
            pl.BlockSpec((D_MODEL, TF), lambda i, j: (0, j)),
            pl.BlockSpec((D_MODEL, TF), lambda i, j: (0, j)),
            pl.BlockSpec((FFN_CONV, TF), lambda i, j: (0, j)),
            pl.BlockSpec((1, TF), lambda i, j: (0, j)),
            pl.BlockSpec((TF, D_MODEL), lambda i, j: (j, 0)),
        ],
        out_specs=[
            pl.BlockSpec((TM, D_MODEL), lambda i, j: (i, 0)),
            pl.BlockSpec((DEC_BATCH, D_MODEL), lambda i, j: (0, 0)),
            pl.BlockSpec((N_UST, TF), lambda i, j: (i, j)),
        ],
        out_shape=[
            jax.ShapeDtypeStruct((SEQ, D_MODEL), F32),
            jax.ShapeDtypeStruct((DEC_BATCH, D_MODEL), F32),
            jax.ShapeDtypeStruct((N_RT * N_UST, D_FF), F32),
        ],
        scratch_shapes=[
            pltpu.VMEM((FH + TM, D_MODEL), BF16),
            pltpu.VMEM((FH + TM, TF), F32),
            pltpu.VMEM((TM, TF), BF16),
        ],
        compiler_params=pltpu.CompilerParams(
            dimension_semantics=("arbitrary", "arbitrary"), vmem_limit_bytes=VMEM_LIMIT_BIG),
        name="ffn",
    )(x1, x1, x1s, st, st, gn, gf, wu, wg, cw, cb, wd)


def _gate_weights(wa, wx):
    per = GW // RNN_BLOCK
    eye = jnp.eye(per, dtype=wa.dtype)

    def bd(w):
        w = w.reshape(N_GW, per, RNN_BLOCK, RNN_BLOCK)
        return jnp.einsum('gpcd,pq->gpcqd', w, eye).reshape(N_GW, GW, GW)

    return jnp.concatenate([bd(wa), bd(wx)], axis=-1).astype(BF16)


def kernel(x_prompt, x_sample, state_rnn_conv, state_rnn_h, cache_k_win, cache_v_win, state_ffn_conv,
           meta_tokens, rel_bias_table, norm_mix, w_in, rnn_conv_w, rnn_conv_b, gate_a_w, gate_a_b,
           gate_x_w, gate_x_b, rnn_lambda, attn_sinks, norm_rnn_out, norm_attn_out, w_out, norm_ffn,
           w_up, w_gate, ffn_conv_w, ffn_conv_b, w_down, norm_final):
    l = 0
    xp = x_prompt[0]
    xs = x_sample[:, 0, :]
    xt = jnp.concatenate([xs, jnp.zeros((N_PAD, D_MODEL), F32), meta_tokens], axis=0)
    row = lambda v: v.reshape(1, -1)

    qi = np.arange(BLOCK)[:, None]
    sj = np.arange(2 * BLOCK)[None, :]
    bucket_p = jnp.asarray(_rel_buckets(BLOCK + qi - sj))
    pos = (np.arange(SB * WINDOW * N_KV) >> 1) & (WINDOW - 1)
    bucket_s = jnp.asarray(np.tile(_rel_buckets(WINDOW - pos)[None, :], (HALO, 1)))

    wg = _gate_weights(gate_a_w[l], gate_x_w[l])
    seq_w = (rnn_conv_w[l], row(rnn_conv_b[l]), wg, row(gate_a_b[l]), row(gate_x_b[l]), row(rnn_lambda[l]),
             row(norm_rnn_out[l]), row(norm_attn_out[l]))

    x1, zs, h_last, kv_last, xr_tail, w_out_b, w_up_b, w_gate_b, w_down_b = _front(
        rel_bias_table, attn_sinks[l], xp, xt, row(norm_mix[l]), w_in[l], w_out[l], bucket_p, *seq_w,
        w_up[l], w_gate[l], w_down[l])
    x1s, conv_s, h_s, k_s, v_s = _smixer(
        rel_bias_table, attn_sinks[l], zs,
        jnp.swapaxes(state_rnn_conv[l], 0, 1), state_rnn_h[l],
        cache_k_win[l].reshape(DEC_BATCH * WINDOW * N_KV, HEAD_DIM),
        cache_v_win[l].reshape(DEC_BATCH * WINDOW * N_KV, HEAD_DIM),
        xs, w_out_b, bucket_s, *seq_w)
    y_p, y_s, ust = _ffn(x1, x1s, state_ffn_conv[l].reshape(DEC_BATCH, (FFN_CONV - 1) * D_FF),
                         row(norm_ffn[l]), row(norm_final), w_up_b, w_gate_b,
                         ffn_conv_w[l], row(ffn_conv_b[l]), w_down_b)

    ust = ust[(N_RT - 1) * N_UST:]
    p_states = (
        xr_tail[HALO - (RNN_CONV - 1):HALO][None, None],
        h_last[0:1][None],
        kv_last[:, :D_KV].reshape(1, 1, WINDOW, N_KV, HEAD_DIM),
        kv_last[:, D_KV:].reshape(1, 1, WINDOW, N_KV, HEAD_DIM),
        ust[HALO - (FFN_CONV - 1):HALO][None, None],
    )
    s_states = (
        jnp.swapaxes(conv_s, 0, 1)[None],
        h_s[None],
        k_s.reshape(1, DEC_BATCH, WINDOW, N_KV, HEAD_DIM),
        v_s.reshape(1, DEC_BATCH, WINDOW, N_KV, HEAD_DIM),
        jnp.stack([state_ffn_conv[l][:, FFN_CONV - 2, :], ust[HALO:]], axis=1)[None],
    )
    return (y_p[None], y_s[:, None, :]) + p_states + s_states
```

```python
import math

import numpy as np
import jax
import jax.numpy as jnp
from jax import lax
from jax.experimental import pallas as pl
from jax.experimental.pallas import tpu as pltpu

F32 = jnp.float32
BF16 = jnp.bfloat16

D_MODEL = 2048
SEQ = 8192
DEC_BATCH = 128
D_RNN = 1024
N_RNN_BLOCKS = 16
RNN_BLOCK = D_RNN // N_RNN_BLOCKS
RNN_CONV = 4
LRU_C = 8.0
N_HEADS = 8
HEAD_DIM = 128
N_KV = 2
GROUP = N_HEADS // N_KV
D_ATTN = N_HEADS * HEAD_DIM
WINDOW = 128
BLOCK = 128
NUM_BUCKETS = 32
MAX_DISTANCE = 128
D_FF = 3 * D_MODEL
FFN_CONV = 3
N_META = 16
EPS = 1e-6
NEG = -1e30
D_KV = N_KV * HEAD_DIM
D_IN = 2 * D_RNN + D_ATTN + 2 * D_KV
SCALE = HEAD_DIM ** -0.5
O_GR = D_RNN
O_Q = 2 * D_RNN
O_KV = 2 * D_RNN + D_ATTN

N_PAD = BLOCK - N_META
R_TAIL = DEC_BATCH + BLOCK
R_ALL = SEQ + R_TAIL
N_PBLK = SEQ // BLOCK

FB = 2 * BLOCK
N_FSTEP = (N_PBLK + 2) // 2
PC = 512
N_IN_EARLY = 5
N_STAGE = 4
N_SLAB = N_FSTEP - 1

TM = 704
N_RT = R_ALL // TM
LAST_P = SEQ - (N_RT - 1) * TM
TF = 768
N_FT = D_FF // TF
KC = 256
HALO = 8
FH = 16
GW = 256
N_GW = D_RNN // GW
SB = 16
VMEM_LIMIT = 56 * 1024 * 1024
VMEM_LIMIT_BIG = 60 * 1024 * 1024


def _rms(x, g):
    return x * lax.rsqrt(jnp.mean(x * x, axis=-1, keepdims=True) + EPS) * g


def _rel_buckets(d):
    d = np.maximum(d, 0)
    exact = NUM_BUCKETS // 2
    ratio = np.maximum(d, 1).astype(np.float32) / np.float32(exact)
    large = exact + (np.log(ratio) / np.float32(math.log(MAX_DISTANCE / exact))
                     * np.float32(NUM_BUCKETS - exact)).astype(np.int32)
    large = np.minimum(large, NUM_BUCKETS - 1)
    return np.where(d < exact, d, large).astype(np.int32)


def _table_lookup(bucket, tab_ref):
    outs = [jnp.zeros(bucket.shape, F32) for _ in range(N_HEADS)]
    for b in range(NUM_BUCKETS):
        hit = bucket == b
        outs = [jnp.where(hit, tab_ref[b, h], o) for h, o in enumerate(outs)]
    return outs


def _gates(xc, wg_ref, gab, gxb, lam):
    xcb = xc.astype(BF16)
    ga, gx = [], []
    for j in range(N_GW):
        gj = jnp.dot(xcb[:, GW * j:GW * (j + 1)], wg_ref[j], preferred_element_type=F32)
        ga.append(gj[:, :GW])
        gx.append(gj[:, GW:])
    r = jax.nn.sigmoid(jnp.concatenate(ga, axis=1) + gab)
    i = jax.nn.sigmoid(jnp.concatenate(gx, axis=1) + gxb)
    log_a = -LRU_C * r * jax.nn.softplus(-lam)
    a = jnp.exp(log_a)
    t = 1.0 - a * a
    b = jnp.where(t > 0.0, t * lax.rsqrt(t), 0.0) * i * xc
    return a, b


def _front_kernel(tab_ref, sink_ref, xa_ref, xb_ref, xt_ref, gm_ref, win_f, wout_f, bucket_ref,
                  cw_ref, cb_ref, wg_ref, gab_ref, gxb_ref, lam_ref, nr_ref, na_ref, wu_f, wgt_f, wd_f,
                  x1_ref, zs_ref, hlast_ref, kvlast_ref, xrt_ref, wout_hbm, wu_b, wgt_b, wd_b,
                  win_ref, wout_ref, zp, zb_o, lhs_p, xsp, mixp, xbuf, h_scr, kvbuf, bias_scr,
                  su, sg, sd, tu, tg, td, sem_in, sem_out, sem_w):
    k = pl.program_id(0)
    last = N_FSTEP - 1

    stages = ((wu_f, su, tu, wu_b), (wgt_f, sg, tg, wgt_b), (wd_f, sd, td, wd_b))

    def slab_in(s):
        return [pltpu.make_async_copy(w.at[pl.ds(s * st.shape[0], st.shape[0])], st, sem_in.at[n])
                for n, (w, st, _, _) in enumerate(stages)]

    def slab_out(s):
        return [pltpu.make_async_copy(t, w.at[pl.ds(s * t.shape[0], t.shape[0])], sem_out.at[n])
                for n, (_, _, t, w) in enumerate(stages)]

    @pl.when(k == 0)
    def _():
        for c in slab_in(0):
            c.start()

    @pl.when(k >= 1)
    def _():
        for c in slab_out(k - 1):
            c.wait()

    @pl.when(k < N_SLAB)
    def _():
        for c in slab_in(k):
            c.wait()
        for _, st, t, _ in stages:
            t[...] = st[...].astype(BF16)
        for c in slab_out(k):
            c.start()

    @pl.when(k < N_SLAB - 1)
    def _():
        for c in slab_in(k + 1):
            c.start()

    wout_copy = pltpu.make_async_copy(wout_ref, wout_hbm, sem_w.at[2 * N_STAGE])

    @pl.when(k == 0)
    def _():
        streams = ((win_f, win_ref, zp, 0), (wout_f, wout_ref, xsp, N_STAGE))
        rows = FB // N_STAGE
        n = D_MODEL // rows

        def chunk(stream, c):
            w_f, _, buf, sem0 = stream
            slot = c % N_STAGE
            return pltpu.make_async_copy(w_f.at[pl.ds(c * rows, rows)], buf.at[pl.ds(slot * rows, rows)],
                                         sem_w.at[sem0 + slot])

        for c in range(N_STAGE - 1):
            for stream in streams:
                chunk(stream, c).start()
        for c in range(n):
            for stream in streams:
                if c + N_STAGE - 1 < n:
                    chunk(stream, c + N_STAGE - 1).start()
                chunk(stream, c).wait()
                _, w_b, buf, _ = stream
                w_b[c * rows:(c + 1) * rows] = buf[pl.ds((c % N_STAGE) * rows, rows)].astype(BF16)
        wout_copy.start()

    def inproj_chunk(c):
        def run():
            cs = slice(c * PC, (c + 1) * PC)
            zp[:, cs] = jnp.dot(lhs_p[...], win_ref[:, cs], preferred_element_type=F32)
        return run

    def outproj_chunk(c):
        def run():
            cs = slice(c * PC, (c + 1) * PC)
            x1_ref[:, cs] = xsp[:, cs] + jnp.dot(mixp[...], wout_ref[:, cs], preferred_element_type=F32)
        return run

    @pl.when(k == 0)
    def _():
        prefix = xt_ref[DEC_BATCH:R_TAIL]
        zb_o[...] = jnp.dot(_rms(prefix, gm_ref[...]).astype(BF16), win_ref[...], preferred_element_type=F32)
        xsp[0:BLOCK] = jnp.zeros((BLOCK, D_MODEL), F32)
        xsp[BLOCK:FB] = prefix
        mixp[...] = jnp.zeros((FB, D_MODEL), BF16)
        xbuf[0:HALO] = jnp.zeros((HALO, D_RNN), F32)
        h_scr[...] = jnp.zeros((HALO, D_RNN), F32)
        kvbuf[...] = jnp.zeros((BLOCK, 2 * D_KV), F32)
        for h, bias in enumerate(_table_lookup(bucket_ref[...], tab_ref)):
            bias_scr[h] = bias

    def mixer(zb, is_prefix, first_key, fillers):
        fill = iter(fillers)

        def between():
            f = next(fill, None)
            if f is not None:
                f()

        between()
        xr = zb[:, 0:D_RNN]
        xbuf[HALO:HALO + BLOCK] = xr
        cw = cw_ref[...]
        xc = cb_ref[...]
        for j in range(RNN_CONV - 1):
            lo = HALO - (RNN_CONV - 1) + j
            xc = xc + xbuf[lo:lo + BLOCK] * cw[j:j + 1]
        xc = xc + xr * cw[RNN_CONV - 1:RNN_CONV]
        xbuf[0:HALO] = xr[BLOCK - HALO:BLOCK]

        a, b = _gates(xc, wg_ref, gab_ref[...], gxb_ref[...], lam_ref[...])
        between()
        h = h_scr[0:1]
        if is_prefix is not None:
            row = lax.broadcasted_iota(jnp.int32, (BLOCK, D_RNN), 0)
            b = jnp.where(jnp.logical_and(is_prefix, row < N_PAD), 0.0, b)
            h = jnp.where(is_prefix, 0.0, h)

        ng = BLOCK // HALO
        a3 = a.reshape(ng, HALO, D_RNN)
        b3 = b.reshape(ng, HALO, D_RNN)
        sub = lax.broadcasted_iota(jnp.int32, (ng, HALO, D_RNN), 1)
        sh = 1
        while sh < HALO:
            a_prev = pltpu.roll(a3, sh, 1)
            b_prev = pltpu.roll(b3, sh, 1)
            m = sub >= sh
            b3 = jnp.where(m, a3 * b_prev + b3, b3)
            a3 = jnp.where(m, a3 * a_prev, a3)
            sh *= 2
        hs = []
        for g in range(ng):
            hg = a3[g] * h + b3[g]
            hs.append(hg)
            h = hg[HALO - 1:HALO]
        h_all = jnp.concatenate(hs, axis=0)
        h_scr[...] = jnp.broadcast_to(h, (HALO, D_RNN))
        between()
        y_rnn = h_all * jax.nn.gelu(zb[:, O_GR:O_GR + D_RNN])
        mix_a = _rms(y_rnn, nr_ref[...]).astype(BF16)

        q = zb[:, O_Q:O_Q + D_ATTN]
        kv = zb[:, O_KV:O_KV + 2 * D_KV]
        kvp = kvbuf[...]
        kvbuf[...] = kv
        col = lax.broadcasted_iota(jnp.int32, (BLOCK, 2 * BLOCK), 1)
        rowq = lax.broadcasted_iota(jnp.int32, (BLOCK, 2 * BLOCK), 0)
        d = BLOCK + rowq - col
        mask = (d >= 0) & (d < WINDOW) & (col >= first_key)
        outs = []
        for kh in range(N_KV):
            between()
            ks = slice(kh * HEAD_DIM, (kh + 1) * HEAD_DIM)
            vs = slice(D_KV + kh * HEAD_DIM, D_KV + (kh + 1) * HEAD_DIM)
            qs = jnp.concatenate(
                [q[:, (kh * GROUP + g) * HEAD_DIM:(kh * GROUP + g + 1) * HEAD_DIM] for g in range(GROUP)],
                axis=0).astype(BF16)
            kk = jnp.concatenate([kvp[:, ks], kv[:, ks]], axis=0).astype(BF16)
            vv = jnp.concatenate([kvp[:, vs], kv[:, vs]], axis=0).astype(BF16)
            sc = lax.dot_general(qs, kk, (((1,), (1,)), ((), ())), preferred_element_type=F32)
            es, dens = [], []
            for g in range(GROUP):
                hh = kh * GROUP + g
                lg = sc[g * BLOCK:(g + 1) * BLOCK] * SCALE + bias_scr[hh]
                lg = jnp.where(mask, lg, NEG)
                sink = sink_ref[hh]
                mx = jnp.maximum(jnp.max(lg, axis=-1, keepdims=True), sink)
                e = jnp.exp(lg - mx)
                dens.append(jnp.sum(e, axis=-1, keepdims=True) + jnp.exp(sink - mx))
                es.append(e.astype(BF16))
            pv = jnp.dot(jnp.concatenate(es, axis=0), vv, preferred_element_type=F32)
            for g in range(GROUP):
                outs.append(pv[g * BLOCK:(g + 1) * BLOCK] / dens[g])
        between()
        mix_b = _rms(jnp.concatenate(outs, axis=1), na_ref[...]).astype(BF16)

        return jnp.concatenate([mix_a, mix_b], axis=1), h, kv, xr

    xa = jnp.where(k == last, xt_ref[0:DEC_BATCH], xa_ref[...])
    xb = xb_ref[...]
    lhs_p[0:BLOCK] = _rms(xa, gm_ref[...]).astype(BF16)
    lhs_p[BLOCK:FB] = _rms(xb, gm_ref[...]).astype(BF16)
    n_in, n_out = D_IN // PC, D_MODEL // PC
    mix_o, h, kv, xr = mixer(zb_o, k == 0, jnp.where(k == 0, BLOCK + N_PAD, 0),
                             [inproj_chunk(c) for c in range(N_IN_EARLY)])
    mixp[BLOCK:FB] = mix_o
    hlast_ref[...] = jnp.broadcast_to(h, (HALO, D_RNN))
    kvlast_ref[...] = kv
    xrt_ref[...] = xr[BLOCK - HALO:BLOCK]

    mix_e = mixer(zp.at[0:BLOCK], None, jnp.where(k == 0, N_PAD, 0),
                  [inproj_chunk(c) for c in range(N_IN_EARLY, n_in)] + [outproj_chunk(c) for c in range(n_out)])[0]
    mixp[0:BLOCK] = mix_e
    xsp[0:BLOCK] = xa
    xsp[BLOCK:FB] = xb
    zb_o[...] = zp[BLOCK:FB]

    @pl.when(k == last)
    def _():
        zs_ref[...] = zp[0:BLOCK]
        wout_copy.wait()


def _front(tab, sinks, xp, xt, gm, w_in, w_out, bucket, cw, cb, wg, gab, gxb, lam, nr, na, w_up, w_gate, w_down):
    ffn_w = (w_up, w_gate, w_down)
    slab = lambda w: (w.shape[0] // N_SLAB, w.shape[1])
    hbm = pl.BlockSpec(memory_space=pl.ANY)
    vec = lambda n: pl.BlockSpec((1, n), lambda k: (0, 0))
    smem = pl.BlockSpec(memory_space=pltpu.SMEM)
    once = lambda shape: pl.BlockSpec(shape, lambda k: (0,) * len(shape), pipeline_mode=pl.Buffered(1))
    return pl.pallas_call(
        _front_kernel,
        grid=(N_FSTEP,),
        in_specs=[
            smem, smem,
            pl.BlockSpec((BLOCK, D_MODEL), lambda k: (jnp.minimum(2 * k, N_PBLK - 1), 0)),
            pl.BlockSpec((BLOCK, D_MODEL), lambda k: (jnp.minimum(2 * k + 1, N_PBLK - 1), 0)),
            once((R_TAIL, D_MODEL)),
            vec(D_MODEL),
            hbm, hbm,
            pl.BlockSpec((BLOCK, 2 * BLOCK), lambda k: (0, 0)),
            pl.BlockSpec((RNN_CONV, D_RNN), lambda k: (0, 0)),
            vec(D_RNN),
            pl.BlockSpec((N_GW, GW, 2 * GW), lambda k: (0, 0, 0)),
            vec(D_RNN), vec(D_RNN), vec(D_RNN), vec(D_RNN), vec(D_ATTN),
            hbm, hbm, hbm,
        ],
        out_specs=[
            pl.BlockSpec((FB, D_MODEL), lambda k: (jnp.where(k == 0, SEQ // FB, k - 1), 0)),
            pl.BlockSpec((BLOCK, D_IN), lambda k: (0, 0)),
            pl.BlockSpec((HALO, D_RNN), lambda k: (0, 0)),
            pl.BlockSpec((BLOCK, 2 * D_KV), lambda k: (0, 0)),
            pl.BlockSpec((HALO, D_RNN), lambda k: (0, 0)),
            hbm, hbm, hbm, hbm,
        ],
        out_shape=[
            jax.ShapeDtypeStruct((R_ALL, D_MODEL), F32),
            jax.ShapeDtypeStruct((DEC_BATCH, D_IN), F32),
            jax.ShapeDtypeStruct((HALO, D_RNN), F32),
            jax.ShapeDtypeStruct((BLOCK, 2 * D_KV), F32),
            jax.ShapeDtypeStruct((HALO, D_RNN), F32),
        ] + [jax.ShapeDtypeStruct(w.shape, BF16) for w in (w_out,) + ffn_w],
        scratch_shapes=[
            pltpu.VMEM(w_in.shape, BF16),
            pltpu.VMEM(w_out.shape, BF16),
            pltpu.VMEM((FB, D_IN), F32),
            pltpu.VMEM((BLOCK, D_IN), F32),
            pltpu.VMEM((FB, D_MODEL), BF16),
            pltpu.VMEM((FB, D_MODEL), F32),
            pltpu.VMEM((FB, D_MODEL), BF16),
            pltpu.VMEM((HALO + BLOCK, D_RNN), F32),
            pltpu.VMEM((HALO, D_RNN), F32),
            pltpu.VMEM((BLOCK, 2 * D_KV), F32),
            pltpu.VMEM((N_HEADS, BLOCK, 2 * BLOCK), F32),
        ] + [pltpu.VMEM(slab(w), F32) for w in ffn_w] + [pltpu.VMEM(slab(w), BF16) for w in ffn_w] + [
            pltpu.SemaphoreType.DMA((len(ffn_w),)),
            pltpu.SemaphoreType.DMA((len(ffn_w),)),
            pltpu.SemaphoreType.DMA((2 * N_STAGE + 1,)),
        ],
        compiler_params=pltpu.CompilerParams(
            dimension_semantics=("arbitrary",), vmem_limit_bytes=VMEM_LIMIT_BIG),
        name="front",
    )(tab, sinks, xp, xp, xt, gm, w_in, w_out, bucket, cw, cb, wg, gab, gxb, lam, nr, na, *ffn_w)


def _smixer_kernel(tab_ref, sink_ref, z_ref, st_ref, h0_ref, kc_ref, vc_ref, xs_ref, wout_ref,
                   bucket_ref, cw_ref, cb_ref, wg_ref, gab_ref, gxb_ref, lam_ref, nr_ref, na_ref,
                   x1s_ref, conv_ref, h_ref, ko_ref, vo_ref, bias_scr, mix_scr):
    c = pl.program_id(0)
    nrow = N_HEADS * SB
    ncol = SB * WINDOW * N_KV

    @pl.when(c == 0)
    def _():
        col = lax.broadcasted_iota(jnp.int32, (SB, ncol), 1)
        row = lax.broadcasted_iota(jnp.int32, (SB, ncol), 0)
        own = ((col >> 8) == row) & (((col >> 1) & (WINDOW - 1)) >= 1)
        for h, bias in enumerate(_table_lookup(bucket_ref[...], tab_ref)):
            ok = own & ((col & 1) == h // GROUP)
            bias_scr[h * SB:(h + 1) * SB] = jnp.where(ok, jnp.concatenate([bias] * (SB // HALO), axis=0), NEG)

    r0 = pl.multiple_of(c * SB, SB)

    xr = z_ref[:, 0:D_RNN]
    cw = cw_ref[...]
    xc = cb_ref[...]
    for j in range(RNN_CONV - 1):
        xc = xc + st_ref[j] * cw[j:j + 1]
    xc = xc + xr * cw[RNN_CONV - 1:RNN_CONV]
    for j in range(1, RNN_CONV - 1):
        conv_ref[j - 1] = st_ref[j]
    conv_ref[RNN_CONV - 2] = xr
    a, b = _gates(xc, wg_ref, gab_ref[...], gxb_ref[...], lam_ref[...])
    h = a * h0_ref[...] + b
    h_ref[...] = h
    y_rnn = h * jax.nn.gelu(z_ref[:, O_GR:O_GR + D_RNN])
    mix_scr[pl.ds(r0, SB), 0:D_RNN] = _rms(y_rnn, nr_ref[...]).astype(BF16)

    q = z_ref[:, O_Q:O_Q + D_ATTN]
    kv = z_ref[:, O_KV:O_KV + 2 * D_KV]
    qs = jnp.concatenate([q[:, h * HEAD_DIM:(h + 1) * HEAD_DIM] for h in range(N_HEADS)], axis=0).astype(BF16)
    new_rows = lambda off: jnp.concatenate(
        [kv[:, off + (h // GROUP) * HEAD_DIM:off + (h // GROUP + 1) * HEAD_DIM] for h in range(N_HEADS)],
        axis=0).astype(BF16).astype(F32)
    k_new = new_rows(0)
    v_new = new_rows(D_KV)
    sc = lax.dot_general(qs, kc_ref[...].astype(BF16), (((1,), (1,)), ((), ())), preferred_element_type=F32)
    lg = sc * SCALE + bias_scr[...]
    rh = lax.broadcasted_iota(jnp.int32, (nrow, 1), 0) >> 4
    sink = jnp.zeros((nrow, 1), F32)
    bias_new = jnp.zeros((nrow, 1), F32)
    for h in range(N_HEADS):
        sink = jnp.where(rh == h, sink_ref[h], sink)
        bias_new = jnp.where(rh == h, tab_ref[0, h], bias_new)
    lg_new = jnp.sum(qs.astype(F32) * k_new, axis=-1, keepdims=True) * SCALE + bias_new
    mx = jnp.maximum(jnp.maximum(jnp.max(lg, axis=-1, keepdims=True), lg_new), sink)
    e = jnp.exp(lg - mx)
    e_new = jnp.exp(lg_new - mx)
    den = jnp.sum(e, axis=-1, keepdims=True) + e_new + jnp.exp(sink - mx)
    pv = jnp.dot(e.astype(BF16), vc_ref[...].astype(BF16), preferred_element_type=F32)
    o = (pv + e_new.astype(BF16).astype(F32) * v_new) / den
    y_attn = jnp.concatenate([o[h * SB:(h + 1) * SB] for h in range(N_HEADS)], axis=1)
    mix_scr[pl.ds(r0, SB), D_RNN:] = _rms(y_attn, na_ref[...]).astype(BF16)

    per = WINDOW * N_KV
    slide = lambda ref: pltpu.roll(ref[...].reshape(SB, per, HEAD_DIM), per - N_KV, 1).reshape(SB * per, HEAD_DIM)
    ko_ref[...] = slide(kc_ref)
    vo_ref[...] = slide(vc_ref)
    for bi in range(SB):
        for kh in range(N_KV):
            r = (bi + 1) * per - N_KV + kh
            ko_ref[r:r + 1, :] = kv[bi:bi + 1, kh * HEAD_DIM:(kh + 1) * HEAD_DIM]
            vo_ref[r:r + 1, :] = kv[bi:bi + 1, D_KV + kh * HEAD_DIM:D_KV + (kh + 1) * HEAD_DIM]

    @pl.when(c == DEC_BATCH // SB - 1)
    def _():
        x1s_ref[...] = xs_ref[...] + jnp.dot(mix_scr[...], wout_ref[...], preferred_element_type=F32)


def _smixer(tab, sinks, zs, st, h0, kc, vc, xs, w_out, bucket, cw, cb, wg, gab, gxb, lam, nr, na):
    vec = lambda n: pl.BlockSpec((1, n), lambda c: (0, 0))
    smem = pl.BlockSpec(memory_space=pltpu.SMEM)
    cache = pl.BlockSpec((SB * WINDOW * N_KV, HEAD_DIM), lambda c: (c, 0))
    return pl.pallas_call(
        _smixer_kernel,
        grid=(DEC_BATCH // SB,),
        in_specs=[
            smem, smem,
            pl.BlockSpec((SB, D_IN), lambda c: (c, 0)),
            pl.BlockSpec((RNN_CONV - 1, SB, D_RNN), lambda c: (0, c, 0)),
            pl.BlockSpec((SB, D_RNN), lambda c: (c, 0)),
            cache, cache,
            pl.BlockSpec((DEC_BATCH, D_MODEL), lambda c: (0, 0)),
            pl.BlockSpec((D_MODEL, D_MODEL), lambda c: (0, 0), pipeline_mode=pl.Buffered(1)),
            pl.BlockSpec((HALO, SB * WINDOW * N_KV), lambda c: (0, 0)),
            pl.BlockSpec((RNN_CONV, D_RNN), lambda c: (0, 0)),
            vec(D_RNN),
            pl.BlockSpec((N_GW, GW, 2 * GW), lambda c: (0, 0, 0)),
            vec(D_RNN), vec(D_RNN), vec(D_RNN), vec(D_RNN), vec(D_ATTN),
        ],
        out_specs=[
            pl.BlockSpec((DEC_BATCH, D_MODEL), lambda c: (0, 0)),
            pl.BlockSpec((RNN_CONV - 1, SB, D_RNN), lambda c: (0, c, 0)),
            pl.BlockSpec((SB, D_RNN), lambda c: (c, 0)),
            cache, cache,
        ],
        out_shape=[
            jax.ShapeDtypeStruct((DEC_BATCH, D_MODEL), F32),
            jax.ShapeDtypeStruct((RNN_CONV - 1, DEC_BATCH, D_RNN), F32),
            jax.ShapeDtypeStruct((DEC_BATCH, D_RNN), F32),
            jax.ShapeDtypeStruct((DEC_BATCH * WINDOW * N_KV, HEAD_DIM), F32),
            jax.ShapeDtypeStruct((DEC_BATCH * WINDOW * N_KV, HEAD_DIM), F32),
        ],
        scratch_shapes=[
            pltpu.VMEM((N_HEADS * SB, SB * WINDOW * N_KV), F32),
            pltpu.VMEM((DEC_BATCH, D_MODEL), BF16),
        ],
        compiler_params=pltpu.CompilerParams(
            dimension_semantics=("arbitrary",), vmem_limit_bytes=VMEM_LIMIT),
        name="smixer",
    )(tab, sinks, zs, st, h0, kc, vc, xs, w_out, bucket, cw, cb, wg, gab, gxb, lam, nr, na)


U_LO = LAST_P - HALO
U_HI = LAST_P + DEC_BATCH
N_UST = U_HI - U_LO


def _ffn_kernel(x1_ref, halo_ref, x1s_ref, st0_ref, st1_ref, gn_ref, gf_ref, wu_ref, wg_ref, cw_ref, cb_ref,
                wd_ref, y_ref, ys_ref, ust_ref, h2_scr, ubuf, abuf):
    i = pl.program_id(0)
    j = pl.program_id(1)

    @pl.when(j == 0)
    def _():
        h2_scr[0:FH] = _rms(halo_ref[...], gn_ref[...]).astype(BF16)
        h2_scr[FH:FH + TM] = _rms(x1_ref[...], gn_ref[...]).astype(BF16)
        y_ref[...] = jnp.zeros((TM, D_MODEL), F32)

        @pl.when(i == N_RT - 1)
        def _():
            h2_scr[FH + LAST_P:FH + U_HI] = _rms(x1s_ref[...], gn_ref[...]).astype(BF16)

    ubuf[...] = jnp.dot(h2_scr[...], wu_ref[...], preferred_element_type=F32)
    gate = jnp.dot(h2_scr[FH:FH + TM], wg_ref[...], preferred_element_type=F32)
    cw = cw_ref[...]
    cb = cb_ref[...]
    for c in range(TF // KC):
        cs = slice(c * KC, (c + 1) * KC)
        tap = lambda r0, n: ubuf[r0:r0 + n, cs]
        uc = (cb[:, cs] + tap(FH - 2, TM) * cw[0:1, cs] + tap(FH - 1, TM) * cw[1:2, cs] + tap(FH, TM) * cw[2:3, cs])
        act = jax.nn.gelu(uc) * gate[:, cs]
        abuf[:, cs] = act.astype(BF16)
        ucs = (cb[:, cs] + st0_ref[:, cs] * cw[0:1, cs] + st1_ref[:, cs] * cw[1:2, cs]
               + tap(FH + LAST_P, DEC_BATCH) * cw[2:3, cs])
        act_s = jax.nn.gelu(ucs) * gate[LAST_P:U_HI, cs]
        abuf[LAST_P:U_HI, cs] = jnp.where(i == N_RT - 1, act_s, act[LAST_P:U_HI]).astype(BF16)
        y_ref[...] += jnp.dot(abuf[:, cs], wd_ref[cs, :], preferred_element_type=F32)
    ust_ref[...] = ubuf[FH + U_LO:FH + U_HI]

    @pl.when(j == N_FT - 1)
    def _():
        @pl.when(i == N_RT - 1)
        def _():
            ys_ref[...] = _rms(x1s_ref[...] + y_ref[LAST_P:U_HI], gf_ref[...])

        y_ref[...] = _rms(x1_ref[...] + y_ref[...], gf_ref[...])


def _ffn(x1, x1s, st, gn, gf, wu, wg, cw, cb, wd):
    def halo_idx(i, j):
        return (jnp.where(i == 0, R_ALL // FH - 1, i * (TM // FH) - 1), 0)

    return pl.pallas_call(
        _ffn_kernel,
        grid=(N_RT, N_FT),
        in_specs=[
            pl.BlockSpec((TM, D_MODEL), lambda i, j: (i, 0)),
            pl.BlockSpec((FH, D_MODEL), halo_idx),
            pl.BlockSpec((DEC_BATCH, D_MODEL), lambda i, j: (0, 0)),
            pl.BlockSpec((DEC_BATCH, TF), lambda i, j: (0, j)),
            pl.BlockSpec((DEC_BATCH, TF), lambda i, j: (0, N_FT + j)),
            pl.BlockSpec((1, D_MODEL), lambda i, j: (0, 0)),
            pl.BlockSpec((1, D_MODEL), lambda i, j: (0, 0)),
            pl.BlockSpec((D_MODEL, TF), lambda i, j: (0, j)),
            pl.BlockSpec((D_MODEL, TF), lambda i, j: (0, j)),
            pl.BlockSpec((FFN_CONV, TF), lambda i, j: (0, j)),
            pl.BlockSpec((1, TF), lambda i, j: (0, j)),
            pl.BlockSpec((TF, D_MODEL), lambda i, j: (j, 0)),
        ],
        out_specs=[
            pl.BlockSpec((TM, D_MODEL), lambda i, j: (i, 0)),
            pl.BlockSpec((DEC_BATCH, D_MODEL), lambda i, j: (0, 0)),
            pl.BlockSpec((N_UST, TF), lambda i, j: (i, j)),
        ],
        out_shape=[
            jax.ShapeDtypeStruct((SEQ, D_MODEL), F32),
            jax.ShapeDtypeStruct((DEC_BATCH, D_MODEL), F32),
            jax.ShapeDtypeStruct((N_RT * N_UST, D_FF), F32),
        ],
        scratch_shapes=[
            pltpu.VMEM((FH + TM, D_MODEL), BF16),
            pltpu.VMEM((FH + TM, TF), F32),
            pltpu.VMEM((TM, TF), BF16),
        ],
        compiler_params=pltpu.CompilerParams(
            dimension_semantics=("arbitrary", "arbitrary"), vmem_limit_bytes=VMEM_LIMIT_BIG),
        name="ffn",
    )(x1, x1, x1s, st, st, gn, gf, wu, wg, cw, cb, wd)


def _gate_weights(wa, wx):
    per = GW // RNN_BLOCK
    eye = jnp.eye(per, dtype=wa.dtype)

    def bd(w):
        w = w.reshape(N_GW, per, RNN_BLOCK, RNN_BLOCK)
        return jnp.einsum('gpcd,pq->gpcqd', w, eye).reshape(N_GW, GW, GW)

    return jnp.concatenate([bd(wa), bd(wx)], axis=-1).astype(BF16)


def kernel(x_prompt, x_sample, state_rnn_conv, state_rnn_h, cache_k_win, cache_v_win, state_ffn_conv,
           meta_tokens, rel_bias_table, norm_mix, w_in, rnn_conv_w, rnn_conv_b, gate_a_w, gate_a_b,
           gate_x_w, gate_x_b, rnn_lambda, attn_sinks, norm_rnn_out, norm_attn_out, w_out, norm_ffn,
           w_up, w_gate, ffn_conv_w, ffn_conv_b, w_down, norm_final):
    l = 0
    xp = x_prompt[0]
    xs = x_sample[:, 0, :]
    xt = jnp.concatenate([xs, jnp.zeros((N_PAD, D_MODEL), F32), meta_tokens], axis=0)
    row = lambda v: v.reshape(1, -1)

    qi = np.arange(BLOCK)[:, None]
    sj = np.arange(2 * BLOCK)[None, :]
    bucket_p = jnp.asarray(_rel_buckets(BLOCK + qi - sj))
    pos = (np.arange(SB * WINDOW * N_KV) >> 1) & (WINDOW - 1)
    bucket_s = jnp.asarray(np.tile(_rel_buckets(WINDOW - pos)[None, :], (HALO, 1)))

    wg = _gate_weights(gate_a_w[l], gate_x_w[l])
    seq_w = (rnn_conv_w[l], row(rnn_conv_b[l]), wg, row(gate_a_b[l]), row(gate_x_b[l]), row(rnn_lambda[l]),
             row(norm_rnn_out[l]), row(norm_attn_out[l]))

    x1, zs, h_last, kv_last, xr_tail, w_out_b, w_up_b, w_gate_b, w_down_b = _front(
        rel_bias_table, attn_sinks[l], xp, xt, row(norm_mix[l]), w_in[l], w_out[l], bucket_p, *seq_w,
        w_up[l], w_gate[l], w_down[l])
    x1s, conv_s, h_s, k_s, v_s = _smixer(
        rel_bias_table, attn_sinks[l], zs,
        jnp.swapaxes(state_rnn_conv[l], 0, 1), state_rnn_h[l],
        cache_k_win[l].reshape(DEC_BATCH * WINDOW * N_KV, HEAD_DIM),
        cache_v_win[l].reshape(DEC_BATCH * WINDOW * N_KV, HEAD_DIM),
        xs, w_out_b, bucket_s, *seq_w)
    y_p, y_s, ust = _ffn(x1, x1s, state_ffn_conv[l].reshape(DEC_BATCH, (FFN_CONV - 1) * D_FF),
                         row(norm_ffn[l]), row(norm_final), w_up_b, w_gate_b,
                         ffn_conv_w[l], row(ffn_conv_b[l]), w_down_b)

    ust = ust[(N_RT - 1) * N_UST:]
    p_states = (
        xr_tail[HALO - (RNN_CONV - 1):HALO][None, None],
        h_last[0:1][None],
        kv_last[:, :D_KV].reshape(1, 1, WINDOW, N_KV, HEAD_DIM),
        kv_last[:, D_KV:].reshape(1, 1, WINDOW, N_KV, HEAD_DIM),
        ust[HALO - (FFN_CONV - 1):HALO][None, None],
    )
    s_states = (
        jnp.swapaxes(conv_s, 0, 1)[None],
        h_s[None],
        k_s.reshape(1, DEC_BATCH, WINDOW, N_KV, HEAD_DIM),
        v_s.reshape(1, DEC_BATCH, WINDOW, N_KV, HEAD_DIM),
        jnp.stack([state_ffn_conv[l][:, FFN_CONV - 2, :], ust[HALO:]], axis=1)[None],
    )
    return (y_p[None], y_s[:, None, :]) + p_states + s_states
```

```python
import math

import numpy as np
import jax
import jax.numpy as jnp
from jax import lax
from jax.experimental import pallas as pl
from jax.experimental.pallas import tpu as pltpu

F32 = jnp.float32
BF16 = jnp.bfloat16

D_MODEL = 2048
SEQ = 8192
DEC_BATCH = 128
D_RNN = 1024
N_RNN_BLOCKS = 16
RNN_BLOCK = D_RNN // N_RNN_BLOCKS
RNN_CONV = 4
LRU_C = 8.0
N_HEADS = 8
HEAD_DIM = 128
N_KV = 2
GROUP = N_HEADS // N_KV
D_ATTN = N_HEADS * HEAD_DIM
WINDOW = 128
BLOCK = 128
NUM_BUCKETS = 32
MAX_DISTANCE = 128
D_FF = 3 * D_MODEL
FFN_CONV = 3
N_META = 16
EPS = 1e-6
NEG = -1e30
D_KV = N_KV * HEAD_DIM
D_IN = 2 * D_RNN + D_ATTN + 2 * D_KV
SCALE = HEAD_DIM ** -0.5
O_GR = D_RNN
O_Q = 2 * D_RNN
O_KV = 2 * D_RNN + D_ATTN

N_PAD = BLOCK - N_META
R_TAIL = DEC_BATCH + BLOCK
R_ALL = SEQ + R_TAIL
N_PBLK = SEQ // BLOCK

FB = 2 * BLOCK
N_FSTEP = (N_PBLK + 2) // 2
PC = 512
N_IN_EARLY = 5
N_STAGE = 4
N_SLAB = N_FSTEP - 1

TM = 640
N_RT = (SEQ + DEC_BATCH) // TM
LAST_P = SEQ - (N_RT - 1) * TM
TF = 768
N_FT = D_FF // TF
KC = 256
HALO = 8
FH = 16
GW = 256
N_GW = D_RNN // GW
SB = 16
VMEM_LIMIT = 56 * 1024 * 1024
VMEM_LIMIT_BIG = 60 * 1024 * 1024


def _rms(x, g):
    return x * lax.rsqrt(jnp.mean(x * x, axis=-1, keepdims=True) + EPS) * g


def _rel_buckets(d):
    d = np.maximum(d, 0)
    exact = NUM_BUCKETS // 2
    ratio = np.maximum(d, 1).astype(np.float32) / np.float32(exact)
    large = exact + (np.log(ratio) / np.float32(math.log(MAX_DISTANCE / exact))
                     * np.float32(NUM_BUCKETS - exact)).astype(np.int32)
    large = np.minimum(large, NUM_BUCKETS - 1)
    return np.where(d < exact, d, large).astype(np.int32)


def _table_lookup(bucket, tab_ref):
    outs = [jnp.zeros(bucket.shape, F32) for _ in range(N_HEADS)]
    for b in range(NUM_BUCKETS):
        hit = bucket == b
        outs = [jnp.where(hit, tab_ref[b, h], o) for h, o in enumerate(outs)]
    return outs


def _gates(xc, wg_ref, gab, gxb, lam):
    xcb = xc.astype(BF16)
    ga, gx = [], []
    for j in range(N_GW):
        gj = jnp.dot(xcb[:, GW * j:GW * (j + 1)], wg_ref[j], preferred_element_type=F32)
        ga.append(gj[:, :GW])
        gx.append(gj[:, GW:])
    r = jax.nn.sigmoid(jnp.concatenate(ga, axis=1) + gab)
    i = jax.nn.sigmoid(jnp.concatenate(gx, axis=1) + gxb)
    log_a = -LRU_C * r * jax.nn.softplus(-lam)
    a = jnp.exp(log_a)
    t = 1.0 - a * a
    b = jnp.where(t > 0.0, t * lax.rsqrt(t), 0.0) * i * xc
    return a, b


def _front_kernel(tab_ref, sink_ref, xa_ref, xb_ref, xt_ref, gm_ref, win_f, wout_f, bucket_ref,
                  cw_ref, cb_ref, wg_ref, gab_ref, gxb_ref, lam_ref, nr_ref, na_ref, wu_f, wgt_f, wd_f, kc_hbm, vc_hbm,
                  x1_ref, zs_ref, hlast_ref, kvlast_ref, xrt_ref, wout_hbm, wu_b, wgt_b, wd_b, ko_hbm, vo_hbm,
                  win_ref, wout_ref, zp, zb_o, lhs_p, xsp, mixp, xbuf, h_scr, kvbuf, bias_scr,
                  su, sg, sd, tu, tg, td, sem_in, sem_out, sem_w, sem_c):
    k = pl.program_id(0)
    last = N_FSTEP - 1

    n_cache = kc_hbm.shape[0]
    slides = []
    for n, (src, dst) in enumerate(((kc_hbm, ko_hbm), (vc_hbm, vo_hbm))):
        slides.append(pltpu.make_async_copy(src.at[pl.ds(N_KV, n_cache - N_KV)], dst.at[pl.ds(0, n_cache - N_KV)],
                                            sem_c.at[2 * n]))
        slides.append(pltpu.make_async_copy(src.at[pl.ds(0, N_KV)], dst.at[pl.ds(n_cache - N_KV, N_KV)],
                                            sem_c.at[2 * n + 1]))

    @pl.when(k == 0)
    def _():
        for cp in slides:
            cp.start()

    stages = ((wu_f, su, tu, wu_b), (wgt_f, sg, tg, wgt_b), (wd_f, sd, td, wd_b))

    def slab_in(s):
        return [pltpu.make_async_copy(w.at[pl.ds(s * st.shape[0], st.shape[0])], st, sem_in.at[n])
                for n, (w, st, _, _) in enumerate(stages)]

    def slab_out(s):
        return [pltpu.make_async_copy(t, w.at[pl.ds(s * t.shape[0], t.shape[0])], sem_out.at[n])
                for n, (_, _, t, w) in enumerate(stages)]

    @pl.when(k == 0)
    def _():
        for c in slab_in(0):
            c.start()

    @pl.when(k >= 1)
    def _():
        for c in slab_out(k - 1):
            c.wait()

    @pl.when(k < N_SLAB)
    def _():
        for c in slab_in(k):
            c.wait()
        for _, st, t, _ in stages:
            t[...] = st[...].astype(BF16)
        for c in slab_out(k):
            c.start()

    @pl.when(k < N_SLAB - 1)
    def _():
        for c in slab_in(k + 1):
            c.start()

    wout_copy = pltpu.make_async_copy(wout_ref, wout_hbm, sem_w.at[2 * N_STAGE])

    @pl.when(k == 0)
    def _():
        streams = ((win_f, win_ref, zp, 0), (wout_f, wout_ref, xsp, N_STAGE))
        rows = FB // N_STAGE
        n = D_MODEL // rows

        def chunk(stream, c):
            w_f, _, buf, sem0 = stream
            slot = c % N_STAGE
            return pltpu.make_async_copy(w_f.at[pl.ds(c * rows, rows)], buf.at[pl.ds(slot * rows, rows)],
                                         sem_w.at[sem0 + slot])

        for c in range(N_STAGE - 1):
            for stream in streams:
                chunk(stream, c).start()
        for c in range(n):
            for stream in streams:
                if c + N_STAGE - 1 < n:
                    chunk(stream, c + N_STAGE - 1).start()
                chunk(stream, c).wait()
                _, w_b, buf, _ = stream
                w_b[c * rows:(c + 1) * rows] = buf[pl.ds((c % N_STAGE) * rows, rows)].astype(BF16)
        wout_copy.start()

    def inproj_chunk(c):
        def run():
            cs = slice(c * PC, (c + 1) * PC)
            zp[:, cs] = jnp.dot(lhs_p[...], win_ref[:, cs], preferred_element_type=F32)
        return run

    def outproj_chunk(c):
        def run():
            cs = slice(c * PC, (c + 1) * PC)
            x1_ref[:, cs] = xsp[:, cs] + jnp.dot(mixp[...], wout_ref[:, cs], preferred_element_type=F32)
        return run

    @pl.when(k == 0)
    def _():
        prefix = xt_ref[DEC_BATCH:R_TAIL]
        zb_o[...] = jnp.dot(_rms(prefix, gm_ref[...]).astype(BF16), win_ref[...], preferred_element_type=F32)
        xsp[0:BLOCK] = jnp.zeros((BLOCK, D_MODEL), F32)
        xsp[BLOCK:FB] = prefix
        mixp[...] = jnp.zeros((FB, D_MODEL), BF16)
        xbuf[0:HALO] = jnp.zeros((HALO, D_RNN), F32)
        h_scr[...] = jnp.zeros((HALO, D_RNN), F32)
        kvbuf[...] = jnp.zeros((BLOCK, 2 * D_KV), F32)
        for h, bias in enumerate(_table_lookup(bucket_ref[...], tab_ref)):
            bias_scr[h] = bias

    def mixer(zb, is_prefix, first_key, fillers):
        fill = iter(fillers)

        def between():
            f = next(fill, None)
            if f is not None:
                f()

        between()
        xr = zb[:, 0:D_RNN]
        xbuf[HALO:HALO + BLOCK] = xr
        cw = cw_ref[...]
        xc = cb_ref[...]
        for j in range(RNN_CONV - 1):
            lo = HALO - (RNN_CONV - 1) + j
            xc = xc + xbuf[lo:lo + BLOCK] * cw[j:j + 1]
        xc = xc + xr * cw[RNN_CONV - 1:RNN_CONV]
        xbuf[0:HALO] = xr[BLOCK - HALO:BLOCK]

        a, b = _gates(xc, wg_ref, gab_ref[...], gxb_ref[...], lam_ref[...])
        between()
        h = h_scr[0:1]
        if is_prefix is not None:
            row = lax.broadcasted_iota(jnp.int32, (BLOCK, D_RNN), 0)
            b = jnp.where(jnp.logical_and(is_prefix, row < N_PAD), 0.0, b)
            h = jnp.where(is_prefix, 0.0, h)

        ng = BLOCK // HALO
        a3 = a.reshape(ng, HALO, D_RNN)
        b3 = b.reshape(ng, HALO, D_RNN)
        sub = lax.broadcasted_iota(jnp.int32, (ng, HALO, D_RNN), 1)
        sh = 1
        while sh < HALO:
            a_prev = pltpu.roll(a3, sh, 1)
            b_prev = pltpu.roll(b3, sh, 1)
            m = sub >= sh
            b3 = jnp.where(m, a3 * b_prev + b3, b3)
            a3 = jnp.where(m, a3 * a_prev, a3)
            sh *= 2
        hs = []
        for g in range(ng):
            hg = a3[g] * h + b3[g]
            hs.append(hg)
            h = hg[HALO - 1:HALO]
        h_all = jnp.concatenate(hs, axis=0)
        h_scr[...] = jnp.broadcast_to(h, (HALO, D_RNN))
        between()
        y_rnn = h_all * jax.nn.gelu(zb[:, O_GR:O_GR + D_RNN])
        mix_a = _rms(y_rnn, nr_ref[...]).astype(BF16)

        q = zb[:, O_Q:O_Q + D_ATTN]
        kv = zb[:, O_KV:O_KV + 2 * D_KV]
        kvp = kvbuf[...]
        kvbuf[...] = kv
        col = lax.broadcasted_iota(jnp.int32, (BLOCK, 2 * BLOCK), 1)
        rowq = lax.broadcasted_iota(jnp.int32, (BLOCK, 2 * BLOCK), 0)
        d = BLOCK + rowq - col
        mask = (d >= 0) & (d < WINDOW) & (col >= first_key)
        outs = []
        for kh in range(N_KV):
            between()
            ks = slice(kh * HEAD_DIM, (kh + 1) * HEAD_DIM)
            vs = slice(D_KV + kh * HEAD_DIM, D_KV + (kh + 1) * HEAD_DIM)
            qs = jnp.concatenate(
                [q[:, (kh * GROUP + g) * HEAD_DIM:(kh * GROUP + g + 1) * HEAD_DIM] for g in range(GROUP)],
                axis=0).astype(BF16)
            kk = jnp.concatenate([kvp[:, ks], kv[:, ks]], axis=0).astype(BF16)
            vv = jnp.concatenate([kvp[:, vs], kv[:, vs]], axis=0).astype(BF16)
            sc = lax.dot_general(qs, kk, (((1,), (1,)), ((), ())), preferred_element_type=F32)
            es, dens = [], []
            for g in range(GROUP):
                hh = kh * GROUP + g
                lg = sc[g * BLOCK:(g + 1) * BLOCK] * SCALE + bias_scr[hh]
                lg = jnp.where(mask, lg, NEG)
                sink = sink_ref[hh]
                mx = jnp.maximum(jnp.max(lg, axis=-1, keepdims=True), sink)
                e = jnp.exp(lg - mx)
                dens.append(jnp.sum(e, axis=-1, keepdims=True) + jnp.exp(sink - mx))
                es.append(e.astype(BF16))
            pv = jnp.dot(jnp.concatenate(es, axis=0), vv, preferred_element_type=F32)
            for g in range(GROUP):
                outs.append(pv[g * BLOCK:(g + 1) * BLOCK] / dens[g])
        between()
        mix_b = _rms(jnp.concatenate(outs, axis=1), na_ref[...]).astype(BF16)

        return jnp.concatenate([mix_a, mix_b], axis=1), h, kv, xr

    xa = jnp.where(k == last, xt_ref[0:DEC_BATCH], xa_ref[...])
    xb = xb_ref[...]
    lhs_p[0:BLOCK] = _rms(xa, gm_ref[...]).astype(BF16)
    lhs_p[BLOCK:FB] = _rms(xb, gm_ref[...]).astype(BF16)
    n_in, n_out = D_IN // PC, D_MODEL // PC
    mix_o, h, kv, xr = mixer(zb_o, k == 0, jnp.where(k == 0, BLOCK + N_PAD, 0),
                             [inproj_chunk(c) for c in range(N_IN_EARLY)])
    mixp[BLOCK:FB] = mix_o
    hlast_ref[...] = jnp.broadcast_to(h, (HALO, D_RNN))
    kvlast_ref[...] = kv
    xrt_ref[...] = xr[BLOCK - HALO:BLOCK]

    mix_e = mixer(zp.at[0:BLOCK], None, jnp.where(k == 0, N_PAD, 0),
                  [inproj_chunk(c) for c in range(N_IN_EARLY, n_in)] + [outproj_chunk(c) for c in range(n_out)])[0]
    mixp[0:BLOCK] = mix_e
    xsp[0:BLOCK] = xa
    xsp[BLOCK:FB] = xb
    zb_o[...] = zp[BLOCK:FB]

    @pl.when(k == last)
    def _():
        zs_ref[...] = zp[0:BLOCK]
        wout_copy.wait()
        for cp in slides:
            cp.wait()


def _front(tab, sinks, xp, xt, gm, w_in, w_out, bucket, cw, cb, wg, gab, gxb, lam, nr, na, w_up, w_gate, w_down,
           kc, vc):
    ffn_w = (w_up, w_gate, w_down)
    slab = lambda w: (w.shape[0] // N_SLAB, w.shape[1])
    hbm = pl.BlockSpec(memory_space=pl.ANY)
    vec = lambda n: pl.BlockSpec((1, n), lambda k: (0, 0))
    smem = pl.BlockSpec(memory_space=pltpu.SMEM)
    once = lambda shape: pl.BlockSpec(shape, lambda k: (0,) * len(shape), pipeline_mode=pl.Buffered(1))
    return pl.pallas_call(
        _front_kernel,
        grid=(N_FSTEP,),
        in_specs=[
            smem, smem,
            pl.BlockSpec((BLOCK, D_MODEL), lambda k: (jnp.minimum(2 * k, N_PBLK - 1), 0)),
            pl.BlockSpec((BLOCK, D_MODEL), lambda k: (jnp.minimum(2 * k + 1, N_PBLK - 1), 0)),
            once((R_TAIL, D_MODEL)),
            vec(D_MODEL),
            hbm, hbm,
            pl.BlockSpec((BLOCK, 2 * BLOCK), lambda k: (0, 0)),
            pl.BlockSpec((RNN_CONV, D_RNN), lambda k: (0, 0)),
            vec(D_RNN),
            pl.BlockSpec((N_GW, GW, 2 * GW), lambda k: (0, 0, 0)),
            vec(D_RNN), vec(D_RNN), vec(D_RNN), vec(D_RNN), vec(D_ATTN),
            hbm, hbm, hbm, hbm, hbm,
        ],
        out_specs=[
            pl.BlockSpec((FB, D_MODEL), lambda k: (jnp.where(k == 0, SEQ // FB, k - 1), 0)),
            pl.BlockSpec((BLOCK, D_IN), lambda k: (0, 0)),
            pl.BlockSpec((HALO, D_RNN), lambda k: (0, 0)),
            pl.BlockSpec((BLOCK, 2 * D_KV), lambda k: (0, 0)),
            pl.BlockSpec((HALO, D_RNN), lambda k: (0, 0)),
            hbm, hbm, hbm, hbm, hbm, hbm,
        ],
        out_shape=[
            jax.ShapeDtypeStruct((R_ALL, D_MODEL), F32),
            jax.ShapeDtypeStruct((DEC_BATCH, D_IN), F32),
            jax.ShapeDtypeStruct((HALO, D_RNN), F32),
            jax.ShapeDtypeStruct((BLOCK, 2 * D_KV), F32),
            jax.ShapeDtypeStruct((HALO, D_RNN), F32),
        ] + [jax.ShapeDtypeStruct(w.shape, BF16) for w in (w_out,) + ffn_w] + [
            jax.ShapeDtypeStruct(kc.shape, F32), jax.ShapeDtypeStruct(vc.shape, F32)],
        scratch_shapes=[
            pltpu.VMEM(w_in.shape, BF16),
            pltpu.VMEM(w_out.shape, BF16),
            pltpu.VMEM((FB, D_IN), F32),
            pltpu.VMEM((BLOCK, D_IN), F32),
            pltpu.VMEM((FB, D_MODEL), BF16),
            pltpu.VMEM((FB, D_MODEL), F32),
            pltpu.VMEM((FB, D_MODEL), BF16),
            pltpu.VMEM((HALO + BLOCK, D_RNN), F32),
            pltpu.VMEM((HALO, D_RNN), F32),
            pltpu.VMEM((BLOCK, 2 * D_KV), F32),
            pltpu.VMEM((N_HEADS, BLOCK, 2 * BLOCK), F32),
        ] + [pltpu.VMEM(slab(w), F32) for w in ffn_w] + [pltpu.VMEM(slab(w), BF16) for w in ffn_w] + [
            pltpu.SemaphoreType.DMA((len(ffn_w),)),
            pltpu.SemaphoreType.DMA((len(ffn_w),)),
            pltpu.SemaphoreType.DMA((2 * N_STAGE + 1,)),
            pltpu.SemaphoreType.DMA((4,)),
        ],
        compiler_params=pltpu.CompilerParams(
            dimension_semantics=("arbitrary",), vmem_limit_bytes=VMEM_LIMIT_BIG),
        name="front",
    )(tab, sinks, xp, xp, xt, gm, w_in, w_out, bucket, cw, cb, wg, gab, gxb, lam, nr, na, *ffn_w, kc, vc)


def _smixer_kernel(tab_ref, sink_ref, z_ref, st_ref, h0_ref, kc_ref, vc_ref, xs_ref, wout_ref,
                   bucket_ref, cw_ref, cb_ref, wg_ref, gab_ref, gxb_ref, lam_ref, nr_ref, na_ref, ko_in, vo_in,
                   x1s_ref, conv_ref, h_ref, ko_hbm, vo_hbm, bias_scr, mix_scr, new_k, new_v, sem_n):
    del ko_in, vo_in
    c = pl.program_id(0)
    nrow = N_HEADS * SB
    ncol = SB * WINDOW * N_KV

    @pl.when(c == 0)
    def _():
        col = lax.broadcasted_iota(jnp.int32, (SB, ncol), 1)
        row = lax.broadcasted_iota(jnp.int32, (SB, ncol), 0)
        own = ((col >> 8) == row) & (((col >> 1) & (WINDOW - 1)) >= 1)
        for h, bias in enumerate(_table_lookup(bucket_ref[...], tab_ref)):
            ok = own & ((col & 1) == h // GROUP)
            bias_scr[h * SB:(h + 1) * SB] = jnp.where(ok, jnp.concatenate([bias] * (SB // HALO), axis=0), NEG)

    r0 = pl.multiple_of(c * SB, SB)

    xr = z_ref[:, 0:D_RNN]
    cw = cw_ref[...]
    xc = cb_ref[...]
    for j in range(RNN_CONV - 1):
        xc = xc + st_ref[j] * cw[j:j + 1]
    xc = xc + xr * cw[RNN_CONV - 1:RNN_CONV]
    for j in range(1, RNN_CONV - 1):
        conv_ref[j - 1] = st_ref[j]
    conv_ref[RNN_CONV - 2] = xr
    a, b = _gates(xc, wg_ref, gab_ref[...], gxb_ref[...], lam_ref[...])
    h = a * h0_ref[...] + b
    h_ref[...] = h
    y_rnn = h * jax.nn.gelu(z_ref[:, O_GR:O_GR + D_RNN])
    mix_scr[pl.ds(r0, SB), 0:D_RNN] = _rms(y_rnn, nr_ref[...]).astype(BF16)

    q = z_ref[:, O_Q:O_Q + D_ATTN]
    kv = z_ref[:, O_KV:O_KV + 2 * D_KV]
    qs = jnp.concatenate([q[:, h * HEAD_DIM:(h + 1) * HEAD_DIM] for h in range(N_HEADS)], axis=0).astype(BF16)
    new_rows = lambda off: jnp.concatenate(
        [kv[:, off + (h // GROUP) * HEAD_DIM:off + (h // GROUP + 1) * HEAD_DIM] for h in range(N_HEADS)],
        axis=0).astype(BF16).astype(F32)
    k_new = new_rows(0)
    v_new = new_rows(D_KV)
    sc = lax.dot_general(qs, kc_ref[...].astype(BF16), (((1,), (1,)), ((), ())), preferred_element_type=F32)
    lg = sc * SCALE + bias_scr[...]
    rh = lax.broadcasted_iota(jnp.int32, (nrow, 1), 0) >> 4
    sink = jnp.zeros((nrow, 1), F32)
    bias_new = jnp.zeros((nrow, 1), F32)
    for h in range(N_HEADS):
        sink = jnp.where(rh == h, sink_ref[h], sink)
        bias_new = jnp.where(rh == h, tab_ref[0, h], bias_new)
    lg_new = jnp.sum(qs.astype(F32) * k_new, axis=-1, keepdims=True) * SCALE + bias_new
    mx = jnp.maximum(jnp.maximum(jnp.max(lg, axis=-1, keepdims=True), lg_new), sink)
    e = jnp.exp(lg - mx)
    e_new = jnp.exp(lg_new - mx)
    den = jnp.sum(e, axis=-1, keepdims=True) + e_new + jnp.exp(sink - mx)
    pv = jnp.dot(e.astype(BF16), vc_ref[...].astype(BF16), preferred_element_type=F32)
    o = (pv + e_new.astype(BF16).astype(F32) * v_new) / den
    y_attn = jnp.concatenate([o[h * SB:(h + 1) * SB] for h in range(N_HEADS)], axis=1)
    mix_scr[pl.ds(r0, SB), D_RNN:] = _rms(y_attn, na_ref[...]).astype(BF16)

    per = WINDOW * N_KV
    tail = lambda dst: dst.at[pl.ds(r0, SB), pl.ds(per - N_KV, N_KV), :]
    puts = (pltpu.make_async_copy(new_k, tail(ko_hbm), sem_n.at[0]),
            pltpu.make_async_copy(new_v, tail(vo_hbm), sem_n.at[1]))

    @pl.when(c > 0)
    def _():
        for cp in puts:
            cp.wait()

    for bi in range(SB):
        for kh in range(N_KV):
            new_k[bi, kh:kh + 1, :] = kv[bi:bi + 1, kh * HEAD_DIM:(kh + 1) * HEAD_DIM]
            new_v[bi, kh:kh + 1, :] = kv[bi:bi + 1, D_KV + kh * HEAD_DIM:D_KV + (kh + 1) * HEAD_DIM]
    for cp in puts:
        cp.start()

    @pl.when(c == DEC_BATCH // SB - 1)
    def _():
        x1s_ref[...] = xs_ref[...] + jnp.dot(mix_scr[...], wout_ref[...], preferred_element_type=F32)
        for cp in puts:
            cp.wait()


def _smixer(tab, sinks, zs, st, h0, kc, vc, xs, w_out, bucket, cw, cb, wg, gab, gxb, lam, nr, na, ko, vo):
    hbm = pl.BlockSpec(memory_space=pl.ANY)
    vec = lambda n: pl.BlockSpec((1, n), lambda c: (0, 0))
    smem = pl.BlockSpec(memory_space=pltpu.SMEM)
    cache = pl.BlockSpec((SB * WINDOW * N_KV, HEAD_DIM), lambda c: (c, 0))
    return pl.pallas_call(
        _smixer_kernel,
        grid=(DEC_BATCH // SB,),
        in_specs=[
            smem, smem,
            pl.BlockSpec((SB, D_IN), lambda c: (c, 0)),
            pl.BlockSpec((RNN_CONV - 1, SB, D_RNN), lambda c: (0, c, 0)),
            pl.BlockSpec((SB, D_RNN), lambda c: (c, 0)),
            cache, cache,
            pl.BlockSpec((DEC_BATCH, D_MODEL), lambda c: (0, 0)),
            pl.BlockSpec((D_MODEL, D_MODEL), lambda c: (0, 0), pipeline_mode=pl.Buffered(1)),
            pl.BlockSpec((HALO, SB * WINDOW * N_KV), lambda c: (0, 0)),
            pl.BlockSpec((RNN_CONV, D_RNN), lambda c: (0, 0)),
            vec(D_RNN),
            pl.BlockSpec((N_GW, GW, 2 * GW), lambda c: (0, 0, 0)),
            vec(D_RNN), vec(D_RNN), vec(D_RNN), vec(D_RNN), vec(D_ATTN),
            hbm, hbm,
        ],
        out_specs=[
            pl.BlockSpec((DEC_BATCH, D_MODEL), lambda c: (0, 0)),
            pl.BlockSpec((RNN_CONV - 1, SB, D_RNN), lambda c: (0, c, 0)),
            pl.BlockSpec((SB, D_RNN), lambda c: (c, 0)),
            hbm, hbm,
        ],
        out_shape=[
            jax.ShapeDtypeStruct((DEC_BATCH, D_MODEL), F32),
            jax.ShapeDtypeStruct((RNN_CONV - 1, DEC_BATCH, D_RNN), F32),
            jax.ShapeDtypeStruct((DEC_BATCH, D_RNN), F32),
            jax.ShapeDtypeStruct(ko.shape, F32),
            jax.ShapeDtypeStruct(vo.shape, F32),
        ],
        input_output_aliases={18: 3, 19: 4},
        scratch_shapes=[
            pltpu.VMEM((N_HEADS * SB, SB * WINDOW * N_KV), F32),
            pltpu.VMEM((DEC_BATCH, D_MODEL), BF16),
            pltpu.VMEM((SB, N_KV, HEAD_DIM), F32),
            pltpu.VMEM((SB, N_KV, HEAD_DIM), F32),
            pltpu.SemaphoreType.DMA((2,)),
        ],
        compiler_params=pltpu.CompilerParams(
            dimension_semantics=("arbitrary",), vmem_limit_bytes=VMEM_LIMIT),
        name="smixer",
    )(tab, sinks, zs, st, h0, kc, vc, xs, w_out, bucket, cw, cb, wg, gab, gxb, lam, nr, na, ko, vo)


U_LO = LAST_P - HALO
U_HI = LAST_P + DEC_BATCH
N_UST = U_HI - U_LO


def _ffn_kernel(x1_ref, halo_ref, x1s_ref, st0_ref, st1_ref, gn_ref, gf_ref, wu_ref, wg_ref, cw_ref, cb_ref,
                wd_ref, y_ref, ys_ref, ust_ref, h2_scr, ubuf, abuf):
    i = pl.program_id(0)
    j = pl.program_id(1)

    @pl.when(j == 0)
    def _():
        h2_scr[0:FH] = _rms(halo_ref[...], gn_ref[...]).astype(BF16)
        h2_scr[FH:FH + TM] = _rms(x1_ref[...], gn_ref[...]).astype(BF16)
        y_ref[...] = jnp.zeros((TM, D_MODEL), F32)

        @pl.when(i == N_RT - 1)
        def _():
            h2_scr[FH + LAST_P:FH + U_HI] = _rms(x1s_ref[...], gn_ref[...]).astype(BF16)

    ubuf[...] = jnp.dot(h2_scr[...], wu_ref[...], preferred_element_type=F32)
    gate = jnp.dot(h2_scr[FH:FH + TM], wg_ref[...], preferred_element_type=F32)
    cw = cw_ref[...]
    cb = cb_ref[...]
    for c in range(TF // KC):
        cs = slice(c * KC, (c + 1) * KC)
        tap = lambda r0, n: ubuf[r0:r0 + n, cs]
        uc = (cb[:, cs] + tap(FH - 2, TM) * cw[0:1, cs] + tap(FH - 1, TM) * cw[1:2, cs] + tap(FH, TM) * cw[2:3, cs])
        act = jax.nn.gelu(uc) * gate[:, cs]
        abuf[:, cs] = act.astype(BF16)
        ucs = (cb[:, cs] + st0_ref[:, cs] * cw[0:1, cs] + st1_ref[:, cs] * cw[1:2, cs]
               + tap(FH + LAST_P, DEC_BATCH) * cw[2:3, cs])
        act_s = jax.nn.gelu(ucs) * gate[LAST_P:U_HI, cs]
        abuf[LAST_P:U_HI, cs] = jnp.where(i == N_RT - 1, act_s, act[LAST_P:U_HI]).astype(BF16)
        y_ref[...] += jnp.dot(abuf[:, cs], wd_ref[cs, :], preferred_element_type=F32)
    ust_ref[...] = ubuf[FH + U_LO:FH + U_HI]

    @pl.when(j == N_FT - 1)
    def _():
        @pl.when(i == N_RT - 1)
        def _():
            ys_ref[...] = _rms(x1s_ref[...] + y_ref[LAST_P:U_HI], gf_ref[...])

        y_ref[...] = _rms(x1_ref[...] + y_ref[...], gf_ref[...])


def _ffn(x1, x1s, st, gn, gf, wu, wg, cw, cb, wd):
    def halo_idx(i, j):
        return (jnp.where(i == 0, R_ALL // FH - 1, i * (TM // FH) - 1), 0)

    return pl.pallas_call(
        _ffn_kernel,
        grid=(N_RT, N_FT),
        in_specs=[
            pl.BlockSpec((TM, D_MODEL), lambda i, j: (i, 0)),
            pl.BlockSpec((FH, D_MODEL), halo_idx),
            pl.BlockSpec((DEC_BATCH, D_MODEL), lambda i, j: (0, 0)),
            pl.BlockSpec((DEC_BATCH, TF), lambda i, j: (0, j)),
            pl.BlockSpec((DEC_BATCH, TF), lambda i, j: (0, N_FT + j)),
            pl.BlockSpec((1, D_MODEL), lambda i, j: (0, 0)),
            pl.BlockSpec((1, D_MODEL), lambda i, j: (0, 0)),
            pl.BlockSpec((D_MODEL, TF), lambda i, j: (0, j)),
            pl.BlockSpec((D_MODEL, TF), lambda i, j: (0, j)),
            pl.BlockSpec((FFN_CONV, TF), lambda i, j: (0, j)),
            pl.BlockSpec((1, TF), lambda i, j: (0, j)),
            pl.BlockSpec((TF, D_MODEL), lambda i, j: (j, 0)),
        ],
        out_specs=[
            pl.BlockSpec((TM, D_MODEL), lambda i, j: (i, 0)),
            pl.BlockSpec((DEC_BATCH, D_MODEL), lambda i, j: (0, 0)),
            pl.BlockSpec((N_UST, TF), lambda i, j: (i, j)),
        ],
        out_shape=[
            jax.ShapeDtypeStruct((SEQ, D_MODEL), F32),
            jax.ShapeDtypeStruct((DEC_BATCH, D_MODEL), F32),
            jax.ShapeDtypeStruct((N_RT * N_UST, D_FF), F32),
        ],
        scratch_shapes=[
            pltpu.VMEM((FH + TM, D_MODEL), BF16),
            pltpu.VMEM((FH + TM, TF), F32),
            pltpu.VMEM((TM, TF), BF16),
        ],
        compiler_params=pltpu.CompilerParams(
            dimension_semantics=("arbitrary", "arbitrary"), vmem_limit_bytes=VMEM_LIMIT_BIG),
        name="ffn",
    )(x1, x1, x1s, st, st, gn, gf, wu, wg, cw, cb, wd)


def _gate_weights(wa, wx):
    per = GW // RNN_BLOCK
    eye = jnp.eye(per, dtype=wa.dtype)

    def bd(w):
        w = w.reshape(N_GW, per, RNN_BLOCK, RNN_BLOCK)
        return jnp.einsum('gpcd,pq->gpcqd', w, eye).reshape(N_GW, GW, GW)

    return jnp.concatenate([bd(wa), bd(wx)], axis=-1).astype(BF16)


def kernel(x_prompt, x_sample, state_rnn_conv, state_rnn_h, cache_k_win, cache_v_win, state_ffn_conv,
           meta_tokens, rel_bias_table, norm_mix, w_in, rnn_conv_w, rnn_conv_b, gate_a_w, gate_a_b,
           gate_x_w, gate_x_b, rnn_lambda, attn_sinks, norm_rnn_out, norm_attn_out, w_out, norm_ffn,
           w_up, w_gate, ffn_conv_w, ffn_conv_b, w_down, norm_final):
    l = 0
    xp = x_prompt[0]
    xs = x_sample[:, 0, :]
    xt = jnp.concatenate([xs, jnp.zeros((N_PAD, D_MODEL), F32), meta_tokens], axis=0)
    row = lambda v: v.reshape(1, -1)

    qi = np.arange(BLOCK)[:, None]
    sj = np.arange(2 * BLOCK)[None, :]
    bucket_p = jnp.asarray(_rel_buckets(BLOCK + qi - sj))
    pos = (np.arange(SB * WINDOW * N_KV) >> 1) & (WINDOW - 1)
    bucket_s = jnp.asarray(np.tile(_rel_buckets(WINDOW - pos)[None, :], (HALO, 1)))

    wg = _gate_weights(gate_a_w[l], gate_x_w[l])
    seq_w = (rnn_conv_w[l], row(rnn_conv_b[l]), wg, row(gate_a_b[l]), row(gate_x_b[l]), row(rnn_lambda[l]),
             row(norm_rnn_out[l]), row(norm_attn_out[l]))

    kc = cache_k_win[l].reshape(DEC_BATCH * WINDOW * N_KV, HEAD_DIM)
    vc = cache_v_win[l].reshape(DEC_BATCH * WINDOW * N_KV, HEAD_DIM)
    x1, zs, h_last, kv_last, xr_tail, w_out_b, w_up_b, w_gate_b, w_down_b, k_slid, v_slid = _front(
        rel_bias_table, attn_sinks[l], xp, xt, row(norm_mix[l]), w_in[l], w_out[l], bucket_p, *seq_w,
        w_up[l], w_gate[l], w_down[l], kc, vc)
    per_row = (DEC_BATCH, WINDOW * N_KV, HEAD_DIM)
    x1s, conv_s, h_s, k_s, v_s = _smixer(
        rel_bias_table, attn_sinks[l], zs,
        jnp.swapaxes(state_rnn_conv[l], 0, 1), state_rnn_h[l],
        kc, vc, xs, w_out_b, bucket_s, *seq_w, k_slid.reshape(per_row), v_slid.reshape(per_row))
    y_p, y_s, ust = _ffn(x1, x1s, state_ffn_conv[l].reshape(DEC_BATCH, (FFN_CONV - 1) * D_FF),
                         row(norm_ffn[l]), row(norm_final), w_up_b, w_gate_b,
                         ffn_conv_w[l], row(ffn_conv_b[l]), w_down_b)

    ust = ust[(N_RT - 1) * N_UST:]
    p_states = (
        xr_tail[HALO - (RNN_CONV - 1):HALO][None, None],
        h_last[0:1][None],
        kv_last[:, :D_KV].reshape(1, 1, WINDOW, N_KV, HEAD_DIM),
        kv_last[:, D_KV:].reshape(1, 1, WINDOW, N_KV, HEAD_DIM),
        ust[HALO - (FFN_CONV - 1):HALO][None, None],
    )
    s_states = (
        jnp.swapaxes(conv_s, 0, 1)[None],
        h_s[None],
        k_s.reshape(1, DEC_BATCH, WINDOW, N_KV, HEAD_DIM),
        v_s.reshape(1, DEC_BATCH, WINDOW, N_KV, HEAD_DIM),
        jnp.stack([state_ffn_conv[l][:, FFN_CONV - 2, :], ust[HALO:]], axis=1)[None],
    )
    return (y_p[None], y_s[:, None, :]) + p_states + s_states
```

```python
import math

import numpy as np
import jax
import jax.numpy as jnp
from jax import lax
from jax.experimental import pallas as pl
from jax.experimental.pallas import tpu as pltpu

F32 = jnp.float32
BF16 = jnp.bfloat16

D_MODEL = 2048
SEQ = 8192
DEC_BATCH = 128
D_RNN = 1024
N_RNN_BLOCKS = 16
RNN_BLOCK = D_RNN // N_RNN_BLOCKS
RNN_CONV = 4
LRU_C = 8.0
N_HEADS = 8
HEAD_DIM = 128
N_KV = 2
GROUP = N_HEADS // N_KV
D_ATTN = N_HEADS * HEAD_DIM
WINDOW = 128
BLOCK = 128
NUM_BUCKETS = 32
MAX_DISTANCE = 128
D_FF = 3 * D_MODEL
FFN_CONV = 3
N_META = 16
EPS = 1e-6
NEG = -1e30
D_KV = N_KV * HEAD_DIM
D_IN = 2 * D_RNN + D_ATTN + 2 * D_KV
SCALE = HEAD_DIM ** -0.5
O_GR = D_RNN
O_Q = 2 * D_RNN
O_KV = 2 * D_RNN + D_ATTN

N_PAD = BLOCK - N_META
R_TAIL = DEC_BATCH + BLOCK
R_ALL = SEQ + R_TAIL
N_PBLK = SEQ // BLOCK

FB = 2 * BLOCK
N_FSTEP = (N_PBLK + 2) // 2
PC = 512
N_IN_EARLY = 5
N_STAGE = 4
N_SLAB = N_FSTEP - 1

TM = 640
N_RT = (SEQ + DEC_BATCH) // TM
LAST_P = SEQ - (N_RT - 1) * TM
TF = 768
N_FT = D_FF // TF
KC = 256
HALO = 8
FH = 16
GW = 256
N_GW = D_RNN // GW
SB = 16
VMEM_LIMIT = 56 * 1024 * 1024
VMEM_LIMIT_BIG = 60 * 1024 * 1024


def _rms(x, g):
    return x * lax.rsqrt(jnp.mean(x * x, axis=-1, keepdims=True) + EPS) * g


def _rel_buckets(d):
    d = np.maximum(d, 0)
    exact = NUM_BUCKETS // 2
    ratio = np.maximum(d, 1).astype(np.float32) / np.float32(exact)
    large = exact + (np.log(ratio) / np.float32(math.log(MAX_DISTANCE / exact))
                     * np.float32(NUM_BUCKETS - exact)).astype(np.int32)
    large = np.minimum(large, NUM_BUCKETS - 1)
    return np.where(d < exact, d, large).astype(np.int32)


def _table_lookup(bucket, tab_ref):
    outs = [jnp.zeros(bucket.shape, F32) for _ in range(N_HEADS)]
    for b in range(NUM_BUCKETS):
        hit = bucket == b
        outs = [jnp.where(hit, tab_ref[b, h], o) for h, o in enumerate(outs)]
    return outs


def _gates(xc, wg_ref, gab, gxb, lam):
    xcb = xc.astype(BF16)
    ga, gx = [], []
    for j in range(N_GW):
        gj = jnp.dot(xcb[:, GW * j:GW * (j + 1)], wg_ref[j], preferred_element_type=F32)
        ga.append(gj[:, :GW])
        gx.append(gj[:, GW:])
    r = jax.nn.sigmoid(jnp.concatenate(ga, axis=1) + gab)
    i = jax.nn.sigmoid(jnp.concatenate(gx, axis=1) + gxb)
    log_a = -LRU_C * r * jax.nn.softplus(-lam)
    a = jnp.exp(log_a)
    t = 1.0 - a * a
    b = jnp.where(t > 0.0, t * lax.rsqrt(t), 0.0) * i * xc
    return a, b


def _front_kernel(tab_ref, sink_ref, xa_ref, xb_ref, xt_ref, gm_ref, win_f, wout_f, bucket_ref,
                  cw_ref, cb_ref, wg_ref, gab_ref, gxb_ref, lam_ref, nr_ref, na_ref, wu_f, wgt_f, wd_f, kc_hbm, vc_hbm,
                  x1_ref, zs_ref, hlast_ref, kvlast_ref, xrt_ref, wout_hbm, wu_b, wgt_b, wd_b, ko_hbm, vo_hbm,
                  win_ref, wout_ref, zp, zb_o, lhs_p, xsp, mixp, xbuf, h_scr, kvbuf, bias_scr,
                  su, sg, sd, tu, tg, td, sem_in, sem_out, sem_w, sem_c):
    k = pl.program_id(0)
    last = N_FSTEP - 1

    slides = []
    for n, (src, dst) in enumerate(((kc_hbm, ko_hbm), (vc_hbm, vo_hbm))):
        slides.append(pltpu.make_async_copy(src.at[:, pl.ds(1, WINDOW - 1)], dst.at[:, pl.ds(0, WINDOW - 1)],
                                            sem_c.at[2 * n]))
        slides.append(pltpu.make_async_copy(src.at[:, pl.ds(0, 1)], dst.at[:, pl.ds(WINDOW - 1, 1)],
                                            sem_c.at[2 * n + 1]))

    @pl.when(k == 0)
    def _():
        for cp in slides:
            cp.start()

    stages = ((wu_f, su, tu, wu_b), (wgt_f, sg, tg, wgt_b), (wd_f, sd, td, wd_b))

    def slab_in(s):
        return [pltpu.make_async_copy(w.at[pl.ds(s * st.shape[0], st.shape[0])], st, sem_in.at[n])
                for n, (w, st, _, _) in enumerate(stages)]

    def slab_out(s):
        return [pltpu.make_async_copy(t, w.at[pl.ds(s * t.shape[0], t.shape[0])], sem_out.at[n])
                for n, (_, _, t, w) in enumerate(stages)]

    @pl.when(k == 0)
    def _():
        for c in slab_in(0):
            c.start()

    @pl.when(k >= 1)
    def _():
        for c in slab_out(k - 1):
            c.wait()

    @pl.when(k < N_SLAB)
    def _():
        for c in slab_in(k):
            c.wait()
        for _, st, t, _ in stages:
            t[...] = st[...].astype(BF16)
        for c in slab_out(k):
            c.start()

    @pl.when(k < N_SLAB - 1)
    def _():
        for c in slab_in(k + 1):
            c.start()

    wout_copy = pltpu.make_async_copy(wout_ref, wout_hbm, sem_w.at[2 * N_STAGE])

    @pl.when(k == 0)
    def _():
        streams = ((win_f, win_ref, zp, 0), (wout_f, wout_ref, xsp, N_STAGE))
        rows = FB // N_STAGE
        n = D_MODEL // rows

        def chunk(stream, c):
            w_f, _, buf, sem0 = stream
            slot = c % N_STAGE
            return pltpu.make_async_copy(w_f.at[pl.ds(c * rows, rows)], buf.at[pl.ds(slot * rows, rows)],
                                         sem_w.at[sem0 + slot])

        for c in range(N_STAGE - 1):
            for stream in streams:
                chunk(stream, c).start()
        for c in range(n):
            for stream in streams:
                if c + N_STAGE - 1 < n:
                    chunk(stream, c + N_STAGE - 1).start()
                chunk(stream, c).wait()
                _, w_b, buf, _ = stream
                w_b[c * rows:(c + 1) * rows] = buf[pl.ds((c % N_STAGE) * rows, rows)].astype(BF16)
        wout_copy.start()

    def inproj_chunk(c):
        def run():
            cs = slice(c * PC, (c + 1) * PC)
            zp[:, cs] = jnp.dot(lhs_p[...], win_ref[:, cs], preferred_element_type=F32)
        return run

    def outproj_chunk(c):
        def run():
            cs = slice(c * PC, (c + 1) * PC)
            x1_ref[:, cs] = xsp[:, cs] + jnp.dot(mixp[...], wout_ref[:, cs], preferred_element_type=F32)
        return run

    @pl.when(k == 0)
    def _():
        prefix = xt_ref[DEC_BATCH:R_TAIL]
        zb_o[...] = jnp.dot(_rms(prefix, gm_ref[...]).astype(BF16), win_ref[...], preferred_element_type=F32)
        xsp[0:BLOCK] = jnp.zeros((BLOCK, D_MODEL), F32)
        xsp[BLOCK:FB] = prefix
        mixp[...] = jnp.zeros((FB, D_MODEL), BF16)
        xbuf[0:HALO] = jnp.zeros((HALO, D_RNN), F32)
        h_scr[...] = jnp.zeros((HALO, D_RNN), F32)
        kvbuf[...] = jnp.zeros((BLOCK, 2 * D_KV), F32)
        for h, bias in enumerate(_table_lookup(bucket_ref[...], tab_ref)):
            bias_scr[h] = bias

    def mixer(zb, is_prefix, first_key, fillers):
        fill = iter(fillers)

        def between():
            f = next(fill, None)
            if f is not None:
                f()

        between()
        xr = zb[:, 0:D_RNN]
        xbuf[HALO:HALO + BLOCK] = xr
        cw = cw_ref[...]
        xc = cb_ref[...]
        for j in range(RNN_CONV - 1):
            lo = HALO - (RNN_CONV - 1) + j
            xc = xc + xbuf[lo:lo + BLOCK] * cw[j:j + 1]
        xc = xc + xr * cw[RNN_CONV - 1:RNN_CONV]
        xbuf[0:HALO] = xr[BLOCK - HALO:BLOCK]

        a, b = _gates(xc, wg_ref, gab_ref[...], gxb_ref[...], lam_ref[...])
        between()
        h = h_scr[0:1]
        if is_prefix is not None:
            row = lax.broadcasted_iota(jnp.int32, (BLOCK, D_RNN), 0)
            b = jnp.where(jnp.logical_and(is_prefix, row < N_PAD), 0.0, b)
            h = jnp.where(is_prefix, 0.0, h)

        ng = BLOCK // HALO
        a3 = a.reshape(ng, HALO, D_RNN)
        b3 = b.reshape(ng, HALO, D_RNN)
        sub = lax.broadcasted_iota(jnp.int32, (ng, HALO, D_RNN), 1)
        sh = 1
        while sh < HALO:
            a_prev = pltpu.roll(a3, sh, 1)
            b_prev = pltpu.roll(b3, sh, 1)
            m = sub >= sh
            b3 = jnp.where(m, a3 * b_prev + b3, b3)
            a3 = jnp.where(m, a3 * a_prev, a3)
            sh *= 2
        hs = []
        for g in range(ng):
            hg = a3[g] * h + b3[g]
            hs.append(hg)
            h = hg[HALO - 1:HALO]
        h_all = jnp.concatenate(hs, axis=0)
        h_scr[...] = jnp.broadcast_to(h, (HALO, D_RNN))
        between()
        y_rnn = h_all * jax.nn.gelu(zb[:, O_GR:O_GR + D_RNN])
        mix_a = _rms(y_rnn, nr_ref[...]).astype(BF16)

        q = zb[:, O_Q:O_Q + D_ATTN]
        kv = zb[:, O_KV:O_KV + 2 * D_KV]
        kvp = kvbuf[...]
        kvbuf[...] = kv
        col = lax.broadcasted_iota(jnp.int32, (BLOCK, 2 * BLOCK), 1)
        rowq = lax.broadcasted_iota(jnp.int32, (BLOCK, 2 * BLOCK), 0)
        d = BLOCK + rowq - col
        mask = (d >= 0) & (d < WINDOW) & (col >= first_key)
        outs = []
        for kh in range(N_KV):
            between()
            ks = slice(kh * HEAD_DIM, (kh + 1) * HEAD_DIM)
            vs = slice(D_KV + kh * HEAD_DIM, D_KV + (kh + 1) * HEAD_DIM)
            qs = jnp.concatenate(
                [q[:, (kh * GROUP + g) * HEAD_DIM:(kh * GROUP + g + 1) * HEAD_DIM] for g in range(GROUP)],
                axis=0).astype(BF16)
            kk = jnp.concatenate([kvp[:, ks], kv[:, ks]], axis=0).astype(BF16)
            vv = jnp.concatenate([kvp[:, vs], kv[:, vs]], axis=0).astype(BF16)
            sc = lax.dot_general(qs, kk, (((1,), (1,)), ((), ())), preferred_element_type=F32)
            es, dens = [], []
            for g in range(GROUP):
                hh = kh * GROUP + g
                lg = sc[g * BLOCK:(g + 1) * BLOCK] * SCALE + bias_scr[hh]
                lg = jnp.where(mask, lg, NEG)
                sink = sink_ref[hh]
                mx = jnp.maximum(jnp.max(lg, axis=-1, keepdims=True), sink)
                e = jnp.exp(lg - mx)
                dens.append(jnp.sum(e, axis=-1, keepdims=True) + jnp.exp(sink - mx))
                es.append(e.astype(BF16))
            pv = jnp.dot(jnp.concatenate(es, axis=0), vv, preferred_element_type=F32)
            for g in range(GROUP):
                outs.append(pv[g * BLOCK:(g + 1) * BLOCK] / dens[g])
        between()
        mix_b = _rms(jnp.concatenate(outs, axis=1), na_ref[...]).astype(BF16)

        return jnp.concatenate([mix_a, mix_b], axis=1), h, kv, xr

    xa = jnp.where(k == last, xt_ref[0:DEC_BATCH], xa_ref[...])
    xb = xb_ref[...]
    lhs_p[0:BLOCK] = _rms(xa, gm_ref[...]).astype(BF16)
    lhs_p[BLOCK:FB] = _rms(xb, gm_ref[...]).astype(BF16)
    n_in, n_out = D_IN // PC, D_MODEL // PC
    mix_o, h, kv, xr = mixer(zb_o, k == 0, jnp.where(k == 0, BLOCK + N_PAD, 0),
                             [inproj_chunk(c) for c in range(N_IN_EARLY)])
    mixp[BLOCK:FB] = mix_o
    hlast_ref[...] = jnp.broadcast_to(h, (HALO, D_RNN))
    kvlast_ref[...] = kv
    xrt_ref[...] = xr[BLOCK - HALO:BLOCK]

    mix_e = mixer(zp.at[0:BLOCK], None, jnp.where(k == 0, N_PAD, 0),
                  [inproj_chunk(c) for c in range(N_IN_EARLY, n_in)] + [outproj_chunk(c) for c in range(n_out)])[0]
    mixp[0:BLOCK] = mix_e
    xsp[0:BLOCK] = xa
    xsp[BLOCK:FB] = xb
    zb_o[...] = zp[BLOCK:FB]

    @pl.when(k == last)
    def _():
        zs_ref[...] = zp[0:BLOCK]
        wout_copy.wait()
        for cp in slides:
            cp.wait()


def _front(tab, sinks, xp, xt, gm, w_in, w_out, bucket, cw, cb, wg, gab, gxb, lam, nr, na, w_up, w_gate, w_down,
           kc, vc):
    ffn_w = (w_up, w_gate, w_down)
    slab = lambda w: (w.shape[0] // N_SLAB, w.shape[1])
    hbm = pl.BlockSpec(memory_space=pl.ANY)
    vec = lambda n: pl.BlockSpec((1, n), lambda k: (0, 0))
    smem = pl.BlockSpec(memory_space=pltpu.SMEM)
    once = lambda shape: pl.BlockSpec(shape, lambda k: (0,) * len(shape), pipeline_mode=pl.Buffered(1))
    return pl.pallas_call(
        _front_kernel,
        grid=(N_FSTEP,),
        in_specs=[
            smem, smem,
            pl.BlockSpec((BLOCK, D_MODEL), lambda k: (jnp.minimum(2 * k, N_PBLK - 1), 0)),
            pl.BlockSpec((BLOCK, D_MODEL), lambda k: (jnp.minimum(2 * k + 1, N_PBLK - 1), 0)),
            once((R_TAIL, D_MODEL)),
            vec(D_MODEL),
            hbm, hbm,
            pl.BlockSpec((BLOCK, 2 * BLOCK), lambda k: (0, 0)),
            pl.BlockSpec((RNN_CONV, D_RNN), lambda k: (0, 0)),
            vec(D_RNN),
            pl.BlockSpec((N_GW, GW, 2 * GW), lambda k: (0, 0, 0)),
            vec(D_RNN), vec(D_RNN), vec(D_RNN), vec(D_RNN), vec(D_ATTN),
            hbm, hbm, hbm, hbm, hbm,
        ],
        out_specs=[
            pl.BlockSpec((FB, D_MODEL), lambda k: (jnp.where(k == 0, SEQ // FB, k - 1), 0)),
            pl.BlockSpec((BLOCK, D_IN), lambda k: (0, 0)),
            pl.BlockSpec((HALO, D_RNN), lambda k: (0, 0)),
            pl.BlockSpec((BLOCK, 2 * D_KV), lambda k: (0, 0)),
            pl.BlockSpec((HALO, D_RNN), lambda k: (0, 0)),
            hbm, hbm, hbm, hbm, hbm, hbm,
        ],
        out_shape=[
            jax.ShapeDtypeStruct((R_ALL, D_MODEL), F32),
            jax.ShapeDtypeStruct((DEC_BATCH, D_IN), F32),
            jax.ShapeDtypeStruct((HALO, D_RNN), F32),
            jax.ShapeDtypeStruct((BLOCK, 2 * D_KV), F32),
            jax.ShapeDtypeStruct((HALO, D_RNN), F32),
        ] + [jax.ShapeDtypeStruct(w.shape, BF16) for w in (w_out,) + ffn_w] + [
            jax.ShapeDtypeStruct(kc.shape, F32), jax.ShapeDtypeStruct(vc.shape, F32)],
        scratch_shapes=[
            pltpu.VMEM(w_in.shape, BF16),
            pltpu.VMEM(w_out.shape, BF16),
            pltpu.VMEM((FB, D_IN), F32),
            pltpu.VMEM((BLOCK, D_IN), F32),
            pltpu.VMEM((FB, D_MODEL), BF16),
            pltpu.VMEM((FB, D_MODEL), F32),
            pltpu.VMEM((FB, D_MODEL), BF16),
            pltpu.VMEM((HALO + BLOCK, D_RNN), F32),
            pltpu.VMEM((HALO, D_RNN), F32),
            pltpu.VMEM((BLOCK, 2 * D_KV), F32),
            pltpu.VMEM((N_HEADS, BLOCK, 2 * BLOCK), F32),
        ] + [pltpu.VMEM(slab(w), F32) for w in ffn_w] + [pltpu.VMEM(slab(w), BF16) for w in ffn_w] + [
            pltpu.SemaphoreType.DMA((len(ffn_w),)),
            pltpu.SemaphoreType.DMA((len(ffn_w),)),
            pltpu.SemaphoreType.DMA((2 * N_STAGE + 1,)),
            pltpu.SemaphoreType.DMA((4,)),
        ],
        compiler_params=pltpu.CompilerParams(
            dimension_semantics=("arbitrary",), vmem_limit_bytes=VMEM_LIMIT_BIG),
        name="front",
    )(tab, sinks, xp, xp, xt, gm, w_in, w_out, bucket, cw, cb, wg, gab, gxb, lam, nr, na, *ffn_w, kc, vc)


def _smixer_kernel(tab_ref, sink_ref, z_ref, st_ref, h0_ref, kc_ref, vc_ref, xs_ref, wout_ref,
                   bucket_ref, cw_ref, cb_ref, wg_ref, gab_ref, gxb_ref, lam_ref, nr_ref, na_ref, ko_in, vo_in,
                   x1s_ref, conv_ref, h_ref, ko_hbm, vo_hbm, bias_scr, mix_scr, new_k, new_v, sem_n):
    del ko_in, vo_in
    c = pl.program_id(0)
    nrow = N_HEADS * SB
    ncol = SB * WINDOW * N_KV

    @pl.when(c == 0)
    def _():
        col = lax.broadcasted_iota(jnp.int32, (SB, ncol), 1)
        row = lax.broadcasted_iota(jnp.int32, (SB, ncol), 0)
        own = ((col >> 8) == row) & (((col >> 1) & (WINDOW - 1)) >= 1)
        for h, bias in enumerate(_table_lookup(bucket_ref[...], tab_ref)):
            ok = own & ((col & 1) == h // GROUP)
            bias_scr[h * SB:(h + 1) * SB] = jnp.where(ok, jnp.concatenate([bias] * (SB // HALO), axis=0), NEG)

    r0 = pl.multiple_of(c * SB, SB)

    xr = z_ref[:, 0:D_RNN]
    cw = cw_ref[...]
    xc = cb_ref[...]
    for j in range(RNN_CONV - 1):
        xc = xc + st_ref[j] * cw[j:j + 1]
    xc = xc + xr * cw[RNN_CONV - 1:RNN_CONV]
    for j in range(1, RNN_CONV - 1):
        conv_ref[j - 1] = st_ref[j]
    conv_ref[RNN_CONV - 2] = xr
    a, b = _gates(xc, wg_ref, gab_ref[...], gxb_ref[...], lam_ref[...])
    h = a * h0_ref[...] + b
    h_ref[...] = h
    y_rnn = h * jax.nn.gelu(z_ref[:, O_GR:O_GR + D_RNN])
    mix_scr[pl.ds(r0, SB), 0:D_RNN] = _rms(y_rnn, nr_ref[...]).astype(BF16)

    q = z_ref[:, O_Q:O_Q + D_ATTN]
    kv = z_ref[:, O_KV:O_KV + 2 * D_KV]
    qs = jnp.concatenate([q[:, h * HEAD_DIM:(h + 1) * HEAD_DIM] for h in range(N_HEADS)], axis=0).astype(BF16)
    new_rows = lambda off: jnp.concatenate(
        [kv[:, off + (h // GROUP) * HEAD_DIM:off + (h // GROUP + 1) * HEAD_DIM] for h in range(N_HEADS)],
        axis=0).astype(BF16).astype(F32)
    k_new = new_rows(0)
    v_new = new_rows(D_KV)
    sc = lax.dot_general(qs, kc_ref[...].astype(BF16), (((1,), (1,)), ((), ())), preferred_element_type=F32)
    lg = sc * SCALE + bias_scr[...]
    rh = lax.broadcasted_iota(jnp.int32, (nrow, 1), 0) >> 4
    sink = jnp.zeros((nrow, 1), F32)
    bias_new = jnp.zeros((nrow, 1), F32)
    for h in range(N_HEADS):
        sink = jnp.where(rh == h, sink_ref[h], sink)
        bias_new = jnp.where(rh == h, tab_ref[0, h], bias_new)
    lg_new = jnp.sum(qs.astype(F32) * k_new, axis=-1, keepdims=True) * SCALE + bias_new
    mx = jnp.maximum(jnp.maximum(jnp.max(lg, axis=-1, keepdims=True), lg_new), sink)
    e = jnp.exp(lg - mx)
    e_new = jnp.exp(lg_new - mx)
    den = jnp.sum(e, axis=-1, keepdims=True) + e_new + jnp.exp(sink - mx)
    pv = jnp.dot(e.astype(BF16), vc_ref[...].astype(BF16), preferred_element_type=F32)
    o = (pv + e_new.astype(BF16).astype(F32) * v_new) / den
    y_attn = jnp.concatenate([o[h * SB:(h + 1) * SB] for h in range(N_HEADS)], axis=1)
    mix_scr[pl.ds(r0, SB), D_RNN:] = _rms(y_attn, na_ref[...]).astype(BF16)

    tail = lambda dst: dst.at[pl.ds(r0, SB), pl.ds(WINDOW - 1, 1)]
    puts = (pltpu.make_async_copy(new_k, tail(ko_hbm), sem_n.at[0]),
            pltpu.make_async_copy(new_v, tail(vo_hbm), sem_n.at[1]))

    @pl.when(c > 0)
    def _():
        for cp in puts:
            cp.wait()

    for bi in range(SB):
        for kh in range(N_KV):
            new_k[bi, 0, kh:kh + 1, :] = kv[bi:bi + 1, kh * HEAD_DIM:(kh + 1) * HEAD_DIM]
            new_v[bi, 0, kh:kh + 1, :] = kv[bi:bi + 1, D_KV + kh * HEAD_DIM:D_KV + (kh + 1) * HEAD_DIM]
    for cp in puts:
        cp.start()

    @pl.when(c == DEC_BATCH // SB - 1)
    def _():
        x1s_ref[...] = xs_ref[...] + jnp.dot(mix_scr[...], wout_ref[...], preferred_element_type=F32)
        for cp in puts:
            cp.wait()


def _smixer(tab, sinks, zs, st, h0, kc, vc, xs, w_out, bucket, cw, cb, wg, gab, gxb, lam, nr, na, ko, vo):
    hbm = pl.BlockSpec(memory_space=pl.ANY)
    vec = lambda n: pl.BlockSpec((1, n), lambda c: (0, 0))
    smem = pl.BlockSpec(memory_space=pltpu.SMEM)
    cache = pl.BlockSpec((SB * WINDOW * N_KV, HEAD_DIM), lambda c: (c, 0))
    return pl.pallas_call(
        _smixer_kernel,
        grid=(DEC_BATCH // SB,),
        in_specs=[
            smem, smem,
            pl.BlockSpec((SB, D_IN), lambda c: (c, 0)),
            pl.BlockSpec((RNN_CONV - 1, SB, D_RNN), lambda c: (0, c, 0)),
            pl.BlockSpec((SB, D_RNN), lambda c: (c, 0)),
            cache, cache,
            pl.BlockSpec((DEC_BATCH, D_MODEL), lambda c: (0, 0)),
            pl.BlockSpec((D_MODEL, D_MODEL), lambda c: (0, 0), pipeline_mode=pl.Buffered(1)),
            pl.BlockSpec((HALO, SB * WINDOW * N_KV), lambda c: (0, 0)),
            pl.BlockSpec((RNN_CONV, D_RNN), lambda c: (0, 0)),
            vec(D_RNN),
            pl.BlockSpec((N_GW, GW, 2 * GW), lambda c: (0, 0, 0)),
            vec(D_RNN), vec(D_RNN), vec(D_RNN), vec(D_RNN), vec(D_ATTN),
            hbm, hbm,
        ],
        out_specs=[
            pl.BlockSpec((DEC_BATCH, D_MODEL), lambda c: (0, 0)),
            pl.BlockSpec((RNN_CONV - 1, SB, D_RNN), lambda c: (0, c, 0)),
            pl.BlockSpec((SB, D_RNN), lambda c: (c, 0)),
            hbm, hbm,
        ],
        out_shape=[
            jax.ShapeDtypeStruct((DEC_BATCH, D_MODEL), F32),
            jax.ShapeDtypeStruct((RNN_CONV - 1, DEC_BATCH, D_RNN), F32),
            jax.ShapeDtypeStruct((DEC_BATCH, D_RNN), F32),
            jax.ShapeDtypeStruct(ko.shape, F32),
            jax.ShapeDtypeStruct(vo.shape, F32),
        ],
        input_output_aliases={18: 3, 19: 4},
        scratch_shapes=[
            pltpu.VMEM((N_HEADS * SB, SB * WINDOW * N_KV), F32),
            pltpu.VMEM((DEC_BATCH, D_MODEL), BF16),
            pltpu.VMEM((SB, 1, N_KV, HEAD_DIM), F32),
            pltpu.VMEM((SB, 1, N_KV, HEAD_DIM), F32),
            pltpu.SemaphoreType.DMA((2,)),
        ],
        compiler_params=pltpu.CompilerParams(
            dimension_semantics=("arbitrary",), vmem_limit_bytes=VMEM_LIMIT),
        name="smixer",
    )(tab, sinks, zs, st, h0, kc, vc, xs, w_out, bucket, cw, cb, wg, gab, gxb, lam, nr, na, ko, vo)


U_LO = LAST_P - HALO
U_HI = LAST_P + DEC_BATCH
N_UST = U_HI - U_LO


def _ffn_kernel(x1_ref, halo_ref, x1s_ref, st0_ref, st1_ref, gn_ref, gf_ref, wu_ref, wg_ref, cw_ref, cb_ref,
                wd_ref, y_ref, ys_ref, ust_ref, h2_scr, ubuf, abuf):
    i = pl.program_id(0)
    j = pl.program_id(1)

    @pl.when(j == 0)
    def _():
        h2_scr[0:FH] = _rms(halo_ref[...], gn_ref[...]).astype(BF16)
        h2_scr[FH:FH + TM] = _rms(x1_ref[...], gn_ref[...]).astype(BF16)
        y_ref[...] = jnp.zeros((TM, D_MODEL), F32)

        @pl.when(i == N_RT - 1)
        def _():
            h2_scr[FH + LAST_P:FH + U_HI] = _rms(x1s_ref[...], gn_ref[...]).astype(BF16)

    ubuf[...] = jnp.dot(h2_scr[...], wu_ref[...], preferred_element_type=F32)
    gate = jnp.dot(h2_scr[FH:FH + TM], wg_ref[...], preferred_element_type=F32)
    cw = cw_ref[...]
    cb = cb_ref[...]
    for c in range(TF // KC):
        cs = slice(c * KC, (c + 1) * KC)
        tap = lambda r0, n: ubuf[r0:r0 + n, cs]
        uc = (cb[:, cs] + tap(FH - 2, TM) * cw[0:1, cs] + tap(FH - 1, TM) * cw[1:2, cs] + tap(FH, TM) * cw[2:3, cs])
        act = jax.nn.gelu(uc) * gate[:, cs]
        abuf[:, cs] = act.astype(BF16)
        ucs = (cb[:, cs] + st0_ref[:, cs] * cw[0:1, cs] + st1_ref[:, cs] * cw[1:2, cs]
               + tap(FH + LAST_P, DEC_BATCH) * cw[2:3, cs])
        act_s = jax.nn.gelu(ucs) * gate[LAST_P:U_HI, cs]
        abuf[LAST_P:U_HI, cs] = jnp.where(i == N_RT - 1, act_s, act[LAST_P:U_HI]).astype(BF16)
        y_ref[...] += jnp.dot(abuf[:, cs], wd_ref[cs, :], preferred_element_type=F32)
    ust_ref[...] = ubuf[FH + U_LO:FH + U_HI]

    @pl.when(j == N_FT - 1)
    def _():
        @pl.when(i == N_RT - 1)
        def _():
            ys_ref[...] = _rms(x1s_ref[...] + y_ref[LAST_P:U_HI], gf_ref[...])

        y_ref[...] = _rms(x1_ref[...] + y_ref[...], gf_ref[...])


def _ffn(x1, x1s, st, gn, gf, wu, wg, cw, cb, wd):
    def halo_idx(i, j):
        return (jnp.where(i == 0, R_ALL // FH - 1, i * (TM // FH) - 1), 0)

    return pl.pallas_call(
        _ffn_kernel,
        grid=(N_RT, N_FT),
        in_specs=[
            pl.BlockSpec((TM, D_MODEL), lambda i, j: (i, 0)),
            pl.BlockSpec((FH, D_MODEL), halo_idx),
            pl.BlockSpec((DEC_BATCH, D_MODEL), lambda i, j: (0, 0)),
            pl.BlockSpec((DEC_BATCH, TF), lambda i, j: (0, j)),
            pl.BlockSpec((DEC_BATCH, TF), lambda i, j: (0, N_FT + j)),
            pl.BlockSpec((1, D_MODEL), lambda i, j: (0, 0)),
            pl.BlockSpec((1, D_MODEL), lambda i, j: (0, 0)),
            pl.BlockSpec((D_MODEL, TF), lambda i, j: (0, j)),
            pl.BlockSpec((D_MODEL, TF), lambda i, j: (0, j)),
            pl.BlockSpec((FFN_CONV, TF), lambda i, j: (0, j)),
            pl.BlockSpec((1, TF), lambda i, j: (0, j)),
            pl.BlockSpec((TF, D_MODEL), lambda i, j: (j, 0)),
        ],
        out_specs=[
            pl.BlockSpec((TM, D_MODEL), lambda i, j: (i, 0)),
            pl.BlockSpec((DEC_BATCH, D_MODEL), lambda i, j: (0, 0)),
            pl.BlockSpec((N_UST, TF), lambda i, j: (i, j)),
        ],
        out_shape=[
            jax.ShapeDtypeStruct((SEQ, D_MODEL), F32),
            jax.ShapeDtypeStruct((DEC_BATCH, D_MODEL), F32),
            jax.ShapeDtypeStruct((N_RT * N_UST, D_FF), F32),
        ],
        scratch_shapes=[
            pltpu.VMEM((FH + TM, D_MODEL), BF16),
            pltpu.VMEM((FH + TM, TF), F32),
            pltpu.VMEM((TM, TF), BF16),
        ],
        compiler_params=pltpu.CompilerParams(
            dimension_semantics=("arbitrary", "arbitrary"), vmem_limit_bytes=VMEM_LIMIT_BIG),
        name="ffn",
    )(x1, x1, x1s, st, st, gn, gf, wu, wg, cw, cb, wd)


def _gate_weights(wa, wx):
    per = GW // RNN_BLOCK
    eye = jnp.eye(per, dtype=wa.dtype)

    def bd(w):
        w = w.reshape(N_GW, per, RNN_BLOCK, RNN_BLOCK)
        return jnp.einsum('gpcd,pq->gpcqd', w, eye).reshape(N_GW, GW, GW)

    return jnp.concatenate([bd(wa), bd(wx)], axis=-1).astype(BF16)


def kernel(x_prompt, x_sample, state_rnn_conv, state_rnn_h, cache_k_win, cache_v_win, state_ffn_conv,
           meta_tokens, rel_bias_table, norm_mix, w_in, rnn_conv_w, rnn_conv_b, gate_a_w, gate_a_b,
           gate_x_w, gate_x_b, rnn_lambda, attn_sinks, norm_rnn_out, norm_attn_out, w_out, norm_ffn,
           w_up, w_gate, ffn_conv_w, ffn_conv_b, w_down, norm_final):
    l = 0
    xp = x_prompt[0]
    xs = x_sample[:, 0, :]
    xt = jnp.concatenate([xs, jnp.zeros((N_PAD, D_MODEL), F32), meta_tokens], axis=0)
    row = lambda v: v.reshape(1, -1)

    qi = np.arange(BLOCK)[:, None]
    sj = np.arange(2 * BLOCK)[None, :]
    bucket_p = jnp.asarray(_rel_buckets(BLOCK + qi - sj))
    pos = (np.arange(SB * WINDOW * N_KV) >> 1) & (WINDOW - 1)
    bucket_s = jnp.asarray(np.tile(_rel_buckets(WINDOW - pos)[None, :], (HALO, 1)))

    wg = _gate_weights(gate_a_w[l], gate_x_w[l])
    seq_w = (rnn_conv_w[l], row(rnn_conv_b[l]), wg, row(gate_a_b[l]), row(gate_x_b[l]), row(rnn_lambda[l]),
             row(norm_rnn_out[l]), row(norm_attn_out[l]))

    kc = cache_k_win[l].reshape(DEC_BATCH * WINDOW * N_KV, HEAD_DIM)
    vc = cache_v_win[l].reshape(DEC_BATCH * WINDOW * N_KV, HEAD_DIM)
    x1, zs, h_last, kv_last, xr_tail, w_out_b, w_up_b, w_gate_b, w_down_b, k_slid, v_slid = _front(
        rel_bias_table, attn_sinks[l], xp, xt, row(norm_mix[l]), w_in[l], w_out[l], bucket_p, *seq_w,
        w_up[l], w_gate[l], w_down[l], cache_k_win[l], cache_v_win[l])
    x1s, conv_s, h_s, k_s, v_s = _smixer(
        rel_bias_table, attn_sinks[l], zs,
        jnp.swapaxes(state_rnn_conv[l], 0, 1), state_rnn_h[l],
        kc, vc, xs, w_out_b, bucket_s, *seq_w, k_slid, v_slid)
    y_p, y_s, ust = _ffn(x1, x1s, state_ffn_conv[l].reshape(DEC_BATCH, (FFN_CONV - 1) * D_FF),
                         row(norm_ffn[l]), row(norm_final), w_up_b, w_gate_b,
                         ffn_conv_w[l], row(ffn_conv_b[l]), w_down_b)

    ust = ust[(N_RT - 1) * N_UST:]
    p_states = (
        xr_tail[HALO - (RNN_CONV - 1):HALO][None, None],
        h_last[0:1][None],
        kv_last[:, :D_KV].reshape(1, 1, WINDOW, N_KV, HEAD_DIM),
        kv_last[:, D_KV:].reshape(1, 1, WINDOW, N_KV, HEAD_DIM),
        ust[HALO - (FFN_CONV - 1):HALO][None, None],
    )
    s_states = (
        jnp.swapaxes(conv_s, 0, 1)[None],
        h_s[None],
        k_s[None],
        v_s[None],
        jnp.stack([state_ffn_conv[l][:, FFN_CONV - 2, :], ust[HALO:]], axis=1)[None],
    )
    return (y_p[None], y_s[:, None, :]) + p_states + s_states
```

```python
import math

import numpy as np
import jax
import jax.numpy as jnp
from jax import lax
from jax.experimental import pallas as pl
from jax.experimental.pallas import tpu as pltpu

F32 = jnp.float32
BF16 = jnp.bfloat16

D_MODEL = 2048
SEQ = 8192
DEC_BATCH = 128
D_RNN = 1024
N_RNN_BLOCKS = 16
RNN_BLOCK = D_RNN // N_RNN_BLOCKS
RNN_CONV = 4
LRU_C = 8.0
N_HEADS = 8
HEAD_DIM = 128
N_KV = 2
GROUP = N_HEADS // N_KV
D_ATTN = N_HEADS * HEAD_DIM
WINDOW = 128
BLOCK = 128
NUM_BUCKETS = 32
MAX_DISTANCE = 128
D_FF = 3 * D_MODEL
FFN_CONV = 3
N_META = 16
EPS = 1e-6
NEG = -1e30
D_KV = N_KV * HEAD_DIM
D_IN = 2 * D_RNN + D_ATTN + 2 * D_KV
SCALE = HEAD_DIM ** -0.5
O_GR = D_RNN
O_Q = 2 * D_RNN
O_KV = 2 * D_RNN + D_ATTN

N_PAD = BLOCK - N_META
R_TAIL = DEC_BATCH + BLOCK
R_ALL = SEQ + R_TAIL
N_PBLK = SEQ // BLOCK

FB = 2 * BLOCK
N_FSTEP = (N_PBLK + 2) // 2
PC = 512
N_IN_EARLY = 5
N_SLIDE = 16
N_STAGE = 4
N_SLAB = N_FSTEP - 1

TM = 640
N_RT = (SEQ + DEC_BATCH) // TM
LAST_P = SEQ - (N_RT - 1) * TM
TF = 768
N_FT = D_FF // TF
KC = 256
HALO = 8
FH = 16
GW = 256
N_GW = D_RNN // GW
SB = 16
VMEM_LIMIT = 56 * 1024 * 1024
VMEM_LIMIT_BIG = 60 * 1024 * 1024


def _rms(x, g):
    return x * lax.rsqrt(jnp.mean(x * x, axis=-1, keepdims=True) + EPS) * g


def _rel_buckets(d):
    d = np.maximum(d, 0)
    exact = NUM_BUCKETS // 2
    ratio = np.maximum(d, 1).astype(np.float32) / np.float32(exact)
    large = exact + (np.log(ratio) / np.float32(math.log(MAX_DISTANCE / exact))
                     * np.float32(NUM_BUCKETS - exact)).astype(np.int32)
    large = np.minimum(large, NUM_BUCKETS - 1)
    return np.where(d < exact, d, large).astype(np.int32)


def _table_lookup(bucket, tab_ref):
    outs = [jnp.zeros(bucket.shape, F32) for _ in range(N_HEADS)]
    for b in range(NUM_BUCKETS):
        hit = bucket == b
        outs = [jnp.where(hit, tab_ref[b, h], o) for h, o in enumerate(outs)]
    return outs


def _gates(xc, wg_ref, gab, gxb, lam):
    xcb = xc.astype(BF16)
    ga, gx = [], []
    for j in range(N_GW):
        gj = jnp.dot(xcb[:, GW * j:GW * (j + 1)], wg_ref[j], preferred_element_type=F32)
        ga.append(gj[:, :GW])
        gx.append(gj[:, GW:])
    r = jax.nn.sigmoid(jnp.concatenate(ga, axis=1) + gab)
    i = jax.nn.sigmoid(jnp.concatenate(gx, axis=1) + gxb)
    log_a = -LRU_C * r * jax.nn.softplus(-lam)
    a = jnp.exp(log_a)
    t = 1.0 - a * a
    b = jnp.where(t > 0.0, t * lax.rsqrt(t), 0.0) * i * xc
    return a, b


def _front_kernel(tab_ref, sink_ref, xa_ref, xb_ref, xt_ref, gm_ref, win_f, wout_f, bucket_ref,
                  cw_ref, cb_ref, wg_ref, gab_ref, gxb_ref, lam_ref, nr_ref, na_ref, wu_f, wgt_f, wd_f, kc_hbm, vc_hbm,
                  x1_ref, zs_ref, hlast_ref, kvlast_ref, xrt_ref, wout_hbm, wu_b, wgt_b, wd_b, ko_hbm, vo_hbm,
                  win_ref, wout_ref, zp, zb_o, lhs_p, xsp, mixp, xbuf, h_scr, kvbuf, bias_scr,
                  su, sg, sd, tu, tg, td, sem_in, sem_out, sem_w, sem_c):
    k = pl.program_id(0)
    last = N_FSTEP - 1

    slides = []
    rows_c = DEC_BATCH // N_SLIDE
    for n, (src, dst) in enumerate(((kc_hbm, ko_hbm), (vc_hbm, vo_hbm))):
        for g in range(N_SLIDE):
            rows = pl.ds(g * rows_c, rows_c)
            s0 = 2 * (n * N_SLIDE + g)
            slides.append(pltpu.make_async_copy(src.at[rows, pl.ds(1, WINDOW - 1)], dst.at[rows, pl.ds(0, WINDOW - 1)],
                                                sem_c.at[s0]))
            slides.append(pltpu.make_async_copy(src.at[rows, pl.ds(0, 1)], dst.at[rows, pl.ds(WINDOW - 1, 1)],
                                                sem_c.at[s0 + 1]))

    @pl.when(k == 0)
    def _():
        for cp in slides:
            cp.start()

    stages = ((wu_f, su, tu, wu_b), (wgt_f, sg, tg, wgt_b), (wd_f, sd, td, wd_b))

    def slab_in(s):
        return [pltpu.make_async_copy(w.at[pl.ds(s * st.shape[0], st.shape[0])], st, sem_in.at[n])
                for n, (w, st, _, _) in enumerate(stages)]

    def slab_out(s):
        return [pltpu.make_async_copy(t, w.at[pl.ds(s * t.shape[0], t.shape[0])], sem_out.at[n])
                for n, (_, _, t, w) in enumerate(stages)]

    @pl.when(k == 0)
    def _():
        for c in slab_in(0):
            c.start()

    @pl.when(k >= 1)
    def _():
        for c in slab_out(k - 1):
            c.wait()

    @pl.when(k < N_SLAB)
    def _():
        for c in slab_in(k):
            c.wait()
        for _, st, t, _ in stages:
            t[...] = st[...].astype(BF16)
        for c in slab_out(k):
            c.start()

    @pl.when(k < N_SLAB - 1)
    def _():
        for c in slab_in(k + 1):
            c.start()

    wout_copy = pltpu.make_async_copy(wout_ref, wout_hbm, sem_w.at[2 * N_STAGE])

    @pl.when(k == 0)
    def _():
        streams = ((win_f, win_ref, zp, 0), (wout_f, wout_ref, xsp, N_STAGE))
        rows = FB // N_STAGE
        n = D_MODEL // rows

        def chunk(stream, c):
            w_f, _, buf, sem0 = stream
            slot = c % N_STAGE
            return pltpu.make_async_copy(w_f.at[pl.ds(c * rows, rows)], buf.at[pl.ds(slot * rows, rows)],
                                         sem_w.at[sem0 + slot])

        for c in range(N_STAGE - 1):
            for stream in streams:
                chunk(stream, c).start()
        for c in range(n):
            for stream in streams:
                if c + N_STAGE - 1 < n:
                    chunk(stream, c + N_STAGE - 1).start()
                chunk(stream, c).wait()
                _, w_b, buf, _ = stream
                w_b[c * rows:(c + 1) * rows] = buf[pl.ds((c % N_STAGE) * rows, rows)].astype(BF16)
        wout_copy.start()

    def inproj_chunk(c):
        def run():
            cs = slice(c * PC, (c + 1) * PC)
            zp[:, cs] = jnp.dot(lhs_p[...], win_ref[:, cs], preferred_element_type=F32)
        return run

    def outproj_chunk(c):
        def run():
            cs = slice(c * PC, (c + 1) * PC)
            x1_ref[:, cs] = xsp[:, cs] + jnp.dot(mixp[...], wout_ref[:, cs], preferred_element_type=F32)
        return run

    @pl.when(k == 0)
    def _():
        prefix = xt_ref[DEC_BATCH:R_TAIL]
        zb_o[...] = jnp.dot(_rms(prefix, gm_ref[...]).astype(BF16), win_ref[...], preferred_element_type=F32)
        xsp[0:BLOCK] = jnp.zeros((BLOCK, D_MODEL), F32)
        xsp[BLOCK:FB] = prefix
        mixp[...] = jnp.zeros((FB, D_MODEL), BF16)
        xbuf[0:HALO] = jnp.zeros((HALO, D_RNN), F32)
        h_scr[...] = jnp.zeros((HALO, D_RNN), F32)
        kvbuf[...] = jnp.zeros((BLOCK, 2 * D_KV), F32)
        for h, bias in enumerate(_table_lookup(bucket_ref[...], tab_ref)):
            bias_scr[h] = bias

    def mixer(zb, is_prefix, first_key, fillers):
        fill = iter(fillers)

        def between():
            f = next(fill, None)
            if f is not None:
                f()

        between()
        xr = zb[:, 0:D_RNN]
        xbuf[HALO:HALO + BLOCK] = xr
        cw = cw_ref[...]
        xc = cb_ref[...]
        for j in range(RNN_CONV - 1):
            lo = HALO - (RNN_CONV - 1) + j
            xc = xc + xbuf[lo:lo + BLOCK] * cw[j:j + 1]
        xc = xc + xr * cw[RNN_CONV - 1:RNN_CONV]
        xbuf[0:HALO] = xr[BLOCK - HALO:BLOCK]

        a, b = _gates(xc, wg_ref, gab_ref[...], gxb_ref[...], lam_ref[...])
        between()
        h = h_scr[0:1]
        if is_prefix is not None:
            row = lax.broadcasted_iota(jnp.int32, (BLOCK, D_RNN), 0)
            b = jnp.where(jnp.logical_and(is_prefix, row < N_PAD), 0.0, b)
            h = jnp.where(is_prefix, 0.0, h)

        ng = BLOCK // HALO
        a3 = a.reshape(ng, HALO, D_RNN)
        b3 = b.reshape(ng, HALO, D_RNN)
        sub = lax.broadcasted_iota(jnp.int32, (ng, HALO, D_RNN), 1)
        sh = 1
        while sh < HALO:
            a_prev = pltpu.roll(a3, sh, 1)
            b_prev = pltpu.roll(b3, sh, 1)
            m = sub >= sh
            b3 = jnp.where(m, a3 * b_prev + b3, b3)
            a3 = jnp.where(m, a3 * a_prev, a3)
            sh *= 2
        hs = []
        for g in range(ng):
            hg = a3[g] * h + b3[g]
            hs.append(hg)
            h = hg[HALO - 1:HALO]
        h_all = jnp.concatenate(hs, axis=0)
        h_scr[...] = jnp.broadcast_to(h, (HALO, D_RNN))
        between()
        y_rnn = h_all * jax.nn.gelu(zb[:, O_GR:O_GR + D_RNN])
        mix_a = _rms(y_rnn, nr_ref[...]).astype(BF16)

        q = zb[:, O_Q:O_Q + D_ATTN]
        kv = zb[:, O_KV:O_KV + 2 * D_KV]
        kvp = kvbuf[...]
        kvbuf[...] = kv
        col = lax.broadcasted_iota(jnp.int32, (BLOCK, 2 * BLOCK), 1)
        rowq = lax.broadcasted_iota(jnp.int32, (BLOCK, 2 * BLOCK), 0)
        d = BLOCK + rowq - col
        mask = (d >= 0) & (d < WINDOW) & (col >= first_key)
        outs = []
        for kh in range(N_KV):
            between()
            ks = slice(kh * HEAD_DIM, (kh + 1) * HEAD_DIM)
            vs = slice(D_KV + kh * HEAD_DIM, D_KV + (kh + 1) * HEAD_DIM)
            qs = jnp.concatenate(
                [q[:, (kh * GROUP + g) * HEAD_DIM:(kh * GROUP + g + 1) * HEAD_DIM] for g in range(GROUP)],
                axis=0).astype(BF16)
            kk = jnp.concatenate([kvp[:, ks], kv[:, ks]], axis=0).astype(BF16)
            vv = jnp.concatenate([kvp[:, vs], kv[:, vs]], axis=0).astype(BF16)
            sc = lax.dot_general(qs, kk, (((1,), (1,)), ((), ())), preferred_element_type=F32)
            es, dens = [], []
            for g in range(GROUP):
                hh = kh * GROUP + g
                lg = sc[g * BLOCK:(g + 1) * BLOCK] * SCALE + bias_scr[hh]
                lg = jnp.where(mask, lg, NEG)
                sink = sink_ref[hh]
                mx = jnp.maximum(jnp.max(lg, axis=-1, keepdims=True), sink)
                e = jnp.exp(lg - mx)
                dens.append(jnp.sum(e, axis=-1, keepdims=True) + jnp.exp(sink - mx))
                es.append(e.astype(BF16))
            pv = jnp.dot(jnp.concatenate(es, axis=0), vv, preferred_element_type=F32)
            for g in range(GROUP):
                outs.append(pv[g * BLOCK:(g + 1) * BLOCK] / dens[g])
        between()
        mix_b = _rms(jnp.concatenate(outs, axis=1), na_ref[...]).astype(BF16)

        return jnp.concatenate([mix_a, mix_b], axis=1), h, kv, xr

    xa = jnp.where(k == last, xt_ref[0:DEC_BATCH], xa_ref[...])
    xb = xb_ref[...]
    lhs_p[0:BLOCK] = _rms(xa, gm_ref[...]).astype(BF16)
    lhs_p[BLOCK:FB] = _rms(xb, gm_ref[...]).astype(BF16)
    n_in, n_out = D_IN // PC, D_MODEL // PC
    mix_o, h, kv, xr = mixer(zb_o, k == 0, jnp.where(k == 0, BLOCK + N_PAD, 0),
                             [inproj_chunk(c) for c in range(N_IN_EARLY)])
    mixp[BLOCK:FB] = mix_o
    hlast_ref[...] = jnp.broadcast_to(h, (HALO, D_RNN))
    kvlast_ref[...] = kv
    xrt_ref[...] = xr[BLOCK - HALO:BLOCK]

    mix_e = mixer(zp.at[0:BLOCK], None, jnp.where(k == 0, N_PAD, 0),
                  [inproj_chunk(c) for c in range(N_IN_EARLY, n_in)] + [outproj_chunk(c) for c in range(n_out)])[0]
    mixp[0:BLOCK] = mix_e
    xsp[0:BLOCK] = xa
    xsp[BLOCK:FB] = xb
    zb_o[...] = zp[BLOCK:FB]

    @pl.when(k == last)
    def _():
        zs_ref[...] = zp[0:BLOCK]
        wout_copy.wait()
        for cp in slides:
            cp.wait()


def _front(tab, sinks, xp, xt, gm, w_in, w_out, bucket, cw, cb, wg, gab, gxb, lam, nr, na, w_up, w_gate, w_down,
           kc, vc):
    ffn_w = (w_up, w_gate, w_down)
    slab = lambda w: (w.shape[0] // N_SLAB, w.shape[1])
    hbm = pl.BlockSpec(memory_space=pl.ANY)
    vec = lambda n: pl.BlockSpec((1, n), lambda k: (0, 0))
    smem = pl.BlockSpec(memory_space=pltpu.SMEM)
    once = lambda shape: pl.BlockSpec(shape, lambda k: (0,) * len(shape), pipeline_mode=pl.Buffered(1))
    return pl.pallas_call(
        _front_kernel,
        grid=(N_FSTEP,),
        in_specs=[
            smem, smem,
            pl.BlockSpec((BLOCK, D_MODEL), lambda k: (jnp.minimum(2 * k, N_PBLK - 1), 0)),
            pl.BlockSpec((BLOCK, D_MODEL), lambda k: (jnp.minimum(2 * k + 1, N_PBLK - 1), 0)),
            once((R_TAIL, D_MODEL)),
            vec(D_MODEL),
            hbm, hbm,
            pl.BlockSpec((BLOCK, 2 * BLOCK), lambda k: (0, 0)),
            pl.BlockSpec((RNN_CONV, D_RNN), lambda k: (0, 0)),
            vec(D_RNN),
            pl.BlockSpec((N_GW, GW, 2 * GW), lambda k: (0, 0, 0)),
            vec(D_RNN), vec(D_RNN), vec(D_RNN), vec(D_RNN), vec(D_ATTN),
            hbm, hbm, hbm, hbm, hbm,
        ],
        out_specs=[
            pl.BlockSpec((FB, D_MODEL), lambda k: (jnp.where(k == 0, SEQ // FB, k - 1), 0)),
            pl.BlockSpec((BLOCK, D_IN), lambda k: (0, 0)),
            pl.BlockSpec((HALO, D_RNN), lambda k: (0, 0)),
            pl.BlockSpec((BLOCK, 2 * D_KV), lambda k: (0, 0)),
            pl.BlockSpec((HALO, D_RNN), lambda k: (0, 0)),
            hbm, hbm, hbm, hbm, hbm, hbm,
        ],
        out_shape=[
            jax.ShapeDtypeStruct((R_ALL, D_MODEL), F32),
            jax.ShapeDtypeStruct((DEC_BATCH, D_IN), F32),
            jax.ShapeDtypeStruct((HALO, D_RNN), F32),
            jax.ShapeDtypeStruct((BLOCK, 2 * D_KV), F32),
            jax.ShapeDtypeStruct((HALO, D_RNN), F32),
        ] + [jax.ShapeDtypeStruct(w.shape, BF16) for w in (w_out,) + ffn_w] + [
            jax.ShapeDtypeStruct(kc.shape, F32), jax.ShapeDtypeStruct(vc.shape, F32)],
        scratch_shapes=[
            pltpu.VMEM(w_in.shape, BF16),
            pltpu.VMEM(w_out.shape, BF16),
            pltpu.VMEM((FB, D_IN), F32),
            pltpu.VMEM((BLOCK, D_IN), F32),
            pltpu.VMEM((FB, D_MODEL), BF16),
            pltpu.VMEM((FB, D_MODEL), F32),
            pltpu.VMEM((FB, D_MODEL), BF16),
            pltpu.VMEM((HALO + BLOCK, D_RNN), F32),
            pltpu.VMEM((HALO, D_RNN), F32),
            pltpu.VMEM((BLOCK, 2 * D_KV), F32),
            pltpu.VMEM((N_HEADS, BLOCK, 2 * BLOCK), F32),
        ] + [pltpu.VMEM(slab(w), F32) for w in ffn_w] + [pltpu.VMEM(slab(w), BF16) for w in ffn_w] + [
            pltpu.SemaphoreType.DMA((len(ffn_w),)),
            pltpu.SemaphoreType.DMA((len(ffn_w),)),
            pltpu.SemaphoreType.DMA((2 * N_STAGE + 1,)),
            pltpu.SemaphoreType.DMA((4 * N_SLIDE,)),
        ],
        compiler_params=pltpu.CompilerParams(
            dimension_semantics=("arbitrary",), vmem_limit_bytes=VMEM_LIMIT_BIG),
        name="front",
    )(tab, sinks, xp, xp, xt, gm, w_in, w_out, bucket, cw, cb, wg, gab, gxb, lam, nr, na, *ffn_w, kc, vc)


def _smixer_kernel(tab_ref, sink_ref, z_ref, st_ref, h0_ref, kc_ref, vc_ref, xs_ref, wout_ref,
                   bucket_ref, cw_ref, cb_ref, wg_ref, gab_ref, gxb_ref, lam_ref, nr_ref, na_ref, ko_in, vo_in,
                   x1s_ref, conv_ref, h_ref, ko_hbm, vo_hbm, bias_scr, mix_scr, new_k, new_v, sem_n):
    del ko_in, vo_in
    c = pl.program_id(0)
    nrow = N_HEADS * SB
    ncol = SB * WINDOW * N_KV

    @pl.when(c == 0)
    def _():
        col = lax.broadcasted_iota(jnp.int32, (SB, ncol), 1)
        row = lax.broadcasted_iota(jnp.int32, (SB, ncol), 0)
        own = ((col >> 8) == row) & (((col >> 1) & (WINDOW - 1)) >= 1)
        for h, bias in enumerate(_table_lookup(bucket_ref[...], tab_ref)):
            ok = own & ((col & 1) == h // GROUP)
            bias_scr[h * SB:(h + 1) * SB] = jnp.where(ok, jnp.concatenate([bias] * (SB // HALO), axis=0), NEG)

    r0 = pl.multiple_of(c * SB, SB)

    xr = z_ref[:, 0:D_RNN]
    cw = cw_ref[...]
    xc = cb_ref[...]
    for j in range(RNN_CONV - 1):
        xc = xc + st_ref[j] * cw[j:j + 1]
    xc = xc + xr * cw[RNN_CONV - 1:RNN_CONV]
    for j in range(1, RNN_CONV - 1):
        conv_ref[j - 1] = st_ref[j]
    conv_ref[RNN_CONV - 2] = xr
    a, b = _gates(xc, wg_ref, gab_ref[...], gxb_ref[...], lam_ref[...])
    h = a * h0_ref[...] + b
    h_ref[...] = h
    y_rnn = h * jax.nn.gelu(z_ref[:, O_GR:O_GR + D_RNN])
    mix_scr[pl.ds(r0, SB), 0:D_RNN] = _rms(y_rnn, nr_ref[...]).astype(BF16)

    q = z_ref[:, O_Q:O_Q + D_ATTN]
    kv = z_ref[:, O_KV:O_KV + 2 * D_KV]
    qs = jnp.concatenate([q[:, h * HEAD_DIM:(h + 1) * HEAD_DIM] for h in range(N_HEADS)], axis=0).astype(BF16)
    new_rows = lambda off: jnp.concatenate(
        [kv[:, off + (h // GROUP) * HEAD_DIM:off + (h // GROUP + 1) * HEAD_DIM] for h in range(N_HEADS)],
        axis=0).astype(BF16).astype(F32)
    k_new = new_rows(0)
    v_new = new_rows(D_KV)
    sc = lax.dot_general(qs, kc_ref[...].astype(BF16), (((1,), (1,)), ((), ())), preferred_element_type=F32)
    lg = sc * SCALE + bias_scr[...]
    rh = lax.broadcasted_iota(jnp.int32, (nrow, 1), 0) >> 4
    sink = jnp.zeros((nrow, 1), F32)
    bias_new = jnp.zeros((nrow, 1), F32)
    for h in range(N_HEADS):
        sink = jnp.where(rh == h, sink_ref[h], sink)
        bias_new = jnp.where(rh == h, tab_ref[0, h], bias_new)
    lg_new = jnp.sum(qs.astype(F32) * k_new, axis=-1, keepdims=True) * SCALE + bias_new
    mx = jnp.maximum(jnp.maximum(jnp.max(lg, axis=-1, keepdims=True), lg_new), sink)
    e = jnp.exp(lg - mx)
    e_new = jnp.exp(lg_new - mx)
    den = jnp.sum(e, axis=-1, keepdims=True) + e_new + jnp.exp(sink - mx)
    pv = jnp.dot(e.astype(BF16), vc_ref[...].astype(BF16), preferred_element_type=F32)
    o = (pv + e_new.astype(BF16).astype(F32) * v_new) / den
    y_attn = jnp.concatenate([o[h * SB:(h + 1) * SB] for h in range(N_HEADS)], axis=1)
    mix_scr[pl.ds(r0, SB), D_RNN:] = _rms(y_attn, na_ref[...]).astype(BF16)

    tail = lambda dst: dst.at[pl.ds(r0, SB), pl.ds(WINDOW - 1, 1)]
    puts = (pltpu.make_async_copy(new_k, tail(ko_hbm), sem_n.at[0]),
            pltpu.make_async_copy(new_v, tail(vo_hbm), sem_n.at[1]))

    @pl.when(c > 0)
    def _():
        for cp in puts:
            cp.wait()

    for bi in range(SB):
        for kh in range(N_KV):
            new_k[bi, 0, kh:kh + 1, :] = kv[bi:bi + 1, kh * HEAD_DIM:(kh + 1) * HEAD_DIM]
            new_v[bi, 0, kh:kh + 1, :] = kv[bi:bi + 1, D_KV + kh * HEAD_DIM:D_KV + (kh + 1) * HEAD_DIM]
    for cp in puts:
        cp.start()

    @pl.when(c == DEC_BATCH // SB - 1)
    def _():
        x1s_ref[...] = xs_ref[...] + jnp.dot(mix_scr[...], wout_ref[...], preferred_element_type=F32)
        for cp in puts:
            cp.wait()


def _smixer(tab, sinks, zs, st, h0, kc, vc, xs, w_out, bucket, cw, cb, wg, gab, gxb, lam, nr, na, ko, vo):
    hbm = pl.BlockSpec(memory_space=pl.ANY)
    vec = lambda n: pl.BlockSpec((1, n), lambda c: (0, 0))
    smem = pl.BlockSpec(memory_space=pltpu.SMEM)
    cache = pl.BlockSpec((SB * WINDOW * N_KV, HEAD_DIM), lambda c: (c, 0))
    return pl.pallas_call(
        _smixer_kernel,
        grid=(DEC_BATCH // SB,),
        in_specs=[
            smem, smem,
            pl.BlockSpec((SB, D_IN), lambda c: (c, 0)),
            pl.BlockSpec((RNN_CONV - 1, SB, D_RNN), lambda c: (0, c, 0)),
            pl.BlockSpec((SB, D_RNN), lambda c: (c, 0)),
            cache, cache,
            pl.BlockSpec((DEC_BATCH, D_MODEL), lambda c: (0, 0)),
            pl.BlockSpec((D_MODEL, D_MODEL), lambda c: (0, 0), pipeline_mode=pl.Buffered(1)),
            pl.BlockSpec((HALO, SB * WINDOW * N_KV), lambda c: (0, 0)),
            pl.BlockSpec((RNN_CONV, D_RNN), lambda c: (0, 0)),
            vec(D_RNN),
            pl.BlockSpec((N_GW, GW, 2 * GW), lambda c: (0, 0, 0)),
            vec(D_RNN), vec(D_RNN), vec(D_RNN), vec(D_RNN), vec(D_ATTN),
            hbm, hbm,
        ],
        out_specs=[
            pl.BlockSpec((DEC_BATCH, D_MODEL), lambda c: (0, 0)),
            pl.BlockSpec((RNN_CONV - 1, SB, D_RNN), lambda c: (0, c, 0)),
            pl.BlockSpec((SB, D_RNN), lambda c: (c, 0)),
            hbm, hbm,
        ],
        out_shape=[
            jax.ShapeDtypeStruct((DEC_BATCH, D_MODEL), F32),
            jax.ShapeDtypeStruct((RNN_CONV - 1, DEC_BATCH, D_RNN), F32),
            jax.ShapeDtypeStruct((DEC_BATCH, D_RNN), F32),
            jax.ShapeDtypeStruct(ko.shape, F32),
            jax.ShapeDtypeStruct(vo.shape, F32),
        ],
        input_output_aliases={18: 3, 19: 4},
        scratch_shapes=[
            pltpu.VMEM((N_HEADS * SB, SB * WINDOW * N_KV), F32),
            pltpu.VMEM((DEC_BATCH, D_MODEL), BF16),
            pltpu.VMEM((SB, 1, N_KV, HEAD_DIM), F32),
            pltpu.VMEM((SB, 1, N_KV, HEAD_DIM), F32),
            pltpu.SemaphoreType.DMA((2,)),
        ],
        compiler_params=pltpu.CompilerParams(
            dimension_semantics=("arbitrary",), vmem_limit_bytes=VMEM_LIMIT),
        name="smixer",
    )(tab, sinks, zs, st, h0, kc, vc, xs, w_out, bucket, cw, cb, wg, gab, gxb, lam, nr, na, ko, vo)


U_LO = LAST_P - HALO
U_HI = LAST_P + DEC_BATCH
N_UST = U_HI - U_LO


def _ffn_kernel(x1_ref, halo_ref, x1s_ref, st0_ref, st1_ref, gn_ref, gf_ref, wu_ref, wg_ref, cw_ref, cb_ref,
                wd_ref, y_ref, ys_ref, ust_ref, h2_scr, ubuf, abuf):
    i = pl.program_id(0)
    j = pl.program_id(1)

    @pl.when(j == 0)
    def _():
        h2_scr[0:FH] = _rms(halo_ref[...], gn_ref[...]).astype(BF16)
        h2_scr[FH:FH + TM] = _rms(x1_ref[...], gn_ref[...]).astype(BF16)
        y_ref[...] = jnp.zeros((TM, D_MODEL), F32)

        @pl.when(i == N_RT - 1)
        def _():
            h2_scr[FH + LAST_P:FH + U_HI] = _rms(x1s_ref[...], gn_ref[...]).astype(BF16)

    ubuf[...] = jnp.dot(h2_scr[...], wu_ref[...], preferred_element_type=F32)
    gate = jnp.dot(h2_scr[FH:FH + TM], wg_ref[...], preferred_element_type=F32)
    cw = cw_ref[...]
    cb = cb_ref[...]
    for c in range(TF // KC):
        cs = slice(c * KC, (c + 1) * KC)
        tap = lambda r0, n: ubuf[r0:r0 + n, cs]
        uc = (cb[:, cs] + tap(FH - 2, TM) * cw[0:1, cs] + tap(FH - 1, TM) * cw[1:2, cs] + tap(FH, TM) * cw[2:3, cs])
        act = jax.nn.gelu(uc) * gate[:, cs]
        abuf[:, cs] = act.astype(BF16)
        ucs = (cb[:, cs] + st0_ref[:, cs] * cw[0:1, cs] + st1_ref[:, cs] * cw[1:2, cs]
               + tap(FH + LAST_P, DEC_BATCH) * cw[2:3, cs])
        act_s = jax.nn.gelu(ucs) * gate[LAST_P:U_HI, cs]
        abuf[LAST_P:U_HI, cs] = jnp.where(i == N_RT - 1, act_s, act[LAST_P:U_HI]).astype(BF16)
        y_ref[...] += jnp.dot(abuf[:, cs], wd_ref[cs, :], preferred_element_type=F32)
    ust_ref[...] = ubuf[FH + U_LO:FH + U_HI]

    @pl.when(j == N_FT - 1)
    def _():
        @pl.when(i == N_RT - 1)
        def _():
            ys_ref[...] = _rms(x1s_ref[...] + y_ref[LAST_P:U_HI], gf_ref[...])

        y_ref[...] = _rms(x1_ref[...] + y_ref[...], gf_ref[...])


def _ffn(x1, x1s, st, gn, gf, wu, wg, cw, cb, wd):
    def halo_idx(i, j):
        return (jnp.where(i == 0, R_ALL // FH - 1, i * (TM // FH) - 1), 0)

    return pl.pallas_call(
        _ffn_kernel,
        grid=(N_RT, N_FT),
        in_specs=[
            pl.BlockSpec((TM, D_MODEL), lambda i, j: (i, 0)),
            pl.BlockSpec((FH, D_MODEL), halo_idx),
            pl.BlockSpec((DEC_BATCH, D_MODEL), lambda i, j: (0, 0)),
            pl.BlockSpec((DEC_BATCH, TF), lambda i, j: (0, j)),
            pl.BlockSpec((DEC_BATCH, TF), lambda i, j: (0, N_FT + j)),
            pl.BlockSpec((1, D_MODEL), lambda i, j: (0, 0)),
            pl.BlockSpec((1, D_MODEL), lambda i, j: (0, 0)),
            pl.BlockSpec((D_MODEL, TF), lambda i, j: (0, j)),
            pl.BlockSpec((D_MODEL, TF), lambda i, j: (0, j)),
            pl.BlockSpec((FFN_CONV, TF), lambda i, j: (0, j)),
            pl.BlockSpec((1, TF), lambda i, j: (0, j)),
            pl.BlockSpec((TF, D_MODEL), lambda i, j: (j, 0)),
        ],
        out_specs=[
            pl.BlockSpec((TM, D_MODEL), lambda i, j: (i, 0)),
            pl.BlockSpec((DEC_BATCH, D_MODEL), lambda i, j: (0, 0)),
            pl.BlockSpec((N_UST, TF), lambda i, j: (i, j)),
        ],
        out_shape=[
            jax.ShapeDtypeStruct((SEQ, D_MODEL), F32),
            jax.ShapeDtypeStruct((DEC_BATCH, D_MODEL), F32),
            jax.ShapeDtypeStruct((N_RT * N_UST, D_FF), F32),
        ],
        scratch_shapes=[
            pltpu.VMEM((FH + TM, D_MODEL), BF16),
            pltpu.VMEM((FH + TM, TF), F32),
            pltpu.VMEM((TM, TF), BF16),
        ],
        compiler_params=pltpu.CompilerParams(
            dimension_semantics=("arbitrary", "arbitrary"), vmem_limit_bytes=VMEM_LIMIT_BIG),
        name="ffn",
    )(x1, x1, x1s, st, st, gn, gf, wu, wg, cw, cb, wd)


def _gate_weights(wa, wx):
    per = GW // RNN_BLOCK
    eye = jnp.eye(per, dtype=wa.dtype)

    def bd(w):
        w = w.reshape(N_GW, per, RNN_BLOCK, RNN_BLOCK)
        return jnp.einsum('gpcd,pq->gpcqd', w, eye).reshape(N_GW, GW, GW)

    return jnp.concatenate([bd(wa), bd(wx)], axis=-1).astype(BF16)


def kernel(x_prompt, x_sample, state_rnn_conv, state_rnn_h, cache_k_win, cache_v_win, state_ffn_conv,
           meta_tokens, rel_bias_table, norm_mix, w_in, rnn_conv_w, rnn_conv_b, gate_a_w, gate_a_b,
           gate_x_w, gate_x_b, rnn_lambda, attn_sinks, norm_rnn_out, norm_attn_out, w_out, norm_ffn,
           w_up, w_gate, ffn_conv_w, ffn_conv_b, w_down, norm_final):
    l = 0
    xp = x_prompt[0]
    xs = x_sample[:, 0, :]
    xt = jnp.concatenate([xs, jnp.zeros((N_PAD, D_MODEL), F32), meta_tokens], axis=0)
    row = lambda v: v.reshape(1, -1)

    qi = np.arange(BLOCK)[:, None]
    sj = np.arange(2 * BLOCK)[None, :]
    bucket_p = jnp.asarray(_rel_buckets(BLOCK + qi - sj))
    pos = (np.arange(SB * WINDOW * N_KV) >> 1) & (WINDOW - 1)
    bucket_s = jnp.asarray(np.tile(_rel_buckets(WINDOW - pos)[None, :], (HALO, 1)))

    wg = _gate_weights(gate_a_w[l], gate_x_w[l])
    seq_w = (rnn_conv_w[l], row(rnn_conv_b[l]), wg, row(gate_a_b[l]), row(gate_x_b[l]), row(rnn_lambda[l]),
             row(norm_rnn_out[l]), row(norm_attn_out[l]))

    kc = cache_k_win[l].reshape(DEC_BATCH * WINDOW * N_KV, HEAD_DIM)
    vc = cache_v_win[l].reshape(DEC_BATCH * WINDOW * N_KV, HEAD_DIM)
    x1, zs, h_last, kv_last, xr_tail, w_out_b, w_up_b, w_gate_b, w_down_b, k_slid, v_slid = _front(
        rel_bias_table, attn_sinks[l], xp, xt, row(norm_mix[l]), w_in[l], w_out[l], bucket_p, *seq_w,
        w_up[l], w_gate[l], w_down[l], cache_k_win[l], cache_v_win[l])
    x1s, conv_s, h_s, k_s, v_s = _smixer(
        rel_bias_table, attn_sinks[l], zs,
        jnp.swapaxes(state_rnn_conv[l], 0, 1), state_rnn_h[l],
        kc, vc, xs, w_out_b, bucket_s, *seq_w, k_slid, v_slid)
    y_p, y_s, ust = _ffn(x1, x1s, state_ffn_conv[l].reshape(DEC_BATCH, (FFN_CONV - 1) * D_FF),
                         row(norm_ffn[l]), row(norm_final), w_up_b, w_gate_b,
                         ffn_conv_w[l], row(ffn_conv_b[l]), w_down_b)

    ust = ust[(N_RT - 1) * N_UST:]
    p_states = (
        xr_tail[HALO - (RNN_CONV - 1):HALO][None, None],
        h_last[0:1][None],
        kv_last[:, :D_KV].reshape(1, 1, WINDOW, N_KV, HEAD_DIM),
        kv_last[:, D_KV:].reshape(1, 1, WINDOW, N_KV, HEAD_DIM),
        ust[HALO - (FFN_CONV - 1):HALO][None, None],
    )
    s_states = (
        jnp.swapaxes(conv_s, 0, 1)[None],
        h_s[None],
        k_s[None],
        v_s[None],
        jnp.stack([state_ffn_conv[l][:, FFN_CONV - 2, :], ust[HALO:]], axis=1)[None],
    )
    return (y_p[None], y_s[:, None, :]) + p_states + s_states
```

```python
import math

import numpy as np
import jax
import jax.numpy as jnp
from jax import lax
from jax.experimental import pallas as pl
from jax.experimental.pallas import tpu as pltpu

F32 = jnp.float32
BF16 = jnp.bfloat16

D_MODEL = 2048
SEQ = 8192
DEC_BATCH = 128
D_RNN = 1024
N_RNN_BLOCKS = 16
RNN_BLOCK = D_RNN // N_RNN_BLOCKS
RNN_CONV = 4
LRU_C = 8.0
N_HEADS = 8
HEAD_DIM = 128
N_KV = 2
GROUP = N_HEADS // N_KV
D_ATTN = N_HEADS * HEAD_DIM
WINDOW = 128
BLOCK = 128
NUM_BUCKETS = 32
MAX_DISTANCE = 128
D_FF = 3 * D_MODEL
FFN_CONV = 3
N_META = 16
EPS = 1e-6
NEG = -1e30
D_KV = N_KV * HEAD_DIM
D_IN = 2 * D_RNN + D_ATTN + 2 * D_KV
SCALE = HEAD_DIM ** -0.5
O_GR = D_RNN
O_Q = 2 * D_RNN
O_KV = 2 * D_RNN + D_ATTN

N_PAD = BLOCK - N_META
R_TAIL = DEC_BATCH + BLOCK
R_ALL = SEQ + R_TAIL
N_PBLK = SEQ // BLOCK

FB = 2 * BLOCK
N_FSTEP = (N_PBLK + 2) // 2
PC = 512
N_IN_EARLY = 5
N_STAGE = 4
N_SLAB = N_FSTEP - 1

TM = 640
N_RT = (SEQ + DEC_BATCH) // TM
LAST_P = SEQ - (N_RT - 1) * TM
TF = 768
N_FT = D_FF // TF
KC = 256
HALO = 8
FH = 16
GW = 256
N_GW = D_RNN // GW
SB = 16
N_VP = 6
VMEM_LIMIT = 56 * 1024 * 1024
VMEM_LIMIT_BIG = 60 * 1024 * 1024


def _rms(x, g):
    return x * lax.rsqrt(jnp.mean(x * x, axis=-1, keepdims=True) + EPS) * g


def _rel_buckets(d):
    d = np.maximum(d, 0)
    exact = NUM_BUCKETS // 2
    ratio = np.maximum(d, 1).astype(np.float32) / np.float32(exact)
    large = exact + (np.log(ratio) / np.float32(math.log(MAX_DISTANCE / exact))
                     * np.float32(NUM_BUCKETS - exact)).astype(np.int32)
    large = np.minimum(large, NUM_BUCKETS - 1)
    return np.where(d < exact, d, large).astype(np.int32)


def _table_lookup(bucket, tab_ref):
    outs = [jnp.zeros(bucket.shape, F32) for _ in range(N_HEADS)]
    for b in range(NUM_BUCKETS):
        hit = bucket == b
        outs = [jnp.where(hit, tab_ref[b, h], o) for h, o in enumerate(outs)]
    return outs


def _gates(xc, wg_ref, gab, gxb, lam):
    xcb = xc.astype(BF16)
    ga, gx = [], []
    for j in range(N_GW):
        gj = jnp.dot(xcb[:, GW * j:GW * (j + 1)], wg_ref[j], preferred_element_type=F32)
        ga.append(gj[:, :GW])
        gx.append(gj[:, GW:])
    r = jax.nn.sigmoid(jnp.concatenate(ga, axis=1) + gab)
    i = jax.nn.sigmoid(jnp.concatenate(gx, axis=1) + gxb)
    log_a = -LRU_C * r * jax.nn.softplus(-lam)
    a = jnp.exp(log_a)
    t = 1.0 - a * a
    b = jnp.where(t > 0.0, t * lax.rsqrt(t), 0.0) * i * xc
    return a, b


def _front_kernel(tab_ref, sink_ref, xa_ref, xb_ref, xt_ref, gm_ref, win_f, wout_f, bucket_ref,
                  cw_ref, wg_ref, vp_ref, wu_f, wgt_f, wd_f,
                  x1_ref, zs_ref, hlast_ref, kvlast_ref, xrt_ref, wout_hbm, wu_b, wgt_b, wd_b,
                  win_ref, wout_ref, zp, zb_o, lhs_p, xsp, mixp, xbuf, h_scr, kvbuf, bias_scr,
                  su, sg, sd, tu, tg, td, sem_in, sem_out, sem_w):
    k = pl.program_id(0)
    last = N_FSTEP - 1
    cb_v, gab_v, gxb_v, lam_v, nr_v, na_v = (vp_ref[r:r + 1] for r in range(N_VP))

    stages = ((wu_f, su, tu, wu_b), (wgt_f, sg, tg, wgt_b), (wd_f, sd, td, wd_b))

    def slab_in(s):
        return [pltpu.make_async_copy(w.at[pl.ds(s * st.shape[0], st.shape[0])], st, sem_in.at[n])
                for n, (w, st, _, _) in enumerate(stages)]

    def slab_out(s):
        return [pltpu.make_async_copy(t, w.at[pl.ds(s * t.shape[0], t.shape[0])], sem_out.at[n])
                for n, (_, _, t, w) in enumerate(stages)]

    @pl.when(k == 0)
    def _():
        for c in slab_in(0):
            c.start()

    @pl.when(k >= 1)
    def _():
        for c in slab_out(k - 1):
            c.wait()

    @pl.when(k < N_SLAB)
    def _():
        for c in slab_in(k):
            c.wait()
        for _, st, t, _ in stages:
            t[...] = st[...].astype(BF16)
        for c in slab_out(k):
            c.start()

    @pl.when(k < N_SLAB - 1)
    def _():
        for c in slab_in(k + 1):
            c.start()

    wout_copy = pltpu.make_async_copy(wout_ref, wout_hbm, sem_w.at[2 * N_STAGE])

    @pl.when(k == 0)
    def _():
        streams = ((win_f, win_ref, zp, 0), (wout_f, wout_ref, xsp, N_STAGE))
        rows = FB // N_STAGE
        n = D_MODEL // rows

        def chunk(stream, c):
            w_f, _, buf, sem0 = stream
            slot = c % N_STAGE
            return pltpu.make_async_copy(w_f.at[pl.ds(c * rows, rows)], buf.at[pl.ds(slot * rows, rows)],
                                         sem_w.at[sem0 + slot])

        for c in range(N_STAGE - 1):
            for stream in streams:
                chunk(stream, c).start()
        for c in range(n):
            for stream in streams:
                if c + N_STAGE - 1 < n:
                    chunk(stream, c + N_STAGE - 1).start()
                chunk(stream, c).wait()
                _, w_b, buf, _ = stream
                w_b[c * rows:(c + 1) * rows] = buf[pl.ds((c % N_STAGE) * rows, rows)].astype(BF16)
        wout_copy.start()

    def inproj_chunk(c):
        def run():
            cs = slice(c * PC, (c + 1) * PC)
            zp[:, cs] = jnp.dot(lhs_p[...], win_ref[:, cs], preferred_element_type=F32)
        return run

    def outproj_chunk(c):
        def run():
            cs = slice(c * PC, (c + 1) * PC)
            x1_ref[:, cs] = xsp[:, cs] + jnp.dot(mixp[...], wout_ref[:, cs], preferred_element_type=F32)
        return run

    @pl.when(k == 0)
    def _():
        prefix = xt_ref[DEC_BATCH:R_TAIL]
        zb_o[...] = jnp.dot(_rms(prefix, gm_ref[...]).astype(BF16), win_ref[...], preferred_element_type=F32)
        xsp[0:BLOCK] = jnp.zeros((BLOCK, D_MODEL), F32)
        xsp[BLOCK:FB] = prefix
        mixp[...] = jnp.zeros((FB, D_MODEL), BF16)
        xbuf[0:HALO] = jnp.zeros((HALO, D_RNN), F32)
        h_scr[...] = jnp.zeros((HALO, D_RNN), F32)
        kvbuf[...] = jnp.zeros((BLOCK, 2 * D_KV), F32)
        for h, bias in enumerate(_table_lookup(bucket_ref[...], tab_ref)):
            bias_scr[h] = bias

    def mixer(zb, is_prefix, first_key, fillers):
        fill = iter(fillers)

        def between():
            f = next(fill, None)
            if f is not None:
                f()

        between()
        xr = zb[:, 0:D_RNN]
        xbuf[HALO:HALO + BLOCK] = xr
        cw = cw_ref[...]
        xc = cb_v
        for j in range(RNN_CONV - 1):
            lo = HALO - (RNN_CONV - 1) + j
            xc = xc + xbuf[lo:lo + BLOCK] * cw[j:j + 1]
        xc = xc + xr * cw[RNN_CONV - 1:RNN_CONV]
        xbuf[0:HALO] = xr[BLOCK - HALO:BLOCK]

        a, b = _gates(xc, wg_ref, gab_v, gxb_v, lam_v)
        between()
        h = h_scr[0:1]
        if is_prefix is not None:
            row = lax.broadcasted_iota(jnp.int32, (BLOCK, D_RNN), 0)
            b = jnp.where(jnp.logical_and(is_prefix, row < N_PAD), 0.0, b)
            h = jnp.where(is_prefix, 0.0, h)

        ng = BLOCK // HALO
        a3 = a.reshape(ng, HALO, D_RNN)
        b3 = b.reshape(ng, HALO, D_RNN)
        sub = lax.broadcasted_iota(jnp.int32, (ng, HALO, D_RNN), 1)
        sh = 1
        while sh < HALO:
            a_prev = pltpu.roll(a3, sh, 1)
            b_prev = pltpu.roll(b3, sh, 1)
            m = sub >= sh
            b3 = jnp.where(m, a3 * b_prev + b3, b3)
            a3 = jnp.where(m, a3 * a_prev, a3)
            sh *= 2
        hs = []
        for g in range(ng):
            hg = a3[g] * h + b3[g]
            hs.append(hg)
            h = hg[HALO - 1:HALO]
        h_all = jnp.concatenate(hs, axis=0)
        h_scr[...] = jnp.broadcast_to(h, (HALO, D_RNN))
        between()
        y_rnn = h_all * jax.nn.gelu(zb[:, O_GR:O_GR + D_RNN])
        mix_a = _rms(y_rnn, nr_v).astype(BF16)

        q = zb[:, O_Q:O_Q + D_ATTN]
        kv = zb[:, O_KV:O_KV + 2 * D_KV]
        kvp = kvbuf[...]
        kvbuf[...] = kv
        col = lax.broadcasted_iota(jnp.int32, (BLOCK, 2 * BLOCK), 1)
        rowq = lax.broadcasted_iota(jnp.int32, (BLOCK, 2 * BLOCK), 0)
        d = BLOCK + rowq - col
        mask = (d >= 0) & (d < WINDOW) & (col >= first_key)
        outs = []
        for kh in range(N_KV):
            between()
            ks = slice(kh * HEAD_DIM, (kh + 1) * HEAD_DIM)
            vs = slice(D_KV + kh * HEAD_DIM, D_KV + (kh + 1) * HEAD_DIM)
            qs = jnp.concatenate(
                [q[:, (kh * GROUP + g) * HEAD_DIM:(kh * GROUP + g + 1) * HEAD_DIM] for g in range(GROUP)],
                axis=0).astype(BF16)
            kk = jnp.concatenate([kvp[:, ks], kv[:, ks]], axis=0).astype(BF16)
            vv = jnp.concatenate([kvp[:, vs], kv[:, vs]], axis=0).astype(BF16)
            sc = lax.dot_general(qs, kk, (((1,), (1,)), ((), ())), preferred_element_type=F32)
            es, dens = [], []
            for g in range(GROUP):
                hh = kh * GROUP + g
                lg = sc[g * BLOCK:(g + 1) * BLOCK] * SCALE + bias_scr[hh]
                lg = jnp.where(mask, lg, NEG)
                sink = sink_ref[hh]
                mx = jnp.maximum(jnp.max(lg, axis=-1, keepdims=True), sink)
                e = jnp.exp(lg - mx)
                dens.append(jnp.sum(e, axis=-1, keepdims=True) + jnp.exp(sink - mx))
                es.append(e.astype(BF16))
            pv = jnp.dot(jnp.concatenate(es, axis=0), vv, preferred_element_type=F32)
            for g in range(GROUP):
                outs.append(pv[g * BLOCK:(g + 1) * BLOCK] / dens[g])
        between()
        mix_b = _rms(jnp.concatenate(outs, axis=1), na_v).astype(BF16)

        return jnp.concatenate([mix_a, mix_b], axis=1), h, kv, xr

    xa = jnp.where(k == last, xt_ref[0:DEC_BATCH], xa_ref[...])
    xb = xb_ref[...]
    lhs_p[0:BLOCK] = _rms(xa, gm_ref[...]).astype(BF16)
    lhs_p[BLOCK:FB] = _rms(xb, gm_ref[...]).astype(BF16)
    n_in, n_out = D_IN // PC, D_MODEL // PC
    mix_o, h, kv, xr = mixer(zb_o, k == 0, jnp.where(k == 0, BLOCK + N_PAD, 0),
                             [inproj_chunk(c) for c in range(N_IN_EARLY)])
    mixp[BLOCK:FB] = mix_o
    hlast_ref[...] = jnp.broadcast_to(h, (HALO, D_RNN))
    kvlast_ref[...] = kv
    xrt_ref[...] = xr[BLOCK - HALO:BLOCK]

    mix_e = mixer(zp.at[0:BLOCK], None, jnp.where(k == 0, N_PAD, 0),
                  [inproj_chunk(c) for c in range(N_IN_EARLY, n_in)] + [outproj_chunk(c) for c in range(n_out)])[0]
    mixp[0:BLOCK] = mix_e
    xsp[0:BLOCK] = xa
    xsp[BLOCK:FB] = xb
    zb_o[...] = zp[BLOCK:FB]

    @pl.when(k == last)
    def _():
        zs_ref[...] = zp[0:BLOCK]
        wout_copy.wait()


def _front(tab, sinks, xp, xt, gm, w_in, w_out, bucket, cw, wg, vp, w_up, w_gate, w_down):
    ffn_w = (w_up, w_gate, w_down)
    slab = lambda w: (w.shape[0] // N_SLAB, w.shape[1])
    hbm = pl.BlockSpec(memory_space=pl.ANY)
    vec = lambda n: pl.BlockSpec((1, n), lambda k: (0, 0))
    smem = pl.BlockSpec(memory_space=pltpu.SMEM)
    once = lambda shape: pl.BlockSpec(shape, lambda k: (0,) * len(shape), pipeline_mode=pl.Buffered(1))
    return pl.pallas_call(
        _front_kernel,
        grid=(N_FSTEP,),
        in_specs=[
            smem, smem,
            pl.BlockSpec((BLOCK, D_MODEL), lambda k: (jnp.minimum(2 * k, N_PBLK - 1), 0)),
            pl.BlockSpec((BLOCK, D_MODEL), lambda k: (jnp.minimum(2 * k + 1, N_PBLK - 1), 0)),
            once((R_TAIL, D_MODEL)),
            vec(D_MODEL),
            hbm, hbm,
            pl.BlockSpec((BLOCK, 2 * BLOCK), lambda k: (0, 0)),
            pl.BlockSpec((RNN_CONV, D_RNN), lambda k: (0, 0)),
            pl.BlockSpec((N_GW, GW, 2 * GW), lambda k: (0, 0, 0)),
            pl.BlockSpec((HALO, D_RNN), lambda k: (0, 0)),
            hbm, hbm, hbm,
        ],
        out_specs=[
            pl.BlockSpec((FB, D_MODEL), lambda k: (jnp.where(k == 0, SEQ // FB, k - 1), 0)),
            pl.BlockSpec((BLOCK, D_IN), lambda k: (0, 0)),
            pl.BlockSpec((HALO, D_RNN), lambda k: (0, 0)),
            pl.BlockSpec((BLOCK, 2 * D_KV), lambda k: (0, 0)),
            pl.BlockSpec((HALO, D_RNN), lambda k: (0, 0)),
            hbm, hbm, hbm, hbm,
        ],
        out_shape=[
            jax.ShapeDtypeStruct((R_ALL, D_MODEL), F32),
            jax.ShapeDtypeStruct((DEC_BATCH, D_IN), F32),
            jax.ShapeDtypeStruct((HALO, D_RNN), F32),
            jax.ShapeDtypeStruct((BLOCK, 2 * D_KV), F32),
            jax.ShapeDtypeStruct((HALO, D_RNN), F32),
        ] + [jax.ShapeDtypeStruct(w.shape, BF16) for w in (w_out,) + ffn_w],
        scratch_shapes=[
            pltpu.VMEM(w_in.shape, BF16),
            pltpu.VMEM(w_out.shape, BF16),
            pltpu.VMEM((FB, D_IN), F32),
            pltpu.VMEM((BLOCK, D_IN), F32),
            pltpu.VMEM((FB, D_MODEL), BF16),
            pltpu.VMEM((FB, D_MODEL), F32),
            pltpu.VMEM((FB, D_MODEL), BF16),
            pltpu.VMEM((HALO + BLOCK, D_RNN), F32),
            pltpu.VMEM((HALO, D_RNN), F32),
            pltpu.VMEM((BLOCK, 2 * D_KV), F32),
            pltpu.VMEM((N_HEADS, BLOCK, 2 * BLOCK), F32),
        ] + [pltpu.VMEM(slab(w), F32) for w in ffn_w] + [pltpu.VMEM(slab(w), BF16) for w in ffn_w] + [
            pltpu.SemaphoreType.DMA((len(ffn_w),)),
            pltpu.SemaphoreType.DMA((len(ffn_w),)),
            pltpu.SemaphoreType.DMA((2 * N_STAGE + 1,)),
        ],
        compiler_params=pltpu.CompilerParams(
            dimension_semantics=("arbitrary",), vmem_limit_bytes=VMEM_LIMIT_BIG),
        name="front",
    )(tab, sinks, xp, xp, xt, gm, w_in, w_out, bucket, cw, wg, vp, *ffn_w)


def _smixer_kernel(tab_ref, sink_ref, z_ref, st_ref, h0_ref, kc_ref, vc_ref, xs_ref, wout_hbm,
                   bucket_ref, cw_ref, wg_ref, vp_ref,
                   x1s_ref, conv_ref, h_ref, ko_ref, vo_ref, bias_scr, mix_scr, wout_ref, sem_w):
    c = pl.program_id(0)
    cb_v, gab_v, gxb_v, lam_v, nr_v, na_v = (vp_ref[r:r + 1] for r in range(N_VP))
    nrow = N_HEADS * SB
    ncol = SB * WINDOW * N_KV
    wout_copy = pltpu.make_async_copy(wout_hbm, wout_ref, sem_w)

    @pl.when(c == 0)
    def _():
        wout_copy.start()
        col = lax.broadcasted_iota(jnp.int32, (SB, ncol), 1)
        row = lax.broadcasted_iota(jnp.int32, (SB, ncol), 0)
        own = ((col >> 8) == row) & (((col >> 1) & (WINDOW - 1)) >= 1)
        for h, bias in enumerate(_table_lookup(bucket_ref[...], tab_ref)):
            ok = own & ((col & 1) == h // GROUP)
            bias_scr[h * SB:(h + 1) * SB] = jnp.where(ok, jnp.concatenate([bias] * (SB // HALO), axis=0), NEG)

    r0 = pl.multiple_of(c * SB, SB)

    xr = z_ref[:, 0:D_RNN]
    cw = cw_ref[...]
    xc = cb_v
    for j in range(RNN_CONV - 1):
        xc = xc + st_ref[j] * cw[j:j + 1]
    xc = xc + xr * cw[RNN_CONV - 1:RNN_CONV]
    for j in range(1, RNN_CONV - 1):
        conv_ref[j - 1] = st_ref[j]
    conv_ref[RNN_CONV - 2] = xr
    a, b = _gates(xc, wg_ref, gab_v, gxb_v, lam_v)
    h = a * h0_ref[...] + b
    h_ref[...] = h
    y_rnn = h * jax.nn.gelu(z_ref[:, O_GR:O_GR + D_RNN])
    mix_scr[pl.ds(r0, SB), 0:D_RNN] = _rms(y_rnn, nr_v).astype(BF16)

    q = z_ref[:, O_Q:O_Q + D_ATTN]
    kv = z_ref[:, O_KV:O_KV + 2 * D_KV]
    qs = jnp.concatenate([q[:, h * HEAD_DIM:(h + 1) * HEAD_DIM] for h in range(N_HEADS)], axis=0).astype(BF16)
    new_rows = lambda off: jnp.concatenate(
        [kv[:, off + (h // GROUP) * HEAD_DIM:off + (h // GROUP + 1) * HEAD_DIM] for h in range(N_HEADS)],
        axis=0).astype(BF16).astype(F32)
    k_new = new_rows(0)
    v_new = new_rows(D_KV)
    sc = lax.dot_general(qs, kc_ref[...].astype(BF16), (((1,), (1,)), ((), ())), preferred_element_type=F32)
    lg = sc * SCALE + bias_scr[...]
    rh = lax.broadcasted_iota(jnp.int32, (nrow, 1), 0) >> 4
    sink = jnp.zeros((nrow, 1), F32)
    bias_new = jnp.zeros((nrow, 1), F32)
    for h in range(N_HEADS):
        sink = jnp.where(rh == h, sink_ref[h], sink)
        bias_new = jnp.where(rh == h, tab_ref[0, h], bias_new)
    lg_new = jnp.sum(qs.astype(F32) * k_new, axis=-1, keepdims=True) * SCALE + bias_new
    mx = jnp.maximum(jnp.maximum(jnp.max(lg, axis=-1, keepdims=True), lg_new), sink)
    e = jnp.exp(lg - mx)
    e_new = jnp.exp(lg_new - mx)
    den = jnp.sum(e, axis=-1, keepdims=True) + e_new + jnp.exp(sink - mx)
    pv = jnp.dot(e.astype(BF16), vc_ref[...].astype(BF16), preferred_element_type=F32)
    o = (pv + e_new.astype(BF16).astype(F32) * v_new) / den
    y_attn = jnp.concatenate([o[h * SB:(h + 1) * SB] for h in range(N_HEADS)], axis=1)
    mix_scr[pl.ds(r0, SB), D_RNN:] = _rms(y_attn, na_v).astype(BF16)

    per = WINDOW * N_KV
    slide = lambda ref: pltpu.roll(ref[...].reshape(SB, per, HEAD_DIM), per - N_KV, 1).reshape(SB * per, HEAD_DIM)
    ko_ref[...] = slide(kc_ref)
    vo_ref[...] = slide(vc_ref)
    for bi in range(SB):
        for kh in range(N_KV):
            r = (bi + 1) * per - N_KV + kh
            ko_ref[r:r + 1, :] = kv[bi:bi + 1, kh * HEAD_DIM:(kh + 1) * HEAD_DIM]
            vo_ref[r:r + 1, :] = kv[bi:bi + 1, D_KV + kh * HEAD_DIM:D_KV + (kh + 1) * HEAD_DIM]

    @pl.when(c == DEC_BATCH // SB - 1)
    def _():
        wout_copy.wait()
        x1s_ref[...] = xs_ref[...] + jnp.dot(mix_scr[...], wout_ref[...], preferred_element_type=F32)


def _smixer(tab, sinks, zs, st, h0, kc, vc, xs, w_out, bucket, cw, wg, vp):
    smem = pl.BlockSpec(memory_space=pltpu.SMEM)
    cache = pl.BlockSpec((SB * WINDOW * N_KV, HEAD_DIM), lambda c: (c, 0))
    return pl.pallas_call(
        _smixer_kernel,
        grid=(DEC_BATCH // SB,),
        in_specs=[
            smem, smem,
            pl.BlockSpec((SB, D_IN), lambda c: (c, 0)),
            pl.BlockSpec((RNN_CONV - 1, SB, D_RNN), lambda c: (0, c, 0)),
            pl.BlockSpec((SB, D_RNN), lambda c: (c, 0)),
            cache, cache,
            pl.BlockSpec((DEC_BATCH, D_MODEL), lambda c: (0, 0)),
            pl.BlockSpec(memory_space=pl.ANY),
            pl.BlockSpec((HALO, SB * WINDOW * N_KV), lambda c: (0, 0)),
            pl.BlockSpec((RNN_CONV, D_RNN), lambda c: (0, 0)),
            pl.BlockSpec((N_GW, GW, 2 * GW), lambda c: (0, 0, 0)),
            pl.BlockSpec((HALO, D_RNN), lambda c: (0, 0)),
        ],
        out_specs=[
            pl.BlockSpec((DEC_BATCH, D_MODEL), lambda c: (0, 0)),
            pl.BlockSpec((RNN_CONV - 1, SB, D_RNN), lambda c: (0, c, 0)),
            pl.BlockSpec((SB, D_RNN), lambda c: (c, 0)),
            cache, cache,
        ],
        out_shape=[
            jax.ShapeDtypeStruct((DEC_BATCH, D_MODEL), F32),
            jax.ShapeDtypeStruct((RNN_CONV - 1, DEC_BATCH, D_RNN), F32),
            jax.ShapeDtypeStruct((DEC_BATCH, D_RNN), F32),
            jax.ShapeDtypeStruct((DEC_BATCH * WINDOW * N_KV, HEAD_DIM), F32),
            jax.ShapeDtypeStruct((DEC_BATCH * WINDOW * N_KV, HEAD_DIM), F32),
        ],
        scratch_shapes=[
            pltpu.VMEM((N_HEADS * SB, SB * WINDOW * N_KV), F32),
            pltpu.VMEM((DEC_BATCH, D_MODEL), BF16),
            pltpu.VMEM((D_MODEL, D_MODEL), BF16),
            pltpu.SemaphoreType.DMA(()),
        ],
        compiler_params=pltpu.CompilerParams(
            dimension_semantics=("arbitrary",), vmem_limit_bytes=VMEM_LIMIT),
        name="smixer",
    )(tab, sinks, zs, st, h0, kc, vc, xs, w_out, bucket, cw, wg, vp)


U_LO = LAST_P - HALO
U_HI = LAST_P + DEC_BATCH
N_UST = U_HI - U_LO


def _ffn_kernel(x1_ref, halo_ref, x1s_ref, st0_ref, st1_ref, gn_ref, gf_ref, wu_ref, wg_ref, cw_ref, cb_ref,
                wd_ref, y_ref, ys_ref, ust_ref, h2_scr, ubuf, abuf):
    i = pl.program_id(0)
    j = pl.program_id(1)

    @pl.when(j == 0)
    def _():
        h2_scr[0:FH] = _rms(halo_ref[...], gn_ref[...]).astype(BF16)
        h2_scr[FH:FH + TM] = _rms(x1_ref[...], gn_ref[...]).astype(BF16)
        y_ref[...] = jnp.zeros((TM, D_MODEL), F32)

        @pl.when(i == N_RT - 1)
        def _():
            h2_scr[FH + LAST_P:FH + U_HI] = _rms(x1s_ref[...], gn_ref[...]).astype(BF16)

    ubuf[...] = jnp.dot(h2_scr[...], wu_ref[...], preferred_element_type=F32)
    gate = jnp.dot(h2_scr[FH:FH + TM], wg_ref[...], preferred_element_type=F32)
    cw = cw_ref[...]
    cb = cb_ref[...]
    for c in range(TF // KC):
        cs = slice(c * KC, (c + 1) * KC)
        tap = lambda r0, n: ubuf[r0:r0 + n, cs]
        uc = (cb[:, cs] + tap(FH - 2, TM) * cw[0:1, cs] + tap(FH - 1, TM) * cw[1:2, cs] + tap(FH, TM) * cw[2:3, cs])
        act = jax.nn.gelu(uc) * gate[:, cs]
        abuf[:, cs] = act.astype(BF16)
        ucs = (cb[:, cs] + st0_ref[:, cs] * cw[0:1, cs] + st1_ref[:, cs] * cw[1:2, cs]
               + tap(FH + LAST_P, DEC_BATCH) * cw[2:3, cs])
        act_s = jax.nn.gelu(ucs) * gate[LAST_P:U_HI, cs]
        abuf[LAST_P:U_HI, cs] = jnp.where(i == N_RT - 1, act_s, act[LAST_P:U_HI]).astype(BF16)
        y_ref[...] += jnp.dot(abuf[:, cs], wd_ref[cs, :], preferred_element_type=F32)
    ust_ref[...] = ubuf[FH + U_LO:FH + U_HI]

    @pl.when(j == N_FT - 1)
    def _():
        @pl.when(i == N_RT - 1)
        def _():
            ys_ref[...] = _rms(x1s_ref[...] + y_ref[LAST_P:U_HI], gf_ref[...])

        y_ref[...] = _rms(x1_ref[...] + y_ref[...], gf_ref[...])


def _ffn(x1, x1s, st, gn, gf, wu, wg, cw, cb, wd):
    def halo_idx(i, j):
        return (jnp.where(i == 0, R_ALL // FH - 1, i * (TM // FH) - 1), 0)

    return pl.pallas_call(
        _ffn_kernel,
        grid=(N_RT, N_FT),
        in_specs=[
            pl.BlockSpec((TM, D_MODEL), lambda i, j: (i, 0)),
            pl.BlockSpec((FH, D_MODEL), halo_idx),
            pl.BlockSpec((DEC_BATCH, D_MODEL), lambda i, j: (0, 0)),
            pl.BlockSpec((DEC_BATCH, TF), lambda i, j: (0, j)),
            pl.BlockSpec((DEC_BATCH, TF), lambda i, j: (0, N_FT + j)),
            pl.BlockSpec((1, D_MODEL), lambda i, j: (0, 0)),
            pl.BlockSpec((1, D_MODEL), lambda i, j: (0, 0)),
            pl.BlockSpec((D_MODEL, TF), lambda i, j: (0, j)),
            pl.BlockSpec((D_MODEL, TF), lambda i, j: (0, j)),
            pl.BlockSpec((FFN_CONV, TF), lambda i, j: (0, j)),
            pl.BlockSpec((1, TF), lambda i, j: (0, j)),
            pl.BlockSpec((TF, D_MODEL), lambda i, j: (j, 0)),
        ],
        out_specs=[
            pl.BlockSpec((TM, D_MODEL), lambda i, j: (i, 0)),
            pl.BlockSpec((DEC_BATCH, D_MODEL), lambda i, j: (0, 0)),
            pl.BlockSpec((N_UST, TF), lambda i, j: (i, j)),
        ],
        out_shape=[
            jax.ShapeDtypeStruct((SEQ, D_MODEL), F32),
            jax.ShapeDtypeStruct((DEC_BATCH, D_MODEL), F32),
            jax.ShapeDtypeStruct((N_RT * N_UST, D_FF), F32),
        ],
        scratch_shapes=[
            pltpu.VMEM((FH + TM, D_MODEL), BF16),
            pltpu.VMEM((FH + TM, TF), F32),
            pltpu.VMEM((TM, TF), BF16),
        ],
        compiler_params=pltpu.CompilerParams(
            dimension_semantics=("arbitrary", "arbitrary"), vmem_limit_bytes=VMEM_LIMIT_BIG),
        name="ffn",
    )(x1, x1, x1s, st, st, gn, gf, wu, wg, cw, cb, wd)


def _gate_weights(wa, wx):
    per = GW // RNN_BLOCK
    eye = jnp.eye(per, dtype=wa.dtype)

    def bd(w):
        w = w.reshape(N_GW, per, RNN_BLOCK, RNN_BLOCK)
        return jnp.einsum('gpcd,pq->gpcqd', w, eye).reshape(N_GW, GW, GW)

    return jnp.concatenate([bd(wa), bd(wx)], axis=-1).astype(BF16)


def kernel(x_prompt, x_sample, state_rnn_conv, state_rnn_h, cache_k_win, cache_v_win, state_ffn_conv,
           meta_tokens, rel_bias_table, norm_mix, w_in, rnn_conv_w, rnn_conv_b, gate_a_w, gate_a_b,
           gate_x_w, gate_x_b, rnn_lambda, attn_sinks, norm_rnn_out, norm_attn_out, w_out, norm_ffn,
           w_up, w_gate, ffn_conv_w, ffn_conv_b, w_down, norm_final):
    l = 0
    xp = x_prompt[0]
    xs = x_sample[:, 0, :]
    xt = jnp.concatenate([xs, jnp.zeros((N_PAD, D_MODEL), F32), meta_tokens], axis=0)
    row = lambda v: v.reshape(1, -1)

    qi = np.arange(BLOCK)[:, None]
    sj = np.arange(2 * BLOCK)[None, :]
    bucket_p = jnp.asarray(_rel_buckets(BLOCK + qi - sj))
    pos = (np.arange(SB * WINDOW * N_KV) >> 1) & (WINDOW - 1)
    bucket_s = jnp.asarray(np.tile(_rel_buckets(WINDOW - pos)[None, :], (HALO, 1)))

    wg = _gate_weights(gate_a_w[l], gate_x_w[l])
    vecs = (rnn_conv_b[l], gate_a_b[l], gate_x_b[l], rnn_lambda[l], norm_rnn_out[l], norm_attn_out[l])
    vp = jnp.concatenate([jnp.stack(vecs), jnp.zeros((HALO - N_VP, D_RNN), F32)], axis=0)
    seq_w = (rnn_conv_w[l], wg, vp)

    x1, zs, h_last, kv_last, xr_tail, w_out_b, w_up_b, w_gate_b, w_down_b = _front(
        rel_bias_table, attn_sinks[l], xp, xt, row(norm_mix[l]), w_in[l], w_out[l], bucket_p, *seq_w,
        w_up[l], w_gate[l], w_down[l])
    x1s, conv_s, h_s, k_s, v_s = _smixer(
        rel_bias_table, attn_sinks[l], zs,
        jnp.swapaxes(state_rnn_conv[l], 0, 1), state_rnn_h[l],
        cache_k_win[l].reshape(DEC_BATCH * WINDOW * N_KV, HEAD_DIM),
        cache_v_win[l].reshape(DEC_BATCH * WINDOW * N_KV, HEAD_DIM),
        xs, w_out_b, bucket_s, *seq_w)
    y_p, y_s, ust = _ffn(x1, x1s, state_ffn_conv[l].reshape(DEC_BATCH, (FFN_CONV - 1) * D_FF),
                         row(norm_ffn[l]), row(norm_final), w_up_b, w_gate_b,
                         ffn_conv_w[l], row(ffn_conv_b[l]), w_down_b)

    ust = ust[(N_RT - 1) * N_UST:]
    p_states = (
        xr_tail[HALO - (RNN_CONV - 1):HALO][None, None],
        h_last[0:1][None],
        kv_last[:, :D_KV].reshape(1, 1, WINDOW, N_KV, HEAD_DIM),
        kv_last[:, D_KV:].reshape(1, 1, WINDOW, N_KV, HEAD_DIM),
        ust[HALO - (FFN_CONV - 1):HALO][None, None],
    )
    s_states = (
        jnp.swapaxes(conv_s, 0, 1)[None],
        h_s[None],
        k_s.reshape(1, DEC_BATCH, WINDOW, N_KV, HEAD_DIM),
        v_s.reshape(1, DEC_BATCH, WINDOW, N_KV, HEAD_DIM),
        jnp.stack([state_ffn_conv[l][:, FFN_CONV - 2, :], ust[HALO:]], axis=1)[None],
    )
    return (y_p[None], y_s[:, None, :]) + p_states + s_states
```

```python
import math

import numpy as np
import jax
import jax.numpy as jnp
from jax import lax
from jax.experimental import pallas as pl
from jax.experimental.pallas import tpu as pltpu

F32 = jnp.float32
BF16 = jnp.bfloat16

D_MODEL = 2048
SEQ = 8192
DEC_BATCH = 128
D_RNN = 1024
N_RNN_BLOCKS = 16
RNN_BLOCK = D_RNN // N_RNN_BLOCKS
RNN_CONV = 4
LRU_C = 8.0
N_HEADS = 8
HEAD_DIM = 128
N_KV = 2
GROUP = N_HEADS // N_KV
D_ATTN = N_HEADS * HEAD_DIM
WINDOW = 128
BLOCK = 128
NUM_BUCKETS = 32
MAX_DISTANCE = 128
D_FF = 3 * D_MODEL
FFN_CONV = 3
N_META = 16
EPS = 1e-6
NEG = -1e30
D_KV = N_KV * HEAD_DIM
D_IN = 2 * D_RNN + D_ATTN + 2 * D_KV
SCALE = HEAD_DIM ** -0.5
O_GR = D_RNN
O_Q = 2 * D_RNN
O_KV = 2 * D_RNN + D_ATTN

N_PAD = BLOCK - N_META
R_TAIL = DEC_BATCH + BLOCK
R_ALL = SEQ + R_TAIL
N_PBLK = SEQ // BLOCK

FB = 2 * BLOCK
N_FSTEP = (N_PBLK + 2) // 2
PC = 512
N_IN_EARLY = 5
N_STAGE = 8
N_SLAB = N_FSTEP - 1

TM = 640
N_RT = (SEQ + DEC_BATCH) // TM
LAST_P = SEQ - (N_RT - 1) * TM
TF = 768
N_FT = D_FF // TF
KC = 256
HALO = 8
FH = 16
GW = 256
N_GW = D_RNN // GW
SB = 16
N_VP = 6
VMEM_LIMIT = 56 * 1024 * 1024
VMEM_LIMIT_BIG = 60 * 1024 * 1024


def _rms(x, g):
    return x * lax.rsqrt(jnp.mean(x * x, axis=-1, keepdims=True) + EPS) * g


def _rel_buckets(d):
    d = np.maximum(d, 0)
    exact = NUM_BUCKETS // 2
    ratio = np.maximum(d, 1).astype(np.float32) / np.float32(exact)
    large = exact + (np.log(ratio) / np.float32(math.log(MAX_DISTANCE / exact))
                     * np.float32(NUM_BUCKETS - exact)).astype(np.int32)
    large = np.minimum(large, NUM_BUCKETS - 1)
    return np.where(d < exact, d, large).astype(np.int32)


def _table_lookup(bucket, tab_ref):
    outs = [jnp.zeros(bucket.shape, F32) for _ in range(N_HEADS)]
    for b in range(NUM_BUCKETS):
        hit = bucket == b
        outs = [jnp.where(hit, tab_ref[b, h], o) for h, o in enumerate(outs)]
    return outs


def _gates(xc, wg_ref, gab, gxb, lam):
    xcb = xc.astype(BF16)
    ga, gx = [], []
    for j in range(N_GW):
        gj = jnp.dot(xcb[:, GW * j:GW * (j + 1)], wg_ref[j], preferred_element_type=F32)
        ga.append(gj[:, :GW])
        gx.append(gj[:, GW:])
    r = jax.nn.sigmoid(jnp.concatenate(ga, axis=1) + gab)
    i = jax.nn.sigmoid(jnp.concatenate(gx, axis=1) + gxb)
    log_a = -LRU_C * r * jax.nn.softplus(-lam)
    a = jnp.exp(log_a)
    t = 1.0 - a * a
    b = jnp.where(t > 0.0, t * lax.rsqrt(t), 0.0) * i * xc
    return a, b


def _front_kernel(tab_ref, sink_ref, xa_ref, xb_ref, xt_ref, gm_ref, win_f, wout_f, bucket_ref,
                  cw_ref, wg_ref, vp_ref, wu_f, wgt_f, wd_f,
                  x1_ref, zs_ref, hlast_ref, kvlast_ref, xrt_ref, wout_hbm, wu_b, wgt_b, wd_b,
                  win_ref, wout_ref, zp, zb_o, lhs_p, xsp, mixp, xbuf, h_scr, kvbuf, bias_scr,
                  su, sg, sd, tu, tg, td, sem_in, sem_out, sem_w):
    k = pl.program_id(0)
    last = N_FSTEP - 1
    cb_v, gab_v, gxb_v, lam_v, nr_v, na_v = (vp_ref[r:r + 1] for r in range(N_VP))

    stages = ((wu_f, su, tu, wu_b), (wgt_f, sg, tg, wgt_b), (wd_f, sd, td, wd_b))

    def slab_in(s):
        return [pltpu.make_async_copy(w.at[pl.ds(s * st.shape[0], st.shape[0])], st, sem_in.at[n])
                for n, (w, st, _, _) in enumerate(stages)]

    def slab_out(s):
        return [pltpu.make_async_copy(t, w.at[pl.ds(s * t.shape[0], t.shape[0])], sem_out.at[n])
                for n, (_, _, t, w) in enumerate(stages)]

    @pl.when(k == 0)
    def _():
        for c in slab_in(0):
            c.start()

    @pl.when(k >= 1)
    def _():
        for c in slab_out(k - 1):
            c.wait()

    @pl.when(k < N_SLAB)
    def _():
        for c in slab_in(k):
            c.wait()
        for _, st, t, _ in stages:
            t[...] = st[...].astype(BF16)
        for c in slab_out(k):
            c.start()

    @pl.when(k < N_SLAB - 1)
    def _():
        for c in slab_in(k + 1):
            c.start()

    wout_copy = pltpu.make_async_copy(wout_ref, wout_hbm, sem_w.at[2 * N_STAGE])

    @pl.when(k == 0)
    def _():
        streams = ((win_f, win_ref, zp, 0), (wout_f, wout_ref, xsp, N_STAGE))
        rows = FB // N_STAGE
        n = D_MODEL // rows

        def chunk(stream, c):
            w_f, _, buf, sem0 = stream
            slot = c % N_STAGE
            return pltpu.make_async_copy(w_f.at[pl.ds(c * rows, rows)], buf.at[pl.ds(slot * rows, rows)],
                                         sem_w.at[sem0 + slot])

        for c in range(N_STAGE - 1):
            for stream in streams:
                chunk(stream, c).start()
        for c in range(n):
            for stream in streams:
                if c + N_STAGE - 1 < n:
                    chunk(stream, c + N_STAGE - 1).start()
                chunk(stream, c).wait()
                _, w_b, buf, _ = stream
                w_b[c * rows:(c + 1) * rows] = buf[pl.ds((c % N_STAGE) * rows, rows)].astype(BF16)
        wout_copy.start()

    def inproj_chunk(c):
        def run():
            cs = slice(c * PC, (c + 1) * PC)
            zp[:, cs] = jnp.dot(lhs_p[...], win_ref[:, cs], preferred_element_type=F32)
        return run

    def outproj_chunk(c):
        def run():
            cs = slice(c * PC, (c + 1) * PC)
            x1_ref[:, cs] = xsp[:, cs] + jnp.dot(mixp[...], wout_ref[:, cs], preferred_element_type=F32)
        return run

    @pl.when(k == 0)
    def _():
        prefix = xt_ref[DEC_BATCH:R_TAIL]
        zb_o[...] = jnp.dot(_rms(prefix, gm_ref[...]).astype(BF16), win_ref[...], preferred_element_type=F32)
        xsp[0:BLOCK] = jnp.zeros((BLOCK, D_MODEL), F32)
        xsp[BLOCK:FB] = prefix
        mixp[...] = jnp.zeros((FB, D_MODEL), BF16)
        xbuf[0:HALO] = jnp.zeros((HALO, D_RNN), F32)
        h_scr[...] = jnp.zeros((HALO, D_RNN), F32)
        kvbuf[...] = jnp.zeros((BLOCK, 2 * D_KV), F32)
        for h, bias in enumerate(_table_lookup(bucket_ref[...], tab_ref)):
            bias_scr[h] = bias

    def mixer(zb, is_prefix, first_key, fillers):
        fill = iter(fillers)

        def between():
            f = next(fill, None)
            if f is not None:
                f()

        between()
        xr = zb[:, 0:D_RNN]
        xbuf[HALO:HALO + BLOCK] = xr
        cw = cw_ref[...]
        xc = cb_v
        for j in range(RNN_CONV - 1):
            lo = HALO - (RNN_CONV - 1) + j
            xc = xc + xbuf[lo:lo + BLOCK] * cw[j:j + 1]
        xc = xc + xr * cw[RNN_CONV - 1:RNN_CONV]
        xbuf[0:HALO] = xr[BLOCK - HALO:BLOCK]

        a, b = _gates(xc, wg_ref, gab_v, gxb_v, lam_v)
        between()
        h = h_scr[0:1]
        if is_prefix is not None:
            row = lax.broadcasted_iota(jnp.int32, (BLOCK, D_RNN), 0)
            b = jnp.where(jnp.logical_and(is_prefix, row < N_PAD), 0.0, b)
            h = jnp.where(is_prefix, 0.0, h)

        ng = BLOCK // HALO
        a3 = a.reshape(ng, HALO, D_RNN)
        b3 = b.reshape(ng, HALO, D_RNN)
        sub = lax.broadcasted_iota(jnp.int32, (ng, HALO, D_RNN), 1)
        sh = 1
        while sh < HALO:
            a_prev = pltpu.roll(a3, sh, 1)
            b_prev = pltpu.roll(b3, sh, 1)
            m = sub >= sh
            b3 = jnp.where(m, a3 * b_prev + b3, b3)
            a3 = jnp.where(m, a3 * a_prev, a3)
            sh *= 2
        hs = []
        for g in range(ng):
            hg = a3[g] * h + b3[g]
            hs.append(hg)
            h = hg[HALO - 1:HALO]
        h_all = jnp.concatenate(hs, axis=0)
        h_scr[...] = jnp.broadcast_to(h, (HALO, D_RNN))
        between()
        y_rnn = h_all * jax.nn.gelu(zb[:, O_GR:O_GR + D_RNN])
        mix_a = _rms(y_rnn, nr_v).astype(BF16)

        q = zb[:, O_Q:O_Q + D_ATTN]
        kv = zb[:, O_KV:O_KV + 2 * D_KV]
        kvp = kvbuf[...]
        kvbuf[...] = kv
        col = lax.broadcasted_iota(jnp.int32, (BLOCK, 2 * BLOCK), 1)
        rowq = lax.broadcasted_iota(jnp.int32, (BLOCK, 2 * BLOCK), 0)
        d = BLOCK + rowq - col
        mask = (d >= 0) & (d < WINDOW) & (col >= first_key)
        outs = []
        for kh in range(N_KV):
            between()
            ks = slice(kh * HEAD_DIM, (kh + 1) * HEAD_DIM)
            vs = slice(D_KV + kh * HEAD_DIM, D_KV + (kh + 1) * HEAD_DIM)
            qs = jnp.concatenate(
                [q[:, (kh * GROUP + g) * HEAD_DIM:(kh * GROUP + g + 1) * HEAD_DIM] for g in range(GROUP)],
                axis=0).astype(BF16)
            kk = jnp.concatenate([kvp[:, ks], kv[:, ks]], axis=0).astype(BF16)
            vv = jnp.concatenate([kvp[:, vs], kv[:, vs]], axis=0).astype(BF16)
            sc = lax.dot_general(qs, kk, (((1,), (1,)), ((), ())), preferred_element_type=F32)
            es, dens = [], []
            for g in range(GROUP):
                hh = kh * GROUP + g
                lg = sc[g * BLOCK:(g + 1) * BLOCK] * SCALE + bias_scr[hh]
                lg = jnp.where(mask, lg, NEG)
                sink = sink_ref[hh]
                mx = jnp.maximum(jnp.max(lg, axis=-1, keepdims=True), sink)
                e = jnp.exp(lg - mx)
                dens.append(jnp.sum(e, axis=-1, keepdims=True) + jnp.exp(sink - mx))
                es.append(e.astype(BF16))
            pv = jnp.dot(jnp.concatenate(es, axis=0), vv, preferred_element_type=F32)
            for g in range(GROUP):
                outs.append(pv[g * BLOCK:(g + 1) * BLOCK] / dens[g])
        between()
        mix_b = _rms(jnp.concatenate(outs, axis=1), na_v).astype(BF16)

        return jnp.concatenate([mix_a, mix_b], axis=1), h, kv, xr

    xa = jnp.where(k == last, xt_ref[0:DEC_BATCH], xa_ref[...])
    xb = xb_ref[...]
    lhs_p[0:BLOCK] = _rms(xa, gm_ref[...]).astype(BF16)
    lhs_p[BLOCK:FB] = _rms(xb, gm_ref[...]).astype(BF16)
    n_in, n_out = D_IN // PC, D_MODEL // PC
    mix_o, h, kv, xr = mixer(zb_o, k == 0, jnp.where(k == 0, BLOCK + N_PAD, 0),
                             [inproj_chunk(c) for c in range(N_IN_EARLY)])
    mixp[BLOCK:FB] = mix_o
    hlast_ref[...] = jnp.broadcast_to(h, (HALO, D_RNN))
    kvlast_ref[...] = kv
    xrt_ref[...] = xr[BLOCK - HALO:BLOCK]

    mix_e = mixer(zp.at[0:BLOCK], None, jnp.where(k == 0, N_PAD, 0),
                  [inproj_chunk(c) for c in range(N_IN_EARLY, n_in)] + [outproj_chunk(c) for c in range(n_out)])[0]
    mixp[0:BLOCK] = mix_e
    xsp[0:BLOCK] = xa
    xsp[BLOCK:FB] = xb
    zb_o[...] = zp[BLOCK:FB]

    @pl.when(k == last)
    def _():
        zs_ref[...] = zp[0:BLOCK]
        wout_copy.wait()


def _front(tab, sinks, xp, xt, gm, w_in, w_out, bucket, cw, wg, vp, w_up, w_gate, w_down):
    ffn_w = (w_up, w_gate, w_down)
    slab = lambda w: (w.shape[0] // N_SLAB, w.shape[1])
    hbm = pl.BlockSpec(memory_space=pl.ANY)
    vec = lambda n: pl.BlockSpec((1, n), lambda k: (0, 0))
    smem = pl.BlockSpec(memory_space=pltpu.SMEM)
    once = lambda shape: pl.BlockSpec(shape, lambda k: (0,) * len(shape), pipeline_mode=pl.Buffered(1))
    return pl.pallas_call(
        _front_kernel,
        grid=(N_FSTEP,),
        in_specs=[
            smem, smem,
            pl.BlockSpec((BLOCK, D_MODEL), lambda k: (jnp.minimum(2 * k, N_PBLK - 1), 0)),
            pl.BlockSpec((BLOCK, D_MODEL), lambda k: (jnp.minimum(2 * k + 1, N_PBLK - 1), 0)),
            once((R_TAIL, D_MODEL)),
            vec(D_MODEL),
            hbm, hbm,
            pl.BlockSpec((BLOCK, 2 * BLOCK), lambda k: (0, 0)),
            pl.BlockSpec((RNN_CONV, D_RNN), lambda k: (0, 0)),
            pl.BlockSpec((N_GW, GW, 2 * GW), lambda k: (0, 0, 0)),
            pl.BlockSpec((HALO, D_RNN), lambda k: (0, 0)),
            hbm, hbm, hbm,
        ],
        out_specs=[
            pl.BlockSpec((FB, D_MODEL), lambda k: (jnp.where(k == 0, SEQ // FB, k - 1), 0)),
            pl.BlockSpec((BLOCK, D_IN), lambda k: (0, 0)),
            pl.BlockSpec((HALO, D_RNN), lambda k: (0, 0)),
            pl.BlockSpec((BLOCK, 2 * D_KV), lambda k: (0, 0)),
            pl.BlockSpec((HALO, D_RNN), lambda k: (0, 0)),
            hbm, hbm, hbm, hbm,
        ],
        out_shape=[
            jax.ShapeDtypeStruct((R_ALL, D_MODEL), F32),
            jax.ShapeDtypeStruct((DEC_BATCH, D_IN), F32),
            jax.ShapeDtypeStruct((HALO, D_RNN), F32),
            jax.ShapeDtypeStruct((BLOCK, 2 * D_KV), F32),
            jax.ShapeDtypeStruct((HALO, D_RNN), F32),
        ] + [jax.ShapeDtypeStruct(w.shape, BF16) for w in (w_out,) + ffn_w],
        scratch_shapes=[
            pltpu.VMEM(w_in.shape, BF16),
            pltpu.VMEM(w_out.shape, BF16),
            pltpu.VMEM((FB, D_IN), F32),
            pltpu.VMEM((BLOCK, D_IN), F32),
            pltpu.VMEM((FB, D_MODEL), BF16),
            pltpu.VMEM((FB, D_MODEL), F32),
            pltpu.VMEM((FB, D_MODEL), BF16),
            pltpu.VMEM((HALO + BLOCK, D_RNN), F32),
            pltpu.VMEM((HALO, D_RNN), F32),
            pltpu.VMEM((BLOCK, 2 * D_KV), F32),
            pltpu.VMEM((N_HEADS, BLOCK, 2 * BLOCK), F32),
        ] + [pltpu.VMEM(slab(w), F32) for w in ffn_w] + [pltpu.VMEM(slab(w), BF16) for w in ffn_w] + [
            pltpu.SemaphoreType.DMA((len(ffn_w),)),
            pltpu.SemaphoreType.DMA((len(ffn_w),)),
            pltpu.SemaphoreType.DMA((2 * N_STAGE + 1,)),
        ],
        compiler_params=pltpu.CompilerParams(
            dimension_semantics=("arbitrary",), vmem_limit_bytes=VMEM_LIMIT_BIG),
        name="front",
    )(tab, sinks, xp, xp, xt, gm, w_in, w_out, bucket, cw, wg, vp, *ffn_w)


def _smixer_kernel(tab_ref, sink_ref, z_ref, st_ref, h0_ref, kc_ref, vc_ref, xs_ref, wout_hbm,
                   bucket_ref, cw_ref, wg_ref, vp_ref,
                   x1s_ref, conv_ref, h_ref, ko_ref, vo_ref, bias_scr, mix_scr, wout_ref, sem_w):
    c = pl.program_id(0)
    cb_v, gab_v, gxb_v, lam_v, nr_v, na_v = (vp_ref[r:r + 1] for r in range(N_VP))
    nrow = N_HEADS * SB
    ncol = SB * WINDOW * N_KV
    wout_copy = pltpu.make_async_copy(wout_hbm, wout_ref, sem_w)

    @pl.when(c == 0)
    def _():
        wout_copy.start()
        col = lax.broadcasted_iota(jnp.int32, (SB, ncol), 1)
        row = lax.broadcasted_iota(jnp.int32, (SB, ncol), 0)
        own = ((col >> 8) == row) & (((col >> 1) & (WINDOW - 1)) >= 1)
        for h, bias in enumerate(_table_lookup(bucket_ref[...], tab_ref)):
            ok = own & ((col & 1) == h // GROUP)
            bias_scr[h * SB:(h + 1) * SB] = jnp.where(ok, jnp.concatenate([bias] * (SB // HALO), axis=0), NEG)

    r0 = pl.multiple_of(c * SB, SB)

    xr = z_ref[:, 0:D_RNN]
    cw = cw_ref[...]
    xc = cb_v
    for j in range(RNN_CONV - 1):
        xc = xc + st_ref[j] * cw[j:j + 1]
    xc = xc + xr * cw[RNN_CONV - 1:RNN_CONV]
    for j in range(1, RNN_CONV - 1):
        conv_ref[j - 1] = st_ref[j]
    conv_ref[RNN_CONV - 2] = xr
    a, b = _gates(xc, wg_ref, gab_v, gxb_v, lam_v)
    h = a * h0_ref[...] + b
    h_ref[...] = h
    y_rnn = h * jax.nn.gelu(z_ref[:, O_GR:O_GR + D_RNN])
    mix_scr[pl.ds(r0, SB), 0:D_RNN] = _rms(y_rnn, nr_v).astype(BF16)

    q = z_ref[:, O_Q:O_Q + D_ATTN]
    kv = z_ref[:, O_KV:O_KV + 2 * D_KV]
    qs = jnp.concatenate([q[:, h * HEAD_DIM:(h + 1) * HEAD_DIM] for h in range(N_HEADS)], axis=0).astype(BF16)
    new_rows = lambda off: jnp.concatenate(
        [kv[:, off + (h // GROUP) * HEAD_DIM:off + (h // GROUP + 1) * HEAD_DIM] for h in range(N_HEADS)],
        axis=0).astype(BF16).astype(F32)
    k_new = new_rows(0)
    v_new = new_rows(D_KV)
    sc = lax.dot_general(qs, kc_ref[...].astype(BF16), (((1,), (1,)), ((), ())), preferred_element_type=F32)
    lg = sc * SCALE + bias_scr[...]
    rh = lax.broadcasted_iota(jnp.int32, (nrow, 1), 0) >> 4
    sink = jnp.zeros((nrow, 1), F32)
    bias_new = jnp.zeros((nrow, 1), F32)
    for h in range(N_HEADS):
        sink = jnp.where(rh == h, sink_ref[h], sink)
        bias_new = jnp.where(rh == h, tab_ref[0, h], bias_new)
    lg_new = jnp.sum(qs.astype(F32) * k_new, axis=-1, keepdims=True) * SCALE + bias_new
    mx = jnp.maximum(jnp.maximum(jnp.max(lg, axis=-1, keepdims=True), lg_new), sink)
    e = jnp.exp(lg - mx)
    e_new = jnp.exp(lg_new - mx)
    den = jnp.sum(e, axis=-1, keepdims=True) + e_new + jnp.exp(sink - mx)
    pv = jnp.dot(e.astype(BF16), vc_ref[...].astype(BF16), preferred_element_type=F32)
    o = (pv + e_new.astype(BF16).astype(F32) * v_new) / den
    y_attn = jnp.concatenate([o[h * SB:(h + 1) * SB] for h in range(N_HEADS)], axis=1)
    mix_scr[pl.ds(r0, SB), D_RNN:] = _rms(y_attn, na_v).astype(BF16)

    per = WINDOW * N_KV
    slide = lambda ref: pltpu.roll(ref[...].reshape(SB, per, HEAD_DIM), per - N_KV, 1).reshape(SB * per, HEAD_DIM)
    ko_ref[...] = slide(kc_ref)
    vo_ref[...] = slide(vc_ref)
    for bi in range(SB):
        for kh in range(N_KV):
            r = (bi + 1) * per - N_KV + kh
            ko_ref[r:r + 1, :] = kv[bi:bi + 1, kh * HEAD_DIM:(kh + 1) * HEAD_DIM]
            vo_ref[r:r + 1, :] = kv[bi:bi + 1, D_KV + kh * HEAD_DIM:D_KV + (kh + 1) * HEAD_DIM]

    @pl.when(c == DEC_BATCH // SB - 1)
    def _():
        wout_copy.wait()
        x1s_ref[...] = xs_ref[...] + jnp.dot(mix_scr[...], wout_ref[...], preferred_element_type=F32)


def _smixer(tab, sinks, zs, st, h0, kc, vc, xs, w_out, bucket, cw, wg, vp):
    smem = pl.BlockSpec(memory_space=pltpu.SMEM)
    cache = pl.BlockSpec((SB * WINDOW * N_KV, HEAD_DIM), lambda c: (c, 0))
    return pl.pallas_call(
        _smixer_kernel,
        grid=(DEC_BATCH // SB,),
        in_specs=[
            smem, smem,
            pl.BlockSpec((SB, D_IN), lambda c: (c, 0)),
            pl.BlockSpec((RNN_CONV - 1, SB, D_RNN), lambda c: (0, c, 0)),
            pl.BlockSpec((SB, D_RNN), lambda c: (c, 0)),
            cache, cache,
            pl.BlockSpec((DEC_BATCH, D_MODEL), lambda c: (0, 0)),
            pl.BlockSpec(memory_space=pl.ANY),
            pl.BlockSpec((HALO, SB * WINDOW * N_KV), lambda c: (0, 0)),
            pl.BlockSpec((RNN_CONV, D_RNN), lambda c: (0, 0)),
            pl.BlockSpec((N_GW, GW, 2 * GW), lambda c: (0, 0, 0)),
            pl.BlockSpec((HALO, D_RNN), lambda c: (0, 0)),
        ],
        out_specs=[
            pl.BlockSpec((DEC_BATCH, D_MODEL), lambda c: (0, 0)),
            pl.BlockSpec((RNN_CONV - 1, SB, D_RNN), lambda c: (0, c, 0)),
            pl.BlockSpec((SB, D_RNN), lambda c: (c, 0)),
            cache, cache,
        ],
        out_shape=[
            jax.ShapeDtypeStruct((DEC_BATCH, D_MODEL), F32),
            jax.ShapeDtypeStruct((RNN_CONV - 1, DEC_BATCH, D_RNN), F32),
            jax.ShapeDtypeStruct((DEC_BATCH, D_RNN), F32),
            jax.ShapeDtypeStruct((DEC_BATCH * WINDOW * N_KV, HEAD_DIM), F32),
            jax.ShapeDtypeStruct((DEC_BATCH * WINDOW * N_KV, HEAD_DIM), F32),
        ],
        scratch_shapes=[
            pltpu.VMEM((N_HEADS * SB, SB * WINDOW * N_KV), F32),
            pltpu.VMEM((DEC_BATCH, D_MODEL), BF16),
            pltpu.VMEM((D_MODEL, D_MODEL), BF16),
            pltpu.SemaphoreType.DMA(()),
        ],
        compiler_params=pltpu.CompilerParams(
            dimension_semantics=("arbitrary",), vmem_limit_bytes=VMEM_LIMIT),
        name="smixer",
    )(tab, sinks, zs, st, h0, kc, vc, xs, w_out, bucket, cw, wg, vp)


U_LO = LAST_P - HALO
U_HI = LAST_P + DEC_BATCH
N_UST = U_HI - U_LO


def _ffn_kernel(x1_ref, halo_ref, x1s_ref, st0_ref, st1_ref, g_ref, wu_ref, wg_ref, cwb_ref,
                wd_ref, y_ref, ys_ref, ust_ref, h2_scr, ubuf, abuf):
    i = pl.program_id(0)
    j = pl.program_id(1)
    gn, gf = g_ref[0:1], g_ref[1:2]

    @pl.when(j == 0)
    def _():
        h2_scr[0:FH] = _rms(halo_ref[...], gn).astype(BF16)
        h2_scr[FH:FH + TM] = _rms(x1_ref[...], gn).astype(BF16)
        y_ref[...] = jnp.zeros((TM, D_MODEL), F32)

        @pl.when(i == N_RT - 1)
        def _():
            h2_scr[FH + LAST_P:FH + U_HI] = _rms(x1s_ref[...], gn).astype(BF16)

    ubuf[...] = jnp.dot(h2_scr[...], wu_ref[...], preferred_element_type=F32)
    gate = jnp.dot(h2_scr[FH:FH + TM], wg_ref[...], preferred_element_type=F32)
    cw = cwb_ref[0:FFN_CONV]
    cb = cwb_ref[FFN_CONV:FFN_CONV + 1]
    for c in range(TF // KC):
        cs = slice(c * KC, (c + 1) * KC)
        tap = lambda r0, n: ubuf[r0:r0 + n, cs]
        uc = (cb[:, cs] + tap(FH - 2, TM) * cw[0:1, cs] + tap(FH - 1, TM) * cw[1:2, cs] + tap(FH, TM) * cw[2:3, cs])
        act = jax.nn.gelu(uc) * gate[:, cs]
        abuf[:, cs] = act.astype(BF16)
        ucs = (cb[:, cs] + st0_ref[:, cs] * cw[0:1, cs] + st1_ref[:, cs] * cw[1:2, cs]
               + tap(FH + LAST_P, DEC_BATCH) * cw[2:3, cs])
        act_s = jax.nn.gelu(ucs) * gate[LAST_P:U_HI, cs]
        abuf[LAST_P:U_HI, cs] = jnp.where(i == N_RT - 1, act_s, act[LAST_P:U_HI]).astype(BF16)
        y_ref[...] += jnp.dot(abuf[:, cs], wd_ref[cs, :], preferred_element_type=F32)
    ust_ref[...] = ubuf[FH + U_LO:FH + U_HI]

    @pl.when(j == N_FT - 1)
    def _():
        @pl.when(i == N_RT - 1)
        def _():
            ys_ref[...] = _rms(x1s_ref[...] + y_ref[LAST_P:U_HI], gf)

        y_ref[...] = _rms(x1_ref[...] + y_ref[...], gf)


def _ffn(x1, x1s, st, g, wu, wg, cwb, wd):
    def halo_idx(i, j):
        return (jnp.where(i == 0, R_ALL // FH - 1, i * (TM // FH) - 1), 0)

    return pl.pallas_call(
        _ffn_kernel,
        grid=(N_RT, N_FT),
        in_specs=[
            pl.BlockSpec((TM, D_MODEL), lambda i, j: (i, 0)),
            pl.BlockSpec((FH, D_MODEL), halo_idx),
            pl.BlockSpec((DEC_BATCH, D_MODEL), lambda i, j: (0, 0)),
            pl.BlockSpec((DEC_BATCH, TF), lambda i, j: (0, jnp.where(i == N_RT - 1, j, 0))),
            pl.BlockSpec((DEC_BATCH, TF), lambda i, j: (0, N_FT + jnp.where(i == N_RT - 1, j, 0))),
            pl.BlockSpec((2, D_MODEL), lambda i, j: (0, 0)),
            pl.BlockSpec((D_MODEL, TF), lambda i, j: (0, j)),
            pl.BlockSpec((D_MODEL, TF), lambda i, j: (0, j)),
            pl.BlockSpec((FFN_CONV + 1, TF), lambda i, j: (0, j)),
            pl.BlockSpec((TF, D_MODEL), lambda i, j: (j, 0)),
        ],
        out_specs=[
            pl.BlockSpec((TM, D_MODEL), lambda i, j: (i, 0)),
            pl.BlockSpec((DEC_BATCH, D_MODEL), lambda i, j: (0, 0)),
            pl.BlockSpec((N_UST, TF), lambda i, j: (0, jnp.where(i == N_RT - 1, j, 0))),
        ],
        out_shape=[
            jax.ShapeDtypeStruct((SEQ, D_MODEL), F32),
            jax.ShapeDtypeStruct((DEC_BATCH, D_MODEL), F32),
            jax.ShapeDtypeStruct((N_UST, D_FF), F32),
        ],
        scratch_shapes=[
            pltpu.VMEM((FH + TM, D_MODEL), BF16),
            pltpu.VMEM((FH + TM, TF), F32),
            pltpu.VMEM((TM, TF), BF16),
        ],
        compiler_params=pltpu.CompilerParams(
            dimension_semantics=("arbitrary", "arbitrary"), vmem_limit_bytes=VMEM_LIMIT_BIG),
        name="ffn",
    )(x1, x1, x1s, st, st, g, wu, wg, cwb, wd)


def _gate_weights(wa, wx):
    per = GW // RNN_BLOCK
    eye = jnp.eye(per, dtype=wa.dtype)

    def bd(w):
        w = w.reshape(N_GW, per, RNN_BLOCK, RNN_BLOCK)
        return jnp.einsum('gpcd,pq->gpcqd', w, eye).reshape(N_GW, GW, GW)

    return jnp.concatenate([bd(wa), bd(wx)], axis=-1).astype(BF16)


def kernel(x_prompt, x_sample, state_rnn_conv, state_rnn_h, cache_k_win, cache_v_win, state_ffn_conv,
           meta_tokens, rel_bias_table, norm_mix, w_in, rnn_conv_w, rnn_conv_b, gate_a_w, gate_a_b,
           gate_x_w, gate_x_b, rnn_lambda, attn_sinks, norm_rnn_out, norm_attn_out, w_out, norm_ffn,
           w_up, w_gate, ffn_conv_w, ffn_conv_b, w_down, norm_final):
    l = 0
    xp = x_prompt[0]
    xs = x_sample[:, 0, :]
    xt = jnp.concatenate([xs, jnp.zeros((N_PAD, D_MODEL), F32), meta_tokens], axis=0)
    row = lambda v: v.reshape(1, -1)

    qi = np.arange(BLOCK)[:, None]
    sj = np.arange(2 * BLOCK)[None, :]
    bucket_p = jnp.asarray(_rel_buckets(BLOCK + qi - sj))
    pos = (np.arange(SB * WINDOW * N_KV) >> 1) & (WINDOW - 1)
    bucket_s = jnp.asarray(np.tile(_rel_buckets(WINDOW - pos)[None, :], (HALO, 1)))

    wg = _gate_weights(gate_a_w[l], gate_x_w[l])
    vecs = (rnn_conv_b[l], gate_a_b[l], gate_x_b[l], rnn_lambda[l], norm_rnn_out[l], norm_attn_out[l])
    vp = jnp.concatenate([jnp.stack(vecs), jnp.zeros((HALO - N_VP, D_RNN), F32)], axis=0)
    seq_w = (rnn_conv_w[l], wg, vp)

    x1, zs, h_last, kv_last, xr_tail, w_out_b, w_up_b, w_gate_b, w_down_b = _front(
        rel_bias_table, attn_sinks[l], xp, xt, row(norm_mix[l]), w_in[l], w_out[l], bucket_p, *seq_w,
        w_up[l], w_gate[l], w_down[l])
    x1s, conv_s, h_s, k_s, v_s = _smixer(
        rel_bias_table, attn_sinks[l], zs,
        jnp.swapaxes(state_rnn_conv[l], 0, 1), state_rnn_h[l],
        cache_k_win[l].reshape(DEC_BATCH * WINDOW * N_KV, HEAD_DIM),
        cache_v_win[l].reshape(DEC_BATCH * WINDOW * N_KV, HEAD_DIM),
        xs, w_out_b, bucket_s, *seq_w)
    y_p, y_s, ust = _ffn(x1, x1s, state_ffn_conv[l].reshape(DEC_BATCH, (FFN_CONV - 1) * D_FF),
                         jnp.stack([norm_ffn[l], norm_final]), w_up_b, w_gate_b,
                         jnp.concatenate([ffn_conv_w[l], row(ffn_conv_b[l])], axis=0), w_down_b)

    p_states = (
        xr_tail[HALO - (RNN_CONV - 1):HALO][None, None],
        h_last[0:1][None],
        kv_last[:, :D_KV].reshape(1, 1, WINDOW, N_KV, HEAD_DIM),
        kv_last[:, D_KV:].reshape(1, 1, WINDOW, N_KV, HEAD_DIM),
        ust[HALO - (FFN_CONV - 1):HALO][None, None],
    )
    s_states = (
        jnp.swapaxes(conv_s, 0, 1)[None],
        h_s[None],
        k_s.reshape(1, DEC_BATCH, WINDOW, N_KV, HEAD_DIM),
        v_s.reshape(1, DEC_BATCH, WINDOW, N_KV, HEAD_DIM),
        jnp.stack([state_ffn_conv[l][:, FFN_CONV - 2, :], ust[HALO:]], axis=1)[None],
    )
    return (y_p[None], y_s[:, None, :]) + p_states + s_states
```

```python
import math

import numpy as np
import jax
import jax.numpy as jnp
from jax import lax
from jax.experimental import pallas as pl
from jax.experimental.pallas import tpu as pltpu

F32 = jnp.float32
BF16 = jnp.bfloat16

D_MODEL = 2048
SEQ = 8192
DEC_BATCH = 128
D_RNN = 1024
N_RNN_BLOCKS = 16
RNN_BLOCK = D_RNN // N_RNN_BLOCKS
RNN_CONV = 4
LRU_C = 8.0
N_HEADS = 8
HEAD_DIM = 128
N_KV = 2
GROUP = N_HEADS // N_KV
D_ATTN = N_HEADS * HEAD_DIM
WINDOW = 128
BLOCK = 128
NUM_BUCKETS = 32
MAX_DISTANCE = 128
D_FF = 3 * D_MODEL
FFN_CONV = 3
N_META = 16
EPS = 1e-6
NEG = -1e30
D_KV = N_KV * HEAD_DIM
D_IN = 2 * D_RNN + D_ATTN + 2 * D_KV
SCALE = HEAD_DIM ** -0.5
O_GR = D_RNN
O_Q = 2 * D_RNN
O_KV = 2 * D_RNN + D_ATTN

N_PAD = BLOCK - N_META
R_TAIL = DEC_BATCH + BLOCK
R_ALL = SEQ + R_TAIL
N_PBLK = SEQ // BLOCK

FB = 2 * BLOCK
N_FSTEP = (N_PBLK + 2) // 2
PC = 512
N_IN_EARLY = 5
N_STAGE = 4
N_SLAB = N_FSTEP - 1

TM = 640
N_RT = (SEQ + DEC_BATCH) // TM
LAST_P = SEQ - (N_RT - 1) * TM
TF = 768
N_FT = D_FF // TF
KC = 256
HALO = 8
FH = 16
GW = 256
N_GW = D_RNN // GW
SB = 16
N_RING = 3
N_VP = 6
VMEM_LIMIT = 56 * 1024 * 1024
VMEM_LIMIT_BIG = 60 * 1024 * 1024


def _rms(x, g):
    return x * lax.rsqrt(jnp.mean(x * x, axis=-1, keepdims=True) + EPS) * g


def _rel_buckets(d):
    d = np.maximum(d, 0)
    exact = NUM_BUCKETS // 2
    ratio = np.maximum(d, 1).astype(np.float32) / np.float32(exact)
    large = exact + (np.log(ratio) / np.float32(math.log(MAX_DISTANCE / exact))
                     * np.float32(NUM_BUCKETS - exact)).astype(np.int32)
    large = np.minimum(large, NUM_BUCKETS - 1)
    return np.where(d < exact, d, large).astype(np.int32)


def _table_lookup(bucket, tab_ref):
    outs = [jnp.zeros(bucket.shape, F32) for _ in range(N_HEADS)]
    for b in range(NUM_BUCKETS):
        hit = bucket == b
        outs = [jnp.where(hit, tab_ref[b, h], o) for h, o in enumerate(outs)]
    return outs


def _gates(xc, wg_ref, gab, gxb, lam):
    xcb = xc.astype(BF16)
    ga, gx = [], []
    for j in range(N_GW):
        gj = jnp.dot(xcb[:, GW * j:GW * (j + 1)], wg_ref[j], preferred_element_type=F32)
        ga.append(gj[:, :GW])
        gx.append(gj[:, GW:])
    r = jax.nn.sigmoid(jnp.concatenate(ga, axis=1) + gab)
    i = jax.nn.sigmoid(jnp.concatenate(gx, axis=1) + gxb)
    log_a = -LRU_C * r * jax.nn.softplus(-lam)
    a = jnp.exp(log_a)
    t = 1.0 - a * a
    b = jnp.where(t > 0.0, t * lax.rsqrt(t), 0.0) * i * xc
    return a, b


def _front_kernel(tab_ref, sink_ref, xa_ref, xb_ref, xt_ref, gm_ref, win_f, wout_f, bucket_ref,
                  cw_ref, wg_ref, vp_ref, wu_f, wgt_f, wd_f,
                  x1_ref, zs_ref, hlast_ref, kvlast_ref, xrt_ref, wout_hbm, wu_b, wgt_b, wd_b,
                  win_ref, wout_ref, zp, zb_o, lhs_p, xsp, mixp, xbuf, h_scr, kvbuf, bias_scr,
                  su, sg, sd, tu, tg, td, sem_in, sem_out, sem_w):
    k = pl.program_id(0)
    last = N_FSTEP - 1
    cb_v, gab_v, gxb_v, lam_v, nr_v, na_v = (vp_ref[r:r + 1] for r in range(N_VP))

    stages = ((wu_f, su, tu, wu_b), (wgt_f, sg, tg, wgt_b), (wd_f, sd, td, wd_b))

    def slab_in(s):
        return [pltpu.make_async_copy(w.at[pl.ds(s * st.shape[0], st.shape[0])], st, sem_in.at[n])
                for n, (w, st, _, _) in enumerate(stages)]

    def slab_out(s):
        return [pltpu.make_async_copy(t, w.at[pl.ds(s * t.shape[0], t.shape[0])], sem_out.at[n])
                for n, (_, _, t, w) in enumerate(stages)]

    @pl.when(k == 0)
    def _():
        for c in slab_in(0):
            c.start()

    @pl.when(k >= 1)
    def _():
        for c in slab_out(k - 1):
            c.wait()

    @pl.when(k < N_SLAB)
    def _():
        for c in slab_in(k):
            c.wait()
        for _, st, t, _ in stages:
            t[...] = st[...].astype(BF16)
        for c in slab_out(k):
            c.start()

    @pl.when(k < N_SLAB - 1)
    def _():
        for c in slab_in(k + 1):
            c.start()

    wout_copy = pltpu.make_async_copy(wout_ref, wout_hbm, sem_w.at[2 * N_STAGE])

    @pl.when(k == 0)
    def _():
        streams = ((win_f, win_ref, zp, 0), (wout_f, wout_ref, xsp, N_STAGE))
        rows = FB // N_STAGE
        n = D_MODEL // rows

        def chunk(stream, c):
            w_f, _, buf, sem0 = stream
            slot = c % N_STAGE
            return pltpu.make_async_copy(w_f.at[pl.ds(c * rows, rows)], buf.at[pl.ds(slot * rows, rows)],
                                         sem_w.at[sem0 + slot])

        for c in range(N_STAGE - 1):
            for stream in streams:
                chunk(stream, c).start()
        for c in range(n):
            for stream in streams:
                if c + N_STAGE - 1 < n:
                    chunk(stream, c + N_STAGE - 1).start()
                chunk(stream, c).wait()
                _, w_b, buf, _ = stream
                w_b[c * rows:(c + 1) * rows] = buf[pl.ds((c % N_STAGE) * rows, rows)].astype(BF16)
        wout_copy.start()

    def inproj_chunk(c):
        def run():
            cs = slice(c * PC, (c + 1) * PC)
            zp[:, cs] = jnp.dot(lhs_p[...], win_ref[:, cs], preferred_element_type=F32)
        return run

    def outproj_chunk(c):
        def run():
            cs = slice(c * PC, (c + 1) * PC)
            x1_ref[:, cs] = xsp[:, cs] + jnp.dot(mixp[...], wout_ref[:, cs], preferred_element_type=F32)
        return run

    @pl.when(k == 0)
    def _():
        prefix = xt_ref[DEC_BATCH:R_TAIL]
        zb_o[...] = jnp.dot(_rms(prefix, gm_ref[...]).astype(BF16), win_ref[...], preferred_element_type=F32)
        xsp[0:BLOCK] = jnp.zeros((BLOCK, D_MODEL), F32)
        xsp[BLOCK:FB] = prefix
        mixp[...] = jnp.zeros((FB, D_MODEL), BF16)
        xbuf[0:HALO] = jnp.zeros((HALO, D_RNN), F32)
        h_scr[...] = jnp.zeros((HALO, D_RNN), F32)
        kvbuf[...] = jnp.zeros((BLOCK, 2 * D_KV), F32)
        for h, bias in enumerate(_table_lookup(bucket_ref[...], tab_ref)):
            bias_scr[h] = bias

    def mixer(zb, is_prefix, first_key, fillers):
        fill = iter(fillers)

        def between():
            f = next(fill, None)
            if f is not None:
                f()

        between()
        xr = zb[:, 0:D_RNN]
        xbuf[HALO:HALO + BLOCK] = xr
        cw = cw_ref[...]
        xc = cb_v
        for j in range(RNN_CONV - 1):
            lo = HALO - (RNN_CONV - 1) + j
            xc = xc + xbuf[lo:lo + BLOCK] * cw[j:j + 1]
        xc = xc + xr * cw[RNN_CONV - 1:RNN_CONV]
        xbuf[0:HALO] = xr[BLOCK - HALO:BLOCK]

        a, b = _gates(xc, wg_ref, gab_v, gxb_v, lam_v)
        between()
        h = h_scr[0:1]
        if is_prefix is not None:
            row = lax.broadcasted_iota(jnp.int32, (BLOCK, D_RNN), 0)
            b = jnp.where(jnp.logical_and(is_prefix, row < N_PAD), 0.0, b)
            h = jnp.where(is_prefix, 0.0, h)

        ng = BLOCK // HALO
        a3 = a.reshape(ng, HALO, D_RNN)
        b3 = b.reshape(ng, HALO, D_RNN)
        sub = lax.broadcasted_iota(jnp.int32, (ng, HALO, D_RNN), 1)
        sh = 1
        while sh < HALO:
            a_prev = pltpu.roll(a3, sh, 1)
            b_prev = pltpu.roll(b3, sh, 1)
            m = sub >= sh
            b3 = jnp.where(m, a3 * b_prev + b3, b3)
            a3 = jnp.where(m, a3 * a_prev, a3)
            sh *= 2
        hs = []
        for g in range(ng):
            hg = a3[g] * h + b3[g]
            hs.append(hg)
            h = hg[HALO - 1:HALO]
        h_all = jnp.concatenate(hs, axis=0)
        h_scr[...] = jnp.broadcast_to(h, (HALO, D_RNN))
        between()
        y_rnn = h_all * jax.nn.gelu(zb[:, O_GR:O_GR + D_RNN])
        mix_a = _rms(y_rnn, nr_v).astype(BF16)

        q = zb[:, O_Q:O_Q + D_ATTN]
        kv = zb[:, O_KV:O_KV + 2 * D_KV]
        kvp = kvbuf[...]
        kvbuf[...] = kv
        col = lax.broadcasted_iota(jnp.int32, (BLOCK, 2 * BLOCK), 1)
        rowq = lax.broadcasted_iota(jnp.int32, (BLOCK, 2 * BLOCK), 0)
        d = BLOCK + rowq - col
        mask = (d >= 0) & (d < WINDOW) & (col >= first_key)
        outs = []
        for kh in range(N_KV):
            between()
            ks = slice(kh * HEAD_DIM, (kh + 1) * HEAD_DIM)
            vs = slice(D_KV + kh * HEAD_DIM, D_KV + (kh + 1) * HEAD_DIM)
            qs = jnp.concatenate(
                [q[:, (kh * GROUP + g) * HEAD_DIM:(kh * GROUP + g + 1) * HEAD_DIM] for g in range(GROUP)],
                axis=0).astype(BF16)
            kk = jnp.concatenate([kvp[:, ks], kv[:, ks]], axis=0).astype(BF16)
            vv = jnp.concatenate([kvp[:, vs], kv[:, vs]], axis=0).astype(BF16)
            sc = lax.dot_general(qs, kk, (((1,), (1,)), ((), ())), preferred_element_type=F32)
            es, dens = [], []
            for g in range(GROUP):
                hh = kh * GROUP + g
                lg = sc[g * BLOCK:(g + 1) * BLOCK] * SCALE + bias_scr[hh]
                lg = jnp.where(mask, lg, NEG)
                sink = sink_ref[hh]
                mx = jnp.maximum(jnp.max(lg, axis=-1, keepdims=True), sink)
                e = jnp.exp(lg - mx)
                dens.append(jnp.sum(e, axis=-1, keepdims=True) + jnp.exp(sink - mx))
                es.append(e.astype(BF16))
            pv = jnp.dot(jnp.concatenate(es, axis=0), vv, preferred_element_type=F32)
            for g in range(GROUP):
                outs.append(pv[g * BLOCK:(g + 1) * BLOCK] / dens[g])
        between()
        mix_b = _rms(jnp.concatenate(outs, axis=1), na_v).astype(BF16)

        return jnp.concatenate([mix_a, mix_b], axis=1), h, kv, xr

    xa = jnp.where(k == last, xt_ref[0:DEC_BATCH], xa_ref[...])
    xb = xb_ref[...]
    lhs_p[0:BLOCK] = _rms(xa, gm_ref[...]).astype(BF16)
    lhs_p[BLOCK:FB] = _rms(xb, gm_ref[...]).astype(BF16)
    n_in, n_out = D_IN // PC, D_MODEL // PC
    mix_o, h, kv, xr = mixer(zb_o, k == 0, jnp.where(k == 0, BLOCK + N_PAD, 0),
                             [inproj_chunk(c) for c in range(N_IN_EARLY)])
    mixp[BLOCK:FB] = mix_o
    hlast_ref[...] = jnp.broadcast_to(h, (HALO, D_RNN))
    kvlast_ref[...] = kv
    xrt_ref[...] = xr[BLOCK - HALO:BLOCK]

    mix_e = mixer(zp.at[0:BLOCK], None, jnp.where(k == 0, N_PAD, 0),
                  [inproj_chunk(c) for c in range(N_IN_EARLY, n_in)] + [outproj_chunk(c) for c in range(n_out)])[0]
    mixp[0:BLOCK] = mix_e
    xsp[0:BLOCK] = xa
    xsp[BLOCK:FB] = xb
    zb_o[...] = zp[BLOCK:FB]

    @pl.when(k == last)
    def _():
        zs_ref[...] = zp[0:BLOCK]
        wout_copy.wait()


def _front(tab, sinks, xp, xt, gm, w_in, w_out, bucket, cw, wg, vp, w_up, w_gate, w_down):
    ffn_w = (w_up, w_gate, w_down)
    slab = lambda w: (w.shape[0] // N_SLAB, w.shape[1])
    hbm = pl.BlockSpec(memory_space=pl.ANY)
    vec = lambda n: pl.BlockSpec((1, n), lambda k: (0, 0))
    smem = pl.BlockSpec(memory_space=pltpu.SMEM)
    once = lambda shape: pl.BlockSpec(shape, lambda k: (0,) * len(shape), pipeline_mode=pl.Buffered(1))
    return pl.pallas_call(
        _front_kernel,
        grid=(N_FSTEP,),
        in_specs=[
            smem, smem,
            pl.BlockSpec((BLOCK, D_MODEL), lambda k: (jnp.minimum(2 * k, N_PBLK - 1), 0)),
            pl.BlockSpec((BLOCK, D_MODEL), lambda k: (jnp.minimum(2 * k + 1, N_PBLK - 1), 0)),
            once((R_TAIL, D_MODEL)),
            vec(D_MODEL),
            hbm, hbm,
            pl.BlockSpec((BLOCK, 2 * BLOCK), lambda k: (0, 0)),
            pl.BlockSpec((RNN_CONV, D_RNN), lambda k: (0, 0)),
            pl.BlockSpec((N_GW, GW, 2 * GW), lambda k: (0, 0, 0)),
            pl.BlockSpec((HALO, D_RNN), lambda k: (0, 0)),
            hbm, hbm, hbm,
        ],
        out_specs=[
            pl.BlockSpec((FB, D_MODEL), lambda k: (jnp.where(k == 0, SEQ // FB, k - 1), 0)),
            pl.BlockSpec((BLOCK, D_IN), lambda k: (0, 0)),
            pl.BlockSpec((HALO, D_RNN), lambda k: (0, 0)),
            pl.BlockSpec((BLOCK, 2 * D_KV), lambda k: (0, 0)),
            pl.BlockSpec((HALO, D_RNN), lambda k: (0, 0)),
            hbm, hbm, hbm, hbm,
        ],
        out_shape=[
            jax.ShapeDtypeStruct((R_ALL, D_MODEL), F32),
            jax.ShapeDtypeStruct((DEC_BATCH, D_IN), F32),
            jax.ShapeDtypeStruct((HALO, D_RNN), F32),
            jax.ShapeDtypeStruct((BLOCK, 2 * D_KV), F32),
            jax.ShapeDtypeStruct((HALO, D_RNN), F32),
        ] + [jax.ShapeDtypeStruct(w.shape, BF16) for w in (w_out,) + ffn_w],
        scratch_shapes=[
            pltpu.VMEM(w_in.shape, BF16),
            pltpu.VMEM(w_out.shape, BF16),
            pltpu.VMEM((FB, D_IN), F32),
            pltpu.VMEM((BLOCK, D_IN), F32),
            pltpu.VMEM((FB, D_MODEL), BF16),
            pltpu.VMEM((FB, D_MODEL), F32),
            pltpu.VMEM((FB, D_MODEL), BF16),
            pltpu.VMEM((HALO + BLOCK, D_RNN), F32),
            pltpu.VMEM((HALO, D_RNN), F32),
            pltpu.VMEM((BLOCK, 2 * D_KV), F32),
            pltpu.VMEM((N_HEADS, BLOCK, 2 * BLOCK), F32),
        ] + [pltpu.VMEM(slab(w), F32) for w in ffn_w] + [pltpu.VMEM(slab(w), BF16) for w in ffn_w] + [
            pltpu.SemaphoreType.DMA((len(ffn_w),)),
            pltpu.SemaphoreType.DMA((len(ffn_w),)),
            pltpu.SemaphoreType.DMA((2 * N_STAGE + 1,)),
        ],
        compiler_params=pltpu.CompilerParams(
            dimension_semantics=("arbitrary",), vmem_limit_bytes=VMEM_LIMIT_BIG),
        name="front",
    )(tab, sinks, xp, xp, xt, gm, w_in, w_out, bucket, cw, wg, vp, *ffn_w)


def _smixer_kernel(tab_ref, sink_ref, z_ref, st_ref, h0_ref, kc_hbm, vc_hbm, xs_ref, wout_hbm,
                   bucket_ref, cw_ref, wg_ref, vp_ref,
                   x1s_ref, conv_ref, h_ref, ko_ref, vo_ref, bias_scr, mix_scr, wout_ref, sem_w,
                   k_ring, v_ring, sem_r):
    c = pl.program_id(0)

    n_chunk = DEC_BATCH // SB
    rows_c = SB * WINDOW * N_KV

    def fetch(ch):
        slot = ch % N_RING
        return [pltpu.make_async_copy(src.at[pl.ds(ch * rows_c, rows_c)], ring.at[slot], sem_r.at[n, slot])
                for n, (src, ring) in enumerate(((kc_hbm, k_ring), (vc_hbm, v_ring)))]

    @pl.when(c == 0)
    def _():
        for ch in range(N_RING - 1):
            for cp in fetch(ch):
                cp.start()

    @pl.when(c + N_RING - 1 < n_chunk)
    def _():
        for cp in fetch(c + N_RING - 1):
            cp.start()
    cb_v, gab_v, gxb_v, lam_v, nr_v, na_v = (vp_ref[r:r + 1] for r in range(N_VP))
    nrow = N_HEADS * SB
    ncol = SB * WINDOW * N_KV
    wout_copy = pltpu.make_async_copy(wout_hbm, wout_ref, sem_w)

    @pl.when(c == 0)
    def _():
        wout_copy.start()
        col = lax.broadcasted_iota(jnp.int32, (SB, ncol), 1)
        row = lax.broadcasted_iota(jnp.int32, (SB, ncol), 0)
        own = ((col >> 8) == row) & (((col >> 1) & (WINDOW - 1)) >= 1)
        for h, bias in enumerate(_table_lookup(bucket_ref[...], tab_ref)):
            ok = own & ((col & 1) == h // GROUP)
            bias_scr[h * SB:(h + 1) * SB] = jnp.where(ok, jnp.concatenate([bias] * (SB // HALO), axis=0), NEG)

    r0 = pl.multiple_of(c * SB, SB)
    for cp in fetch(c):
        cp.wait()
    kc_ref = k_ring.at[c % N_RING]
    vc_ref = v_ring.at[c % N_RING]

    xr = z_ref[:, 0:D_RNN]
    cw = cw_ref[...]
    xc = cb_v
    for j in range(RNN_CONV - 1):
        xc = xc + st_ref[j] * cw[j:j + 1]
    xc = xc + xr * cw[RNN_CONV - 1:RNN_CONV]
    for j in range(1, RNN_CONV - 1):
        conv_ref[j - 1] = st_ref[j]
    conv_ref[RNN_CONV - 2] = xr
    a, b = _gates(xc, wg_ref, gab_v, gxb_v, lam_v)
    h = a * h0_ref[...] + b
    h_ref[...] = h
    y_rnn = h * jax.nn.gelu(z_ref[:, O_GR:O_GR + D_RNN])
    mix_scr[pl.ds(r0, SB), 0:D_RNN] = _rms(y_rnn, nr_v).astype(BF16)

    q = z_ref[:, O_Q:O_Q + D_ATTN]
    kv = z_ref[:, O_KV:O_KV + 2 * D_KV]
    qs = jnp.concatenate([q[:, h * HEAD_DIM:(h + 1) * HEAD_DIM] for h in range(N_HEADS)], axis=0).astype(BF16)
    new_rows = lambda off: jnp.concatenate(
        [kv[:, off + (h // GROUP) * HEAD_DIM:off + (h // GROUP + 1) * HEAD_DIM] for h in range(N_HEADS)],
        axis=0).astype(BF16).astype(F32)
    k_new = new_rows(0)
    v_new = new_rows(D_KV)
    sc = lax.dot_general(qs, kc_ref[...].astype(BF16), (((1,), (1,)), ((), ())), preferred_element_type=F32)
    lg = sc * SCALE + bias_scr[...]
    rh = lax.broadcasted_iota(jnp.int32, (nrow, 1), 0) >> 4
    sink = jnp.zeros((nrow, 1), F32)
    bias_new = jnp.zeros((nrow, 1), F32)
    for h in range(N_HEADS):
        sink = jnp.where(rh == h, sink_ref[h], sink)
        bias_new = jnp.where(rh == h, tab_ref[0, h], bias_new)
    lg_new = jnp.sum(qs.astype(F32) * k_new, axis=-1, keepdims=True) * SCALE + bias_new
    mx = jnp.maximum(jnp.maximum(jnp.max(lg, axis=-1, keepdims=True), lg_new), sink)
    e = jnp.exp(lg - mx)
    e_new = jnp.exp(lg_new - mx)
    den = jnp.sum(e, axis=-1, keepdims=True) + e_new + jnp.exp(sink - mx)
    pv = jnp.dot(e.astype(BF16), vc_ref[...].astype(BF16), preferred_element_type=F32)
    o = (pv + e_new.astype(BF16).astype(F32) * v_new) / den
    y_attn = jnp.concatenate([o[h * SB:(h + 1) * SB] for h in range(N_HEADS)], axis=1)
    mix_scr[pl.ds(r0, SB), D_RNN:] = _rms(y_attn, na_v).astype(BF16)

    per = WINDOW * N_KV
    slide = lambda ref: pltpu.roll(ref[...].reshape(SB, per, HEAD_DIM), per - N_KV, 1).reshape(SB * per, HEAD_DIM)
    ko_ref[...] = slide(kc_ref)
    vo_ref[...] = slide(vc_ref)
    for bi in range(SB):
        for kh in range(N_KV):
            r = (bi + 1) * per - N_KV + kh
            ko_ref[r:r + 1, :] = kv[bi:bi + 1, kh * HEAD_DIM:(kh + 1) * HEAD_DIM]
            vo_ref[r:r + 1, :] = kv[bi:bi + 1, D_KV + kh * HEAD_DIM:D_KV + (kh + 1) * HEAD_DIM]

    @pl.when(c == DEC_BATCH // SB - 1)
    def _():
        wout_copy.wait()
        x1s_ref[...] = xs_ref[...] + jnp.dot(mix_scr[...], wout_ref[...], preferred_element_type=F32)


def _smixer(tab, sinks, zs, st, h0, kc, vc, xs, w_out, bucket, cw, wg, vp):
    smem = pl.BlockSpec(memory_space=pltpu.SMEM)
    cache = pl.BlockSpec((SB * WINDOW * N_KV, HEAD_DIM), lambda c: (c, 0))
    return pl.pallas_call(
        _smixer_kernel,
        grid=(DEC_BATCH // SB,),
        in_specs=[
            smem, smem,
            pl.BlockSpec((SB, D_IN), lambda c: (c, 0)),
            pl.BlockSpec((RNN_CONV - 1, SB, D_RNN), lambda c: (0, c, 0)),
            pl.BlockSpec((SB, D_RNN), lambda c: (c, 0)),
            pl.BlockSpec(memory_space=pl.ANY), pl.BlockSpec(memory_space=pl.ANY),
            pl.BlockSpec((DEC_BATCH, D_MODEL), lambda c: (0, 0)),
            pl.BlockSpec(memory_space=pl.ANY),
            pl.BlockSpec((HALO, SB * WINDOW * N_KV), lambda c: (0, 0)),
            pl.BlockSpec((RNN_CONV, D_RNN), lambda c: (0, 0)),
            pl.BlockSpec((N_GW, GW, 2 * GW), lambda c: (0, 0, 0)),
            pl.BlockSpec((HALO, D_RNN), lambda c: (0, 0)),
        ],
        out_specs=[
            pl.BlockSpec((DEC_BATCH, D_MODEL), lambda c: (0, 0)),
            pl.BlockSpec((RNN_CONV - 1, SB, D_RNN), lambda c: (0, c, 0)),
            pl.BlockSpec((SB, D_RNN), lambda c: (c, 0)),
            cache, cache,
        ],
        out_shape=[
            jax.ShapeDtypeStruct((DEC_BATCH, D_MODEL), F32),
            jax.ShapeDtypeStruct((RNN_CONV - 1, DEC_BATCH, D_RNN), F32),
            jax.ShapeDtypeStruct((DEC_BATCH, D_RNN), F32),
            jax.ShapeDtypeStruct((DEC_BATCH * WINDOW * N_KV, HEAD_DIM), F32),
            jax.ShapeDtypeStruct((DEC_BATCH * WINDOW * N_KV, HEAD_DIM), F32),
        ],
        scratch_shapes=[
            pltpu.VMEM((N_HEADS * SB, SB * WINDOW * N_KV), F32),
            pltpu.VMEM((DEC_BATCH, D_MODEL), BF16),
            pltpu.VMEM((D_MODEL, D_MODEL), BF16),
            pltpu.SemaphoreType.DMA(()),
            pltpu.VMEM((N_RING, SB * WINDOW * N_KV, HEAD_DIM), F32),
            pltpu.VMEM((N_RING, SB * WINDOW * N_KV, HEAD_DIM), F32),
            pltpu.SemaphoreType.DMA((2, N_RING)),
        ],
        compiler_params=pltpu.CompilerParams(
            dimension_semantics=("arbitrary",), vmem_limit_bytes=VMEM_LIMIT),
        name="smixer",
    )(tab, sinks, zs, st, h0, kc, vc, xs, w_out, bucket, cw, wg, vp)


U_LO = LAST_P - HALO
U_HI = LAST_P + DEC_BATCH
N_UST = U_HI - U_LO


def _ffn_kernel(x1_ref, halo_ref, x1s_ref, st0_ref, st1_ref, gn_ref, gf_ref, wu_ref, wg_ref, cw_ref, cb_ref,
                wd_ref, y_ref, ys_ref, ust_ref, h2_scr, ubuf, abuf):
    i = pl.program_id(0)
    j = pl.program_id(1)

    @pl.when(j == 0)
    def _():
        h2_scr[0:FH] = _rms(halo_ref[...], gn_ref[...]).astype(BF16)
        h2_scr[FH:FH + TM] = _rms(x1_ref[...], gn_ref[...]).astype(BF16)
        y_ref[...] = jnp.zeros((TM, D_MODEL), F32)

        @pl.when(i == N_RT - 1)
        def _():
            h2_scr[FH + LAST_P:FH + U_HI] = _rms(x1s_ref[...], gn_ref[...]).astype(BF16)

    ubuf[...] = jnp.dot(h2_scr[...], wu_ref[...], preferred_element_type=F32)
    gate = jnp.dot(h2_scr[FH:FH + TM], wg_ref[...], preferred_element_type=F32)
    cw = cw_ref[...]
    cb = cb_ref[...]
    for c in range(TF // KC):
        cs = slice(c * KC, (c + 1) * KC)
        tap = lambda r0, n: ubuf[r0:r0 + n, cs]
        uc = (cb[:, cs] + tap(FH - 2, TM) * cw[0:1, cs] + tap(FH - 1, TM) * cw[1:2, cs] + tap(FH, TM) * cw[2:3, cs])
        act = jax.nn.gelu(uc) * gate[:, cs]
        abuf[:, cs] = act.astype(BF16)
        ucs = (cb[:, cs] + st0_ref[:, cs] * cw[0:1, cs] + st1_ref[:, cs] * cw[1:2, cs]
               + tap(FH + LAST_P, DEC_BATCH) * cw[2:3, cs])
        act_s = jax.nn.gelu(ucs) * gate[LAST_P:U_HI, cs]
        abuf[LAST_P:U_HI, cs] = jnp.where(i == N_RT - 1, act_s, act[LAST_P:U_HI]).astype(BF16)
        y_ref[...] += jnp.dot(abuf[:, cs], wd_ref[cs, :], preferred_element_type=F32)
    ust_ref[...] = ubuf[FH + U_LO:FH + U_HI]

    @pl.when(j == N_FT - 1)
    def _():
        @pl.when(i == N_RT - 1)
        def _():
            ys_ref[...] = _rms(x1s_ref[...] + y_ref[LAST_P:U_HI], gf_ref[...])

        y_ref[...] = _rms(x1_ref[...] + y_ref[...], gf_ref[...])


def _ffn(x1, x1s, st, gn, gf, wu, wg, cw, cb, wd):
    def halo_idx(i, j):
        return (jnp.where(i == 0, R_ALL // FH - 1, i * (TM // FH) - 1), 0)

    return pl.pallas_call(
        _ffn_kernel,
        grid=(N_RT, N_FT),
        in_specs=[
            pl.BlockSpec((TM, D_MODEL), lambda i, j: (i, 0)),
            pl.BlockSpec((FH, D_MODEL), halo_idx),
            pl.BlockSpec((DEC_BATCH, D_MODEL), lambda i, j: (0, 0)),
            pl.BlockSpec((DEC_BATCH, TF), lambda i, j: (0, j)),
            pl.BlockSpec((DEC_BATCH, TF), lambda i, j: (0, N_FT + j)),
            pl.BlockSpec((1, D_MODEL), lambda i, j: (0, 0)),
            pl.BlockSpec((1, D_MODEL), lambda i, j: (0, 0)),
            pl.BlockSpec((D_MODEL, TF), lambda i, j: (0, j)),
            pl.BlockSpec((D_MODEL, TF), lambda i, j: (0, j)),
            pl.BlockSpec((FFN_CONV, TF), lambda i, j: (0, j)),
            pl.BlockSpec((1, TF), lambda i, j: (0, j)),
            pl.BlockSpec((TF, D_MODEL), lambda i, j: (j, 0)),
        ],
        out_specs=[
            pl.BlockSpec((TM, D_MODEL), lambda i, j: (i, 0)),
            pl.BlockSpec((DEC_BATCH, D_MODEL), lambda i, j: (0, 0)),
            pl.BlockSpec((N_UST, TF), lambda i, j: (i, j)),
        ],
        out_shape=[
            jax.ShapeDtypeStruct((SEQ, D_MODEL), F32),
            jax.ShapeDtypeStruct((DEC_BATCH, D_MODEL), F32),
            jax.ShapeDtypeStruct((N_RT * N_UST, D_FF), F32),
        ],
        scratch_shapes=[
            pltpu.VMEM((FH + TM, D_MODEL), BF16),
            pltpu.VMEM((FH + TM, TF), F32),
            pltpu.VMEM((TM, TF), BF16),
        ],
        compiler_params=pltpu.CompilerParams(
            dimension_semantics=("arbitrary", "arbitrary"), vmem_limit_bytes=VMEM_LIMIT_BIG),
        name="ffn",
    )(x1, x1, x1s, st, st, gn, gf, wu, wg, cw, cb, wd)


def _gate_weights(wa, wx):
    per = GW // RNN_BLOCK
    eye = jnp.eye(per, dtype=wa.dtype)

    def bd(w):
        w = w.reshape(N_GW, per, RNN_BLOCK, RNN_BLOCK)
        return jnp.einsum('gpcd,pq->gpcqd', w, eye).reshape(N_GW, GW, GW)

    return jnp.concatenate([bd(wa), bd(wx)], axis=-1).astype(BF16)


def kernel(x_prompt, x_sample, state_rnn_conv, state_rnn_h, cache_k_win, cache_v_win, state_ffn_conv,
           meta_tokens, rel_bias_table, norm_mix, w_in, rnn_conv_w, rnn_conv_b, gate_a_w, gate_a_b,
           gate_x_w, gate_x_b, rnn_lambda, attn_sinks, norm_rnn_out, norm_attn_out, w_out, norm_ffn,
           w_up, w_gate, ffn_conv_w, ffn_conv_b, w_down, norm_final):
    l = 0
    xp = x_prompt[0]
    xs = x_sample[:, 0, :]
    xt = jnp.concatenate([xs, jnp.zeros((N_PAD, D_MODEL), F32), meta_tokens], axis=0)
    row = lambda v: v.reshape(1, -1)

    qi = np.arange(BLOCK)[:, None]
    sj = np.arange(2 * BLOCK)[None, :]
    bucket_p = jnp.asarray(_rel_buckets(BLOCK + qi - sj))
    pos = (np.arange(SB * WINDOW * N_KV) >> 1) & (WINDOW - 1)
    bucket_s = jnp.asarray(np.tile(_rel_buckets(WINDOW - pos)[None, :], (HALO, 1)))

    wg = _gate_weights(gate_a_w[l], gate_x_w[l])
    vecs = (rnn_conv_b[l], gate_a_b[l], gate_x_b[l], rnn_lambda[l], norm_rnn_out[l], norm_attn_out[l])
    vp = jnp.concatenate([jnp.stack(vecs), jnp.zeros((HALO - N_VP, D_RNN), F32)], axis=0)
    seq_w = (rnn_conv_w[l], wg, vp)

    x1, zs, h_last, kv_last, xr_tail, w_out_b, w_up_b, w_gate_b, w_down_b = _front(
        rel_bias_table, attn_sinks[l], xp, xt, row(norm_mix[l]), w_in[l], w_out[l], bucket_p, *seq_w,
        w_up[l], w_gate[l], w_down[l])
    x1s, conv_s, h_s, k_s, v_s = _smixer(
        rel_bias_table, attn_sinks[l], zs,
        jnp.swapaxes(state_rnn_conv[l], 0, 1), state_rnn_h[l],
        cache_k_win[l].reshape(DEC_BATCH * WINDOW * N_KV, HEAD_DIM),
        cache_v_win[l].reshape(DEC_BATCH * WINDOW * N_KV, HEAD_DIM),
        xs, w_out_b, bucket_s, *seq_w)
    y_p, y_s, ust = _ffn(x1, x1s, state_ffn_conv[l].reshape(DEC_BATCH, (FFN_CONV - 1) * D_FF),
                         row(norm_ffn[l]), row(norm_final), w_up_b, w_gate_b,
                         ffn_conv_w[l], row(ffn_conv_b[l]), w_down_b)

    ust = ust[(N_RT - 1) * N_UST:]
    p_states = (
        xr_tail[HALO - (RNN_CONV - 1):HALO][None, None],
        h_last[0:1][None],
        kv_last[:, :D_KV].reshape(1, 1, WINDOW, N_KV, HEAD_DIM),
        kv_last[:, D_KV:].reshape(1, 1, WINDOW, N_KV, HEAD_DIM),
        ust[HALO - (FFN_CONV - 1):HALO][None, None],
    )
    s_states = (
        jnp.swapaxes(conv_s, 0, 1)[None],
        h_s[None],
        k_s.reshape(1, DEC_BATCH, WINDOW, N_KV, HEAD_DIM),
        v_s.reshape(1, DEC_BATCH, WINDOW, N_KV, HEAD_DIM),
        jnp.stack([state_ffn_conv[l][:, FFN_CONV - 2, :], ust[HALO:]], axis=1)[None],
    )
    return (y_p[None], y_s[:, None, :]) + p_states + s_states
```

```python
import math

import numpy as np
import jax
import jax.numpy as jnp
from jax import lax
from jax.experimental import pallas as pl
from jax.experimental.pallas import tpu as pltpu

F32 = jnp.float32
BF16 = jnp.bfloat16

D_MODEL = 2048
SEQ = 8192
DEC_BATCH = 128
D_RNN = 1024
N_RNN_BLOCKS = 16
RNN_BLOCK = D_RNN // N_RNN_BLOCKS
RNN_CONV = 4
LRU_C = 8.0
N_HEADS = 8
HEAD_DIM = 128
N_KV = 2
GROUP = N_HEADS // N_KV
D_ATTN = N_HEADS * HEAD_DIM
WINDOW = 128
BLOCK = 128
NUM_BUCKETS = 32
MAX_DISTANCE = 128
D_FF = 3 * D_MODEL
FFN_CONV = 3
N_META = 16
EPS = 1e-6
NEG = -1e30
D_KV = N_KV * HEAD_DIM
D_IN = 2 * D_RNN + D_ATTN + 2 * D_KV
SCALE = HEAD_DIM ** -0.5
O_GR = D_RNN
O_Q = 2 * D_RNN
O_KV = 2 * D_RNN + D_ATTN

N_PAD = BLOCK - N_META
R_TAIL = DEC_BATCH + BLOCK
R_ALL = SEQ + R_TAIL
N_PBLK = SEQ // BLOCK

FB = 2 * BLOCK
N_FSTEP = (N_PBLK + 2) // 2
PC = 512
N_IN_EARLY = 5
N_STAGE = 4
N_SLAB = N_FSTEP - 1

TM = 640
N_RT = (SEQ + DEC_BATCH) // TM
LAST_P = SEQ - (N_RT - 1) * TM
TF = 768
N_FT = D_FF // TF
KC = 256
HALO = 8
FH = 16
GW = 256
N_GW = D_RNN // GW
SB = 16
N_RING = 3
N_VP = 6
VMEM_LIMIT = 56 * 1024 * 1024
VMEM_LIMIT_BIG = 60 * 1024 * 1024


def _rms(x, g):
    return x * lax.rsqrt(jnp.mean(x * x, axis=-1, keepdims=True) + EPS) * g


def _rel_buckets(d):
    d = np.maximum(d, 0)
    exact = NUM_BUCKETS // 2
    ratio = np.maximum(d, 1).astype(np.float32) / np.float32(exact)
    large = exact + (np.log(ratio) / np.float32(math.log(MAX_DISTANCE / exact))
                     * np.float32(NUM_BUCKETS - exact)).astype(np.int32)
    large = np.minimum(large, NUM_BUCKETS - 1)
    return np.where(d < exact, d, large).astype(np.int32)


def _table_lookup(bucket, tab_ref):
    outs = [jnp.zeros(bucket.shape, F32) for _ in range(N_HEADS)]
    for b in range(NUM_BUCKETS):
        hit = bucket == b
        outs = [jnp.where(hit, tab_ref[b, h], o) for h, o in enumerate(outs)]
    return outs


def _gates(xc, wg_ref, gab, gxb, lam):
    xcb = xc.astype(BF16)
    ga, gx = [], []
    for j in range(N_GW):
        gj = jnp.dot(xcb[:, GW * j:GW * (j + 1)], wg_ref[j], preferred_element_type=F32)
        ga.append(gj[:, :GW])
        gx.append(gj[:, GW:])
    r = jax.nn.sigmoid(jnp.concatenate(ga, axis=1) + gab)
    i = jax.nn.sigmoid(jnp.concatenate(gx, axis=1) + gxb)
    log_a = -LRU_C * r * jax.nn.softplus(-lam)
    a = jnp.exp(log_a)
    t = 1.0 - a * a
    b = jnp.where(t > 0.0, t * lax.rsqrt(t), 0.0) * i * xc
    return a, b


def _front_kernel(tab_ref, sink_ref, xa_ref, xb_ref, xt_ref, gm_ref, win_f, wout_f, bucket_ref,
                  cw_ref, wg_ref, vp_ref, wu_f, wgt_f, wd_f,
                  x1_ref, zs_ref, hlast_ref, kvlast_ref, xrt_ref, wout_hbm, wu_b, wgt_b, wd_b,
                  win_ref, wout_ref, zp, zb_o, lhs_p, xsp, mixp, xbuf, h_scr, kvbuf, bias_scr,
                  su, sg, sd, tu, tg, td, sem_in, sem_out, sem_w):
    k = pl.program_id(0)
    last = N_FSTEP - 1
    cb_v, gab_v, gxb_v, lam_v, nr_v, na_v = (vp_ref[r:r + 1] for r in range(N_VP))

    stages = ((wu_f, su, tu, wu_b), (wgt_f, sg, tg, wgt_b), (wd_f, sd, td, wd_b))

    def slab_in(s):
        return [pltpu.make_async_copy(w.at[pl.ds(s * st.shape[0], st.shape[0])], st, sem_in.at[n])
                for n, (w, st, _, _) in enumerate(stages)]

    def slab_out(s):
        return [pltpu.make_async_copy(t, w.at[pl.ds(s * t.shape[0], t.shape[0])], sem_out.at[n])
                for n, (_, _, t, w) in enumerate(stages)]

    @pl.when(k == 0)
    def _():
        for c in slab_in(0):
            c.start()

    @pl.when(k >= 1)
    def _():
        for c in slab_out(k - 1):
            c.wait()

    @pl.when(k < N_SLAB)
    def _():
        for c in slab_in(k):
            c.wait()
        for _, st, t, _ in stages:
            t[...] = st[...].astype(BF16)
        for c in slab_out(k):
            c.start()

    @pl.when(k < N_SLAB - 1)
    def _():
        for c in slab_in(k + 1):
            c.start()

    wout_copy = pltpu.make_async_copy(wout_ref, wout_hbm, sem_w.at[2 * N_STAGE])

    @pl.when(k == 0)
    def _():
        streams = ((win_f, win_ref, zp, 0), (wout_f, wout_ref, xsp, N_STAGE))
        rows = FB // N_STAGE
        n = D_MODEL // rows

        def chunk(stream, c):
            w_f, _, buf, sem0 = stream
            slot = c % N_STAGE
            return pltpu.make_async_copy(w_f.at[pl.ds(c * rows, rows)], buf.at[pl.ds(slot * rows, rows)],
                                         sem_w.at[sem0 + slot])

        for c in range(N_STAGE - 1):
            for stream in streams:
                chunk(stream, c).start()
        for c in range(n):
            for stream in streams:
                if c + N_STAGE - 1 < n:
                    chunk(stream, c + N_STAGE - 1).start()
                chunk(stream, c).wait()
                _, w_b, buf, _ = stream
                w_b[c * rows:(c + 1) * rows] = buf[pl.ds((c % N_STAGE) * rows, rows)].astype(BF16)
        wout_copy.start()

    def inproj_chunk(c):
        def run():
            cs = slice(c * PC, (c + 1) * PC)
            zp[:, cs] = jnp.dot(lhs_p[...], win_ref[:, cs], preferred_element_type=F32)
        return run

    def outproj_chunk(c):
        def run():
            cs = slice(c * PC, (c + 1) * PC)
            x1_ref[:, cs] = xsp[:, cs] + jnp.dot(mixp[...], wout_ref[:, cs], preferred_element_type=F32)
        return run

    @pl.when(k == 0)
    def _():
        prefix = xt_ref[DEC_BATCH:R_TAIL]
        zb_o[...] = jnp.dot(_rms(prefix, gm_ref[...]).astype(BF16), win_ref[...], preferred_element_type=F32)
        xsp[0:BLOCK] = jnp.zeros((BLOCK, D_MODEL), F32)
        xsp[BLOCK:FB] = prefix
        mixp[...] = jnp.zeros((FB, D_MODEL), BF16)
        xbuf[0:HALO] = jnp.zeros((HALO, D_RNN), F32)
        h_scr[...] = jnp.zeros((HALO, D_RNN), F32)
        kvbuf[...] = jnp.zeros((BLOCK, 2 * D_KV), F32)
        for h, bias in enumerate(_table_lookup(bucket_ref[...], tab_ref)):
            bias_scr[h] = bias

    def mixer(zb, is_prefix, first_key, fillers):
        fill = iter(fillers)

        def between():
            f = next(fill, None)
            if f is not None:
                f()

        between()
        xr = zb[:, 0:D_RNN]
        xbuf[HALO:HALO + BLOCK] = xr
        cw = cw_ref[...]
        xc = cb_v
        for j in range(RNN_CONV - 1):
            lo = HALO - (RNN_CONV - 1) + j
            xc = xc + xbuf[lo:lo + BLOCK] * cw[j:j + 1]
        xc = xc + xr * cw[RNN_CONV - 1:RNN_CONV]
        xbuf[0:HALO] = xr[BLOCK - HALO:BLOCK]

        a, b = _gates(xc, wg_ref, gab_v, gxb_v, lam_v)
        between()
        h = h_scr[0:1]
        if is_prefix is not None:
            row = lax.broadcasted_iota(jnp.int32, (BLOCK, D_RNN), 0)
            b = jnp.where(jnp.logical_and(is_prefix, row < N_PAD), 0.0, b)
            h = jnp.where(is_prefix, 0.0, h)

        ng = BLOCK // HALO
        a3 = a.reshape(ng, HALO, D_RNN)
        b3 = b.reshape(ng, HALO, D_RNN)
        sub = lax.broadcasted_iota(jnp.int32, (ng, HALO, D_RNN), 1)
        sh = 1
        while sh < HALO:
            a_prev = pltpu.roll(a3, sh, 1)
            b_prev = pltpu.roll(b3, sh, 1)
            m = sub >= sh
            b3 = jnp.where(m, a3 * b_prev + b3, b3)
            a3 = jnp.where(m, a3 * a_prev, a3)
            sh *= 2
        hs = []
        for g in range(ng):
            hg = a3[g] * h + b3[g]
            hs.append(hg)
            h = hg[HALO - 1:HALO]
        h_all = jnp.concatenate(hs, axis=0)
        h_scr[...] = jnp.broadcast_to(h, (HALO, D_RNN))
        between()
        y_rnn = h_all * jax.nn.gelu(zb[:, O_GR:O_GR + D_RNN])
        mix_a = _rms(y_rnn, nr_v).astype(BF16)

        q = zb[:, O_Q:O_Q + D_ATTN]
        kv = zb[:, O_KV:O_KV + 2 * D_KV]
        kvp = kvbuf[...]
        kvbuf[...] = kv
        col = lax.broadcasted_iota(jnp.int32, (BLOCK, 2 * BLOCK), 1)
        rowq = lax.broadcasted_iota(jnp.int32, (BLOCK, 2 * BLOCK), 0)
        d = BLOCK + rowq - col
        mask = (d >= 0) & (d < WINDOW) & (col >= first_key)
        outs = []
        for kh in range(N_KV):
            between()
            ks = slice(kh * HEAD_DIM, (kh + 1) * HEAD_DIM)
            vs = slice(D_KV + kh * HEAD_DIM, D_KV + (kh + 1) * HEAD_DIM)
            qs = jnp.concatenate(
                [q[:, (kh * GROUP + g) * HEAD_DIM:(kh * GROUP + g + 1) * HEAD_DIM] for g in range(GROUP)],
                axis=0).astype(BF16)
            kk = jnp.concatenate([kvp[:, ks], kv[:, ks]], axis=0).astype(BF16)
            vv = jnp.concatenate([kvp[:, vs], kv[:, vs]], axis=0).astype(BF16)
            sc = lax.dot_general(qs, kk, (((1,), (1,)), ((), ())), preferred_element_type=F32)
            es, dens = [], []
            for g in range(GROUP):
                hh = kh * GROUP + g
                lg = sc[g * BLOCK:(g + 1) * BLOCK] * SCALE + bias_scr[hh]
                lg = jnp.where(mask, lg, NEG)
                sink = sink_ref[hh]
                mx = jnp.maximum(jnp.max(lg, axis=-1, keepdims=True), sink)
                e = jnp.exp(lg - mx)
                dens.append(jnp.sum(e, axis=-1, keepdims=True) + jnp.exp(sink - mx))
                es.append(e.astype(BF16))
            pv = jnp.dot(jnp.concatenate(es, axis=0), vv, preferred_element_type=F32)
            for g in range(GROUP):
                outs.append(pv[g * BLOCK:(g + 1) * BLOCK] / dens[g])
        between()
        mix_b = _rms(jnp.concatenate(outs, axis=1), na_v).astype(BF16)

        return jnp.concatenate([mix_a, mix_b], axis=1), h, kv, xr

    xa = jnp.where(k == last, xt_ref[0:DEC_BATCH], xa_ref[...])
    xb = xb_ref[...]
    lhs_p[0:BLOCK] = _rms(xa, gm_ref[...]).astype(BF16)
    lhs_p[BLOCK:FB] = _rms(xb, gm_ref[...]).astype(BF16)
    n_in, n_out = D_IN // PC, D_MODEL // PC
    mix_o, h, kv, xr = mixer(zb_o, k == 0, jnp.where(k == 0, BLOCK + N_PAD, 0),
                             [inproj_chunk(c) for c in range(N_IN_EARLY)])
    mixp[BLOCK:FB] = mix_o
    hlast_ref[...] = jnp.broadcast_to(h, (HALO, D_RNN))
    kvlast_ref[...] = kv
    xrt_ref[...] = xr[BLOCK - HALO:BLOCK]

    mix_e = mixer(zp.at[0:BLOCK], None, jnp.where(k == 0, N_PAD, 0),
                  [inproj_chunk(c) for c in range(N_IN_EARLY, n_in)] + [outproj_chunk(c) for c in range(n_out)])[0]
    mixp[0:BLOCK] = mix_e
    xsp[0:BLOCK] = xa
    xsp[BLOCK:FB] = xb
    zb_o[...] = zp[BLOCK:FB]

    @pl.when(k == last)
    def _():
        zs_ref[...] = zp[0:BLOCK]
        wout_copy.wait()


def _front(tab, sinks, xp, xt, gm, w_in, w_out, bucket, cw, wg, vp, w_up, w_gate, w_down):
    ffn_w = (w_up, w_gate, w_down)
    slab = lambda w: (w.shape[0] // N_SLAB, w.shape[1])
    hbm = pl.BlockSpec(memory_space=pl.ANY)
    vec = lambda n: pl.BlockSpec((1, n), lambda k: (0, 0))
    smem = pl.BlockSpec(memory_space=pltpu.SMEM)
    once = lambda shape: pl.BlockSpec(shape, lambda k: (0,) * len(shape), pipeline_mode=pl.Buffered(1))
    return pl.pallas_call(
        _front_kernel,
        grid=(N_FSTEP,),
        in_specs=[
            smem, smem,
            pl.BlockSpec((BLOCK, D_MODEL), lambda k: (jnp.minimum(2 * k, N_PBLK - 1), 0)),
            pl.BlockSpec((BLOCK, D_MODEL), lambda k: (jnp.minimum(2 * k + 1, N_PBLK - 1), 0)),
            once((R_TAIL, D_MODEL)),
            vec(D_MODEL),
            hbm, hbm,
            pl.BlockSpec((BLOCK, 2 * BLOCK), lambda k: (0, 0)),
            pl.BlockSpec((RNN_CONV, D_RNN), lambda k: (0, 0)),
            pl.BlockSpec((N_GW, GW, 2 * GW), lambda k: (0, 0, 0)),
            pl.BlockSpec((HALO, D_RNN), lambda k: (0, 0)),
            hbm, hbm, hbm,
        ],
        out_specs=[
            pl.BlockSpec((FB, D_MODEL), lambda k: (jnp.where(k == 0, SEQ // FB, k - 1), 0)),
            pl.BlockSpec((BLOCK, D_IN), lambda k: (0, 0)),
            pl.BlockSpec((HALO, D_RNN), lambda k: (0, 0)),
            pl.BlockSpec((BLOCK, 2 * D_KV), lambda k: (0, 0)),
            pl.BlockSpec((HALO, D_RNN), lambda k: (0, 0)),
            hbm, hbm, hbm, hbm,
        ],
        out_shape=[
            jax.ShapeDtypeStruct((R_ALL, D_MODEL), F32),
            jax.ShapeDtypeStruct((DEC_BATCH, D_IN), F32),
            jax.ShapeDtypeStruct((HALO, D_RNN), F32),
            jax.ShapeDtypeStruct((BLOCK, 2 * D_KV), F32),
            jax.ShapeDtypeStruct((HALO, D_RNN), F32),
        ] + [jax.ShapeDtypeStruct(w.shape, BF16) for w in (w_out,) + ffn_w],
        scratch_shapes=[
            pltpu.VMEM(w_in.shape, BF16),
            pltpu.VMEM(w_out.shape, BF16),
            pltpu.VMEM((FB, D_IN), F32),
            pltpu.VMEM((BLOCK, D_IN), F32),
            pltpu.VMEM((FB, D_MODEL), BF16),
            pltpu.VMEM((FB, D_MODEL), F32),
            pltpu.VMEM((FB, D_MODEL), BF16),
            pltpu.VMEM((HALO + BLOCK, D_RNN), F32),
            pltpu.VMEM((HALO, D_RNN), F32),
            pltpu.VMEM((BLOCK, 2 * D_KV), F32),
            pltpu.VMEM((N_HEADS, BLOCK, 2 * BLOCK), F32),
        ] + [pltpu.VMEM(slab(w), F32) for w in ffn_w] + [pltpu.VMEM(slab(w), BF16) for w in ffn_w] + [
            pltpu.SemaphoreType.DMA((len(ffn_w),)),
            pltpu.SemaphoreType.DMA((len(ffn_w),)),
            pltpu.SemaphoreType.DMA((2 * N_STAGE + 1,)),
        ],
        compiler_params=pltpu.CompilerParams(
            dimension_semantics=("arbitrary",), vmem_limit_bytes=VMEM_LIMIT_BIG),
        name="front",
    )(tab, sinks, xp, xp, xt, gm, w_in, w_out, bucket, cw, wg, vp, *ffn_w)


def _smixer_kernel(tab_ref, sink_ref, z_ref, st_ref, h0_ref, kc_hbm, vc_hbm, xs_ref, wout_hbm,
                   bucket_ref, cw_ref, wg_ref, vp_ref,
                   x1s_ref, conv_ref, h_ref, ko_ref, vo_ref, bias_scr, mix_scr, wout_ref, sem_w,
                   k_ring, v_ring, sem_r):
    c = pl.program_id(0)

    n_chunk = DEC_BATCH // SB
    rows_c = SB * WINDOW * N_KV

    def fetch(ch):
        slot = ch % N_RING
        return [pltpu.make_async_copy(src.at[pl.ds(ch * rows_c, rows_c)], ring.at[slot], sem_r.at[n, slot])
                for n, (src, ring) in enumerate(((kc_hbm, k_ring), (vc_hbm, v_ring)))]

    @pl.when(c == 0)
    def _():
        for ch in range(N_RING - 1):
            for cp in fetch(ch):
                cp.start()

    @pl.when(c + N_RING - 1 < n_chunk)
    def _():
        for cp in fetch(c + N_RING - 1):
            cp.start()
    cb_v, gab_v, gxb_v, lam_v, nr_v, na_v = (vp_ref[r:r + 1] for r in range(N_VP))
    nrow = N_HEADS * SB
    ncol = SB * WINDOW * N_KV
    wout_copy = pltpu.make_async_copy(wout_hbm, wout_ref, sem_w)

    @pl.when(c == 0)
    def _():
        wout_copy.start()
        col = lax.broadcasted_iota(jnp.int32, (SB, ncol), 1)
        row = lax.broadcasted_iota(jnp.int32, (SB, ncol), 0)
        own = ((col >> 8) == row) & (((col >> 1) & (WINDOW - 1)) >= 1)
        for h, bias in enumerate(_table_lookup(bucket_ref[...], tab_ref)):
            ok = own & ((col & 1) == h // GROUP)
            tiled = jnp.concatenate([bias] * SB, axis=1)
            bias_scr[h * SB:(h + 1) * SB] = jnp.where(ok, jnp.concatenate([tiled] * (SB // HALO), axis=0), NEG)

    r0 = pl.multiple_of(c * SB, SB)
    for cp in fetch(c):
        cp.wait()
    kc_ref = k_ring.at[c % N_RING]
    vc_ref = v_ring.at[c % N_RING]

    xr = z_ref[:, 0:D_RNN]
    cw = cw_ref[...]
    xc = cb_v
    for j in range(RNN_CONV - 1):
        xc = xc + st_ref[j] * cw[j:j + 1]
    xc = xc + xr * cw[RNN_CONV - 1:RNN_CONV]
    for j in range(1, RNN_CONV - 1):
        conv_ref[j - 1] = st_ref[j]
    conv_ref[RNN_CONV - 2] = xr
    a, b = _gates(xc, wg_ref, gab_v, gxb_v, lam_v)
    h = a * h0_ref[...] + b
    h_ref[...] = h
    y_rnn = h * jax.nn.gelu(z_ref[:, O_GR:O_GR + D_RNN])
    mix_scr[pl.ds(r0, SB), 0:D_RNN] = _rms(y_rnn, nr_v).astype(BF16)

    q = z_ref[:, O_Q:O_Q + D_ATTN]
    kv = z_ref[:, O_KV:O_KV + 2 * D_KV]
    qs = jnp.concatenate([q[:, h * HEAD_DIM:(h + 1) * HEAD_DIM] for h in range(N_HEADS)], axis=0).astype(BF16)
    new_rows = lambda off: jnp.concatenate(
        [kv[:, off + (h // GROUP) * HEAD_DIM:off + (h // GROUP + 1) * HEAD_DIM] for h in range(N_HEADS)],
        axis=0).astype(BF16).astype(F32)
    k_new = new_rows(0)
    v_new = new_rows(D_KV)
    sc = lax.dot_general(qs, kc_ref[...].astype(BF16), (((1,), (1,)), ((), ())), preferred_element_type=F32)
    lg = sc * SCALE + bias_scr[...]
    rh = lax.broadcasted_iota(jnp.int32, (nrow, 1), 0) >> 4
    sink = jnp.zeros((nrow, 1), F32)
    bias_new = jnp.zeros((nrow, 1), F32)
    for h in range(N_HEADS):
        sink = jnp.where(rh == h, sink_ref[h], sink)
        bias_new = jnp.where(rh == h, tab_ref[0, h], bias_new)
    lg_new = jnp.sum(qs.astype(F32) * k_new, axis=-1, keepdims=True) * SCALE + bias_new
    mx = jnp.maximum(jnp.maximum(jnp.max(lg, axis=-1, keepdims=True), lg_new), sink)
    e = jnp.exp(lg - mx)
    e_new = jnp.exp(lg_new - mx)
    den = jnp.sum(e, axis=-1, keepdims=True) + e_new + jnp.exp(sink - mx)
    pv = jnp.dot(e.astype(BF16), vc_ref[...].astype(BF16), preferred_element_type=F32)
    o = (pv + e_new.astype(BF16).astype(F32) * v_new) / den
    y_attn = jnp.concatenate([o[h * SB:(h + 1) * SB] for h in range(N_HEADS)], axis=1)
    mix_scr[pl.ds(r0, SB), D_RNN:] = _rms(y_attn, na_v).astype(BF16)

    per = WINDOW * N_KV
    slide = lambda ref: pltpu.roll(ref[...].reshape(SB, per, HEAD_DIM), per - N_KV, 1).reshape(SB * per, HEAD_DIM)
    ko_ref[...] = slide(kc_ref)
    vo_ref[...] = slide(vc_ref)
    for bi in range(SB):
        for kh in range(N_KV):
            r = (bi + 1) * per - N_KV + kh
            ko_ref[r:r + 1, :] = kv[bi:bi + 1, kh * HEAD_DIM:(kh + 1) * HEAD_DIM]
            vo_ref[r:r + 1, :] = kv[bi:bi + 1, D_KV + kh * HEAD_DIM:D_KV + (kh + 1) * HEAD_DIM]

    @pl.when(c == DEC_BATCH // SB - 1)
    def _():
        wout_copy.wait()
        x1s_ref[...] = xs_ref[...] + jnp.dot(mix_scr[...], wout_ref[...], preferred_element_type=F32)


def _smixer(tab, sinks, zs, st, h0, kc, vc, xs, w_out, bucket, cw, wg, vp):
    smem = pl.BlockSpec(memory_space=pltpu.SMEM)
    cache = pl.BlockSpec((SB * WINDOW * N_KV, HEAD_DIM), lambda c: (c, 0))
    return pl.pallas_call(
        _smixer_kernel,
        grid=(DEC_BATCH // SB,),
        in_specs=[
            smem, smem,
            pl.BlockSpec((SB, D_IN), lambda c: (c, 0)),
            pl.BlockSpec((RNN_CONV - 1, SB, D_RNN), lambda c: (0, c, 0)),
            pl.BlockSpec((SB, D_RNN), lambda c: (c, 0)),
            pl.BlockSpec(memory_space=pl.ANY), pl.BlockSpec(memory_space=pl.ANY),
            pl.BlockSpec((DEC_BATCH, D_MODEL), lambda c: (0, 0)),
            pl.BlockSpec(memory_space=pl.ANY),
            pl.BlockSpec((HALO, WINDOW * N_KV), lambda c: (0, 0)),
            pl.BlockSpec((RNN_CONV, D_RNN), lambda c: (0, 0)),
            pl.BlockSpec((N_GW, GW, 2 * GW), lambda c: (0, 0, 0)),
            pl.BlockSpec((HALO, D_RNN), lambda c: (0, 0)),
        ],
        out_specs=[
            pl.BlockSpec((DEC_BATCH, D_MODEL), lambda c: (0, 0)),
            pl.BlockSpec((RNN_CONV - 1, SB, D_RNN), lambda c: (0, c, 0)),
            pl.BlockSpec((SB, D_RNN), lambda c: (c, 0)),
            cache, cache,
        ],
        out_shape=[
            jax.ShapeDtypeStruct((DEC_BATCH, D_MODEL), F32),
            jax.ShapeDtypeStruct((RNN_CONV - 1, DEC_BATCH, D_RNN), F32),
            jax.ShapeDtypeStruct((DEC_BATCH, D_RNN), F32),
            jax.ShapeDtypeStruct((DEC_BATCH * WINDOW * N_KV, HEAD_DIM), F32),
            jax.ShapeDtypeStruct((DEC_BATCH * WINDOW * N_KV, HEAD_DIM), F32),
        ],
        scratch_shapes=[
            pltpu.VMEM((N_HEADS * SB, SB * WINDOW * N_KV), F32),
            pltpu.VMEM((DEC_BATCH, D_MODEL), BF16),
            pltpu.VMEM((D_MODEL, D_MODEL), BF16),
            pltpu.SemaphoreType.DMA(()),
            pltpu.VMEM((N_RING, SB * WINDOW * N_KV, HEAD_DIM), F32),
            pltpu.VMEM((N_RING, SB * WINDOW * N_KV, HEAD_DIM), F32),
            pltpu.SemaphoreType.DMA((2, N_RING)),
        ],
        compiler_params=pltpu.CompilerParams(
            dimension_semantics=("arbitrary",), vmem_limit_bytes=VMEM_LIMIT),
        name="smixer",
    )(tab, sinks, zs, st, h0, kc, vc, xs, w_out, bucket, cw, wg, vp)


U_LO = LAST_P - HALO
U_HI = LAST_P + DEC_BATCH
N_UST = U_HI - U_LO


def _ffn_kernel(x1_ref, halo_ref, x1s_ref, st0_ref, st1_ref, gn_ref, gf_ref, wu_ref, wg_ref, cw_ref, cb_ref,
                wd_ref, y_ref, ys_ref, ust_ref, h2_scr, ubuf, abuf):
    i = pl.program_id(0)
    j = pl.program_id(1)

    @pl.when(j == 0)
    def _():
        h2_scr[0:FH] = _rms(halo_ref[...], gn_ref[...]).astype(BF16)
        h2_scr[FH:FH + TM] = _rms(x1_ref[...], gn_ref[...]).astype(BF16)
        y_ref[...] = jnp.zeros((TM, D_MODEL), F32)

        @pl.when(i == N_RT - 1)
        def _():
            h2_scr[FH + LAST_P:FH + U_HI] = _rms(x1s_ref[...], gn_ref[...]).astype(BF16)

    ubuf[...] = jnp.dot(h2_scr[...], wu_ref[...], preferred_element_type=F32)
    gate = jnp.dot(h2_scr[FH:FH + TM], wg_ref[...], preferred_element_type=F32)
    cw = cw_ref[...]
    cb = cb_ref[...]
    for c in range(TF // KC):
        cs = slice(c * KC, (c + 1) * KC)
        tap = lambda r0, n: ubuf[r0:r0 + n, cs]
        uc = (cb[:, cs] + tap(FH - 2, TM) * cw[0:1, cs] + tap(FH - 1, TM) * cw[1:2, cs] + tap(FH, TM) * cw[2:3, cs])
        act = jax.nn.gelu(uc) * gate[:, cs]
        abuf[:, cs] = act.astype(BF16)
        ucs = (cb[:, cs] + st0_ref[:, cs] * cw[0:1, cs] + st1_ref[:, cs] * cw[1:2, cs]
               + tap(FH + LAST_P, DEC_BATCH) * cw[2:3, cs])
        act_s = jax.nn.gelu(ucs) * gate[LAST_P:U_HI, cs]
        abuf[LAST_P:U_HI, cs] = jnp.where(i == N_RT - 1, act_s, act[LAST_P:U_HI]).astype(BF16)
        y_ref[...] += jnp.dot(abuf[:, cs], wd_ref[cs, :], preferred_element_type=F32)
    ust_ref[...] = ubuf[FH + U_LO:FH + U_HI]

    @pl.when(j == N_FT - 1)
    def _():
        @pl.when(i == N_RT - 1)
        def _():
            ys_ref[...] = _rms(x1s_ref[...] + y_ref[LAST_P:U_HI], gf_ref[...])

        y_ref[...] = _rms(x1_ref[...] + y_ref[...], gf_ref[...])


def _ffn(x1, x1s, st, gn, gf, wu, wg, cw, cb, wd):
    def halo_idx(i, j):
        return (jnp.where(i == 0, R_ALL // FH - 1, i * (TM // FH) - 1), 0)

    return pl.pallas_call(
        _ffn_kernel,
        grid=(N_RT, N_FT),
        in_specs=[
            pl.BlockSpec((TM, D_MODEL), lambda i, j: (i, 0)),
            pl.BlockSpec((FH, D_MODEL), halo_idx),
            pl.BlockSpec((DEC_BATCH, D_MODEL), lambda i, j: (0, 0)),
            pl.BlockSpec((DEC_BATCH, TF), lambda i, j: (0, j)),
            pl.BlockSpec((DEC_BATCH, TF), lambda i, j: (0, N_FT + j)),
            pl.BlockSpec((1, D_MODEL), lambda i, j: (0, 0)),
            pl.BlockSpec((1, D_MODEL), lambda i, j: (0, 0)),
            pl.BlockSpec((D_MODEL, TF), lambda i, j: (0, j)),
            pl.BlockSpec((D_MODEL, TF), lambda i, j: (0, j)),
            pl.BlockSpec((FFN_CONV, TF), lambda i, j: (0, j)),
            pl.BlockSpec((1, TF), lambda i, j: (0, j)),
            pl.BlockSpec((TF, D_MODEL), lambda i, j: (j, 0)),
        ],
        out_specs=[
            pl.BlockSpec((TM, D_MODEL), lambda i, j: (i, 0)),
            pl.BlockSpec((DEC_BATCH, D_MODEL), lambda i, j: (0, 0)),
            pl.BlockSpec((N_UST, TF), lambda i, j: (i, j)),
        ],
        out_shape=[
            jax.ShapeDtypeStruct((SEQ, D_MODEL), F32),
            jax.ShapeDtypeStruct((DEC_BATCH, D_MODEL), F32),
            jax.ShapeDtypeStruct((N_RT * N_UST, D_FF), F32),
        ],
        scratch_shapes=[
            pltpu.VMEM((FH + TM, D_MODEL), BF16),
            pltpu.VMEM((FH + TM, TF), F32),
            pltpu.VMEM((TM, TF), BF16),
        ],
        compiler_params=pltpu.CompilerParams(
            dimension_semantics=("arbitrary", "arbitrary"), vmem_limit_bytes=VMEM_LIMIT_BIG),
        name="ffn",
    )(x1, x1, x1s, st, st, gn, gf, wu, wg, cw, cb, wd)


def _gate_weights(wa, wx):
    per = GW // RNN_BLOCK
    eye = jnp.eye(per, dtype=wa.dtype)

    def bd(w):
        w = w.reshape(N_GW, per, RNN_BLOCK, RNN_BLOCK)
        return jnp.einsum('gpcd,pq->gpcqd', w, eye).reshape(N_GW, GW, GW)

    return jnp.concatenate([bd(wa), bd(wx)], axis=-1).astype(BF16)


def kernel(x_prompt, x_sample, state_rnn_conv, state_rnn_h, cache_k_win, cache_v_win, state_ffn_conv,
           meta_tokens, rel_bias_table, norm_mix, w_in, rnn_conv_w, rnn_conv_b, gate_a_w, gate_a_b,
           gate_x_w, gate_x_b, rnn_lambda, attn_sinks, norm_rnn_out, norm_attn_out, w_out, norm_ffn,
           w_up, w_gate, ffn_conv_w, ffn_conv_b, w_down, norm_final):
    l = 0
    xp = x_prompt[0]
    xs = x_sample[:, 0, :]
    xt = jnp.concatenate([xs, jnp.zeros((N_PAD, D_MODEL), F32), meta_tokens], axis=0)
    row = lambda v: v.reshape(1, -1)

    qi = np.arange(BLOCK)[:, None]
    sj = np.arange(2 * BLOCK)[None, :]
    bucket_p = jnp.asarray(_rel_buckets(BLOCK + qi - sj))
    pos = np.arange(WINDOW * N_KV) >> 1
    bucket_s = jnp.asarray(np.tile(_rel_buckets(WINDOW - pos)[None, :], (HALO, 1)))

    wg = _gate_weights(gate_a_w[l], gate_x_w[l])
    vecs = (rnn_conv_b[l], gate_a_b[l], gate_x_b[l], rnn_lambda[l], norm_rnn_out[l], norm_attn_out[l])
    vp = jnp.concatenate([jnp.stack(vecs), jnp.zeros((HALO - N_VP, D_RNN), F32)], axis=0)
    seq_w = (rnn_conv_w[l], wg, vp)

    x1, zs, h_last, kv_last, xr_tail, w_out_b, w_up_b, w_gate_b, w_down_b = _front(
        rel_bias_table, attn_sinks[l], xp, xt, row(norm_mix[l]), w_in[l], w_out[l], bucket_p, *seq_w,
        w_up[l], w_gate[l], w_down[l])
    x1s, conv_s, h_s, k_s, v_s = _smixer(
        rel_bias_table, attn_sinks[l], zs,
        jnp.swapaxes(state_rnn_conv[l], 0, 1), state_rnn_h[l],
        cache_k_win[l].reshape(DEC_BATCH * WINDOW * N_KV, HEAD_DIM),
        cache_v_win[l].reshape(DEC_BATCH * WINDOW * N_KV, HEAD_DIM),
        xs, w_out_b, bucket_s, *seq_w)
    y_p, y_s, ust = _ffn(x1, x1s, state_ffn_conv[l].reshape(DEC_BATCH, (FFN_CONV - 1) * D_FF),
                         row(norm_ffn[l]), row(norm_final), w_up_b, w_gate_b,
                         ffn_conv_w[l], row(ffn_conv_b[l]), w_down_b)

    ust = ust[(N_RT - 1) * N_UST:]
    p_states = (
        xr_tail[HALO - (RNN_CONV - 1):HALO][None, None],
        h_last[0:1][None],
        kv_last[:, :D_KV].reshape(1, 1, WINDOW, N_KV, HEAD_DIM),
        kv_last[:, D_KV:].reshape(1, 1, WINDOW, N_KV, HEAD_DIM),
        ust[HALO - (FFN_CONV - 1):HALO][None, None],
    )
    s_states = (
        jnp.swapaxes(conv_s, 0, 1)[None],
        h_s[None],
        k_s.reshape(1, DEC_BATCH, WINDOW, N_KV, HEAD_DIM),
        v_s.reshape(1, DEC_BATCH, WINDOW, N_KV, HEAD_DIM),
        jnp.stack([state_ffn_conv[l][:, FFN_CONV - 2, :], ust[HALO:]], axis=1)[None],
    )
    return (y_p[None], y_s[:, None, :]) + p_states + s_states
```

```python
import math

import numpy as np
import jax
import jax.numpy as jnp
from jax import lax
from jax.experimental import pallas as pl
from jax.experimental.pallas import tpu as pltpu

F32 = jnp.float32
BF16 = jnp.bfloat16

D_MODEL = 2048
SEQ = 8192
DEC_BATCH = 128
D_RNN = 1024
N_RNN_BLOCKS = 16
RNN_BLOCK = D_RNN // N_RNN_BLOCKS
RNN_CONV = 4
LRU_C = 8.0
N_HEADS = 8
HEAD_DIM = 128
N_KV = 2
GROUP = N_HEADS // N_KV
D_ATTN = N_HEADS * HEAD_DIM
WINDOW = 128
BLOCK = 128
NUM_BUCKETS = 32
MAX_DISTANCE = 128
D_FF = 3 * D_MODEL
FFN_CONV = 3
N_META = 16
EPS = 1e-6
NEG = -1e30
D_KV = N_KV * HEAD_DIM
D_IN = 2 * D_RNN + D_ATTN + 2 * D_KV
SCALE = HEAD_DIM ** -0.5
O_GR = D_RNN
O_Q = 2 * D_RNN
O_KV = 2 * D_RNN + D_ATTN

N_PAD = BLOCK - N_META
R_TAIL = DEC_BATCH + BLOCK
R_ALL = SEQ + R_TAIL
N_PBLK = SEQ // BLOCK

FB = 2 * BLOCK
N_FSTEP = (N_PBLK + 2) // 2
PC = 512
N_IN_EARLY = 5
N_STAGE = 4
N_SLAB = N_FSTEP - 1

TM = 640
N_RT = (SEQ + DEC_BATCH) // TM
LAST_P = SEQ - (N_RT - 1) * TM
TF = 768
N_FT = D_FF // TF
KC = 256
HALO = 8
FH = 16
GW = 256
N_GW = D_RNN // GW
SB = 16
N_RING = 3
N_VP = 6
VMEM_LIMIT = 56 * 1024 * 1024
VMEM_LIMIT_BIG = 60 * 1024 * 1024


def _rms(x, g):
    return x * lax.rsqrt(jnp.mean(x * x, axis=-1, keepdims=True) + EPS) * g


def _rel_buckets(d):
    d = np.maximum(d, 0)
    exact = NUM_BUCKETS // 2
    ratio = np.maximum(d, 1).astype(np.float32) / np.float32(exact)
    large = exact + (np.log(ratio) / np.float32(math.log(MAX_DISTANCE / exact))
                     * np.float32(NUM_BUCKETS - exact)).astype(np.int32)
    large = np.minimum(large, NUM_BUCKETS - 1)
    return np.where(d < exact, d, large).astype(np.int32)


def _table_lookup(bucket, tab_ref):
    outs = [jnp.zeros(bucket.shape, F32) for _ in range(N_HEADS)]
    for b in range(NUM_BUCKETS):
        hit = bucket == b
        outs = [jnp.where(hit, tab_ref[b, h], o) for h, o in enumerate(outs)]
    return outs


def _gates(xc, wg_ref, gab, gxb, lam):
    xcb = xc.astype(BF16)
    ga, gx = [], []
    for j in range(N_GW):
        gj = jnp.dot(xcb[:, GW * j:GW * (j + 1)], wg_ref[j], preferred_element_type=F32)
        ga.append(gj[:, :GW])
        gx.append(gj[:, GW:])
    r = jax.nn.sigmoid(jnp.concatenate(ga, axis=1) + gab)
    i = jax.nn.sigmoid(jnp.concatenate(gx, axis=1) + gxb)
    log_a = -LRU_C * r * jax.nn.softplus(-lam)
    a = jnp.exp(log_a)
    t = 1.0 - a * a
    b = jnp.where(t > 0.0, t * lax.rsqrt(t), 0.0) * i * xc
    return a, b


def _front_kernel(tab_ref, sink_ref, xa_ref, xb_ref, xt_ref, gm_ref, win_f, wout_f, bucket_ref,
                  cw_ref, wg_ref, vp_ref, wu_f, wgt_f, wd_f,
                  x1_ref, zs_ref, hlast_ref, kvlast_ref, xrt_ref, wout_hbm, wu_b, wgt_b, wd_b,
                  win_ref, wout_ref, zp, zb_o, lhs_p, xsp, mixp, xbuf, h_scr, kvbuf, bias_scr,
                  su, sg, sd, tu, tg, td, sem_in, sem_out, sem_w):
    k = pl.program_id(0)
    last = N_FSTEP - 1
    cb_v, gab_v, gxb_v, lam_v, nr_v, na_v = (vp_ref[r:r + 1] for r in range(N_VP))

    stages = ((wu_f, su, tu, wu_b), (wgt_f, sg, tg, wgt_b), (wd_f, sd, td, wd_b))

    def slab_in(s):
        return [pltpu.make_async_copy(w.at[pl.ds(s * st.shape[0], st.shape[0])], st, sem_in.at[n])
                for n, (w, st, _, _) in enumerate(stages)]

    def slab_out(s):
        return [pltpu.make_async_copy(t, w.at[pl.ds(s * t.shape[0], t.shape[0])], sem_out.at[n])
                for n, (_, _, t, w) in enumerate(stages)]

    @pl.when(k == 0)
    def _():
        for c in slab_in(0):
            c.start()

    @pl.when(k >= 1)
    def _():
        for c in slab_out(k - 1):
            c.wait()

    @pl.when(k < N_SLAB)
    def _():
        for c in slab_in(k):
            c.wait()
        for _, st, t, _ in stages:
            t[...] = st[...].astype(BF16)
        for c in slab_out(k):
            c.start()

    @pl.when(k < N_SLAB - 1)
    def _():
        for c in slab_in(k + 1):
            c.start()

    wout_copy = pltpu.make_async_copy(wout_ref, wout_hbm, sem_w.at[2 * N_STAGE])

    @pl.when(k == 0)
    def _():
        streams = ((win_f, win_ref, zp, 0), (wout_f, wout_ref, xsp, N_STAGE))
        rows = FB // N_STAGE
        n = D_MODEL // rows

        def chunk(stream, c):
            w_f, _, buf, sem0 = stream
            slot = c % N_STAGE
            return pltpu.make_async_copy(w_f.at[pl.ds(c * rows, rows)], buf.at[pl.ds(slot * rows, rows)],
                                         sem_w.at[sem0 + slot])

        for c in range(N_STAGE - 1):
            for stream in streams:
                chunk(stream, c).start()
        for c in range(n):
            for stream in streams:
                if c + N_STAGE - 1 < n:
                    chunk(stream, c + N_STAGE - 1).start()
                chunk(stream, c).wait()
                _, w_b, buf, _ = stream
                w_b[c * rows:(c + 1) * rows] = buf[pl.ds((c % N_STAGE) * rows, rows)].astype(BF16)
        wout_copy.start()

    def inproj_chunk(c):
        def run():
            cs = slice(c * PC, (c + 1) * PC)
            zp[:, cs] = jnp.dot(lhs_p[...], win_ref[:, cs], preferred_element_type=F32)
        return run

    def outproj_chunk(c):
        def run():
            cs = slice(c * PC, (c + 1) * PC)
            x1_ref[:, cs] = xsp[:, cs] + jnp.dot(mixp[...], wout_ref[:, cs], preferred_element_type=F32)
        return run

    @pl.when(k == 0)
    def _():
        prefix = xt_ref[DEC_BATCH:R_TAIL]
        zb_o[...] = jnp.dot(_rms(prefix, gm_ref[...]).astype(BF16), win_ref[...], preferred_element_type=F32)
        xsp[0:BLOCK] = jnp.zeros((BLOCK, D_MODEL), F32)
        xsp[BLOCK:FB] = prefix
        mixp[...] = jnp.zeros((FB, D_MODEL), BF16)
        xbuf[0:HALO] = jnp.zeros((HALO, D_RNN), F32)
        h_scr[...] = jnp.zeros((HALO, D_RNN), F32)
        kvbuf[...] = jnp.zeros((BLOCK, 2 * D_KV), F32)
        for h, bias in enumerate(_table_lookup(bucket_ref[...], tab_ref)):
            bias_scr[h] = bias

    def mixer(zb, is_prefix, first_key, fillers):
        fill = iter(fillers)

        def between():
            f = next(fill, None)
            if f is not None:
                f()

        between()
        xr = zb[:, 0:D_RNN]
        xbuf[HALO:HALO + BLOCK] = xr
        cw = cw_ref[...]
        xc = cb_v
        for j in range(RNN_CONV - 1):
            lo = HALO - (RNN_CONV - 1) + j
            xc = xc + xbuf[lo:lo + BLOCK] * cw[j:j + 1]
        xc = xc + xr * cw[RNN_CONV - 1:RNN_CONV]
        xbuf[0:HALO] = xr[BLOCK - HALO:BLOCK]

        a, b = _gates(xc, wg_ref, gab_v, gxb_v, lam_v)
        between()
        h = h_scr[0:1]
        if is_prefix is not None:
            row = lax.broadcasted_iota(jnp.int32, (BLOCK, D_RNN), 0)
            b = jnp.where(jnp.logical_and(is_prefix, row < N_PAD), 0.0, b)
            h = jnp.where(is_prefix, 0.0, h)

        ng = BLOCK // HALO
        a3 = a.reshape(ng, HALO, D_RNN)
        b3 = b.reshape(ng, HALO, D_RNN)
        sub = lax.broadcasted_iota(jnp.int32, (ng, HALO, D_RNN), 1)
        sh = 1
        while sh < HALO:
            a_prev = pltpu.roll(a3, sh, 1)
            b_prev = pltpu.roll(b3, sh, 1)
            m = sub >= sh
            b3 = jnp.where(m, a3 * b_prev + b3, b3)
            a3 = jnp.where(m, a3 * a_prev, a3)
            sh *= 2
        hs = []
        for g in range(ng):
            hg = a3[g] * h + b3[g]
            hs.append(hg)
            h = hg[HALO - 1:HALO]
        h_all = jnp.concatenate(hs, axis=0)
        h_scr[...] = jnp.broadcast_to(h, (HALO, D_RNN))
        between()
        y_rnn = h_all * jax.nn.gelu(zb[:, O_GR:O_GR + D_RNN])
        mix_a = _rms(y_rnn, nr_v).astype(BF16)

        q = zb[:, O_Q:O_Q + D_ATTN]
        kv = zb[:, O_KV:O_KV + 2 * D_KV]
        kvp = kvbuf[...]
        kvbuf[...] = kv
        col = lax.broadcasted_iota(jnp.int32, (BLOCK, 2 * BLOCK), 1)
        rowq = lax.broadcasted_iota(jnp.int32, (BLOCK, 2 * BLOCK), 0)
        d = BLOCK + rowq - col
        mask = (d >= 0) & (d < WINDOW) & (col >= first_key)
        outs = []
        for kh in range(N_KV):
            between()
            ks = slice(kh * HEAD_DIM, (kh + 1) * HEAD_DIM)
            vs = slice(D_KV + kh * HEAD_DIM, D_KV + (kh + 1) * HEAD_DIM)
            qs = jnp.concatenate(
                [q[:, (kh * GROUP + g) * HEAD_DIM:(kh * GROUP + g + 1) * HEAD_DIM] for g in range(GROUP)],
                axis=0).astype(BF16)
            kk = jnp.concatenate([kvp[:, ks], kv[:, ks]], axis=0).astype(BF16)
            vv = jnp.concatenate([kvp[:, vs], kv[:, vs]], axis=0).astype(BF16)
            sc = lax.dot_general(qs, kk, (((1,), (1,)), ((), ())), preferred_element_type=F32)
            es, dens = [], []
            for g in range(GROUP):
                hh = kh * GROUP + g
                lg = sc[g * BLOCK:(g + 1) * BLOCK] * SCALE + bias_scr[hh]
                lg = jnp.where(mask, lg, NEG)
                sink = sink_ref[hh]
                mx = jnp.maximum(jnp.max(lg, axis=-1, keepdims=True), sink)
                e = jnp.exp(lg - mx)
                dens.append(jnp.sum(e, axis=-1, keepdims=True) + jnp.exp(sink - mx))
                es.append(e.astype(BF16))
            pv = jnp.dot(jnp.concatenate(es, axis=0), vv, preferred_element_type=F32)
            for g in range(GROUP):
                outs.append(pv[g * BLOCK:(g + 1) * BLOCK] / dens[g])
        between()
        mix_b = _rms(jnp.concatenate(outs, axis=1), na_v).astype(BF16)

        return jnp.concatenate([mix_a, mix_b], axis=1), h, kv, xr

    xa = jnp.where(k == last, xt_ref[0:DEC_BATCH], xa_ref[...])
    xb = xb_ref[...]
    lhs_p[0:BLOCK] = _rms(xa, gm_ref[...]).astype(BF16)
    lhs_p[BLOCK:FB] = _rms(xb, gm_ref[...]).astype(BF16)
    n_in, n_out = D_IN // PC, D_MODEL // PC
    mix_o, h, kv, xr = mixer(zb_o, k == 0, jnp.where(k == 0, BLOCK + N_PAD, 0),
                             [inproj_chunk(c) for c in range(N_IN_EARLY)])
    mixp[BLOCK:FB] = mix_o
    hlast_ref[...] = jnp.broadcast_to(h, (HALO, D_RNN))
    kvlast_ref[...] = kv
    xrt_ref[...] = xr[BLOCK - HALO:BLOCK]

    mix_e = mixer(zp.at[0:BLOCK], None, jnp.where(k == 0, N_PAD, 0),
                  [inproj_chunk(c) for c in range(N_IN_EARLY, n_in)] + [outproj_chunk(c) for c in range(n_out)])[0]
    mixp[0:BLOCK] = mix_e
    xsp[0:BLOCK] = xa
    xsp[BLOCK:FB] = xb
    zb_o[...] = zp[BLOCK:FB]

    @pl.when(k == last)
    def _():
        zs_ref[...] = zp[0:BLOCK]
        wout_copy.wait()


def _front(tab, sinks, xp, xt, gm, w_in, w_out, bucket, cw, wg, vp, w_up, w_gate, w_down):
    ffn_w = (w_up, w_gate, w_down)
    slab = lambda w: (w.shape[0] // N_SLAB, w.shape[1])
    hbm = pl.BlockSpec(memory_space=pl.ANY)
    vec = lambda n: pl.BlockSpec((1, n), lambda k: (0, 0))
    smem = pl.BlockSpec(memory_space=pltpu.SMEM)
    once = lambda shape: pl.BlockSpec(shape, lambda k: (0,) * len(shape), pipeline_mode=pl.Buffered(1))
    return pl.pallas_call(
        _front_kernel,
        grid=(N_FSTEP,),
        in_specs=[
            smem, smem,
            pl.BlockSpec((BLOCK, D_MODEL), lambda k: (jnp.minimum(2 * k, N_PBLK - 1), 0)),
            pl.BlockSpec((BLOCK, D_MODEL), lambda k: (jnp.minimum(2 * k + 1, N_PBLK - 1), 0)),
            once((R_TAIL, D_MODEL)),
            vec(D_MODEL),
            hbm, hbm,
            pl.BlockSpec((BLOCK, 2 * BLOCK), lambda k: (0, 0)),
            pl.BlockSpec((RNN_CONV, D_RNN), lambda k: (0, 0)),
            pl.BlockSpec((N_GW, GW, 2 * GW), lambda k: (0, 0, 0)),
            pl.BlockSpec((HALO, D_RNN), lambda k: (0, 0)),
            hbm, hbm, hbm,
        ],
        out_specs=[
            pl.BlockSpec((FB, D_MODEL), lambda k: (jnp.where(k == 0, SEQ // FB, k - 1), 0)),
            pl.BlockSpec((BLOCK, D_IN), lambda k: (0, 0)),
            pl.BlockSpec((HALO, D_RNN), lambda k: (0, 0)),
            pl.BlockSpec((BLOCK, 2 * D_KV), lambda k: (0, 0)),
            pl.BlockSpec((HALO, D_RNN), lambda k: (0, 0)),
            hbm, hbm, hbm, hbm,
        ],
        out_shape=[
            jax.ShapeDtypeStruct((R_ALL, D_MODEL), F32),
            jax.ShapeDtypeStruct((DEC_BATCH, D_IN), F32),
            jax.ShapeDtypeStruct((HALO, D_RNN), F32),
            jax.ShapeDtypeStruct((BLOCK, 2 * D_KV), F32),
            jax.ShapeDtypeStruct((HALO, D_RNN), F32),
        ] + [jax.ShapeDtypeStruct(w.shape, BF16) for w in (w_out,) + ffn_w],
        scratch_shapes=[
            pltpu.VMEM(w_in.shape, BF16),
            pltpu.VMEM(w_out.shape, BF16),
            pltpu.VMEM((FB, D_IN), F32),
            pltpu.VMEM((BLOCK, D_IN), F32),
            pltpu.VMEM((FB, D_MODEL), BF16),
            pltpu.VMEM((FB, D_MODEL), F32),
            pltpu.VMEM((FB, D_MODEL), BF16),
            pltpu.VMEM((HALO + BLOCK, D_RNN), F32),
            pltpu.VMEM((HALO, D_RNN), F32),
            pltpu.VMEM((BLOCK, 2 * D_KV), F32),
            pltpu.VMEM((N_HEADS, BLOCK, 2 * BLOCK), F32),
        ] + [pltpu.VMEM(slab(w), F32) for w in ffn_w] + [pltpu.VMEM(slab(w), BF16) for w in ffn_w] + [
            pltpu.SemaphoreType.DMA((len(ffn_w),)),
            pltpu.SemaphoreType.DMA((len(ffn_w),)),
            pltpu.SemaphoreType.DMA((2 * N_STAGE + 1,)),
        ],
        compiler_params=pltpu.CompilerParams(
            dimension_semantics=("arbitrary",), vmem_limit_bytes=VMEM_LIMIT_BIG),
        name="front",
    )(tab, sinks, xp, xp, xt, gm, w_in, w_out, bucket, cw, wg, vp, *ffn_w)


def _smixer_kernel(tab_ref, sink_ref, z_ref, st_ref, h0_ref, kc_hbm, vc_hbm, xs_ref, wout_hbm,
                   bucket_ref, cw_ref, wg_ref, vp_ref,
                   x1s_ref, conv_ref, h_ref, ko_ref, vo_ref, bias_scr, mix_scr, wout_ref, sem_w,
                   k_ring, v_ring, sem_r):
    c = pl.program_id(0)

    n_chunk = DEC_BATCH // SB
    rows_c = SB * WINDOW * N_KV

    def fetch(ch):
        slot = ch % N_RING
        return [pltpu.make_async_copy(src.at[pl.ds(ch * rows_c, rows_c)], ring.at[slot], sem_r.at[n, slot])
                for n, (src, ring) in enumerate(((kc_hbm, k_ring), (vc_hbm, v_ring)))]

    @pl.when(c == 0)
    def _():
        for ch in range(N_RING - 1):
            for n, cp in enumerate(fetch(ch)):
                cp.start(priority=n)

    @pl.when(c + N_RING - 1 < n_chunk)
    def _():
        for n, cp in enumerate(fetch(c + N_RING - 1)):
            cp.start(priority=n)
    cb_v, gab_v, gxb_v, lam_v, nr_v, na_v = (vp_ref[r:r + 1] for r in range(N_VP))
    nrow = N_HEADS * SB
    ncol = SB * WINDOW * N_KV
    wout_copy = pltpu.make_async_copy(wout_hbm, wout_ref, sem_w)

    @pl.when(c == 0)
    def _():
        wout_copy.start()
        col = lax.broadcasted_iota(jnp.int32, (SB, ncol), 1)
        row = lax.broadcasted_iota(jnp.int32, (SB, ncol), 0)
        own = ((col >> 8) == row) & (((col >> 1) & (WINDOW - 1)) >= 1)
        for h, bias in enumerate(_table_lookup(bucket_ref[...], tab_ref)):
            ok = own & ((col & 1) == h // GROUP)
            bias_scr[h * SB:(h + 1) * SB] = jnp.where(ok, jnp.concatenate([bias] * (SB // HALO), axis=0), NEG)

    r0 = pl.multiple_of(c * SB, SB)
    for cp in fetch(c):
        cp.wait()
    kc_ref = k_ring.at[c % N_RING]
    vc_ref = v_ring.at[c % N_RING]

    xr = z_ref[:, 0:D_RNN]
    cw = cw_ref[...]
    xc = cb_v
    for j in range(RNN_CONV - 1):
        xc = xc + st_ref[j] * cw[j:j + 1]
    xc = xc + xr * cw[RNN_CONV - 1:RNN_CONV]
    for j in range(1, RNN_CONV - 1):
        conv_ref[j - 1] = st_ref[j]
    conv_ref[RNN_CONV - 2] = xr
    a, b = _gates(xc, wg_ref, gab_v, gxb_v, lam_v)
    h = a * h0_ref[...] + b
    h_ref[...] = h
    y_rnn = h * jax.nn.gelu(z_ref[:, O_GR:O_GR + D_RNN])
    mix_scr[pl.ds(r0, SB), 0:D_RNN] = _rms(y_rnn, nr_v).astype(BF16)

    q = z_ref[:, O_Q:O_Q + D_ATTN]
    kv = z_ref[:, O_KV:O_KV + 2 * D_KV]
    qs = jnp.concatenate([q[:, h * HEAD_DIM:(h + 1) * HEAD_DIM] for h in range(N_HEADS)], axis=0).astype(BF16)
    new_rows = lambda off: jnp.concatenate(
        [kv[:, off + (h // GROUP) * HEAD_DIM:off + (h // GROUP + 1) * HEAD_DIM] for h in range(N_HEADS)],
        axis=0).astype(BF16).astype(F32)
    k_new = new_rows(0)
    v_new = new_rows(D_KV)
    sc = lax.dot_general(qs, kc_ref[...].astype(BF16), (((1,), (1,)), ((), ())), preferred_element_type=F32)
    lg = sc * SCALE + bias_scr[...]
    rh = lax.broadcasted_iota(jnp.int32, (nrow, 1), 0) >> 4
    sink = jnp.zeros((nrow, 1), F32)
    bias_new = jnp.zeros((nrow, 1), F32)
    for h in range(N_HEADS):
        sink = jnp.where(rh == h, sink_ref[h], sink)
        bias_new = jnp.where(rh == h, tab_ref[0, h], bias_new)
    lg_new = jnp.sum(qs.astype(F32) * k_new, axis=-1, keepdims=True) * SCALE + bias_new
    mx = jnp.maximum(jnp.maximum(jnp.max(lg, axis=-1, keepdims=True), lg_new), sink)
    e = jnp.exp(lg - mx)
    e_new = jnp.exp(lg_new - mx)
    den = jnp.sum(e, axis=-1, keepdims=True) + e_new + jnp.exp(sink - mx)
    pv = jnp.dot(e.astype(BF16), vc_ref[...].astype(BF16), preferred_element_type=F32)
    o = (pv + e_new.astype(BF16).astype(F32) * v_new) / den
    y_attn = jnp.concatenate([o[h * SB:(h + 1) * SB] for h in range(N_HEADS)], axis=1)
    mix_scr[pl.ds(r0, SB), D_RNN:] = _rms(y_attn, na_v).astype(BF16)

    per = WINDOW * N_KV
    slide = lambda ref: pltpu.roll(ref[...].reshape(SB, per, HEAD_DIM), per - N_KV, 1).reshape(SB * per, HEAD_DIM)
    ko_ref[...] = slide(kc_ref)
    vo_ref[...] = slide(vc_ref)
    for bi in range(SB):
        for kh in range(N_KV):
            r = (bi + 1) * per - N_KV + kh
            ko_ref[r:r + 1, :] = kv[bi:bi + 1, kh * HEAD_DIM:(kh + 1) * HEAD_DIM]
            vo_ref[r:r + 1, :] = kv[bi:bi + 1, D_KV + kh * HEAD_DIM:D_KV + (kh + 1) * HEAD_DIM]

    @pl.when(c == DEC_BATCH // SB - 1)
    def _():
        wout_copy.wait()
        x1s_ref[...] = xs_ref[...] + jnp.dot(mix_scr[...], wout_ref[...], preferred_element_type=F32)


def _smixer(tab, sinks, zs, st, h0, kc, vc, xs, w_out, bucket, cw, wg, vp):
    smem = pl.BlockSpec(memory_space=pltpu.SMEM)
    cache = pl.BlockSpec((SB * WINDOW * N_KV, HEAD_DIM), lambda c: (c, 0))
    return pl.pallas_call(
        _smixer_kernel,
        grid=(DEC_BATCH // SB,),
        in_specs=[
            smem, smem,
            pl.BlockSpec((SB, D_IN), lambda c: (c, 0)),
            pl.BlockSpec((RNN_CONV - 1, SB, D_RNN), lambda c: (0, c, 0)),
            pl.BlockSpec((SB, D_RNN), lambda c: (c, 0)),
            pl.BlockSpec(memory_space=pl.ANY), pl.BlockSpec(memory_space=pl.ANY),
            pl.BlockSpec((DEC_BATCH, D_MODEL), lambda c: (0, 0)),
            pl.BlockSpec(memory_space=pl.ANY),
            pl.BlockSpec((HALO, SB * WINDOW * N_KV), lambda c: (0, 0)),
            pl.BlockSpec((RNN_CONV, D_RNN), lambda c: (0, 0)),
            pl.BlockSpec((N_GW, GW, 2 * GW), lambda c: (0, 0, 0)),
            pl.BlockSpec((HALO, D_RNN), lambda c: (0, 0)),
        ],
        out_specs=[
            pl.BlockSpec((DEC_BATCH, D_MODEL), lambda c: (0, 0)),
            pl.BlockSpec((RNN_CONV - 1, SB, D_RNN), lambda c: (0, c, 0)),
            pl.BlockSpec((SB, D_RNN), lambda c: (c, 0)),
            cache, cache,
        ],
        out_shape=[
            jax.ShapeDtypeStruct((DEC_BATCH, D_MODEL), F32),
            jax.ShapeDtypeStruct((RNN_CONV - 1, DEC_BATCH, D_RNN), F32),
            jax.ShapeDtypeStruct((DEC_BATCH, D_RNN), F32),
            jax.ShapeDtypeStruct((DEC_BATCH * WINDOW * N_KV, HEAD_DIM), F32),
            jax.ShapeDtypeStruct((DEC_BATCH * WINDOW * N_KV, HEAD_DIM), F32),
        ],
        scratch_shapes=[
            pltpu.VMEM((N_HEADS * SB, SB * WINDOW * N_KV), F32),
            pltpu.VMEM((DEC_BATCH, D_MODEL), BF16),
            pltpu.VMEM((D_MODEL, D_MODEL), BF16),
            pltpu.SemaphoreType.DMA(()),
            pltpu.VMEM((N_RING, SB * WINDOW * N_KV, HEAD_DIM), F32),
            pltpu.VMEM((N_RING, SB * WINDOW * N_KV, HEAD_DIM), F32),
            pltpu.SemaphoreType.DMA((2, N_RING)),
        ],
        compiler_params=pltpu.CompilerParams(
            dimension_semantics=("arbitrary",), vmem_limit_bytes=VMEM_LIMIT),
        name="smixer",
    )(tab, sinks, zs, st, h0, kc, vc, xs, w_out, bucket, cw, wg, vp)


U_LO = LAST_P - HALO
U_HI = LAST_P + DEC_BATCH
N_UST = U_HI - U_LO


def _ffn_kernel(x1_ref, halo_ref, x1s_ref, st0_ref, st1_ref, gn_ref, gf_ref, wu_ref, wg_ref, cw_ref, cb_ref,
                wd_ref, y_ref, ys_ref, ust_ref, h2_scr, ubuf, abuf):
    i = pl.program_id(0)
    j = pl.program_id(1)

    @pl.when(j == 0)
    def _():
        h2_scr[0:FH] = _rms(halo_ref[...], gn_ref[...]).astype(BF16)
        h2_scr[FH:FH + TM] = _rms(x1_ref[...], gn_ref[...]).astype(BF16)
        y_ref[...] = jnp.zeros((TM, D_MODEL), F32)

        @pl.when(i == N_RT - 1)
        def _():
            h2_scr[FH + LAST_P:FH + U_HI] = _rms(x1s_ref[...], gn_ref[...]).astype(BF16)

    ubuf[...] = jnp.dot(h2_scr[...], wu_ref[...], preferred_element_type=F32)
    gate = jnp.dot(h2_scr[FH:FH + TM], wg_ref[...], preferred_element_type=F32)
    cw = cw_ref[...]
    cb = cb_ref[...]
    for c in range(TF // KC):
        cs = slice(c * KC, (c + 1) * KC)
        tap = lambda r0, n: ubuf[r0:r0 + n, cs]
        uc = (cb[:, cs] + tap(FH - 2, TM) * cw[0:1, cs] + tap(FH - 1, TM) * cw[1:2, cs] + tap(FH, TM) * cw[2:3, cs])
        act = jax.nn.gelu(uc) * gate[:, cs]
        abuf[:, cs] = act.astype(BF16)
        ucs = (cb[:, cs] + st0_ref[:, cs] * cw[0:1, cs] + st1_ref[:, cs] * cw[1:2, cs]
               + tap(FH + LAST_P, DEC_BATCH) * cw[2:3, cs])
        act_s = jax.nn.gelu(ucs) * gate[LAST_P:U_HI, cs]
        abuf[LAST_P:U_HI, cs] = jnp.where(i == N_RT - 1, act_s, act[LAST_P:U_HI]).astype(BF16)
        y_ref[...] += jnp.dot(abuf[:, cs], wd_ref[cs, :], preferred_element_type=F32)
    ust_ref[...] = ubuf[FH + U_LO:FH + U_HI]

    @pl.when(j == N_FT - 1)
    def _():
        @pl.when(i == N_RT - 1)
        def _():
            ys_ref[...] = _rms(x1s_ref[...] + y_ref[LAST_P:U_HI], gf_ref[...])

        y_ref[...] = _rms(x1_ref[...] + y_ref[...], gf_ref[...])


def _ffn(x1, x1s, st, gn, gf, wu, wg, cw, cb, wd):
    def halo_idx(i, j):
        return (jnp.where(i == 0, R_ALL // FH - 1, i * (TM // FH) - 1), 0)

    return pl.pallas_call(
        _ffn_kernel,
        grid=(N_RT, N_FT),
        in_specs=[
            pl.BlockSpec((TM, D_MODEL), lambda i, j: (i, 0)),
            pl.BlockSpec((FH, D_MODEL), halo_idx),
            pl.BlockSpec((DEC_BATCH, D_MODEL), lambda i, j: (0, 0)),
            pl.BlockSpec((DEC_BATCH, TF), lambda i, j: (0, j)),
            pl.BlockSpec((DEC_BATCH, TF), lambda i, j: (0, N_FT + j)),
            pl.BlockSpec((1, D_MODEL), lambda i, j: (0, 0)),
            pl.BlockSpec((1, D_MODEL), lambda i, j: (0, 0)),
            pl.BlockSpec((D_MODEL, TF), lambda i, j: (0, j)),
            pl.BlockSpec((D_MODEL, TF), lambda i, j: (0, j)),
            pl.BlockSpec((FFN_CONV, TF), lambda i, j: (0, j)),
            pl.BlockSpec((1, TF), lambda i, j: (0, j)),
            pl.BlockSpec((TF, D_MODEL), lambda i, j: (j, 0)),
        ],
        out_specs=[
            pl.BlockSpec((TM, D_MODEL), lambda i, j: (i, 0)),
            pl.BlockSpec((DEC_BATCH, D_MODEL), lambda i, j: (0, 0)),
            pl.BlockSpec((N_UST, TF), lambda i, j: (i, j)),
        ],
        out_shape=[
            jax.ShapeDtypeStruct((SEQ, D_MODEL), F32),
            jax.ShapeDtypeStruct((DEC_BATCH, D_MODEL), F32),
            jax.ShapeDtypeStruct((N_RT * N_UST, D_FF), F32),
        ],
        scratch_shapes=[
            pltpu.VMEM((FH + TM, D_MODEL), BF16),
            pltpu.VMEM((FH + TM, TF), F32),
            pltpu.VMEM((TM, TF), BF16),
        ],
        compiler_params=pltpu.CompilerParams(
            dimension_semantics=("arbitrary", "arbitrary"), vmem_limit_bytes=VMEM_LIMIT_BIG),
        name="ffn",
    )(x1, x1, x1s, st, st, gn, gf, wu, wg, cw, cb, wd)


def _gate_weights(wa, wx):
    per = GW // RNN_BLOCK
    eye = jnp.eye(per, dtype=wa.dtype)

    def bd(w):
        w = w.reshape(N_GW, per, RNN_BLOCK, RNN_BLOCK)
        return jnp.einsum('gpcd,pq->gpcqd', w, eye).reshape(N_GW, GW, GW)

    return jnp.concatenate([bd(wa), bd(wx)], axis=-1).astype(BF16)


def kernel(x_prompt, x_sample, state_rnn_conv, state_rnn_h, cache_k_win, cache_v_win, state_ffn_conv,
           meta_tokens, rel_bias_table, norm_mix, w_in, rnn_conv_w, rnn_conv_b, gate_a_w, gate_a_b,
           gate_x_w, gate_x_b, rnn_lambda, attn_sinks, norm_rnn_out, norm_attn_out, w_out, norm_ffn,
           w_up, w_gate, ffn_conv_w, ffn_conv_b, w_down, norm_final):
    l = 0
    xp = x_prompt[0]
    xs = x_sample[:, 0, :]
    xt = jnp.concatenate([xs, jnp.zeros((N_PAD, D_MODEL), F32), meta_tokens], axis=0)
    row = lambda v: v.reshape(1, -1)

    qi = np.arange(BLOCK)[:, None]
    sj = np.arange(2 * BLOCK)[None, :]
    bucket_p = jnp.asarray(_rel_buckets(BLOCK + qi - sj))
    pos = (np.arange(SB * WINDOW * N_KV) >> 1) & (WINDOW - 1)
    bucket_s = jnp.asarray(np.tile(_rel_buckets(WINDOW - pos)[None, :], (HALO, 1)))

    wg = _gate_weights(gate_a_w[l], gate_x_w[l])
    vecs = (rnn_conv_b[l], gate_a_b[l], gate_x_b[l], rnn_lambda[l], norm_rnn_out[l], norm_attn_out[l])
    vp = jnp.concatenate([jnp.stack(vecs), jnp.zeros((HALO - N_VP, D_RNN), F32)], axis=0)
    seq_w = (rnn_conv_w[l], wg, vp)

    x1, zs, h_last, kv_last, xr_tail, w_out_b, w_up_b, w_gate_b, w_down_b = _front(
        rel_bias_table, attn_sinks[l], xp, xt, row(norm_mix[l]), w_in[l], w_out[l], bucket_p, *seq_w,
        w_up[l], w_gate[l], w_down[l])
    x1s, conv_s, h_s, k_s, v_s = _smixer(
        rel_bias_table, attn_sinks[l], zs,
        jnp.swapaxes(state_rnn_conv[l], 0, 1), state_rnn_h[l],
        cache_k_win[l].reshape(DEC_BATCH * WINDOW * N_KV, HEAD_DIM),
        cache_v_win[l].reshape(DEC_BATCH * WINDOW * N_KV, HEAD_DIM),
        xs, w_out_b, bucket_s, *seq_w)
    y_p, y_s, ust = _ffn(x1, x1s, state_ffn_conv[l].reshape(DEC_BATCH, (FFN_CONV - 1) * D_FF),
                         row(norm_ffn[l]), row(norm_final), w_up_b, w_gate_b,
                         ffn_conv_w[l], row(ffn_conv_b[l]), w_down_b)

    ust = ust[(N_RT - 1) * N_UST:]
    p_states = (
        xr_tail[HALO - (RNN_CONV - 1):HALO][None, None],
        h_last[0:1][None],
        kv_last[:, :D_KV].reshape(1, 1, WINDOW, N_KV, HEAD_DIM),
        kv_last[:, D_KV:].reshape(1, 1, WINDOW, N_KV, HEAD_DIM),
        ust[HALO - (FFN_CONV - 1):HALO][None, None],
    )
    s_states = (
        jnp.swapaxes(conv_s, 0, 1)[None],
        h_s[None],
        k_s.reshape(1, DEC_BATCH, WINDOW, N_KV, HEAD_DIM),
        v_s.reshape(1, DEC_BATCH, WINDOW, N_KV, HEAD_DIM),
        jnp.stack([state_ffn_conv[l][:, FFN_CONV - 2, :], ust[HALO:]], axis=1)[None],
    )
    return (y_p[None], y_s[:, None, :]) + p_states + s_states
```
